```python
import math
import jax, jax.numpy as jnp
from jax import lax
import numpy as np

D_MODEL = 1024
BATCH = 8
SEQ = 2048
DEPTH = 4
DEC_BATCH = 128
DEC_SEQ = 4
PAST_LEN = 16384
PAGE_SIZE = 128

N_MIXERS = 2
N_CHUNK_LAYERS = (DEPTH + 1) // 2
N_SSM_LAYERS = DEPTH // 2
CHUNK = 128
EXP_A = 2 * D_MODEL
N_GROUPS_A = 8
GROUP_DIM_A = EXP_A // N_GROUPS_A
EXP_B = D_MODEL
SSM_GROUP = 16
N_GROUPS_B = EXP_B // SSM_GROUP
STATE_P = 64
DT_MIN = 1e-3
DT_MAX = 1e-1
EPS = 1e-6

kernel_name = "hybrid_chunk_gmlp_s5_decode_step"


def rmsnorm(x, g):
    xf = x.astype(jnp.float32)
    y = xf * lax.rsqrt(jnp.mean(xf * xf, axis=-1, keepdims=True) + EPS)
    return (y * g.astype(jnp.float32)).astype(x.dtype)


def layernorm(x, g, b):
    xf = x.astype(jnp.float32)
    mu = jnp.mean(xf, axis=-1, keepdims=True)
    xc = xf - mu
    y = xc * lax.rsqrt(jnp.mean(xc * xc, axis=-1, keepdims=True) + EPS)
    return (y * g.astype(jnp.float32) + b.astype(jnp.float32)).astype(x.dtype)


def chunk_gmlp(h, w_in, ln_g, ln_b, w_s, b_s, w_out):
    bn, seq_len, _ = h.shape
    u, v, z = jnp.split(h @ w_in, 3, axis=-1)
    v = layernorm(v, ln_g, ln_b)
    c = min(seq_len, CHUNK)
    n_chunks = seq_len // c
    mask = jnp.tril(jnp.ones((c, c), dtype=bool))
    ws = jnp.where(mask, w_s[:, :c, :c], 0)
    vc = v.reshape(bn, n_chunks, c, N_GROUPS_A, GROUP_DIM_A)
    s = jnp.einsum('hts,bnshd->bnthd', ws, vc) + jnp.transpose(b_s[:, :c])[None, None, :, :, None]
    gated = u * s.reshape(bn, seq_len, EXP_A) * jax.nn.silu(z)
    return gated @ w_out, v


def zoh(a_re, a_im, log_dt, b_re, b_im):
    dt = jnp.exp(log_dt.astype(jnp.float32))[:, None]
    ar = a_re.astype(jnp.float32)
    ai = a_im.astype(jnp.float32)
    mag = jnp.exp(dt * ar)
    ang = dt * ai
    abar_re = mag * jnp.cos(ang)
    abar_im = mag * jnp.sin(ang)
    nr = abar_re - 1.0
    ni = abar_im
    den = ar * ar + ai * ai
    coef_re = (nr * ar + ni * ai) / den
    coef_im = (ni * ar - nr * ai) / den
    br = b_re.astype(jnp.float32)
    bi = b_im.astype(jnp.float32)
    bbar_re = coef_re[..., None] * br - coef_im[..., None] * bi
    bbar_im = coef_re[..., None] * bi + coef_im[..., None] * br
    return abar_re, abar_im, bbar_re, bbar_im


def _combine(e1, e2):
    a1r, a1i, b1r, b1i = e1
    a2r, a2i, b2r, b2i = e2
    return (a2r * a1r - a2i * a1i,
            a2r * a1i + a2i * a1r,
            a2r * b1r - a2i * b1i + b2r,
            a2r * b1i + a2i * b1r + b2i)


def s5_mixer(h, h0_re, h0_im, w_in, a_re, a_im, log_dt, b_re, b_im, c_re, c_im, d_skip,
             w_glu1, b_glu1, w_glu2, b_glu2, w_out):
    bn, seq_len, _ = h.shape
    xb, z = jnp.split(h @ w_in, 2, axis=-1)
    xg = xb.astype(jnp.float32).reshape(bn, seq_len, N_GROUPS_B, SSM_GROUP)
    abr, abi, bbr, bbi = zoh(a_re, a_im, log_dt, b_re, b_im)
    bu_re = jnp.einsum('blgk,gpk->blgp', xg, bbr)
    bu_im = jnp.einsum('blgk,gpk->blgp', xg, bbi)
    ar = jnp.broadcast_to(abr, bu_re.shape)
    ai = jnp.broadcast_to(abi, bu_re.shape)
    acr, aci, hr, hi = lax.associative_scan(_combine, (ar, ai, bu_re, bu_im), axis=1)
    if h0_re is not None:
        h0r = h0_re.astype(jnp.float32)[:, None]
        h0i = h0_im.astype(jnp.float32)[:, None]
        hr = hr + acr * h0r - aci * h0i
        hi = hi + acr * h0i + aci * h0r
    cr = c_re.astype(jnp.float32)
    ci = c_im.astype(jnp.float32)
    y = jnp.einsum('gkp,blgp->blgk', cr, hr) - jnp.einsum('gkp,blgp->blgk', ci, hi)
    y = y.reshape(bn, seq_len, EXP_B) + d_skip.astype(jnp.float32) * xb.astype(jnp.float32)
    y = jax.nn.gelu(y).astype(h.dtype)
    y = (y @ w_glu1 + b_glu1) * jax.nn.sigmoid(y @ w_glu2 + b_glu2)
    out = (y * jax.nn.silu(z)) @ w_out
    return out, hr[:, -1], hi[:, -1]


def _trunk(x, h0_re, h0_im, norm_pre, norm_post,
           w_in_a, ln_v_g, ln_v_b, w_s, b_s, w_out_a,
           w_in_b, a_re, a_im, log_dt, b_re, b_im, c_re, c_im, d_skip,
           w_glu1, b_glu1, w_glu2, b_glu2, w_out_b):
    v_rows, st_re, st_im = [], [], []
    for i in range(DEPTH):
        j = i // N_MIXERS
        hn = rmsnorm(x, norm_pre[i])
        if i % N_MIXERS == 0:
            out, v = chunk_gmlp(hn, w_in_a[j], ln_v_g[j], ln_v_b[j], w_s[j], b_s[j], w_out_a[j])
            v_rows.append(v)
        else:
            h0r = None if h0_re is None else h0_re[j]
            h0i = None if h0_im is None else h0_im[j]
            out, hr, hi = s5_mixer(hn, h0r, h0i, w_in_b[j], a_re[j], a_im[j], log_dt[j],
                                   b_re[j], b_im[j], c_re[j], c_im[j], d_skip[j],
                                   w_glu1[j], b_glu1[j], w_glu2[j], b_glu2[j], w_out_b[j])
            st_re.append(hr)
            st_im.append(hi)
        x = x + rmsnorm(out, norm_post[i])
    return x, jnp.stack(v_rows), jnp.stack(st_re), jnp.stack(st_im)


def setup_inputs(seed: int = 0) -> dict:
    key = jax.random.key(seed)
    ks = jax.random.split(key, 28)
    f32 = jnp.float32
    nrm = lambda k, s, sc: jax.random.normal(k, s, f32) * sc
    na, nb = N_CHUNK_LAYERS, N_SSM_LAYERS
    n_idx = jnp.arange(STATE_P, dtype=f32)
    return {
        "x_prompt": nrm(ks[0], (BATCH, SEQ, D_MODEL), 1.0),
        "x_sample": nrm(ks[1], (DEC_BATCH, DEC_SEQ, D_MODEL), 1.0),
        "state_ssm_re": nrm(ks[2], (nb, DEC_BATCH, N_GROUPS_B, STATE_P), 0.3),
        "state_ssm_im": nrm(ks[3], (nb, DEC_BATCH, N_GROUPS_B, STATE_P), 0.3),
        "norm_pre": 1.0 + nrm(ks[4], (DEPTH, D_MODEL), 0.02),
        "norm_post": 1.0 + nrm(ks[5], (DEPTH, D_MODEL), 0.02),
        "w_in_a": nrm(ks[6], (na, D_MODEL, 3 * EXP_A), D_MODEL ** -0.5),
        "ln_v_g": 1.0 + nrm(ks[7], (na, EXP_A), 0.02),
        "ln_v_b": nrm(ks[8], (na, EXP_A), 0.01),
        "w_s": nrm(ks[9], (na, N_GROUPS_A, CHUNK, CHUNK), CHUNK ** -0.5),
        "b_s": 1.0 + nrm(ks[10], (na, N_GROUPS_A, CHUNK), 0.1),
        "w_out_a": nrm(ks[11], (na, EXP_A, D_MODEL), EXP_A ** -0.5),
        "w_in_b": nrm(ks[12], (nb, D_MODEL, 2 * EXP_B), D_MODEL ** -0.5),
        "a_re": -0.5 + nrm(ks[13], (nb, N_GROUPS_B, STATE_P), 0.01),
        "a_im": jnp.pi * n_idx + nrm(ks[14], (nb, N_GROUPS_B, STATE_P), 0.01),
        "log_dt": jax.random.uniform(ks[15], (nb, N_GROUPS_B), f32, math.log(DT_MIN), math.log(DT_MAX)),
        "b_re": nrm(ks[16], (nb, N_GROUPS_B, STATE_P, SSM_GROUP), (2 * SSM_GROUP) ** -0.5),
        "b_im": nrm(ks[17], (nb, N_GROUPS_B, STATE_P, SSM_GROUP), (2 * SSM_GROUP) ** -0.5),
        "c_re": nrm(ks[18], (nb, N_GROUPS_B, SSM_GROUP, STATE_P), (2 * STATE_P) ** -0.5),
        "c_im": nrm(ks[19], (nb, N_GROUPS_B, SSM_GROUP, STATE_P), (2 * STATE_P) ** -0.5),
        "d_skip": nrm(ks[20], (nb, EXP_B), 1.0),
        "w_glu1": nrm(ks[21], (nb, EXP_B, EXP_B), EXP_B ** -0.5),
        "b_glu1": nrm(ks[22], (nb, EXP_B), 0.01),
        "w_glu2": nrm(ks[23], (nb, EXP_B, EXP_B), EXP_B ** -0.5),
        "b_glu2": nrm(ks[24], (nb, EXP_B), 0.01),
        "w_out_b": nrm(ks[25], (nb, EXP_B, D_MODEL), EXP_B ** -0.5),
    }


def reference(x_prompt, x_sample, state_ssm_re, state_ssm_im, norm_pre, norm_post,
              w_in_a, ln_v_g, ln_v_b, w_s, b_s, w_out_a,
              w_in_b, a_re, a_im, log_dt, b_re, b_im, c_re, c_im, d_skip,
              w_glu1, b_glu1, w_glu2, b_glu2, w_out_b):
    y_prompt, _, ssm_re_prompt, ssm_im_prompt = _trunk(
        x_prompt, None, None, norm_pre, norm_post,
        w_in_a, ln_v_g, ln_v_b, w_s, b_s, w_out_a,
        w_in_b, a_re, a_im, log_dt, b_re, b_im, c_re, c_im, d_skip,
        w_glu1, b_glu1, w_glu2, b_glu2, w_out_b)
    y_sample, chunk_v_sample, ssm_re_sample, ssm_im_sample = _trunk(
        x_sample, state_ssm_re, state_ssm_im, norm_pre, norm_post,
        w_in_a, ln_v_g, ln_v_b, w_s, b_s, w_out_a,
        w_in_b, a_re, a_im, log_dt, b_re, b_im, c_re, c_im, d_skip,
        w_glu1, b_glu1, w_glu2, b_glu2, w_out_b)
    return (y_prompt, y_sample, chunk_v_sample, ssm_re_prompt, ssm_im_prompt, ssm_re_sample, ssm_im_sample)
```

```python
import functools

import jax
import jax.numpy as jnp
from jax import lax
from jax.experimental import pallas as pl
from jax.experimental.pallas import tpu as pltpu

EPS = 1e-6
CHUNK = 128
N_HEADS = 8
SSM_GROUP = 16
STATE_P = 64

V7X_LANES = 128
V7X_SUBLANES = 8
V7X_MXU_DIM = 256
V7X_VMEM_BYTES = 64 * 1024 * 1024

ROW_TILE = 512
SCAN_STEPS = 32
SCAN_LANES = 512

BF16 = jnp.bfloat16
F32 = jnp.float32


def _dot(a, b):
    return jnp.dot(a, b, preferred_element_type=F32)


def _rmsnorm(x, g):
    ms = jnp.mean(x * x, axis=-1, keepdims=True)
    return x * lax.rsqrt(ms + EPS) * g


def _layernorm(x, g, b):
    mu = jnp.mean(x, axis=-1, keepdims=True)
    xc = x - mu
    var = jnp.mean(xc * xc, axis=-1, keepdims=True)
    return xc * lax.rsqrt(var + EPS) * g + b


def _silu(z):
    return z * jax.nn.sigmoid(z)


def _div_pow2(x, n):
    assert n & (n - 1) == 0
    return lax.shift_right_logical(x, n.bit_length() - 1)


def _vmem_limit(nbytes):
    return int(min(V7X_VMEM_BYTES - (4 << 20), nbytes + (12 << 20)))


def _const_spec(shape):
    nd = len(shape)
    return pl.BlockSpec(shape, lambda *_: (0,) * nd, pipeline_mode=pl.Buffered(1))


def _gmlp_front(x, npre_ref, lng_ref, lnb_ref, win_ref, exp_a):
    hn = _rmsnorm(x, npre_ref[...]).astype(BF16)
    v = _dot(hn, win_ref[:, exp_a:2 * exp_a])
    v = _layernorm(v, lng_ref[...], lnb_ref[...])
    return hn, v


def _gmlp_prompt_kernel(x_ref, npre_ref, npost_ref, lng_ref, lnb_ref, win_ref, wout_ref,
                        ws_ref, bst_ref, o_ref, gated_ref):
    rows = x_ref.shape[0]
    exp_a = wout_ref.shape[0]
    hd = exp_a // N_HEADS
    x = x_ref[...]
    hn, v = _gmlp_front(x, npre_ref, lng_ref, lnb_ref, win_ref, exp_a)
    vb = v.astype(BF16)
    causal = (lax.broadcasted_iota(jnp.int32, (CHUNK, CHUNK), 1)
              <= lax.broadcasted_iota(jnp.int32, (CHUNK, CHUNK), 0))
    for h in range(N_HEADS):
        lo = h * hd
        wsh = jnp.where(causal, ws_ref[h], 0.0).astype(BF16)
        bias = bst_ref[:, h:h + 1]
        u = _dot(hn, win_ref[:, lo:lo + hd])
        z = _dot(hn, win_ref[:, 2 * exp_a + lo:2 * exp_a + lo + hd])
        s = jnp.concatenate(
            [_dot(wsh, vb[c * CHUNK:(c + 1) * CHUNK, lo:lo + hd]) + bias
             for c in range(rows // CHUNK)], axis=0)
        gated_ref[:, lo:lo + hd] = (u * s * _silu(z)).astype(BF16)
    out = _dot(gated_ref[...], wout_ref[...])
    o_ref[...] = x + _rmsnorm(out, npost_ref[...])


def _gmlp_sample_kernel(ws_ref, bs_ref, x_ref, npre_ref, npost_ref, lng_ref, lnb_ref, win_ref,
                        wout_ref, o_ref, v_ref, gated_ref, *, n_seq, seq_len, w_stride):
    exp_a = wout_ref.shape[0]
    hd = exp_a // N_HEADS
    x = x_ref[...]
    hn, v = _gmlp_front(x, npre_ref, lng_ref, lnb_ref, win_ref, exp_a)
    v_ref[...] = v
    for h in range(N_HEADS):
        lo = h * hd
        u = _dot(hn, win_ref[:, lo:lo + hd])
        z = _dot(hn, win_ref[:, 2 * exp_a + lo:2 * exp_a + lo + hd])
        vt = [v[t * n_seq:(t + 1) * n_seq, lo:lo + hd] for t in range(seq_len)]
        parts = []
        for t in range(seq_len):
            s = ws_ref[(h * w_stride + t) * w_stride] * vt[0]
            for t2 in range(1, t + 1):
                s = s + ws_ref[(h * w_stride + t) * w_stride + t2] * vt[t2]
            parts.append(s + bs_ref[h * w_stride + t])
        s = jnp.concatenate(parts, axis=0)
        gated_ref[:, lo:lo + hd] = (u * s * _silu(z)).astype(BF16)
    out = _dot(gated_ref[...], wout_ref[...])
    o_ref[...] = x + _rmsnorm(out, npost_ref[...])


def _gmlp_prompt(x2d, in_map, n_batch, seq_len, npre, npost, lng, lnb, win, wout, ws, bst):
    d = npre.shape[-1]
    exp_a = wout.shape[0]
    n_tiles = seq_len // ROW_TILE
    footprint = (win.size + wout.size) * 2 + 4 * ROW_TILE * d * 4 + ROW_TILE * exp_a * (2 + 4 + 2 + 8)
    return pl.pallas_call(
        _gmlp_prompt_kernel,
        grid=(n_batch, n_tiles),
        in_specs=[
            pl.BlockSpec((ROW_TILE, d), in_map),
            _const_spec(npre.shape), _const_spec(npost.shape),
            _const_spec(lng.shape), _const_spec(lnb.shape),
            _const_spec(win.shape), _const_spec(wout.shape),
            _const_spec(ws.shape), _const_spec(bst.shape),
        ],
        out_specs=pl.BlockSpec((ROW_TILE, d), lambda b, ti: (ti, b)),
        out_shape=jax.ShapeDtypeStruct((seq_len, n_batch * d), F32),
        scratch_shapes=[pltpu.VMEM((ROW_TILE, exp_a), BF16)],
        compiler_params=pltpu.CompilerParams(
            dimension_semantics=("parallel", "parallel"),
            vmem_limit_bytes=_vmem_limit(footprint)),
        name="gmlp_prompt",
    )(x2d, npre, npost, lng, lnb, win, wout, ws, bst)


def _gmlp_sample(xs, n_seq, seq_len, npre, npost, lng, lnb, win, wout, ws_flat, bs_flat):
    rows, d = xs.shape
    exp_a = wout.shape[0]
    footprint = (win.size + wout.size) * 2 + 2 * rows * d * 4 + rows * exp_a * (2 + 4 + 4 + 8)
    vm = pl.BlockSpec(memory_space=pltpu.VMEM)
    sm = pl.BlockSpec(memory_space=pltpu.SMEM)
    return pl.pallas_call(
        functools.partial(_gmlp_sample_kernel, n_seq=n_seq, seq_len=seq_len, w_stride=seq_len),
        in_specs=[sm, sm, vm, vm, vm, vm, vm, vm, vm],
        out_specs=[vm, vm],
        out_shape=[jax.ShapeDtypeStruct((rows, d), F32),
                   jax.ShapeDtypeStruct((rows, exp_a), F32)],
        scratch_shapes=[pltpu.VMEM((rows, exp_a), BF16)],
        compiler_params=pltpu.CompilerParams(vmem_limit_bytes=_vmem_limit(footprint)),
        name="gmlp_sample",
    )(ws_flat, bs_flat, xs, npre, npost, lng, lnb, win, wout)


def _s5_prep_kernel(are_ref, aim_ref, ldt_ref, btr_ref, bti_ref, ctr_ref, cti_ref,
                    abr_ref, abi_ref, wbr_ref, wbi_ref, wcr_ref, wci_ref):
    dt = jnp.exp(ldt_ref[...])
    ar = are_ref[...]
    ai = aim_ref[...]
    mag = jnp.exp(dt * ar)
    ang = dt * ai
    abr = mag * jnp.cos(ang)
    abi = mag * jnp.sin(ang)
    abr_ref[...] = abr
    abi_ref[...] = abi
    nr = abr - 1.0
    ni = abi
    den = ar * ar + ai * ai
    cre = (nr * ar + ni * ai) / den
    cim = (ni * ar - nr * ai) / den
    btr = btr_ref[...]
    bti = bti_ref[...]
    bbr = cre * btr - cim * bti
    bbi = cre * bti + cim * btr

    n_q, k_rows, n_cols = wbr_ref.shape
    in_groups = k_rows // SSM_GROUP
    out_groups = n_cols // STATE_P
    rg = _div_pow2(lax.broadcasted_iota(jnp.int32, (k_rows, n_cols), 0), SSM_GROUP)
    cg = _div_pow2(lax.broadcasted_iota(jnp.int32, (k_rows, n_cols), 1), STATE_P)
    for q in range(n_q):
        same = (rg + in_groups * ((q * out_groups) // in_groups)) == (cg + out_groups * q)
        for src, dst in ((bbr, wbr_ref), (bbi, wbi_ref)):
            blk = jnp.concatenate([src[:, q * n_cols:(q + 1) * n_cols]] * in_groups, axis=0)
            dst[q] = jnp.where(same, blk, 0.0).astype(BF16)

    n_n, k_lanes, n_out = wcr_ref.shape
    groups = n_out // SSM_GROUP
    same_t = (_div_pow2(lax.broadcasted_iota(jnp.int32, (n_out, k_lanes), 0), SSM_GROUP)
              == _div_pow2(lax.broadcasted_iota(jnp.int32, (n_out, k_lanes), 1), STATE_P))
    for n in range(n_n):
        for src, dst, sign in ((ctr_ref, wcr_ref, 1.0), (cti_ref, wci_ref, -1.0)):
            blk = jnp.concatenate([src[:, n * k_lanes:(n + 1) * k_lanes]] * groups, axis=0)
            dst[n] = (jnp.where(same_t, blk, 0.0) * sign).T.astype(BF16)


def _s5_prep(are, aim, ldt, btr, bti, ctr, cti):
    n_state = are.shape[-1]
    n_ch = n_state // STATE_P * SSM_GROUP
    n_q = n_state // V7X_MXU_DIM
    n_n = n_ch // V7X_MXU_DIM
    k_lanes = n_state // n_n
    vm = pl.BlockSpec(memory_space=pltpu.VMEM)
    return pl.pallas_call(
        _s5_prep_kernel,
        in_specs=[vm] * 7,
        out_specs=[vm] * 6,
        out_shape=[jax.ShapeDtypeStruct((1, n_state), F32),
                   jax.ShapeDtypeStruct((1, n_state), F32),
                   jax.ShapeDtypeStruct((n_q, V7X_LANES, V7X_MXU_DIM), BF16),
                   jax.ShapeDtypeStruct((n_q, V7X_LANES, V7X_MXU_DIM), BF16),
                   jax.ShapeDtypeStruct((n_n, k_lanes, V7X_MXU_DIM), BF16),
                   jax.ShapeDtypeStruct((n_n, k_lanes, V7X_MXU_DIM), BF16)],
        compiler_params=pltpu.CompilerParams(vmem_limit_bytes=_vmem_limit(24 << 20)),
        name="s5_prep",
    )(are, aim, ldt, btr, bti, ctr, cti)


def _s5_in_kernel(x_ref, npre_ref, win_ref, xb_ref, z_ref):
    d_in = xb_ref.shape[-1]
    hn = _rmsnorm(x_ref[...], npre_ref[...]).astype(BF16)
    xb_ref[...] = _dot(hn, win_ref[:, :d_in])
    z_ref[...] = _dot(hn, win_ref[:, d_in:])


def _s5_in_prompt(xt, n_batch, npre, win):
    seq_len = xt.shape[0]
    d = npre.shape[-1]
    exp_b = win.shape[1] // 2
    spec = pl.BlockSpec((ROW_TILE, d), lambda b, ti: (ti, b))
    ospec = pl.BlockSpec((ROW_TILE, exp_b), lambda b, ti: (ti, b))
    footprint = win.size * 2 + 6 * ROW_TILE * d * 4 + 4 * ROW_TILE * exp_b * 4
    return pl.pallas_call(
        _s5_in_kernel,
        grid=(n_batch, seq_len // ROW_TILE),
        in_specs=[spec, _const_spec(npre.shape), _const_spec(win.shape)],
        out_specs=[ospec, ospec],
        out_shape=[jax.ShapeDtypeStruct((seq_len, n_batch * exp_b), F32)] * 2,
        compiler_params=pltpu.CompilerParams(
            dimension_semantics=("parallel", "parallel"),
            vmem_limit_bytes=_vmem_limit(footprint)),
        name="s5_in_prompt",
    )(xt, npre, win)


def _s5_in_sample(xs, npre, win):
    rows = xs.shape[0]
    exp_b = win.shape[1] // 2
    vm = pl.BlockSpec(memory_space=pltpu.VMEM)
    return pl.pallas_call(
        _s5_in_kernel,
        in_specs=[vm, vm, vm],
        out_specs=[vm, vm],
        out_shape=[jax.ShapeDtypeStruct((rows, exp_b), F32)] * 2,
        name="s5_in_sample",
    )(xs, npre, win)


def _s5_core_kernel(*refs, n_seq, n_steps, has_h0):
    if has_h0:
        (xb_ref, h0r_ref, h0i_ref, abr_ref, abi_ref, wbr_ref, wbi_ref, wcr_ref, wci_ref, dsk_ref,
         y_ref, str_ref, sti_ref, bur_ref, bui_ref, hr_ref, hi_ref) = refs
    else:
        (xb_ref, abr_ref, abi_ref, wbr_ref, wbi_ref, wcr_ref, wci_ref, dsk_ref,
         y_ref, str_ref, sti_ref, bur_ref, bui_ref, hr_ref, hi_ref) = refs
    step = pl.program_id(0)

    @pl.when(step == 0)
    def _():
        if has_h0:
            hr_ref[...] = h0r_ref[...]
            hi_ref[...] = h0i_ref[...]
        else:
            hr_ref[...] = jnp.zeros_like(hr_ref)
            hi_ref[...] = jnp.zeros_like(hi_ref)

    xb = xb_ref[...]
    xb16 = xb.astype(BF16)
    n_q, k_rows, n_cols = wbr_ref.shape
    for q in range(n_q):
        k0 = (q * n_cols // STATE_P * SSM_GROUP) // k_rows * k_rows
        lhs = xb16[:, k0:k0 + k_rows]
        bur_ref[:, q * n_cols:(q + 1) * n_cols] = _dot(lhs, wbr_ref[q])
        bui_ref[:, q * n_cols:(q + 1) * n_cols] = _dot(lhs, wbi_ref[q])

    n_state = hr_ref.shape[-1]
    for lc in range(n_state // SCAN_LANES):
        lanes = pl.ds(lc * SCAN_LANES, SCAN_LANES)
        ar = jnp.broadcast_to(abr_ref[:, lanes], (V7X_SUBLANES, SCAN_LANES))
        ai = jnp.broadcast_to(abi_ref[:, lanes], (V7X_SUBLANES, SCAN_LANES))
        for rc in range(n_seq // V7X_SUBLANES):
            srows = pl.ds(rc * V7X_SUBLANES, V7X_SUBLANES)

            def body(t, carry, rc=rc, lanes=lanes, ar=ar, ai=ai):
                hr, hi = carry
                r0 = pl.multiple_of(t * n_seq + rc * V7X_SUBLANES, V7X_SUBLANES)
                rows = pl.ds(r0, V7X_SUBLANES)
                nhr = ar * hr - ai * hi + bur_ref[rows, lanes]
                nhi = ar * hi + ai * hr + bui_ref[rows, lanes]
                bur_ref[rows, lanes] = nhr
                bui_ref[rows, lanes] = nhi
                return nhr, nhi

            hr, hi = lax.fori_loop(0, n_steps, body, (hr_ref[srows, lanes], hi_ref[srows, lanes]),
                                   unroll=min(n_steps, 8))
            hr_ref[srows, lanes] = hr
            hi_ref[srows, lanes] = hi

    n_n, k_lanes, n_out = wcr_ref.shape
    for n in range(n_n):
        kl = pl.ds(n * k_lanes, k_lanes)
        cols = pl.ds(n * n_out, n_out)
        y = (_dot(bur_ref[:, kl].astype(BF16), wcr_ref[n])
             + _dot(bui_ref[:, kl].astype(BF16), wci_ref[n]))
        y = y + dsk_ref[:, cols] * xb[:, n * n_out:(n + 1) * n_out]
        y_ref[:, cols] = jax.nn.gelu(y).astype(BF16)

    @pl.when(step == pl.num_programs(0) - 1)
    def _():
        str_ref[...] = hr_ref[...]
        sti_ref[...] = hi_ref[...]


def _s5_core(xb, h0, abr, abi, wbr, wbi, wcr, wci, dsk, n_seq, n_steps):
    rows, exp_b = xb.shape
    n_state = abr.shape[-1]
    tile = n_seq * n_steps
    has_h0 = h0 is not None
    consts = [abr, abi, wbr, wbi, wcr, wci, dsk]
    args = [xb] + (list(h0) if has_h0 else []) + consts
    in_specs = [pl.BlockSpec((tile, exp_b), lambda i: (i, 0))]
    if has_h0:
        in_specs += [_const_spec(h0[0].shape)] * 2
    in_specs += [_const_spec(c.shape) for c in consts]
    st_shape = jax.ShapeDtypeStruct((n_seq, n_state), F32)
    footprint = (sum(c.size * c.dtype.itemsize for c in consts) + 2 * tile * exp_b * (4 + 2)
                 + 2 * tile * n_state * 4 + 8 * n_seq * n_state * 4 + 2 * tile * n_state * 2)
    return pl.pallas_call(
        functools.partial(_s5_core_kernel, n_seq=n_seq, n_steps=n_steps, has_h0=has_h0),
        grid=(rows // tile,),
        in_specs=in_specs,
        out_specs=[pl.BlockSpec((tile, exp_b), lambda i: (i, 0)),
                   pl.BlockSpec((n_seq, n_state), lambda i: (0, 0)),
                   pl.BlockSpec((n_seq, n_state), lambda i: (0, 0))],
        out_shape=[jax.ShapeDtypeStruct((rows, exp_b), BF16), st_shape, st_shape],
        scratch_shapes=[pltpu.VMEM((tile, n_state), F32), pltpu.VMEM((tile, n_state), F32),
                        pltpu.VMEM((n_seq, n_state), F32), pltpu.VMEM((n_seq, n_state), F32)],
        compiler_params=pltpu.CompilerParams(
            dimension_semantics=("arbitrary",),
            vmem_limit_bytes=_vmem_limit(footprint)),
        name="s5_core_sample" if has_h0 else "s5_core_prompt",
    )(*args)


def _s5_out_kernel(y_ref, z_ref, x_ref, w1_ref, b1_ref, w2_ref, b2_ref, wout_ref, npost_ref, o_ref):
    y = y_ref[...]
    g = (_dot(y, w1_ref[...]) + b1_ref[...]) * jax.nn.sigmoid(_dot(y, w2_ref[...]) + b2_ref[...])
    out = _dot((g * _silu(z_ref[...])).astype(BF16), wout_ref[...])
    o_ref[...] = x_ref[...] + _rmsnorm(out, npost_ref[...])


def _s5_out_prompt(y, z, xt, n_batch, consts, time_major_out):
    seq_len = xt.shape[0]
    d = xt.shape[1] // n_batch
    exp_b = z.shape[1] // n_batch
    n_tiles = seq_len // ROW_TILE
    tb = lambda b, ti: (ti, b)
    if time_major_out:
        out_shape = jax.ShapeDtypeStruct((seq_len, n_batch * d), F32)
        out_map = tb
    else:
        out_shape = jax.ShapeDtypeStruct((n_batch * seq_len, d), F32)
        out_map = lambda b, ti: (b * n_tiles + ti, 0)
    footprint = (sum(c.size * c.dtype.itemsize for c in consts)
                 + 2 * ROW_TILE * (exp_b * 6 + d * 8) + 6 * ROW_TILE * exp_b * 4)
    return pl.pallas_call(
        _s5_out_kernel,
        grid=(n_batch, n_tiles),
        in_specs=[pl.BlockSpec((ROW_TILE, exp_b), tb), pl.BlockSpec((ROW_TILE, exp_b), tb),
                  pl.BlockSpec((ROW_TILE, d), tb)] + [_const_spec(c.shape) for c in consts],
        out_specs=pl.BlockSpec((ROW_TILE, d), out_map),
        out_shape=out_shape,
        compiler_params=pltpu.CompilerParams(
            dimension_semantics=("parallel", "parallel"),
            vmem_limit_bytes=_vmem_limit(footprint)),
        name="s5_out_prompt",
    )(y, z, xt, *consts)


def _s5_out_sample(y, z, xs, consts):
    vm = pl.BlockSpec(memory_space=pltpu.VMEM)
    return pl.pallas_call(
        _s5_out_kernel,
        in_specs=[vm] * (3 + len(consts)),
        out_specs=vm,
        out_shape=jax.ShapeDtypeStruct(xs.shape, F32),
        name="s5_out_sample",
    )(y, z, xs, *consts)


def kernel(x_prompt, x_sample, state_ssm_re, state_ssm_im, norm_pre, norm_post,
           w_in_a, ln_v_g, ln_v_b, w_s, b_s, w_out_a,
           w_in_b, a_re, a_im, log_dt, b_re, b_im, c_re, c_im, d_skip,
           w_glu1, b_glu1, w_glu2, b_glu2, w_out_b):
    n_batch, seq_len, d = x_prompt.shape
    n_dec, dec_len, _ = x_sample.shape
    depth = norm_pre.shape[0]
    n_groups, state_p = a_re.shape[1:]
    n_state = n_groups * state_p
    assert seq_len % ROW_TILE == 0 and ROW_TILE % CHUNK == 0 and seq_len % SCAN_STEPS == 0
    assert n_batch == V7X_SUBLANES and n_dec % V7X_SUBLANES == 0 and dec_len <= CHUNK
    assert state_p == STATE_P and b_re.shape[-1] == SSM_GROUP and w_s.shape[1] == N_HEADS

    row = lambda a: a.reshape(1, -1)
    n_tiles = seq_len // ROW_TILE

    xp = x_prompt.reshape(n_batch * seq_len, d)
    xp_map = lambda b, ti: (b * n_tiles + ti, 0)
    xs = jnp.transpose(x_sample, (1, 0, 2)).reshape(dec_len * n_dec, d)

    v_rows, st_p_re, st_p_im, st_s_re, st_s_im = [], [], [], [], []
    for i in range(depth):
        j = i // 2
        npre, npost = row(norm_pre[i]), row(norm_post[i])
        if i % 2 == 0:
            win, wout = w_in_a[j].astype(BF16), w_out_a[j].astype(BF16)
            lng, lnb = row(ln_v_g[j]), row(ln_v_b[j])
            xp = _gmlp_prompt(xp, xp_map, n_batch, seq_len, npre, npost, lng, lnb, win, wout,
                              w_s[j], jnp.transpose(b_s[j]))
            xp_map = lambda b, ti: (ti, b)
            xs, v = _gmlp_sample(xs, n_dec, dec_len, npre, npost, lng, lnb, win, wout,
                                 w_s[j, :, :dec_len, :dec_len].reshape(-1),
                                 b_s[j, :, :dec_len].reshape(-1))
            v_rows.append(jnp.transpose(v.reshape(dec_len, n_dec, -1), (1, 0, 2)))
        else:
            win = w_in_b[j].astype(BF16)
            exp_b = win.shape[1] // 2
            to_lanes = lambda a: jnp.transpose(a, (1, 0, 2)).reshape(SSM_GROUP, n_state)
            abr, abi, wbr, wbi, wcr, wci = _s5_prep(
                row(a_re[j]), row(a_im[j]), row(jnp.repeat(log_dt[j], state_p)),
                to_lanes(jnp.swapaxes(b_re[j], 1, 2)), to_lanes(jnp.swapaxes(b_im[j], 1, 2)),
                to_lanes(c_re[j]), to_lanes(c_im[j]))
            dsk = row(d_skip[j])
            out_consts = [w_glu1[j].astype(BF16), row(b_glu1[j]), w_glu2[j].astype(BF16),
                          row(b_glu2[j]), w_out_b[j].astype(BF16), npost]

            xb, z = _s5_in_prompt(xp, n_batch, npre, win)
            y, sr, si = _s5_core(xb.reshape(seq_len * n_batch, exp_b), None,
                                 abr, abi, wbr, wbi, wcr, wci, dsk, n_batch, SCAN_STEPS)
            xp = _s5_out_prompt(y.reshape(seq_len, n_batch * exp_b), z, xp, n_batch, out_consts,
                                time_major_out=(i != depth - 1))
            st_p_re.append(sr.reshape(n_batch, n_groups, state_p))
            st_p_im.append(si.reshape(n_batch, n_groups, state_p))

            xbs, zs = _s5_in_sample(xs, npre, win)
            h0 = (state_ssm_re[j].reshape(n_dec, n_state), state_ssm_im[j].reshape(n_dec, n_state))
            ys, sr, si = _s5_core(xbs, h0, abr, abi, wbr, wbi, wcr, wci, dsk, n_dec, dec_len)
            xs = _s5_out_sample(ys, zs, xs, out_consts)
            st_s_re.append(sr.reshape(n_dec, n_groups, state_p))
            st_s_im.append(si.reshape(n_dec, n_groups, state_p))

    if depth % 2 == 0:
        y_prompt = xp.reshape(n_batch, seq_len, d)
    else:
        y_prompt = jnp.transpose(xp.reshape(seq_len, n_batch, d), (1, 0, 2))
    y_sample = jnp.transpose(xs.reshape(dec_len, n_dec, d), (1, 0, 2))
    return (y_prompt, y_sample, jnp.stack(v_rows), jnp.stack(st_p_re), jnp.stack(st_p_im),
            jnp.stack(st_s_re), jnp.stack(st_s_im))
```

```python
import functools

import jax
import jax.numpy as jnp
from jax import lax
from jax.experimental import pallas as pl
from jax.experimental.pallas import tpu as pltpu

EPS = 1e-6
CHUNK = 128
N_HEADS = 8
SSM_GROUP = 16
STATE_P = 64

V7X_LANES = 128
V7X_SUBLANES = 8
V7X_MXU_DIM = 256
V7X_VMEM_BYTES = 64 * 1024 * 1024

ROW_TILE = 512
SCAN_STEPS = 32
SCAN_LANES = 512

BF16 = jnp.bfloat16
F32 = jnp.float32


def _dot(a, b):
    return jnp.dot(a, b, preferred_element_type=F32)


def _rmsnorm(x, g):
    ms = jnp.mean(x * x, axis=-1, keepdims=True)
    return x * lax.rsqrt(ms + EPS) * g


def _layernorm(x, g, b):
    mu = jnp.mean(x, axis=-1, keepdims=True)
    xc = x - mu
    var = jnp.mean(xc * xc, axis=-1, keepdims=True)
    return xc * lax.rsqrt(var + EPS) * g + b


def _silu(z):
    return z * jax.nn.sigmoid(z)


def _div_pow2(x, n):
    assert n & (n - 1) == 0
    return lax.shift_right_logical(x, n.bit_length() - 1)


def _vmem_limit(nbytes):
    return int(min(V7X_VMEM_BYTES - (4 << 20), nbytes + (12 << 20)))


def _const_spec(shape):
    nd = len(shape)
    return pl.BlockSpec(shape, lambda *_: (0,) * nd, pipeline_mode=pl.Buffered(1))


def _gmlp_front(x, npre_ref, lng_ref, lnb_ref, win_ref, exp_a):
    hn = _rmsnorm(x, npre_ref[...]).astype(BF16)
    v = _dot(hn, win_ref[:, exp_a:2 * exp_a])
    v = _layernorm(v, lng_ref[...], lnb_ref[...])
    return hn, v


def _gmlp_prompt_kernel(x_ref, npre_ref, npost_ref, lng_ref, lnb_ref, win_ref, wout_ref,
                        ws_ref, bst_ref, o_ref, gated_ref):
    rows = x_ref.shape[0]
    exp_a = wout_ref.shape[0]
    hd = exp_a // N_HEADS
    x = x_ref[...]
    hn, v = _gmlp_front(x, npre_ref, lng_ref, lnb_ref, win_ref, exp_a)
    vb = v.astype(BF16)
    causal = (lax.broadcasted_iota(jnp.int32, (CHUNK, CHUNK), 1)
              <= lax.broadcasted_iota(jnp.int32, (CHUNK, CHUNK), 0))
    for h in range(N_HEADS):
        lo = h * hd
        wsh = jnp.where(causal, ws_ref[h], 0.0).astype(BF16)
        bias = bst_ref[:, h:h + 1]
        u = _dot(hn, win_ref[:, lo:lo + hd])
        z = _dot(hn, win_ref[:, 2 * exp_a + lo:2 * exp_a + lo + hd])
        s = jnp.concatenate(
            [_dot(wsh, vb[c * CHUNK:(c + 1) * CHUNK, lo:lo + hd]) + bias
             for c in range(rows // CHUNK)], axis=0)
        gated_ref[:, lo:lo + hd] = (u * s * _silu(z)).astype(BF16)
    out = _dot(gated_ref[...], wout_ref[...])
    o_ref[...] = x + _rmsnorm(out, npost_ref[...])


def _gmlp_sample_kernel(ws_ref, bs_ref, x_ref, npre_ref, npost_ref, lng_ref, lnb_ref, win_ref,
                        wout_ref, o_ref, v_ref, gated_ref, *, n_seq, seq_len, w_stride):
    exp_a = wout_ref.shape[0]
    hd = exp_a // N_HEADS
    x = x_ref[...]
    hn, v = _gmlp_front(x, npre_ref, lng_ref, lnb_ref, win_ref, exp_a)
    v_ref[...] = v
    for h in range(N_HEADS):
        lo = h * hd
        u = _dot(hn, win_ref[:, lo:lo + hd])
        z = _dot(hn, win_ref[:, 2 * exp_a + lo:2 * exp_a + lo + hd])
        vt = [v[t * n_seq:(t + 1) * n_seq, lo:lo + hd] for t in range(seq_len)]
        parts = []
        for t in range(seq_len):
            s = ws_ref[(h * w_stride + t) * w_stride] * vt[0]
            for t2 in range(1, t + 1):
                s = s + ws_ref[(h * w_stride + t) * w_stride + t2] * vt[t2]
            parts.append(s + bs_ref[h * w_stride + t])
        s = jnp.concatenate(parts, axis=0)
        gated_ref[:, lo:lo + hd] = (u * s * _silu(z)).astype(BF16)
    out = _dot(gated_ref[...], wout_ref[...])
    o_ref[...] = x + _rmsnorm(out, npost_ref[...])


def _row_spec(width):
    return pl.BlockSpec((ROW_TILE, width), lambda i: (i, 0))


def _gmlp_prompt(x2d, npre, npost, lng, lnb, win, wout, ws, bst):
    rows, d = x2d.shape
    exp_a = wout.shape[0]
    footprint = (win.size + wout.size) * 2 + 4 * ROW_TILE * d * 4 + ROW_TILE * exp_a * (2 + 4 + 2 + 8)
    return pl.pallas_call(
        _gmlp_prompt_kernel,
        grid=(rows // ROW_TILE,),
        in_specs=[
            _row_spec(d),
            _const_spec(npre.shape), _const_spec(npost.shape),
            _const_spec(lng.shape), _const_spec(lnb.shape),
            _const_spec(win.shape), _const_spec(wout.shape),
            _const_spec(ws.shape), _const_spec(bst.shape),
        ],
        out_specs=_row_spec(d),
        out_shape=jax.ShapeDtypeStruct((rows, d), F32),
        scratch_shapes=[pltpu.VMEM((ROW_TILE, exp_a), BF16)],
        compiler_params=pltpu.CompilerParams(
            dimension_semantics=("parallel",),
            vmem_limit_bytes=_vmem_limit(footprint)),
        name="gmlp_prompt",
    )(x2d, npre, npost, lng, lnb, win, wout, ws, bst)


def _gmlp_sample(xs, n_seq, seq_len, npre, npost, lng, lnb, win, wout, ws_flat, bs_flat):
    rows, d = xs.shape
    exp_a = wout.shape[0]
    footprint = (win.size + wout.size) * 2 + 2 * rows * d * 4 + rows * exp_a * (2 + 4 + 4 + 8)
    vm = pl.BlockSpec(memory_space=pltpu.VMEM)
    sm = pl.BlockSpec(memory_space=pltpu.SMEM)
    return pl.pallas_call(
        functools.partial(_gmlp_sample_kernel, n_seq=n_seq, seq_len=seq_len, w_stride=seq_len),
        in_specs=[sm, sm, vm, vm, vm, vm, vm, vm, vm],
        out_specs=[vm, vm],
        out_shape=[jax.ShapeDtypeStruct((rows, d), F32),
                   jax.ShapeDtypeStruct((rows, exp_a), F32)],
        scratch_shapes=[pltpu.VMEM((rows, exp_a), BF16)],
        compiler_params=pltpu.CompilerParams(vmem_limit_bytes=_vmem_limit(footprint)),
        name="gmlp_sample",
    )(ws_flat, bs_flat, xs, npre, npost, lng, lnb, win, wout)


def _s5_prep_kernel(are_ref, aim_ref, ldt_ref, btr_ref, bti_ref, ctr_ref, cti_ref,
                    abr_ref, abi_ref, wbr_ref, wbi_ref, wcr_ref, wci_ref):
    dt = jnp.exp(ldt_ref[...])
    ar = are_ref[...]
    ai = aim_ref[...]
    mag = jnp.exp(dt * ar)
    ang = dt * ai
    abr = mag * jnp.cos(ang)
    abi = mag * jnp.sin(ang)
    abr_ref[...] = abr
    abi_ref[...] = abi
    nr = abr - 1.0
    ni = abi
    den = ar * ar + ai * ai
    cre = (nr * ar + ni * ai) / den
    cim = (ni * ar - nr * ai) / den
    btr = btr_ref[...]
    bti = bti_ref[...]
    bbr = cre * btr - cim * bti
    bbi = cre * bti + cim * btr

    n_q, k_rows, n_cols = wbr_ref.shape
    in_groups = k_rows // SSM_GROUP
    out_groups = n_cols // STATE_P
    rg = _div_pow2(lax.broadcasted_iota(jnp.int32, (k_rows, n_cols), 0), SSM_GROUP)
    cg = _div_pow2(lax.broadcasted_iota(jnp.int32, (k_rows, n_cols), 1), STATE_P)
    for q in range(n_q):
        same = (rg + in_groups * ((q * out_groups) // in_groups)) == (cg + out_groups * q)
        for src, dst in ((bbr, wbr_ref), (bbi, wbi_ref)):
            blk = jnp.concatenate([src[:, q * n_cols:(q + 1) * n_cols]] * in_groups, axis=0)
            dst[q] = jnp.where(same, blk, 0.0).astype(BF16)

    n_n, k_lanes, n_out = wcr_ref.shape
    groups = n_out // SSM_GROUP
    same_t = (_div_pow2(lax.broadcasted_iota(jnp.int32, (n_out, k_lanes), 0), SSM_GROUP)
              == _div_pow2(lax.broadcasted_iota(jnp.int32, (n_out, k_lanes), 1), STATE_P))
    for n in range(n_n):
        for src, dst, sign in ((ctr_ref, wcr_ref, 1.0), (cti_ref, wci_ref, -1.0)):
            blk = jnp.concatenate([src[:, n * k_lanes:(n + 1) * k_lanes]] * groups, axis=0)
            dst[n] = (jnp.where(same_t, blk, 0.0) * sign).T.astype(BF16)


def _s5_prep(are, aim, ldt, btr, bti, ctr, cti):
    n_state = are.shape[-1]
    n_ch = n_state // STATE_P * SSM_GROUP
    n_q = n_state // V7X_MXU_DIM
    n_n = n_ch // V7X_MXU_DIM
    k_lanes = n_state // n_n
    vm = pl.BlockSpec(memory_space=pltpu.VMEM)
    return pl.pallas_call(
        _s5_prep_kernel,
        in_specs=[vm] * 7,
        out_specs=[vm] * 6,
        out_shape=[jax.ShapeDtypeStruct((1, n_state), F32),
                   jax.ShapeDtypeStruct((1, n_state), F32),
                   jax.ShapeDtypeStruct((n_q, V7X_LANES, V7X_MXU_DIM), BF16),
                   jax.ShapeDtypeStruct((n_q, V7X_LANES, V7X_MXU_DIM), BF16),
                   jax.ShapeDtypeStruct((n_n, k_lanes, V7X_MXU_DIM), BF16),
                   jax.ShapeDtypeStruct((n_n, k_lanes, V7X_MXU_DIM), BF16)],
        compiler_params=pltpu.CompilerParams(vmem_limit_bytes=_vmem_limit(24 << 20)),
        name="s5_prep",
    )(are, aim, ldt, btr, bti, ctr, cti)


def _s5_in_kernel(x_ref, npre_ref, win_ref, xb_ref, z_ref):
    d_in = xb_ref.shape[-1]
    hn = _rmsnorm(x_ref[...], npre_ref[...]).astype(BF16)
    xb_ref[...] = _dot(hn, win_ref[:, :d_in])
    z_ref[...] = _dot(hn, win_ref[:, d_in:])


def _s5_in_prompt(x2d, npre, win):
    rows, d = x2d.shape
    exp_b = win.shape[1] // 2
    footprint = win.size * 2 + 6 * ROW_TILE * d * 4 + 4 * ROW_TILE * exp_b * 4
    return pl.pallas_call(
        _s5_in_kernel,
        grid=(rows // ROW_TILE,),
        in_specs=[_row_spec(d), _const_spec(npre.shape), _const_spec(win.shape)],
        out_specs=[_row_spec(exp_b), _row_spec(exp_b)],
        out_shape=[jax.ShapeDtypeStruct((rows, exp_b), F32)] * 2,
        compiler_params=pltpu.CompilerParams(
            dimension_semantics=("parallel",),
            vmem_limit_bytes=_vmem_limit(footprint)),
        name="s5_in_prompt",
    )(x2d, npre, win)


def _s5_in_sample(xs, npre, win):
    rows = xs.shape[0]
    exp_b = win.shape[1] // 2
    vm = pl.BlockSpec(memory_space=pltpu.VMEM)
    return pl.pallas_call(
        _s5_in_kernel,
        in_specs=[vm, vm, vm],
        out_specs=[vm, vm],
        out_shape=[jax.ShapeDtypeStruct((rows, exp_b), F32)] * 2,
        name="s5_in_sample",
    )(xs, npre, win)


def _s5_core_kernel(*refs, n_seq, n_steps, has_h0, swap_bt):
    if has_h0:
        (xb_ref, h0r_ref, h0i_ref, abr_ref, abi_ref, wbr_ref, wbi_ref, wcr_ref, wci_ref, dsk_ref,
         y_ref, str_ref, sti_ref, bur_ref, bui_ref, hr_ref, hi_ref) = refs
    else:
        (xb_ref, abr_ref, abi_ref, wbr_ref, wbi_ref, wcr_ref, wci_ref, dsk_ref,
         y_ref, str_ref, sti_ref, bur_ref, bui_ref, hr_ref, hi_ref) = refs
    step = pl.program_id(0)

    @pl.when(step == 0)
    def _():
        if has_h0:
            hr_ref[...] = h0r_ref[...]
            hi_ref[...] = h0i_ref[...]
        else:
            hr_ref[...] = jnp.zeros_like(hr_ref)
            hi_ref[...] = jnp.zeros_like(hi_ref)

    xb = xb_ref[...]
    if swap_bt:
        xb = jnp.swapaxes(xb, 0, 1).reshape(n_steps * n_seq, xb.shape[-1])
    xb16 = xb.astype(BF16)
    n_q, k_rows, n_cols = wbr_ref.shape
    for q in range(n_q):
        k0 = (q * n_cols // STATE_P * SSM_GROUP) // k_rows * k_rows
        lhs = xb16[:, k0:k0 + k_rows]
        bur_ref[:, q * n_cols:(q + 1) * n_cols] = _dot(lhs, wbr_ref[q])
        bui_ref[:, q * n_cols:(q + 1) * n_cols] = _dot(lhs, wbi_ref[q])

    n_state = hr_ref.shape[-1]
    for lc in range(n_state // SCAN_LANES):
        lanes = pl.ds(lc * SCAN_LANES, SCAN_LANES)
        ar = jnp.broadcast_to(abr_ref[:, lanes], (V7X_SUBLANES, SCAN_LANES))
        ai = jnp.broadcast_to(abi_ref[:, lanes], (V7X_SUBLANES, SCAN_LANES))
        for rc in range(n_seq // V7X_SUBLANES):
            srows = pl.ds(rc * V7X_SUBLANES, V7X_SUBLANES)

            def body(t, carry, rc=rc, lanes=lanes, ar=ar, ai=ai):
                hr, hi = carry
                r0 = pl.multiple_of(t * n_seq + rc * V7X_SUBLANES, V7X_SUBLANES)
                rows = pl.ds(r0, V7X_SUBLANES)
                nhr = ar * hr - ai * hi + bur_ref[rows, lanes]
                nhi = ar * hi + ai * hr + bui_ref[rows, lanes]
                bur_ref[rows, lanes] = nhr
                bui_ref[rows, lanes] = nhi
                return nhr, nhi

            hr, hi = lax.fori_loop(0, n_steps, body, (hr_ref[srows, lanes], hi_ref[srows, lanes]),
                                   unroll=min(n_steps, 8))
            hr_ref[srows, lanes] = hr
            hi_ref[srows, lanes] = hi

    n_n, k_lanes, n_out = wcr_ref.shape
    for n in range(n_n):
        kl = pl.ds(n * k_lanes, k_lanes)
        cols = pl.ds(n * n_out, n_out)
        y = (_dot(bur_ref[:, kl].astype(BF16), wcr_ref[n])
             + _dot(bui_ref[:, kl].astype(BF16), wci_ref[n]))
        y = jax.nn.gelu(y + dsk_ref[:, cols] * xb[:, n * n_out:(n + 1) * n_out])
        if swap_bt:
            y = jnp.swapaxes(y.reshape(n_steps, n_seq, n_out), 0, 1)
            y_ref[:, :, cols] = y.astype(BF16)
        else:
            y_ref[:, cols] = y.astype(BF16)

    @pl.when(step == pl.num_programs(0) - 1)
    def _():
        str_ref[...] = hr_ref[...]
        sti_ref[...] = hi_ref[...]


def _s5_core(xb, h0, abr, abi, wbr, wbi, wcr, wci, dsk, n_seq, n_steps):
    swap_bt = xb.ndim == 3
    exp_b = xb.shape[-1]
    n_state = abr.shape[-1]
    tile = n_seq * n_steps
    n_tiles = xb.size // (tile * exp_b)
    has_h0 = h0 is not None
    consts = [abr, abi, wbr, wbi, wcr, wci, dsk]
    args = [xb] + (list(h0) if has_h0 else []) + consts
    if swap_bt:
        io_spec = pl.BlockSpec((n_seq, n_steps, exp_b), lambda i: (0, i, 0))
    else:
        io_spec = pl.BlockSpec((tile, exp_b), lambda i: (i, 0))
    in_specs = [io_spec]
    if has_h0:
        in_specs += [_const_spec(h0[0].shape)] * 2
    in_specs += [_const_spec(c.shape) for c in consts]
    st_shape = jax.ShapeDtypeStruct((n_seq, n_state), F32)
    footprint = (sum(c.size * c.dtype.itemsize for c in consts) + 2 * tile * exp_b * (4 + 2)
                 + 2 * tile * n_state * 4 + 8 * n_seq * n_state * 4 + 2 * tile * n_state * 2)
    return pl.pallas_call(
        functools.partial(_s5_core_kernel, n_seq=n_seq, n_steps=n_steps, has_h0=has_h0,
                          swap_bt=swap_bt),
        grid=(n_tiles,),
        in_specs=in_specs,
        out_specs=[io_spec,
                   pl.BlockSpec((n_seq, n_state), lambda i: (0, 0)),
                   pl.BlockSpec((n_seq, n_state), lambda i: (0, 0))],
        out_shape=[jax.ShapeDtypeStruct(xb.shape, BF16), st_shape, st_shape],
        scratch_shapes=[pltpu.VMEM((tile, n_state), F32), pltpu.VMEM((tile, n_state), F32),
                        pltpu.VMEM((n_seq, n_state), F32), pltpu.VMEM((n_seq, n_state), F32)],
        compiler_params=pltpu.CompilerParams(
            dimension_semantics=("arbitrary",),
            vmem_limit_bytes=_vmem_limit(footprint)),
        name="s5_core_sample" if has_h0 else "s5_core_prompt",
    )(*args)


def _s5_out_kernel(y_ref, z_ref, x_ref, w1_ref, b1_ref, w2_ref, b2_ref, wout_ref, npost_ref, o_ref):
    y = y_ref[...]
    g = (_dot(y, w1_ref[...]) + b1_ref[...]) * jax.nn.sigmoid(_dot(y, w2_ref[...]) + b2_ref[...])
    out = _dot((g * _silu(z_ref[...])).astype(BF16), wout_ref[...])
    o_ref[...] = x_ref[...] + _rmsnorm(out, npost_ref[...])


def _s5_out_prompt(y, z, x2d, consts):
    rows, d = x2d.shape
    exp_b = z.shape[1]
    footprint = (sum(c.size * c.dtype.itemsize for c in consts)
                 + 2 * ROW_TILE * (exp_b * 6 + d * 8) + 6 * ROW_TILE * exp_b * 4)
    return pl.pallas_call(
        _s5_out_kernel,
        grid=(rows // ROW_TILE,),
        in_specs=[_row_spec(exp_b), _row_spec(exp_b), _row_spec(d)]
        + [_const_spec(c.shape) for c in consts],
        out_specs=_row_spec(d),
        out_shape=jax.ShapeDtypeStruct((rows, d), F32),
        compiler_params=pltpu.CompilerParams(
            dimension_semantics=("parallel",),
            vmem_limit_bytes=_vmem_limit(footprint)),
        name="s5_out_prompt",
    )(y, z, x2d, *consts)


def _s5_out_sample(y, z, xs, consts):
    vm = pl.BlockSpec(memory_space=pltpu.VMEM)
    return pl.pallas_call(
        _s5_out_kernel,
        in_specs=[vm] * (3 + len(consts)),
        out_specs=vm,
        out_shape=jax.ShapeDtypeStruct(xs.shape, F32),
        name="s5_out_sample",
    )(y, z, xs, *consts)


def kernel(x_prompt, x_sample, state_ssm_re, state_ssm_im, norm_pre, norm_post,
           w_in_a, ln_v_g, ln_v_b, w_s, b_s, w_out_a,
           w_in_b, a_re, a_im, log_dt, b_re, b_im, c_re, c_im, d_skip,
           w_glu1, b_glu1, w_glu2, b_glu2, w_out_b):
    n_batch, seq_len, d = x_prompt.shape
    n_dec, dec_len, _ = x_sample.shape
    depth = norm_pre.shape[0]
    n_groups, state_p = a_re.shape[1:]
    n_state = n_groups * state_p
    assert seq_len % ROW_TILE == 0 and ROW_TILE % CHUNK == 0 and seq_len % SCAN_STEPS == 0
    assert n_batch == V7X_SUBLANES and n_dec % V7X_SUBLANES == 0 and dec_len <= CHUNK
    assert state_p == STATE_P and b_re.shape[-1] == SSM_GROUP and w_s.shape[1] == N_HEADS

    row = lambda a: a.reshape(1, -1)

    xp = x_prompt.reshape(n_batch * seq_len, d)
    xs = jnp.transpose(x_sample, (1, 0, 2)).reshape(dec_len * n_dec, d)

    v_rows, st_p_re, st_p_im, st_s_re, st_s_im = [], [], [], [], []
    for i in range(depth):
        j = i // 2
        npre, npost = row(norm_pre[i]), row(norm_post[i])
        if i % 2 == 0:
            win, wout = w_in_a[j].astype(BF16), w_out_a[j].astype(BF16)
            lng, lnb = row(ln_v_g[j]), row(ln_v_b[j])
            xp = _gmlp_prompt(xp, npre, npost, lng, lnb, win, wout, w_s[j], jnp.transpose(b_s[j]))
            xs, v = _gmlp_sample(xs, n_dec, dec_len, npre, npost, lng, lnb, win, wout,
                                 w_s[j, :, :dec_len, :dec_len].reshape(-1),
                                 b_s[j, :, :dec_len].reshape(-1))
            v_rows.append(jnp.transpose(v.reshape(dec_len, n_dec, -1), (1, 0, 2)))
        else:
            win = w_in_b[j].astype(BF16)
            exp_b = win.shape[1] // 2
            to_lanes = lambda a: jnp.transpose(a, (1, 0, 2)).reshape(SSM_GROUP, n_state)
            abr, abi, wbr, wbi, wcr, wci = _s5_prep(
                row(a_re[j]), row(a_im[j]), row(jnp.repeat(log_dt[j], state_p)),
                to_lanes(jnp.swapaxes(b_re[j], 1, 2)), to_lanes(jnp.swapaxes(b_im[j], 1, 2)),
                to_lanes(c_re[j]), to_lanes(c_im[j]))
            dsk = row(d_skip[j])
            out_consts = [w_glu1[j].astype(BF16), row(b_glu1[j]), w_glu2[j].astype(BF16),
                          row(b_glu2[j]), w_out_b[j].astype(BF16), npost]

            xb, z = _s5_in_prompt(xp, npre, win)
            y, sr, si = _s5_core(xb.reshape(n_batch, seq_len, exp_b), None,
                                 abr, abi, wbr, wbi, wcr, wci, dsk, n_batch, SCAN_STEPS)
            xp = _s5_out_prompt(y.reshape(n_batch * seq_len, exp_b), z, xp, out_consts)
            st_p_re.append(sr.reshape(n_batch, n_groups, state_p))
            st_p_im.append(si.reshape(n_batch, n_groups, state_p))

            xbs, zs = _s5_in_sample(xs, npre, win)
            h0 = (state_ssm_re[j].reshape(n_dec, n_state), state_ssm_im[j].reshape(n_dec, n_state))
            ys, sr, si = _s5_core(xbs, h0, abr, abi, wbr, wbi, wcr, wci, dsk, n_dec, dec_len)
            xs = _s5_out_sample(ys, zs, xs, out_consts)
            st_s_re.append(sr.reshape(n_dec, n_groups, state_p))
            st_s_im.append(si.reshape(n_dec, n_groups, state_p))

    y_prompt = xp.reshape(n_batch, seq_len, d)
    y_sample = jnp.transpose(xs.reshape(dec_len, n_dec, d), (1, 0, 2))
    return (y_prompt, y_sample, jnp.stack(v_rows), jnp.stack(st_p_re), jnp.stack(st_p_im),
            jnp.stack(st_s_re), jnp.stack(st_s_im))
```

```python
import functools

import jax
import jax.numpy as jnp
from jax import lax
from jax.experimental import pallas as pl
from jax.experimental.pallas import tpu as pltpu

EPS = 1e-6
CHUNK = 128
N_HEADS = 8
SSM_GROUP = 16
STATE_P = 64

V7X_LANES = 128
V7X_SUBLANES = 8
V7X_MXU_DIM = 256
V7X_VMEM_BYTES = 64 * 1024 * 1024

ROW_TILE = 512
SCAN_STEPS = 32

BF16 = jnp.bfloat16
F32 = jnp.float32


def _dot(a, b):
    return jnp.dot(a, b, preferred_element_type=F32)


def _rmsnorm(x, g):
    ms = jnp.mean(x * x, axis=-1, keepdims=True)
    return x * lax.rsqrt(ms + EPS) * g


def _layernorm(x, g, b):
    mu = jnp.mean(x, axis=-1, keepdims=True)
    xc = x - mu
    var = jnp.mean(xc * xc, axis=-1, keepdims=True)
    return xc * lax.rsqrt(var + EPS) * g + b


def _silu(z):
    return z * jax.nn.sigmoid(z)


def _div_pow2(x, n):
    assert n & (n - 1) == 0
    return lax.shift_right_logical(x, n.bit_length() - 1)


def _vmem_limit(nbytes):
    return int(min(V7X_VMEM_BYTES - (4 << 20), nbytes + (12 << 20)))


def _const_spec(shape):
    nd = len(shape)
    return pl.BlockSpec(shape, lambda *_: (0,) * nd, pipeline_mode=pl.Buffered(1))


def _gmlp_front(x, npre_ref, lng_ref, lnb_ref, win_ref, exp_a):
    hn = _rmsnorm(x, npre_ref[...]).astype(BF16)
    v = _dot(hn, win_ref[:, exp_a:2 * exp_a])
    v = _layernorm(v, lng_ref[...], lnb_ref[...])
    return hn, v


def _gmlp_prompt_kernel(x_ref, npre_ref, npost_ref, lng_ref, lnb_ref, win_ref, wout_ref,
                        ws_ref, bst_ref, o_ref, gated_ref):
    rows = x_ref.shape[0]
    exp_a = wout_ref.shape[0]
    hd = exp_a // N_HEADS
    x = x_ref[...]
    hn, v = _gmlp_front(x, npre_ref, lng_ref, lnb_ref, win_ref, exp_a)
    vb = v.astype(BF16)
    causal = (lax.broadcasted_iota(jnp.int32, (CHUNK, CHUNK), 1)
              <= lax.broadcasted_iota(jnp.int32, (CHUNK, CHUNK), 0))
    for h in range(N_HEADS):
        lo = h * hd
        wsh = jnp.where(causal, ws_ref[h], 0.0).astype(BF16)
        bias = bst_ref[:, h:h + 1]
        u = _dot(hn, win_ref[:, lo:lo + hd])
        z = _dot(hn, win_ref[:, 2 * exp_a + lo:2 * exp_a + lo + hd])
        s = jnp.concatenate(
            [_dot(wsh, vb[c * CHUNK:(c + 1) * CHUNK, lo:lo + hd]) + bias
             for c in range(rows // CHUNK)], axis=0)
        gated_ref[:, lo:lo + hd] = (u * s * _silu(z)).astype(BF16)
    out = _dot(gated_ref[...], wout_ref[...])
    o_ref[...] = x + _rmsnorm(out, npost_ref[...])


def _gmlp_sample_kernel(ws_ref, bs_ref, x_ref, npre_ref, npost_ref, lng_ref, lnb_ref, win_ref,
                        wout_ref, o_ref, v_ref, gated_ref, *, n_seq, seq_len, w_stride):
    exp_a = wout_ref.shape[0]
    hd = exp_a // N_HEADS
    x = x_ref[...]
    hn, v = _gmlp_front(x, npre_ref, lng_ref, lnb_ref, win_ref, exp_a)
    v_ref[...] = v
    for h in range(N_HEADS):
        lo = h * hd
        u = _dot(hn, win_ref[:, lo:lo + hd])
        z = _dot(hn, win_ref[:, 2 * exp_a + lo:2 * exp_a + lo + hd])
        vt = [v[t * n_seq:(t + 1) * n_seq, lo:lo + hd] for t in range(seq_len)]
        parts = []
        for t in range(seq_len):
            s = ws_ref[(h * w_stride + t) * w_stride] * vt[0]
            for t2 in range(1, t + 1):
                s = s + ws_ref[(h * w_stride + t) * w_stride + t2] * vt[t2]
            parts.append(s + bs_ref[h * w_stride + t])
        s = jnp.concatenate(parts, axis=0)
        gated_ref[:, lo:lo + hd] = (u * s * _silu(z)).astype(BF16)
    out = _dot(gated_ref[...], wout_ref[...])
    o_ref[...] = x + _rmsnorm(out, npost_ref[...])


def _row_spec(width):
    return pl.BlockSpec((ROW_TILE, width), lambda i: (i, 0))


def _gmlp_prompt(x2d, npre, npost, lng, lnb, win, wout, ws, bst):
    rows, d = x2d.shape
    exp_a = wout.shape[0]
    footprint = (win.size + wout.size) * 2 + 4 * ROW_TILE * d * 4 + ROW_TILE * exp_a * (2 + 4 + 2 + 8)
    return pl.pallas_call(
        _gmlp_prompt_kernel,
        grid=(rows // ROW_TILE,),
        in_specs=[
            _row_spec(d),
            _const_spec(npre.shape), _const_spec(npost.shape),
            _const_spec(lng.shape), _const_spec(lnb.shape),
            _const_spec(win.shape), _const_spec(wout.shape),
            _const_spec(ws.shape), _const_spec(bst.shape),
        ],
        out_specs=_row_spec(d),
        out_shape=jax.ShapeDtypeStruct((rows, d), F32),
        scratch_shapes=[pltpu.VMEM((ROW_TILE, exp_a), BF16)],
        compiler_params=pltpu.CompilerParams(
            dimension_semantics=("parallel",),
            vmem_limit_bytes=_vmem_limit(footprint)),
        name="gmlp_prompt",
    )(x2d, npre, npost, lng, lnb, win, wout, ws, bst)


def _gmlp_sample(xs, n_seq, seq_len, npre, npost, lng, lnb, win, wout, ws_flat, bs_flat):
    rows, d = xs.shape
    exp_a = wout.shape[0]
    footprint = (win.size + wout.size) * 2 + 2 * rows * d * 4 + rows * exp_a * (2 + 4 + 4 + 8)
    vm = pl.BlockSpec(memory_space=pltpu.VMEM)
    sm = pl.BlockSpec(memory_space=pltpu.SMEM)
    return pl.pallas_call(
        functools.partial(_gmlp_sample_kernel, n_seq=n_seq, seq_len=seq_len, w_stride=seq_len),
        in_specs=[sm, sm, vm, vm, vm, vm, vm, vm, vm],
        out_specs=[vm, vm],
        out_shape=[jax.ShapeDtypeStruct((rows, d), F32),
                   jax.ShapeDtypeStruct((rows, exp_a), F32)],
        scratch_shapes=[pltpu.VMEM((rows, exp_a), BF16)],
        compiler_params=pltpu.CompilerParams(vmem_limit_bytes=_vmem_limit(footprint)),
        name="gmlp_sample",
    )(ws_flat, bs_flat, xs, npre, npost, lng, lnb, win, wout)


def _s5_prep_kernel(are_ref, aim_ref, ldt_ref, btr_ref, bti_ref, ctr_ref, cti_ref,
                    abr_ref, abi_ref, wbr_ref, wbi_ref, wcr_ref, wci_ref):
    dt = jnp.exp(ldt_ref[...])
    ar = are_ref[...]
    ai = aim_ref[...]
    mag = jnp.exp(dt * ar)
    ang = dt * ai
    abr = mag * jnp.cos(ang)
    abi = mag * jnp.sin(ang)
    abr_ref[...] = abr
    abi_ref[...] = abi
    nr = abr - 1.0
    ni = abi
    den = ar * ar + ai * ai
    cre = (nr * ar + ni * ai) / den
    cim = (ni * ar - nr * ai) / den
    btr = btr_ref[...]
    bti = bti_ref[...]
    bbr = cre * btr - cim * bti
    bbi = cre * bti + cim * btr

    n_q, k_rows, n_cols = wbr_ref.shape
    in_groups = k_rows // SSM_GROUP
    out_groups = n_cols // STATE_P
    rg = _div_pow2(lax.broadcasted_iota(jnp.int32, (k_rows, n_cols), 0), SSM_GROUP)
    cg = _div_pow2(lax.broadcasted_iota(jnp.int32, (k_rows, n_cols), 1), STATE_P)
    for q in range(n_q):
        same = (rg + in_groups * ((q * out_groups) // in_groups)) == (cg + out_groups * q)
        for src, dst in ((bbr, wbr_ref), (bbi, wbi_ref)):
            blk = jnp.concatenate([src[:, q * n_cols:(q + 1) * n_cols]] * in_groups, axis=0)
            dst[q] = jnp.where(same, blk, 0.0).astype(BF16)

    n_n, k_lanes, n_out = wcr_ref.shape
    groups = n_out // SSM_GROUP
    same_t = (_div_pow2(lax.broadcasted_iota(jnp.int32, (n_out, k_lanes), 0), SSM_GROUP)
              == _div_pow2(lax.broadcasted_iota(jnp.int32, (n_out, k_lanes), 1), STATE_P))
    for n in range(n_n):
        for src, dst, sign in ((ctr_ref, wcr_ref, 1.0), (cti_ref, wci_ref, -1.0)):
            blk = jnp.concatenate([src[:, n * k_lanes:(n + 1) * k_lanes]] * groups, axis=0)
            dst[n] = (jnp.where(same_t, blk, 0.0) * sign).T.astype(BF16)


def _s5_prep(are, aim, ldt, btr, bti, ctr, cti):
    n_state = are.shape[-1]
    n_ch = n_state // STATE_P * SSM_GROUP
    n_q = n_state // V7X_MXU_DIM
    n_n = n_ch // V7X_MXU_DIM
    k_lanes = n_state // n_n
    vm = pl.BlockSpec(memory_space=pltpu.VMEM)
    return pl.pallas_call(
        _s5_prep_kernel,
        in_specs=[vm] * 7,
        out_specs=[vm] * 6,
        out_shape=[jax.ShapeDtypeStruct((1, n_state), F32),
                   jax.ShapeDtypeStruct((1, n_state), F32),
                   jax.ShapeDtypeStruct((n_q, V7X_LANES, V7X_MXU_DIM), BF16),
                   jax.ShapeDtypeStruct((n_q, V7X_LANES, V7X_MXU_DIM), BF16),
                   jax.ShapeDtypeStruct((n_n, k_lanes, V7X_MXU_DIM), BF16),
                   jax.ShapeDtypeStruct((n_n, k_lanes, V7X_MXU_DIM), BF16)],
        compiler_params=pltpu.CompilerParams(vmem_limit_bytes=_vmem_limit(24 << 20)),
        name="s5_prep",
    )(are, aim, ldt, btr, bti, ctr, cti)


def _s5_in_kernel(x_ref, npre_ref, win_ref, xb_ref, z_ref):
    d_in = xb_ref.shape[-1]
    hn = _rmsnorm(x_ref[...], npre_ref[...]).astype(BF16)
    xb_ref[...] = _dot(hn, win_ref[:, :d_in])
    z_ref[...] = _dot(hn, win_ref[:, d_in:])


def _s5_in_prompt(x2d, npre, win):
    rows, d = x2d.shape
    exp_b = win.shape[1] // 2
    footprint = win.size * 2 + 6 * ROW_TILE * d * 4 + 4 * ROW_TILE * exp_b * 4
    return pl.pallas_call(
        _s5_in_kernel,
        grid=(rows // ROW_TILE,),
        in_specs=[_row_spec(d), _const_spec(npre.shape), _const_spec(win.shape)],
        out_specs=[_row_spec(exp_b), _row_spec(exp_b)],
        out_shape=[jax.ShapeDtypeStruct((rows, exp_b), F32)] * 2,
        compiler_params=pltpu.CompilerParams(
            dimension_semantics=("parallel",),
            vmem_limit_bytes=_vmem_limit(footprint)),
        name="s5_in_prompt",
    )(x2d, npre, win)


def _s5_in_sample(xs, npre, win):
    rows = xs.shape[0]
    exp_b = win.shape[1] // 2
    vm = pl.BlockSpec(memory_space=pltpu.VMEM)
    return pl.pallas_call(
        _s5_in_kernel,
        in_specs=[vm, vm, vm],
        out_specs=[vm, vm],
        out_shape=[jax.ShapeDtypeStruct((rows, exp_b), F32)] * 2,
        name="s5_in_sample",
    )(xs, npre, win)


def _s5_core_kernel(*refs, n_seq, n_steps, has_h0, swap_bt):
    if has_h0:
        (xb_ref, h0r_ref, h0i_ref, abr_ref, abi_ref, wbr_ref, wbi_ref, wcr_ref, wci_ref, dsk_ref,
         y_ref, str_ref, sti_ref, hbr_ref, hbi_ref, hr_ref, hi_ref) = refs
    else:
        (xb_ref, abr_ref, abi_ref, wbr_ref, wbi_ref, wcr_ref, wci_ref, dsk_ref,
         y_ref, str_ref, sti_ref, hbr_ref, hbi_ref, hr_ref, hi_ref) = refs
    step = pl.program_id(0)

    @pl.when(step == 0)
    def _():
        if has_h0:
            hr_ref[...] = h0r_ref[...]
            hi_ref[...] = h0i_ref[...]
        else:
            hr_ref[...] = jnp.zeros_like(hr_ref)
            hi_ref[...] = jnp.zeros_like(hi_ref)

    xb = xb_ref[...]
    if swap_bt:
        xb = jnp.swapaxes(xb, 0, 1).reshape(n_steps * n_seq, xb.shape[-1])
    xb16 = xb.astype(BF16)
    n_q, k_rows, n_cols = wbr_ref.shape
    sub = V7X_SUBLANES
    pair = 2 * sub
    for q in range(n_q):
        lanes = pl.ds(q * n_cols, n_cols)
        k0 = (q * n_cols // STATE_P * SSM_GROUP) // k_rows * k_rows
        lhs = xb16[:, k0:k0 + k_rows]
        bur = _dot(lhs, wbr_ref[q])
        bui = _dot(lhs, wbi_ref[q])
        ar = jnp.broadcast_to(abr_ref[:, lanes], (sub, n_cols))
        ai = jnp.broadcast_to(abi_ref[:, lanes], (sub, n_cols))
        if n_seq == sub:
            hr, hi = hr_ref[:, lanes], hi_ref[:, lanes]
            for k in range(n_steps // 2):
                pr, pi = [], []
                for r0 in (2 * k * sub, (2 * k + 1) * sub):
                    hr, hi = (ar * hr - ai * hi + bur[r0:r0 + sub],
                              ar * hi + ai * hr + bui[r0:r0 + sub])
                    pr.append(hr)
                    pi.append(hi)
                hbr_ref[k * pair:(k + 1) * pair, lanes] = jnp.concatenate(pr, axis=0).astype(BF16)
                hbi_ref[k * pair:(k + 1) * pair, lanes] = jnp.concatenate(pi, axis=0).astype(BF16)
            hr_ref[:, lanes] = hr
            hi_ref[:, lanes] = hi
        else:
            for m in range(n_seq // pair):
                srows = pl.ds(m * pair, pair)
                hr, hi = hr_ref[srows, lanes], hi_ref[srows, lanes]
                ar2 = jnp.concatenate([ar, ar], axis=0)
                ai2 = jnp.concatenate([ai, ai], axis=0)
                for t in range(n_steps):
                    r0 = t * n_seq + m * pair
                    hr, hi = (ar2 * hr - ai2 * hi + bur[r0:r0 + pair],
                              ar2 * hi + ai2 * hr + bui[r0:r0 + pair])
                    hbr_ref[r0:r0 + pair, lanes] = hr.astype(BF16)
                    hbi_ref[r0:r0 + pair, lanes] = hi.astype(BF16)
                hr_ref[srows, lanes] = hr
                hi_ref[srows, lanes] = hi

    n_n, k_lanes, n_out = wcr_ref.shape
    for n in range(n_n):
        kl = pl.ds(n * k_lanes, k_lanes)
        cols = pl.ds(n * n_out, n_out)
        y = _dot(hbr_ref[:, kl], wcr_ref[n]) + _dot(hbi_ref[:, kl], wci_ref[n])
        y = jax.nn.gelu(y + dsk_ref[:, cols] * xb[:, n * n_out:(n + 1) * n_out])
        if swap_bt:
            y = jnp.swapaxes(y.reshape(n_steps, n_seq, n_out), 0, 1)
            y_ref[:, :, cols] = y.astype(BF16)
        else:
            y_ref[:, cols] = y.astype(BF16)

    @pl.when(step == pl.num_programs(0) - 1)
    def _():
        str_ref[...] = hr_ref[...]
        sti_ref[...] = hi_ref[...]


def _s5_core(xb, h0, abr, abi, wbr, wbi, wcr, wci, dsk, n_seq, n_steps):
    swap_bt = xb.ndim == 3
    exp_b = xb.shape[-1]
    n_state = abr.shape[-1]
    tile = n_seq * n_steps
    n_tiles = xb.size // (tile * exp_b)
    has_h0 = h0 is not None
    consts = [abr, abi, wbr, wbi, wcr, wci, dsk]
    args = [xb] + (list(h0) if has_h0 else []) + consts
    if swap_bt:
        io_spec = pl.BlockSpec((n_seq, n_steps, exp_b), lambda i: (0, i, 0))
    else:
        io_spec = pl.BlockSpec((tile, exp_b), lambda i: (i, 0))
    in_specs = [io_spec]
    if has_h0:
        in_specs += [_const_spec(h0[0].shape)] * 2
    in_specs += [_const_spec(c.shape) for c in consts]
    st_shape = jax.ShapeDtypeStruct((n_seq, n_state), F32)
    footprint = (sum(c.size * c.dtype.itemsize for c in consts) + 2 * tile * exp_b * (4 + 2)
                 + 2 * tile * n_state * 4 + 8 * n_seq * n_state * 4 + 2 * tile * n_state * 2)
    return pl.pallas_call(
        functools.partial(_s5_core_kernel, n_seq=n_seq, n_steps=n_steps, has_h0=has_h0,
                          swap_bt=swap_bt),
        grid=(n_tiles,),
        in_specs=in_specs,
        out_specs=[io_spec,
                   pl.BlockSpec((n_seq, n_state), lambda i: (0, 0)),
                   pl.BlockSpec((n_seq, n_state), lambda i: (0, 0))],
        out_shape=[jax.ShapeDtypeStruct(xb.shape, BF16), st_shape, st_shape],
        scratch_shapes=[pltpu.VMEM((tile, n_state), BF16), pltpu.VMEM((tile, n_state), BF16),
                        pltpu.VMEM((n_seq, n_state), F32), pltpu.VMEM((n_seq, n_state), F32)],
        compiler_params=pltpu.CompilerParams(
            dimension_semantics=("arbitrary",),
            vmem_limit_bytes=_vmem_limit(footprint)),
        name="s5_core_sample" if has_h0 else "s5_core_prompt",
    )(*args)


def _s5_out_kernel(y_ref, z_ref, x_ref, w1_ref, b1_ref, w2_ref, b2_ref, wout_ref, npost_ref, o_ref):
    y = y_ref[...]
    g = (_dot(y, w1_ref[...]) + b1_ref[...]) * jax.nn.sigmoid(_dot(y, w2_ref[...]) + b2_ref[...])
    out = _dot((g * _silu(z_ref[...])).astype(BF16), wout_ref[...])
    o_ref[...] = x_ref[...] + _rmsnorm(out, npost_ref[...])


def _s5_out_prompt(y, z, x2d, consts):
    rows, d = x2d.shape
    exp_b = z.shape[1]
    footprint = (sum(c.size * c.dtype.itemsize for c in consts)
                 + 2 * ROW_TILE * (exp_b * 6 + d * 8) + 6 * ROW_TILE * exp_b * 4)
    return pl.pallas_call(
        _s5_out_kernel,
        grid=(rows // ROW_TILE,),
        in_specs=[_row_spec(exp_b), _row_spec(exp_b), _row_spec(d)]
        + [_const_spec(c.shape) for c in consts],
        out_specs=_row_spec(d),
        out_shape=jax.ShapeDtypeStruct((rows, d), F32),
        compiler_params=pltpu.CompilerParams(
            dimension_semantics=("parallel",),
            vmem_limit_bytes=_vmem_limit(footprint)),
        name="s5_out_prompt",
    )(y, z, x2d, *consts)


def _s5_out_sample(y, z, xs, consts):
    vm = pl.BlockSpec(memory_space=pltpu.VMEM)
    return pl.pallas_call(
        _s5_out_kernel,
        in_specs=[vm] * (3 + len(consts)),
        out_specs=vm,
        out_shape=jax.ShapeDtypeStruct(xs.shape, F32),
        name="s5_out_sample",
    )(y, z, xs, *consts)


def kernel(x_prompt, x_sample, state_ssm_re, state_ssm_im, norm_pre, norm_post,
           w_in_a, ln_v_g, ln_v_b, w_s, b_s, w_out_a,
           w_in_b, a_re, a_im, log_dt, b_re, b_im, c_re, c_im, d_skip,
           w_glu1, b_glu1, w_glu2, b_glu2, w_out_b):
    n_batch, seq_len, d = x_prompt.shape
    n_dec, dec_len, _ = x_sample.shape
    depth = norm_pre.shape[0]
    n_groups, state_p = a_re.shape[1:]
    n_state = n_groups * state_p
    assert seq_len % ROW_TILE == 0 and ROW_TILE % CHUNK == 0 and seq_len % SCAN_STEPS == 0
    assert n_batch == V7X_SUBLANES and n_dec % V7X_SUBLANES == 0 and dec_len <= CHUNK
    assert state_p == STATE_P and b_re.shape[-1] == SSM_GROUP and w_s.shape[1] == N_HEADS

    row = lambda a: a.reshape(1, -1)

    xp = x_prompt.reshape(n_batch * seq_len, d)
    xs = jnp.transpose(x_sample, (1, 0, 2)).reshape(dec_len * n_dec, d)

    v_rows, st_p_re, st_p_im, st_s_re, st_s_im = [], [], [], [], []
    for i in range(depth):
        j = i // 2
        npre, npost = row(norm_pre[i]), row(norm_post[i])
        if i % 2 == 0:
            win, wout = w_in_a[j].astype(BF16), w_out_a[j].astype(BF16)
            lng, lnb = row(ln_v_g[j]), row(ln_v_b[j])
            xp = _gmlp_prompt(xp, npre, npost, lng, lnb, win, wout, w_s[j], jnp.transpose(b_s[j]))
            xs, v = _gmlp_sample(xs, n_dec, dec_len, npre, npost, lng, lnb, win, wout,
                                 w_s[j, :, :dec_len, :dec_len].reshape(-1),
                                 b_s[j, :, :dec_len].reshape(-1))
            v_rows.append(jnp.transpose(v.reshape(dec_len, n_dec, -1), (1, 0, 2)))
        else:
            win = w_in_b[j].astype(BF16)
            exp_b = win.shape[1] // 2
            to_lanes = lambda a: jnp.transpose(a, (1, 0, 2)).reshape(SSM_GROUP, n_state)
            abr, abi, wbr, wbi, wcr, wci = _s5_prep(
                row(a_re[j]), row(a_im[j]), row(jnp.repeat(log_dt[j], state_p)),
                to_lanes(jnp.swapaxes(b_re[j], 1, 2)), to_lanes(jnp.swapaxes(b_im[j], 1, 2)),
                to_lanes(c_re[j]), to_lanes(c_im[j]))
            dsk = row(d_skip[j])
            out_consts = [w_glu1[j].astype(BF16), row(b_glu1[j]), w_glu2[j].astype(BF16),
                          row(b_glu2[j]), w_out_b[j].astype(BF16), npost]

            xb, z = _s5_in_prompt(xp, npre, win)
            y, sr, si = _s5_core(xb.reshape(n_batch, seq_len, exp_b), None,
                                 abr, abi, wbr, wbi, wcr, wci, dsk, n_batch, SCAN_STEPS)
            xp = _s5_out_prompt(y.reshape(n_batch * seq_len, exp_b), z, xp, out_consts)
            st_p_re.append(sr.reshape(n_batch, n_groups, state_p))
            st_p_im.append(si.reshape(n_batch, n_groups, state_p))

            xbs, zs = _s5_in_sample(xs, npre, win)
            h0 = (state_ssm_re[j].reshape(n_dec, n_state), state_ssm_im[j].reshape(n_dec, n_state))
            ys, sr, si = _s5_core(xbs, h0, abr, abi, wbr, wbi, wcr, wci, dsk, n_dec, dec_len)
            xs = _s5_out_sample(ys, zs, xs, out_consts)
            st_s_re.append(sr.reshape(n_dec, n_groups, state_p))
            st_s_im.append(si.reshape(n_dec, n_groups, state_p))

    y_prompt = xp.reshape(n_batch, seq_len, d)
    y_sample = jnp.transpose(xs.reshape(dec_len, n_dec, d), (1, 0, 2))
    return (y_prompt, y_sample, jnp.stack(v_rows), jnp.stack(st_p_re), jnp.stack(st_p_im),
            jnp.stack(st_s_re), jnp.stack(st_s_im))
```

```python
import functools

import jax
import jax.numpy as jnp
from jax import lax
from jax.experimental import pallas as pl
from jax.experimental.pallas import tpu as pltpu

EPS = 1e-6
CHUNK = 128
N_HEADS = 8
SSM_GROUP = 16
STATE_P = 64

V7X_LANES = 128
V7X_SUBLANES = 8
V7X_MXU_DIM = 256
V7X_VMEM_BYTES = 64 * 1024 * 1024

ROW_TILE = 512
SCAN_STEPS = 32

BF16 = jnp.bfloat16
F32 = jnp.float32


def _dot(a, b):
    return jnp.dot(a, b, preferred_element_type=F32)


def _rmsnorm(x, g):
    ms = jnp.mean(x * x, axis=-1, keepdims=True)
    return x * lax.rsqrt(ms + EPS) * g


def _layernorm(x, g, b):
    mu = jnp.mean(x, axis=-1, keepdims=True)
    xc = x - mu
    var = jnp.mean(xc * xc, axis=-1, keepdims=True)
    return xc * lax.rsqrt(var + EPS) * g + b


def _silu(z):
    return z * jax.nn.sigmoid(z)


def _div_pow2(x, n):
    assert n & (n - 1) == 0
    return lax.shift_right_logical(x, n.bit_length() - 1)


def _nbytes(*arrays):
    return sum(a.size * a.dtype.itemsize for a in arrays)


def _vmem_limit(nbytes):
    return int(min(V7X_VMEM_BYTES - (4 << 20), nbytes + (12 << 20)))


def _layer_spec(stacked, layer):
    tail = stacked.shape[1:]
    return pl.BlockSpec((None,) + tail, lambda *_: (layer,) + (0,) * len(tail),
                        pipeline_mode=pl.Buffered(1))


def _layer_bytes(*stacked):
    return sum(a[0].size * a.dtype.itemsize for a in stacked)


def _row_spec(rows, width):
    return pl.BlockSpec((rows, width), lambda i: (i, 0))


_SMEM = pl.BlockSpec(memory_space=pltpu.SMEM)


def _gmlp_front(x, npre_ref, lng_ref, lnb_ref, win_ref, exp_a):
    hn = _rmsnorm(x, npre_ref[...]).astype(BF16)
    v = _dot(hn, win_ref[:, exp_a:2 * exp_a])
    v = _layernorm(v, lng_ref[...], lnb_ref[...])
    return hn, v


def _gmlp_prompt_kernel(x_ref, npre_ref, npost_ref, lng_ref, lnb_ref, win_ref, wout_ref,
                        ws_ref, bst_ref, o_ref, gated_ref):
    rows = x_ref.shape[0]
    exp_a = wout_ref.shape[0]
    hd = exp_a // N_HEADS
    x = x_ref[...]
    hn, v = _gmlp_front(x, npre_ref, lng_ref, lnb_ref, win_ref, exp_a)
    vb = v.astype(BF16)
    causal = (lax.broadcasted_iota(jnp.int32, (CHUNK, CHUNK), 1)
              <= lax.broadcasted_iota(jnp.int32, (CHUNK, CHUNK), 0))
    for h in range(N_HEADS):
        lo = h * hd
        wsh = jnp.where(causal, ws_ref[h], 0.0).astype(BF16)
        bias = bst_ref[:, h:h + 1]
        u = _dot(hn, win_ref[:, lo:lo + hd])
        z = _dot(hn, win_ref[:, 2 * exp_a + lo:2 * exp_a + lo + hd])
        s = jnp.concatenate(
            [_dot(wsh, vb[c * CHUNK:(c + 1) * CHUNK, lo:lo + hd]) + bias
             for c in range(rows // CHUNK)], axis=0)
        gated_ref[:, lo:lo + hd] = (u * s * _silu(z)).astype(BF16)
    out = _dot(gated_ref[...], wout_ref[...])
    o_ref[...] = x + _rmsnorm(out, npost_ref[...])


def _gmlp_sample_kernel(ws_ref, bs_ref, x_ref, npre_ref, npost_ref, lng_ref, lnb_ref, win_ref,
                        wout_ref, o_ref, v_ref, gated_ref, *, n_seq, seq_len, layer):
    exp_a = wout_ref.shape[0]
    hd = exp_a // N_HEADS
    x = x_ref[...]
    hn, v = _gmlp_front(x, npre_ref, lng_ref, lnb_ref, win_ref, exp_a)
    v_ref[...] = v
    for h in range(N_HEADS):
        lo = h * hd
        u = _dot(hn, win_ref[:, lo:lo + hd])
        z = _dot(hn, win_ref[:, 2 * exp_a + lo:2 * exp_a + lo + hd])
        vt = [v[t * n_seq:(t + 1) * n_seq, lo:lo + hd] for t in range(seq_len)]
        parts = []
        for t in range(seq_len):
            b_idx = (layer * N_HEADS + h) * seq_len + t
            s = ws_ref[b_idx * seq_len] * vt[0]
            for t2 in range(1, t + 1):
                s = s + ws_ref[b_idx * seq_len + t2] * vt[t2]
            parts.append(s + bs_ref[b_idx])
        s = jnp.concatenate(parts, axis=0)
        gated_ref[:, lo:lo + hd] = (u * s * _silu(z)).astype(BF16)
    out = _dot(gated_ref[...], wout_ref[...])
    o_ref[...] = x + _rmsnorm(out, npost_ref[...])


def _gmlp_prompt(x2d, layer_a, layer, npre, npost, lng, lnb, win, wout, ws, bst):
    rows, d = x2d.shape
    exp_a = wout.shape[1]
    footprint = (_layer_bytes(win, wout, ws) + 4 * ROW_TILE * d * 4
                 + ROW_TILE * exp_a * (2 + 4 + 2 + 8))
    return pl.pallas_call(
        _gmlp_prompt_kernel,
        grid=(rows // ROW_TILE,),
        in_specs=[
            _row_spec(ROW_TILE, d),
            _layer_spec(npre, layer), _layer_spec(npost, layer),
            _layer_spec(lng, layer_a), _layer_spec(lnb, layer_a),
            _layer_spec(win, layer_a), _layer_spec(wout, layer_a),
            _layer_spec(ws, layer_a), _layer_spec(bst, layer_a),
        ],
        out_specs=_row_spec(ROW_TILE, d),
        out_shape=jax.ShapeDtypeStruct((rows, d), F32),
        scratch_shapes=[pltpu.VMEM((ROW_TILE, exp_a), BF16)],
        compiler_params=pltpu.CompilerParams(
            dimension_semantics=("parallel",),
            vmem_limit_bytes=_vmem_limit(footprint)),
        name="gmlp_prompt",
    )(x2d, npre, npost, lng, lnb, win, wout, ws, bst)


def _gmlp_sample(xs, n_seq, seq_len, layer_a, layer, npre, npost, lng, lnb, win, wout,
                 ws_flat, bs_flat):
    rows, d = xs.shape
    exp_a = wout.shape[1]
    footprint = _layer_bytes(win, wout) + 4 * rows * d * 4 + rows * exp_a * (2 + 8 + 4 + 8)
    return pl.pallas_call(
        functools.partial(_gmlp_sample_kernel, n_seq=n_seq, seq_len=seq_len, layer=layer_a),
        grid=(1,),
        in_specs=[_SMEM, _SMEM, _row_spec(rows, d),
                  _layer_spec(npre, layer), _layer_spec(npost, layer),
                  _layer_spec(lng, layer_a), _layer_spec(lnb, layer_a),
                  _layer_spec(win, layer_a), _layer_spec(wout, layer_a)],
        out_specs=[_row_spec(rows, d), _row_spec(rows, exp_a)],
        out_shape=[jax.ShapeDtypeStruct((rows, d), F32),
                   jax.ShapeDtypeStruct((rows, exp_a), F32)],
        scratch_shapes=[pltpu.VMEM((rows, exp_a), BF16)],
        compiler_params=pltpu.CompilerParams(
            dimension_semantics=("arbitrary",),
            vmem_limit_bytes=_vmem_limit(footprint)),
        name="gmlp_sample",
    )(ws_flat, bs_flat, xs, npre, npost, lng, lnb, win, wout)


def _s5_prep_kernel(are_ref, aim_ref, ldt_ref, btr_ref, bti_ref, ctr_ref, cti_ref,
                    abr_ref, abi_ref, wbr_ref, wbi_ref, wcr_ref, wci_ref):
    dt = jnp.exp(ldt_ref[...])
    ar = are_ref[...]
    ai = aim_ref[...]
    mag = jnp.exp(dt * ar)
    ang = dt * ai
    abr = mag * jnp.cos(ang)
    abi = mag * jnp.sin(ang)
    abr_ref[...] = abr
    abi_ref[...] = abi
    nr = abr - 1.0
    ni = abi
    den = ar * ar + ai * ai
    cre = (nr * ar + ni * ai) / den
    cim = (ni * ar - nr * ai) / den
    btr = btr_ref[...]
    bti = bti_ref[...]
    bbr = cre * btr - cim * bti
    bbi = cre * bti + cim * btr

    n_q, k_rows, n_cols = wbr_ref.shape
    in_groups = k_rows // SSM_GROUP
    out_groups = n_cols // STATE_P
    rg = _div_pow2(lax.broadcasted_iota(jnp.int32, (k_rows, n_cols), 0), SSM_GROUP)
    cg = _div_pow2(lax.broadcasted_iota(jnp.int32, (k_rows, n_cols), 1), STATE_P)
    for q in range(n_q):
        same = (rg + in_groups * ((q * out_groups) // in_groups)) == (cg + out_groups * q)
        for src, dst in ((bbr, wbr_ref), (bbi, wbi_ref)):
            blk = jnp.concatenate([src[:, q * n_cols:(q + 1) * n_cols]] * in_groups, axis=0)
            dst[q] = jnp.where(same, blk, 0.0).astype(BF16)

    n_n, k_lanes, n_out = wcr_ref.shape
    groups = n_out // SSM_GROUP
    same_t = (_div_pow2(lax.broadcasted_iota(jnp.int32, (n_out, k_lanes), 0), SSM_GROUP)
              == _div_pow2(lax.broadcasted_iota(jnp.int32, (n_out, k_lanes), 1), STATE_P))
    for n in range(n_n):
        for src, dst, sign in ((ctr_ref, wcr_ref, 1.0), (cti_ref, wci_ref, -1.0)):
            blk = jnp.concatenate([src[:, n * k_lanes:(n + 1) * k_lanes]] * groups, axis=0)
            dst[n] = (jnp.where(same_t, blk, 0.0) * sign).T.astype(BF16)


def _s5_prep(are, aim, ldt, btr, bti, ctr, cti):
    n_layers, _, n_state = are.shape
    n_ch = n_state // STATE_P * SSM_GROUP
    n_q = n_state // V7X_MXU_DIM
    n_n = n_ch // V7X_MXU_DIM
    k_lanes = n_state // n_n

    def per_layer(shape):
        return pl.BlockSpec((None,) + shape, lambda l: (l,) + (0,) * len(shape))

    out_tails = [(1, n_state), (1, n_state),
                 (n_q, V7X_LANES, V7X_MXU_DIM), (n_q, V7X_LANES, V7X_MXU_DIM),
                 (n_n, k_lanes, V7X_MXU_DIM), (n_n, k_lanes, V7X_MXU_DIM)]
    out_dtypes = [F32, F32, BF16, BF16, BF16, BF16]
    args = (are, aim, ldt, btr, bti, ctr, cti)
    return pl.pallas_call(
        _s5_prep_kernel,
        grid=(n_layers,),
        in_specs=[per_layer(a.shape[1:]) for a in args],
        out_specs=[per_layer(t) for t in out_tails],
        out_shape=[jax.ShapeDtypeStruct((n_layers,) + t, dt) for t, dt in zip(out_tails, out_dtypes)],
        compiler_params=pltpu.CompilerParams(
            dimension_semantics=("parallel",),
            vmem_limit_bytes=_vmem_limit(32 << 20)),
        name="s5_prep",
    )(*args)


def _s5_in_kernel(x_ref, npre_ref, win_ref, xb_ref, z_ref):
    d_in = xb_ref.shape[-1]
    hn = _rmsnorm(x_ref[...], npre_ref[...]).astype(BF16)
    xb_ref[...] = _dot(hn, win_ref[:, :d_in])
    z_ref[...] = _dot(hn, win_ref[:, d_in:])


def _s5_in(x2d, tile, layer_b, layer, npre, win, name):
    rows, d = x2d.shape
    exp_b = win.shape[2] // 2
    footprint = _layer_bytes(win) + 6 * tile * d * 4 + 4 * tile * exp_b * 4
    return pl.pallas_call(
        _s5_in_kernel,
        grid=(rows // tile,),
        in_specs=[_row_spec(tile, d), _layer_spec(npre, layer), _layer_spec(win, layer_b)],
        out_specs=[_row_spec(tile, exp_b), _row_spec(tile, exp_b)],
        out_shape=[jax.ShapeDtypeStruct((rows, exp_b), F32)] * 2,
        compiler_params=pltpu.CompilerParams(
            dimension_semantics=("parallel",),
            vmem_limit_bytes=_vmem_limit(footprint)),
        name=name,
    )(x2d, npre, win)


def _s5_core_kernel(*refs, n_seq, n_steps, has_h0, swap_bt):
    if has_h0:
        (xb_ref, h0r_ref, h0i_ref, abr_ref, abi_ref, wbr_ref, wbi_ref, wcr_ref, wci_ref, dsk_ref,
         y_ref, str_ref, sti_ref, hbr_ref, hbi_ref, hr_ref, hi_ref) = refs
    else:
        (xb_ref, abr_ref, abi_ref, wbr_ref, wbi_ref, wcr_ref, wci_ref, dsk_ref,
         y_ref, str_ref, sti_ref, hbr_ref, hbi_ref, hr_ref, hi_ref) = refs
    step = pl.program_id(0)

    @pl.when(step == 0)
    def _():
        if has_h0:
            hr_ref[...] = h0r_ref[...]
            hi_ref[...] = h0i_ref[...]
        else:
            hr_ref[...] = jnp.zeros_like(hr_ref)
            hi_ref[...] = jnp.zeros_like(hi_ref)

    xb = xb_ref[...]
    if swap_bt:
        xb = jnp.swapaxes(xb, 0, 1).reshape(n_steps * n_seq, xb.shape[-1])
    xb16 = xb.astype(BF16)
    n_q, k_rows, n_cols = wbr_ref.shape
    sub = V7X_SUBLANES
    pair = 2 * sub
    for q in range(n_q):
        lanes = pl.ds(q * n_cols, n_cols)
        k0 = (q * n_cols // STATE_P * SSM_GROUP) // k_rows * k_rows
        lhs = xb16[:, k0:k0 + k_rows]
        bur = _dot(lhs, wbr_ref[q])
        bui = _dot(lhs, wbi_ref[q])
        ar = jnp.broadcast_to(abr_ref[:, lanes], (sub, n_cols))
        ai = jnp.broadcast_to(abi_ref[:, lanes], (sub, n_cols))
        if n_seq == sub:
            hr, hi = hr_ref[:, lanes], hi_ref[:, lanes]
            for k in range(n_steps // 2):
                pr, pi = [], []
                for r0 in (2 * k * sub, (2 * k + 1) * sub):
                    hr, hi = (ar * hr - ai * hi + bur[r0:r0 + sub],
                              ar * hi + ai * hr + bui[r0:r0 + sub])
                    pr.append(hr)
                    pi.append(hi)
                hbr_ref[k * pair:(k + 1) * pair, lanes] = jnp.concatenate(pr, axis=0).astype(BF16)
                hbi_ref[k * pair:(k + 1) * pair, lanes] = jnp.concatenate(pi, axis=0).astype(BF16)
            hr_ref[:, lanes] = hr
            hi_ref[:, lanes] = hi
        else:
            for m in range(n_seq // pair):
                srows = pl.ds(m * pair, pair)
                hr, hi = hr_ref[srows, lanes], hi_ref[srows, lanes]
                ar2 = jnp.concatenate([ar, ar], axis=0)
                ai2 = jnp.concatenate([ai, ai], axis=0)
                for t in range(n_steps):
                    r0 = t * n_seq + m * pair
                    hr, hi = (ar2 * hr - ai2 * hi + bur[r0:r0 + pair],
                              ar2 * hi + ai2 * hr + bui[r0:r0 + pair])
                    hbr_ref[r0:r0 + pair, lanes] = hr.astype(BF16)
                    hbi_ref[r0:r0 + pair, lanes] = hi.astype(BF16)
                hr_ref[srows, lanes] = hr
                hi_ref[srows, lanes] = hi

    n_n, k_lanes, n_out = wcr_ref.shape
    for n in range(n_n):
        kl = pl.ds(n * k_lanes, k_lanes)
        cols = pl.ds(n * n_out, n_out)
        y = _dot(hbr_ref[:, kl], wcr_ref[n]) + _dot(hbi_ref[:, kl], wci_ref[n])
        y = jax.nn.gelu(y + dsk_ref[:, cols] * xb[:, n * n_out:(n + 1) * n_out])
        if swap_bt:
            y = jnp.swapaxes(y.reshape(n_steps, n_seq, n_out), 0, 1)
            y_ref[:, :, cols] = y.astype(BF16)
        else:
            y_ref[:, cols] = y.astype(BF16)

    @pl.when(step == pl.num_programs(0) - 1)
    def _():
        str_ref[...] = hr_ref[...]
        sti_ref[...] = hi_ref[...]


def _s5_core(xb, h0, layer_b, consts, n_seq, n_steps):
    swap_bt = xb.ndim == 3
    exp_b = xb.shape[-1]
    n_state = consts[0].shape[-1]
    tile = n_seq * n_steps
    n_tiles = xb.size // (tile * exp_b)
    has_h0 = h0 is not None
    args = [xb] + (list(h0) if has_h0 else []) + list(consts)
    if swap_bt:
        io_spec = pl.BlockSpec((n_seq, n_steps, exp_b), lambda i: (0, i, 0))
    else:
        io_spec = pl.BlockSpec((tile, exp_b), lambda i: (i, 0))
    in_specs = [io_spec] + [_layer_spec(a, layer_b) for a in args[1:]]
    st_spec = pl.BlockSpec((n_seq, n_state), lambda i: (0, 0))
    st_shape = jax.ShapeDtypeStruct((n_seq, n_state), F32)
    footprint = (_layer_bytes(*args[1:]) + 2 * tile * exp_b * (4 + 2) + 2 * tile * n_state * 2
                 + 8 * n_seq * n_state * 4 + 4 * tile * V7X_MXU_DIM * 4 * 2)
    return pl.pallas_call(
        functools.partial(_s5_core_kernel, n_seq=n_seq, n_steps=n_steps, has_h0=has_h0,
                          swap_bt=swap_bt),
        grid=(n_tiles,),
        in_specs=in_specs,
        out_specs=[io_spec, st_spec, st_spec],
        out_shape=[jax.ShapeDtypeStruct(xb.shape, BF16), st_shape, st_shape],
        scratch_shapes=[pltpu.VMEM((tile, n_state), BF16), pltpu.VMEM((tile, n_state), BF16),
                        pltpu.VMEM((n_seq, n_state), F32), pltpu.VMEM((n_seq, n_state), F32)],
        compiler_params=pltpu.CompilerParams(
            dimension_semantics=("arbitrary",),
            vmem_limit_bytes=_vmem_limit(footprint)),
        name="s5_core_sample" if has_h0 else "s5_core_prompt",
    )(*args)


def _s5_out_kernel(y_ref, z_ref, x_ref, w1_ref, b1_ref, w2_ref, b2_ref, wout_ref, npost_ref, o_ref):
    y = y_ref[...]
    g = (_dot(y, w1_ref[...]) + b1_ref[...]) * jax.nn.sigmoid(_dot(y, w2_ref[...]) + b2_ref[...])
    out = _dot((g * _silu(z_ref[...])).astype(BF16), wout_ref[...])
    o_ref[...] = x_ref[...] + _rmsnorm(out, npost_ref[...])


def _s5_out(y, z, x2d, tile, layer_b, layer, w1, b1, w2, b2, wout, npost, name):
    rows, d = x2d.shape
    exp_b = z.shape[1]
    footprint = (_layer_bytes(w1, w2, wout) + 2 * tile * (exp_b * 6 + d * 8) + 6 * tile * exp_b * 4)
    return pl.pallas_call(
        _s5_out_kernel,
        grid=(rows // tile,),
        in_specs=[_row_spec(tile, exp_b), _row_spec(tile, exp_b), _row_spec(tile, d),
                  _layer_spec(w1, layer_b), _layer_spec(b1, layer_b),
                  _layer_spec(w2, layer_b), _layer_spec(b2, layer_b),
                  _layer_spec(wout, layer_b), _layer_spec(npost, layer)],
        out_specs=_row_spec(tile, d),
        out_shape=jax.ShapeDtypeStruct((rows, d), F32),
        compiler_params=pltpu.CompilerParams(
            dimension_semantics=("parallel",),
            vmem_limit_bytes=_vmem_limit(footprint)),
        name=name,
    )(y, z, x2d, w1, b1, w2, b2, wout, npost)


def kernel(x_prompt, x_sample, state_ssm_re, state_ssm_im, norm_pre, norm_post,
           w_in_a, ln_v_g, ln_v_b, w_s, b_s, w_out_a,
           w_in_b, a_re, a_im, log_dt, b_re, b_im, c_re, c_im, d_skip,
           w_glu1, b_glu1, w_glu2, b_glu2, w_out_b):
    n_batch, seq_len, d = x_prompt.shape
    n_dec, dec_len, _ = x_sample.shape
    depth = norm_pre.shape[0]
    n_ssm, n_groups, state_p = a_re.shape
    n_state = n_groups * state_p
    assert seq_len % ROW_TILE == 0 and ROW_TILE % CHUNK == 0 and seq_len % SCAN_STEPS == 0
    assert n_batch == V7X_SUBLANES and n_dec % (2 * V7X_SUBLANES) == 0 and dec_len <= CHUNK
    assert state_p == STATE_P and b_re.shape[-1] == SSM_GROUP and w_s.shape[1] == N_HEADS

    rows3 = lambda a: a.reshape(a.shape[0], 1, -1)
    npre, npost = rows3(norm_pre), rows3(norm_post)
    lng, lnb = rows3(ln_v_g), rows3(ln_v_b)
    w_in_a16, w_out_a16 = w_in_a.astype(BF16), w_out_a.astype(BF16)
    bst = jnp.swapaxes(b_s, 1, 2)
    ws_dec = w_s[:, :, :dec_len, :dec_len].reshape(-1)
    bs_dec = b_s[:, :, :dec_len].reshape(-1)

    w_in_b16 = w_in_b.astype(BF16)
    w1, w2, w_out_b16 = w_glu1.astype(BF16), w_glu2.astype(BF16), w_out_b.astype(BF16)
    b1, b2, dsk = rows3(b_glu1), rows3(b_glu2), rows3(d_skip)
    lanes_gp = lambda a, perm: jnp.transpose(a, perm).reshape(n_ssm, SSM_GROUP, n_state)
    abr, abi, wbr, wbi, wcr, wci = _s5_prep(
        rows3(a_re), rows3(a_im), rows3(jnp.repeat(log_dt, state_p, axis=1)),
        lanes_gp(b_re, (0, 3, 1, 2)), lanes_gp(b_im, (0, 3, 1, 2)),
        lanes_gp(c_re, (0, 2, 1, 3)), lanes_gp(c_im, (0, 2, 1, 3)))
    core_consts = (abr, abi, wbr, wbi, wcr, wci, dsk)
    h0 = (state_ssm_re.reshape(n_ssm, n_dec, n_state), state_ssm_im.reshape(n_ssm, n_dec, n_state))

    xp = x_prompt.reshape(n_batch * seq_len, d)
    xs = jnp.transpose(x_sample, (1, 0, 2)).reshape(dec_len * n_dec, d)
    n_dec_rows = dec_len * n_dec

    v_rows, st_p_re, st_p_im, st_s_re, st_s_im = [], [], [], [], []
    for i in range(depth):
        j = i // 2
        if i % 2 == 0:
            xp = _gmlp_prompt(xp, j, i, npre, npost, lng, lnb, w_in_a16, w_out_a16, w_s, bst)
            xs, v = _gmlp_sample(xs, n_dec, dec_len, j, i, npre, npost, lng, lnb,
                                 w_in_a16, w_out_a16, ws_dec, bs_dec)
            v_rows.append(v)
        else:
            exp_b = w_in_b.shape[2] // 2
            xb, z = _s5_in(xp, ROW_TILE, j, i, npre, w_in_b16, "s5_in_prompt")
            y, sr, si = _s5_core(xb.reshape(n_batch, seq_len, exp_b), None, j, core_consts,
                                 n_batch, SCAN_STEPS)
            xp = _s5_out(y.reshape(n_batch * seq_len, exp_b), z, xp, ROW_TILE, j, i,
                         w1, b1, w2, b2, w_out_b16, npost, "s5_out_prompt")
            st_p_re.append(sr)
            st_p_im.append(si)

            xbs, zs = _s5_in(xs, n_dec_rows, j, i, npre, w_in_b16, "s5_in_sample")
            ys, sr, si = _s5_core(xbs, h0, j, core_consts, n_dec, dec_len)
            xs = _s5_out(ys, zs, xs, n_dec_rows, j, i, w1, b1, w2, b2, w_out_b16, npost,
                         "s5_out_sample")
            st_s_re.append(sr)
            st_s_im.append(si)

    y_prompt = xp.reshape(n_batch, seq_len, d)
    y_sample = jnp.transpose(xs.reshape(dec_len, n_dec, d), (1, 0, 2))
    chunk_v = jnp.transpose(jnp.stack(v_rows).reshape(len(v_rows), dec_len, n_dec, -1), (0, 2, 1, 3))
    states = lambda parts, n: jnp.stack(parts).reshape(len(parts), n, n_groups, state_p)
    return (y_prompt, y_sample, chunk_v, states(st_p_re, n_batch), states(st_p_im, n_batch),
            states(st_s_re, n_dec), states(st_s_im, n_dec))
```

```python
import functools

import jax
import jax.numpy as jnp
from jax import lax
from jax.experimental import pallas as pl
from jax.experimental.pallas import tpu as pltpu

EPS = 1e-6
CHUNK = 128
N_HEADS = 8
SSM_GROUP = 16
STATE_P = 64

V7X_LANES = 128
V7X_SUBLANES = 8
V7X_MXU_DIM = 256
V7X_VMEM_BYTES = 64 * 1024 * 1024

ROW_TILE = 512
SCAN_STEPS = 128
BLOCK = 4

BF16 = jnp.bfloat16
F32 = jnp.float32


def _dot(a, b):
    return jnp.dot(a, b, preferred_element_type=F32)


def _rmsnorm(x, g):
    ms = jnp.mean(x * x, axis=-1, keepdims=True)
    return x * lax.rsqrt(ms + EPS) * g


def _layernorm(x, g, b):
    mu = jnp.mean(x, axis=-1, keepdims=True)
    xc = x - mu
    var = jnp.mean(xc * xc, axis=-1, keepdims=True)
    return xc * lax.rsqrt(var + EPS) * g + b


def _silu(z):
    return z * jax.nn.sigmoid(z)


def _div_pow2(x, n):
    assert n & (n - 1) == 0
    return lax.shift_right_logical(x, n.bit_length() - 1)


def _nbytes(*arrays):
    return sum(a.size * a.dtype.itemsize for a in arrays)


def _vmem_limit(nbytes):
    return int(min(V7X_VMEM_BYTES - (4 << 20), nbytes + (12 << 20)))


def _layer_spec(stacked, layer):
    tail = stacked.shape[1:]
    return pl.BlockSpec((None,) + tail, lambda *_: (layer,) + (0,) * len(tail),
                        pipeline_mode=pl.Buffered(1))


def _layer_bytes(*stacked):
    return sum(a[0].size * a.dtype.itemsize for a in stacked)


def _row_spec(rows, width):
    return pl.BlockSpec((rows, width), lambda i: (i, 0))


_SMEM = pl.BlockSpec(memory_space=pltpu.SMEM)


def _gmlp_front(x, npre_ref, lng_ref, lnb_ref, win_ref, exp_a):
    hn = _rmsnorm(x, npre_ref[...]).astype(BF16)
    v = _dot(hn, win_ref[:, exp_a:2 * exp_a])
    v = _layernorm(v, lng_ref[...], lnb_ref[...])
    return hn, v


def _gmlp_prompt_kernel(x_ref, npre_ref, npost_ref, lng_ref, lnb_ref, win_ref, wout_ref,
                        ws_ref, bst_ref, o_ref, gated_ref):
    rows = x_ref.shape[0]
    exp_a = wout_ref.shape[0]
    hd = exp_a // N_HEADS
    x = x_ref[...]
    hn, v = _gmlp_front(x, npre_ref, lng_ref, lnb_ref, win_ref, exp_a)
    vb = v.astype(BF16)
    causal = (lax.broadcasted_iota(jnp.int32, (CHUNK, CHUNK), 1)
              <= lax.broadcasted_iota(jnp.int32, (CHUNK, CHUNK), 0))
    for h in range(N_HEADS):
        lo = h * hd
        wsh = jnp.where(causal, ws_ref[h], 0.0).astype(BF16)
        bias = bst_ref[:, h:h + 1]
        u = _dot(hn, win_ref[:, lo:lo + hd])
        z = _dot(hn, win_ref[:, 2 * exp_a + lo:2 * exp_a + lo + hd])
        s = jnp.concatenate(
            [_dot(wsh, vb[c * CHUNK:(c + 1) * CHUNK, lo:lo + hd]) + bias
             for c in range(rows // CHUNK)], axis=0)
        gated_ref[:, lo:lo + hd] = (u * s * _silu(z)).astype(BF16)
    out = _dot(gated_ref[...], wout_ref[...])
    o_ref[...] = x + _rmsnorm(out, npost_ref[...])


def _gmlp_sample_kernel(ws_ref, bs_ref, x_ref, npre_ref, npost_ref, lng_ref, lnb_ref, win_ref,
                        wout_ref, o_ref, v_ref, gated_ref, *, n_seq, seq_len, layer):
    exp_a = wout_ref.shape[0]
    hd = exp_a // N_HEADS
    x = x_ref[...]
    hn, v = _gmlp_front(x, npre_ref, lng_ref, lnb_ref, win_ref, exp_a)
    v_ref[...] = v
    for h in range(N_HEADS):
        lo = h * hd
        u = _dot(hn, win_ref[:, lo:lo + hd])
        z = _dot(hn, win_ref[:, 2 * exp_a + lo:2 * exp_a + lo + hd])
        vt = [v[t * n_seq:(t + 1) * n_seq, lo:lo + hd] for t in range(seq_len)]
        parts = []
        for t in range(seq_len):
            b_idx = (layer * N_HEADS + h) * seq_len + t
            s = ws_ref[b_idx * seq_len] * vt[0]
            for t2 in range(1, t + 1):
                s = s + ws_ref[b_idx * seq_len + t2] * vt[t2]
            parts.append(s + bs_ref[b_idx])
        s = jnp.concatenate(parts, axis=0)
        gated_ref[:, lo:lo + hd] = (u * s * _silu(z)).astype(BF16)
    out = _dot(gated_ref[...], wout_ref[...])
    o_ref[...] = x + _rmsnorm(out, npost_ref[...])


def _gmlp_prompt(x2d, layer_a, layer, npre, npost, lng, lnb, win, wout, ws, bst):
    rows, d = x2d.shape
    exp_a = wout.shape[1]
    footprint = (_layer_bytes(win, wout, ws) + 4 * ROW_TILE * d * 4
                 + ROW_TILE * exp_a * (2 + 4 + 2 + 8))
    return pl.pallas_call(
        _gmlp_prompt_kernel,
        grid=(rows // ROW_TILE,),
        in_specs=[
            _row_spec(ROW_TILE, d),
            _layer_spec(npre, layer), _layer_spec(npost, layer),
            _layer_spec(lng, layer_a), _layer_spec(lnb, layer_a),
            _layer_spec(win, layer_a), _layer_spec(wout, layer_a),
            _layer_spec(ws, layer_a), _layer_spec(bst, layer_a),
        ],
        out_specs=_row_spec(ROW_TILE, d),
        out_shape=jax.ShapeDtypeStruct((rows, d), F32),
        scratch_shapes=[pltpu.VMEM((ROW_TILE, exp_a), BF16)],
        compiler_params=pltpu.CompilerParams(
            dimension_semantics=("parallel",),
            vmem_limit_bytes=_vmem_limit(footprint)),
        name="gmlp_prompt",
    )(x2d, npre, npost, lng, lnb, win, wout, ws, bst)


def _gmlp_sample(xs, n_seq, seq_len, layer_a, layer, npre, npost, lng, lnb, win, wout,
                 ws_flat, bs_flat):
    rows, d = xs.shape
    exp_a = wout.shape[1]
    footprint = _layer_bytes(win, wout) + 4 * rows * d * 4 + rows * exp_a * (2 + 8 + 4 + 8)
    return pl.pallas_call(
        functools.partial(_gmlp_sample_kernel, n_seq=n_seq, seq_len=seq_len, layer=layer_a),
        grid=(1,),
        in_specs=[_SMEM, _SMEM, _row_spec(rows, d),
                  _layer_spec(npre, layer), _layer_spec(npost, layer),
                  _layer_spec(lng, layer_a), _layer_spec(lnb, layer_a),
                  _layer_spec(win, layer_a), _layer_spec(wout, layer_a)],
        out_specs=[_row_spec(rows, d), _row_spec(rows, exp_a)],
        out_shape=[jax.ShapeDtypeStruct((rows, d), F32),
                   jax.ShapeDtypeStruct((rows, exp_a), F32)],
        scratch_shapes=[pltpu.VMEM((rows, exp_a), BF16)],
        compiler_params=pltpu.CompilerParams(
            dimension_semantics=("arbitrary",),
            vmem_limit_bytes=_vmem_limit(footprint)),
        name="gmlp_sample",
    )(ws_flat, bs_flat, xs, npre, npost, lng, lnb, win, wout)


def _s5_prep_kernel(are_ref, aim_ref, ldt_ref, btr_ref, bti_ref, ctr_ref, cti_ref,
                    l4r_ref, l4i_ref, wur_ref, wui_ref, vc_ref, kloc_ref):
    dt = jnp.exp(ldt_ref[...])
    ar = are_ref[...]
    ai = aim_ref[...]
    mag = jnp.exp(dt * ar)
    ang = dt * ai
    abr = mag * jnp.cos(ang)
    abi = mag * jnp.sin(ang)
    nr = abr - 1.0
    ni = abi
    den = ar * ar + ai * ai
    cre = (nr * ar + ni * ai) / den
    cim = (ni * ar - nr * ai) / den
    btr = btr_ref[...]
    bti = bti_ref[...]
    bbr = cre * btr - cim * bti
    bbi = cre * bti + cim * btr

    def cmul(xr, xi, yr, yi):
        return xr * yr - xi * yi, xr * yi + xi * yr

    lam = [(jnp.ones_like(abr), jnp.zeros_like(abi)), (abr, abi)]
    for _ in range(BLOCK - 1):
        lam.append(cmul(*lam[-1], abr, abi))
    l4r_ref[...] = lam[BLOCK][0]
    l4i_ref[...] = lam[BLOCK][1]

    n_q, k_rows, n_cols = wur_ref.shape
    q_groups = n_cols // STATE_P
    q_rows = q_groups * SSM_GROUP
    same_q = (_div_pow2(lax.broadcasted_iota(jnp.int32, (q_rows, n_cols), 0), SSM_GROUP)
              == _div_pow2(lax.broadcasted_iota(jnp.int32, (q_rows, n_cols), 1), STATE_P))
    for ip in range(BLOCK):
        ur, ui = cmul(bbr, bbi, *lam[BLOCK - 1 - ip])
        for q in range(n_q):
            for src, dst in ((ur, wur_ref), (ui, wui_ref)):
                blk = jnp.concatenate([src[:, q * n_cols:(q + 1) * n_cols]] * q_groups, axis=0)
                dst[q, ip * q_rows:(ip + 1) * q_rows, :] = jnp.where(same_q, blk, 0.0).astype(BF16)

    n_v, n_pair, k_state, n_out = vc_ref.shape
    v_lanes = k_state // 2
    v_groups = v_lanes // STATE_P
    same_v = (_div_pow2(lax.broadcasted_iota(jnp.int32, (V7X_LANES, v_lanes), 0), SSM_GROUP)
              == _div_pow2(lax.broadcasted_iota(jnp.int32, (V7X_LANES, v_lanes), 1), STATE_P))
    ctr = ctr_ref[...]
    cti = cti_ref[...]
    cl = [cmul(ctr, cti, *lam[t]) for t in range(BLOCK + 1)]

    def block_rows(a, v):
        blk = jnp.concatenate([a[:, v * v_lanes:(v + 1) * v_lanes]] * v_groups, axis=0)
        return jnp.where(same_v, blk, 0.0)

    for v in range(n_v):
        clm = [(block_rows(c[0], v), block_rows(c[1], v)) for c in cl]
        for pr in range(n_pair):
            rows = [jnp.concatenate([clm[pr * 2 + il + 1][0], -clm[pr * 2 + il + 1][1]], axis=1)
                    for il in range(2)]
            vc_ref[v, pr] = jnp.concatenate(rows, axis=0).T.astype(BF16)
        bcat = jnp.concatenate([block_rows(bbr, v), -block_rows(bbi, v)], axis=1)
        kt = [lax.dot_general(bcat, jnp.concatenate(clm[t], axis=1), (((1,), (1,)), ((), ())),
                              precision=lax.Precision.HIGHEST, preferred_element_type=F32)
              for t in range(BLOCK)]
        zero = jnp.zeros_like(kt[0])
        for pr in range(n_pair):
            kloc_ref[v, pr] = jnp.concatenate(
                [jnp.concatenate([kt[i - ip] if i >= ip else zero for i in (2 * pr, 2 * pr + 1)],
                                 axis=1) for ip in range(BLOCK)], axis=0).astype(BF16)


def _s5_prep(are, aim, ldt, btr, bti, ctr, cti):
    n_layers, _, n_state = are.shape
    n_ch = n_state // STATE_P * SSM_GROUP
    n_q = n_state // V7X_MXU_DIM
    n_v = n_ch // V7X_LANES
    v_lanes = n_state // n_v

    def per_layer(shape):
        return pl.BlockSpec((None,) + shape, lambda l: (l,) + (0,) * len(shape))

    out_tails = [(1, n_state), (1, n_state),
                 (n_q, V7X_MXU_DIM, V7X_MXU_DIM), (n_q, V7X_MXU_DIM, V7X_MXU_DIM),
                 (n_v, BLOCK // 2, 2 * v_lanes, V7X_MXU_DIM),
                 (n_v, BLOCK // 2, BLOCK * V7X_LANES, V7X_MXU_DIM)]
    out_dtypes = [F32, F32, BF16, BF16, BF16, BF16]
    args = (are, aim, ldt, btr, bti, ctr, cti)
    return pl.pallas_call(
        _s5_prep_kernel,
        grid=(n_layers,),
        in_specs=[per_layer(a.shape[1:]) for a in args],
        out_specs=[per_layer(t) for t in out_tails],
        out_shape=[jax.ShapeDtypeStruct((n_layers,) + t, dt) for t, dt in zip(out_tails, out_dtypes)],
        compiler_params=pltpu.CompilerParams(
            dimension_semantics=("parallel",),
            vmem_limit_bytes=_vmem_limit(2 * sum(
                int(jnp.dtype(dt).itemsize) * functools.reduce(lambda a, b: a * b, t)
                for t, dt in zip(out_tails, out_dtypes)))),
        name="s5_prep",
    )(*args)


def _s5_in_kernel(x_ref, npre_ref, win_ref, xb_ref, z_ref):
    d_in = xb_ref.shape[-1]
    hn = _rmsnorm(x_ref[...], npre_ref[...]).astype(BF16)
    xb_ref[...] = _dot(hn, win_ref[:, :d_in])
    z_ref[...] = _dot(hn, win_ref[:, d_in:])


def _s5_in(x2d, tile, layer_b, layer, npre, win, name):
    rows, d = x2d.shape
    exp_b = win.shape[2] // 2
    footprint = _layer_bytes(win) + 6 * tile * d * 4 + 4 * tile * exp_b * 4
    return pl.pallas_call(
        _s5_in_kernel,
        grid=(rows // tile,),
        in_specs=[_row_spec(tile, d), _layer_spec(npre, layer), _layer_spec(win, layer_b)],
        out_specs=[_row_spec(tile, exp_b), _row_spec(tile, exp_b)],
        out_shape=[jax.ShapeDtypeStruct((rows, exp_b), F32)] * 2,
        compiler_params=pltpu.CompilerParams(
            dimension_semantics=("parallel",),
            vmem_limit_bytes=_vmem_limit(footprint)),
        name=name,
    )(x2d, npre, win)


def _s5_core_kernel(*refs, n_seq, n_steps, has_h0, swap_bt):
    if has_h0:
        (xb_ref, h0r_ref, h0i_ref, l4r_ref, l4i_ref, wur_ref, wui_ref, vc_ref, kloc_ref, dsk_ref,
         y_ref, str_ref, sti_ref, hpr_ref, hpi_ref, hr_ref, hi_ref) = refs
    else:
        (xb_ref, l4r_ref, l4i_ref, wur_ref, wui_ref, vc_ref, kloc_ref, dsk_ref,
         y_ref, str_ref, sti_ref, hpr_ref, hpi_ref, hr_ref, hi_ref) = refs
    step = pl.program_id(0)

    @pl.when(step == 0)
    def _():
        if has_h0:
            hr_ref[...] = h0r_ref[...]
            hi_ref[...] = h0i_ref[...]
        else:
            hr_ref[...] = jnp.zeros_like(hr_ref)
            hi_ref[...] = jnp.zeros_like(hi_ref)

    xb = xb_ref[...]
    if swap_bt:
        xb = jnp.swapaxes(xb, 0, 1)
    width = xb.shape[-1]
    n_blocks = n_steps // BLOCK
    rows = n_blocks * n_seq
    x4 = xb.reshape(n_blocks, BLOCK, n_seq, width)
    xi = [x4[:, i].reshape(rows, width) for i in range(BLOCK)]

    sub = V7X_SUBLANES
    pair = 2 * sub
    n_q, _, n_cols = wur_ref.shape
    n_v = width // V7X_LANES
    half = V7X_LANES // 2
    low = lax.broadcasted_iota(jnp.int32, (rows, V7X_LANES), 1) < half

    for v in range(n_v):
        cols = slice(v * V7X_LANES, (v + 1) * V7X_LANES)
        p = [x[:, cols] for x in xi]
        pr = [pltpu.roll(a, half, 1) for a in p]
        lhs_lo = jnp.concatenate([jnp.where(low, p[i], pr[i + 1]) for i in range(0, BLOCK, 2)], axis=1)
        lhs_hi = jnp.concatenate([jnp.where(low, pr[i], p[i + 1]) for i in range(0, BLOCK, 2)], axis=1)
        for q, lhs in ((2 * v, lhs_lo.astype(BF16)), (2 * v + 1, lhs_hi.astype(BF16))):
            lanes = pl.ds(q * n_cols, n_cols)
            wr = _dot(lhs, wur_ref[q])
            wi = _dot(lhs, wui_ref[q])
            if n_seq == sub:
                ar = jnp.broadcast_to(l4r_ref[:, lanes], (sub, n_cols))
                ai = jnp.broadcast_to(l4i_ref[:, lanes], (sub, n_cols))
                hr, hi = hr_ref[:, lanes], hi_ref[:, lanes]
                for k in range(n_blocks // 2):
                    prv_r, prv_i = [], []
                    for r0 in (2 * k * sub, (2 * k + 1) * sub):
                        prv_r.append(hr)
                        prv_i.append(hi)
                        hr, hi = (ar * hr - ai * hi + wr[r0:r0 + sub],
                                  ar * hi + ai * hr + wi[r0:r0 + sub])
                    hpr_ref[k * pair:(k + 1) * pair, lanes] = jnp.concatenate(prv_r, axis=0).astype(BF16)
                    hpi_ref[k * pair:(k + 1) * pair, lanes] = jnp.concatenate(prv_i, axis=0).astype(BF16)
                hr_ref[:, lanes] = hr
                hi_ref[:, lanes] = hi
            else:
                ar = jnp.broadcast_to(l4r_ref[:, lanes], (pair, n_cols))
                ai = jnp.broadcast_to(l4i_ref[:, lanes], (pair, n_cols))
                for m in range(n_seq // pair):
                    srows = pl.ds(m * pair, pair)
                    hr, hi = hr_ref[srows, lanes], hi_ref[srows, lanes]
                    for blk in range(n_blocks):
                        r0 = blk * n_seq + m * pair
                        hpr_ref[r0:r0 + pair, lanes] = hr.astype(BF16)
                        hpi_ref[r0:r0 + pair, lanes] = hi.astype(BF16)
                        hr, hi = (ar * hr - ai * hi + wr[r0:r0 + pair],
                                  ar * hi + ai * hr + wi[r0:r0 + pair])
                    hr_ref[srows, lanes] = hr
                    hi_ref[srows, lanes] = hi

    v_lanes = vc_ref.shape[2] // 2
    for v in range(n_v):
        cols = slice(v * V7X_LANES, (v + 1) * V7X_LANES)
        st = jnp.concatenate([hpr_ref[:, v * v_lanes:(v + 1) * v_lanes],
                              hpi_ref[:, v * v_lanes:(v + 1) * v_lanes]], axis=1)
        xloc = [x[:, cols].astype(BF16) for x in xi]
        ys = []
        for pr_ in range(BLOCK // 2):
            n_loc = 2 * pr_ + 2
            yy = (_dot(st, vc_ref[v, pr_])
                  + _dot(jnp.concatenate(xloc[:n_loc], axis=1), kloc_ref[v, pr_, :n_loc * V7X_LANES, :]))
            for il in range(2):
                i = 2 * pr_ + il
                ys.append(jax.nn.gelu(yy[:, il * V7X_LANES:(il + 1) * V7X_LANES]
                                      + dsk_ref[:, cols] * xi[i][:, cols]))
        y = jnp.stack([a.reshape(n_blocks, n_seq, V7X_LANES) for a in ys], axis=1)
        y = y.reshape(n_steps, n_seq, V7X_LANES)
        if swap_bt:
            y_ref[:, :, cols] = jnp.swapaxes(y, 0, 1).astype(BF16)
        else:
            y_ref[:, cols] = y.reshape(n_steps * n_seq, V7X_LANES).astype(BF16)

    @pl.when(step == pl.num_programs(0) - 1)
    def _():
        str_ref[...] = hr_ref[...]
        sti_ref[...] = hi_ref[...]


def _s5_core(xb, h0, layer_b, consts, n_seq, n_steps):
    swap_bt = xb.ndim == 3
    exp_b = xb.shape[-1]
    n_state = consts[0].shape[-1]
    tile = n_seq * n_steps
    n_tiles = xb.size // (tile * exp_b)
    rows = tile // BLOCK
    has_h0 = h0 is not None
    assert n_steps % BLOCK == 0 and (n_seq > V7X_SUBLANES or (n_steps // BLOCK) % 2 == 0)
    args = [xb] + (list(h0) if has_h0 else []) + list(consts)
    if swap_bt:
        io_spec = pl.BlockSpec((n_seq, n_steps, exp_b), lambda i: (0, i, 0))
    else:
        io_spec = pl.BlockSpec((tile, exp_b), lambda i: (i, 0))
    in_specs = [io_spec] + [_layer_spec(a, layer_b) for a in args[1:]]
    st_spec = pl.BlockSpec((n_seq, n_state), lambda i: (0, 0))
    st_shape = jax.ShapeDtypeStruct((n_seq, n_state), F32)
    footprint = (_layer_bytes(*args[1:]) + 2 * tile * exp_b * (4 + 2) + 2 * rows * n_state * 2
                 + 8 * n_seq * n_state * 4 + 3 * tile * exp_b * 4)
    return pl.pallas_call(
        functools.partial(_s5_core_kernel, n_seq=n_seq, n_steps=n_steps, has_h0=has_h0,
                          swap_bt=swap_bt),
        grid=(n_tiles,),
        in_specs=in_specs,
        out_specs=[io_spec, st_spec, st_spec],
        out_shape=[jax.ShapeDtypeStruct(xb.shape, BF16), st_shape, st_shape],
        scratch_shapes=[pltpu.VMEM((rows, n_state), BF16), pltpu.VMEM((rows, n_state), BF16),
                        pltpu.VMEM((n_seq, n_state), F32), pltpu.VMEM((n_seq, n_state), F32)],
        compiler_params=pltpu.CompilerParams(
            dimension_semantics=("arbitrary",),
            vmem_limit_bytes=_vmem_limit(footprint)),
        name="s5_core_sample" if has_h0 else "s5_core_prompt",
    )(*args)


def _s5_out_kernel(y_ref, z_ref, x_ref, w1_ref, b1_ref, w2_ref, b2_ref, wout_ref, npost_ref, o_ref):
    y = y_ref[...]
    g = (_dot(y, w1_ref[...]) + b1_ref[...]) * jax.nn.sigmoid(_dot(y, w2_ref[...]) + b2_ref[...])
    out = _dot((g * _silu(z_ref[...])).astype(BF16), wout_ref[...])
    o_ref[...] = x_ref[...] + _rmsnorm(out, npost_ref[...])


def _s5_out(y, z, x2d, tile, layer_b, layer, w1, b1, w2, b2, wout, npost, name):
    rows, d = x2d.shape
    exp_b = z.shape[1]
    footprint = (_layer_bytes(w1, w2, wout) + 2 * tile * (exp_b * 6 + d * 8) + 6 * tile * exp_b * 4)
    return pl.pallas_call(
        _s5_out_kernel,
        grid=(rows // tile,),
        in_specs=[_row_spec(tile, exp_b), _row_spec(tile, exp_b), _row_spec(tile, d),
                  _layer_spec(w1, layer_b), _layer_spec(b1, layer_b),
                  _layer_spec(w2, layer_b), _layer_spec(b2, layer_b),
                  _layer_spec(wout, layer_b), _layer_spec(npost, layer)],
        out_specs=_row_spec(tile, d),
        out_shape=jax.ShapeDtypeStruct((rows, d), F32),
        compiler_params=pltpu.CompilerParams(
            dimension_semantics=("parallel",),
            vmem_limit_bytes=_vmem_limit(footprint)),
        name=name,
    )(y, z, x2d, w1, b1, w2, b2, wout, npost)


def kernel(x_prompt, x_sample, state_ssm_re, state_ssm_im, norm_pre, norm_post,
           w_in_a, ln_v_g, ln_v_b, w_s, b_s, w_out_a,
           w_in_b, a_re, a_im, log_dt, b_re, b_im, c_re, c_im, d_skip,
           w_glu1, b_glu1, w_glu2, b_glu2, w_out_b):
    n_batch, seq_len, d = x_prompt.shape
    n_dec, dec_len, _ = x_sample.shape
    depth = norm_pre.shape[0]
    n_ssm, n_groups, state_p = a_re.shape
    n_state = n_groups * state_p
    assert seq_len % ROW_TILE == 0 and ROW_TILE % CHUNK == 0 and seq_len % SCAN_STEPS == 0
    assert n_batch == V7X_SUBLANES and n_dec % (2 * V7X_SUBLANES) == 0
    assert dec_len <= CHUNK and dec_len % BLOCK == 0
    assert state_p == STATE_P and b_re.shape[-1] == SSM_GROUP and w_s.shape[1] == N_HEADS

    rows3 = lambda a: a.reshape(a.shape[0], 1, -1)
    npre, npost = rows3(norm_pre), rows3(norm_post)
    lng, lnb = rows3(ln_v_g), rows3(ln_v_b)
    w_in_a16, w_out_a16 = w_in_a.astype(BF16), w_out_a.astype(BF16)
    bst = jnp.swapaxes(b_s, 1, 2)
    ws_dec = w_s[:, :, :dec_len, :dec_len].reshape(-1)
    bs_dec = b_s[:, :, :dec_len].reshape(-1)

    w_in_b16 = w_in_b.astype(BF16)
    w1, w2, w_out_b16 = w_glu1.astype(BF16), w_glu2.astype(BF16), w_out_b.astype(BF16)
    b1, b2, dsk = rows3(b_glu1), rows3(b_glu2), rows3(d_skip)
    lanes_gp = lambda a, perm: jnp.transpose(a, perm).reshape(n_ssm, SSM_GROUP, n_state)
    l4r, l4i, wur, wui, vcw, kloc = _s5_prep(
        rows3(a_re), rows3(a_im), rows3(jnp.repeat(log_dt, state_p, axis=1)),
        lanes_gp(b_re, (0, 3, 1, 2)), lanes_gp(b_im, (0, 3, 1, 2)),
        lanes_gp(c_re, (0, 2, 1, 3)), lanes_gp(c_im, (0, 2, 1, 3)))
    core_consts = (l4r, l4i, wur, wui, vcw, kloc, dsk)
    h0 = (state_ssm_re.reshape(n_ssm, n_dec, n_state), state_ssm_im.reshape(n_ssm, n_dec, n_state))

    xp = x_prompt.reshape(n_batch * seq_len, d)
    xs = jnp.transpose(x_sample, (1, 0, 2)).reshape(dec_len * n_dec, d)
    n_dec_rows = dec_len * n_dec

    v_rows, st_p_re, st_p_im, st_s_re, st_s_im = [], [], [], [], []
    for i in range(depth):
        j = i // 2
        if i % 2 == 0:
            xp = _gmlp_prompt(xp, j, i, npre, npost, lng, lnb, w_in_a16, w_out_a16, w_s, bst)
            xs, v = _gmlp_sample(xs, n_dec, dec_len, j, i, npre, npost, lng, lnb,
                                 w_in_a16, w_out_a16, ws_dec, bs_dec)
            v_rows.append(v)
        else:
            exp_b = w_in_b.shape[2] // 2
            xb, z = _s5_in(xp, ROW_TILE, j, i, npre, w_in_b16, "s5_in_prompt")
            y, sr, si = _s5_core(xb.reshape(n_batch, seq_len, exp_b), None, j, core_consts,
                                 n_batch, SCAN_STEPS)
            xp = _s5_out(y.reshape(n_batch * seq_len, exp_b), z, xp, ROW_TILE, j, i,
                         w1, b1, w2, b2, w_out_b16, npost, "s5_out_prompt")
            st_p_re.append(sr)
            st_p_im.append(si)

            xbs, zs = _s5_in(xs, n_dec_rows, j, i, npre, w_in_b16, "s5_in_sample")
            ys, sr, si = _s5_core(xbs, h0, j, core_consts, n_dec, dec_len)
            xs = _s5_out(ys, zs, xs, n_dec_rows, j, i, w1, b1, w2, b2, w_out_b16, npost,
                         "s5_out_sample")
            st_s_re.append(sr)
            st_s_im.append(si)

    y_prompt = xp.reshape(n_batch, seq_len, d)
    y_sample = jnp.transpose(xs.reshape(dec_len, n_dec, d), (1, 0, 2))
    chunk_v = jnp.transpose(jnp.stack(v_rows).reshape(len(v_rows), dec_len, n_dec, -1), (0, 2, 1, 3))
    states = lambda parts, n: jnp.stack(parts).reshape(len(parts), n, n_groups, state_p)
    return (y_prompt, y_sample, chunk_v, states(st_p_re, n_batch), states(st_p_im, n_batch),
            states(st_s_re, n_dec), states(st_s_im, n_dec))
```

```python
import functools

import jax
import jax.numpy as jnp
from jax import lax
from jax.experimental import pallas as pl
from jax.experimental.pallas import tpu as pltpu

EPS = 1e-6
CHUNK = 128
N_HEADS = 8
SSM_GROUP = 16
STATE_P = 64

V7X_LANES = 128
V7X_SUBLANES = 8
V7X_MXU_DIM = 256
V7X_VMEM_BYTES = 64 * 1024 * 1024

ROW_TILE = 512
S5_ROW_TILE = 1024
SCAN_STEPS = 128
BLOCK = 4

BF16 = jnp.bfloat16
F32 = jnp.float32


def _dot(a, b):
    return jnp.dot(a, b, preferred_element_type=F32)


def _rmsnorm(x, g):
    ms = jnp.mean(x * x, axis=-1, keepdims=True)
    return x * lax.rsqrt(ms + EPS) * g


def _layernorm(x, g, b):
    mu = jnp.mean(x, axis=-1, keepdims=True)
    xc = x - mu
    var = jnp.mean(xc * xc, axis=-1, keepdims=True)
    return xc * lax.rsqrt(var + EPS) * g + b


def _silu(z):
    return z * jax.nn.sigmoid(z)


def _div_pow2(x, n):
    assert n & (n - 1) == 0
    return lax.shift_right_logical(x, n.bit_length() - 1)


def _nbytes(*arrays):
    return sum(a.size * a.dtype.itemsize for a in arrays)


def _vmem_limit(nbytes):
    return int(min(V7X_VMEM_BYTES - (4 << 20), nbytes + (12 << 20)))


def _layer_spec(stacked, layer):
    tail = stacked.shape[1:]
    return pl.BlockSpec((None,) + tail, lambda *_: (layer,) + (0,) * len(tail),
                        pipeline_mode=pl.Buffered(1))


def _layer_bytes(*stacked):
    return sum(a[0].size * a.dtype.itemsize for a in stacked)


def _row_spec(rows, width):
    return pl.BlockSpec((rows, width), lambda i: (i, 0))


_SMEM = pl.BlockSpec(memory_space=pltpu.SMEM)


def _gmlp_front(x, npre_ref, lng_ref, lnb_ref, win_ref, exp_a):
    hn = _rmsnorm(x, npre_ref[...]).astype(BF16)
    v = _dot(hn, win_ref[:, exp_a:2 * exp_a])
    v = _layernorm(v, lng_ref[...], lnb_ref[...])
    return hn, v


def _gmlp_prompt_kernel(x_ref, npre_ref, npost_ref, lng_ref, lnb_ref, win_ref, wout_ref,
                        ws_ref, bst_ref, o_ref, gated_ref):
    rows = x_ref.shape[0]
    exp_a = wout_ref.shape[0]
    hd = exp_a // N_HEADS
    x = x_ref[...]
    hn, v = _gmlp_front(x, npre_ref, lng_ref, lnb_ref, win_ref, exp_a)
    vb = v.astype(BF16)
    causal = (lax.broadcasted_iota(jnp.int32, (CHUNK, CHUNK), 1)
              <= lax.broadcasted_iota(jnp.int32, (CHUNK, CHUNK), 0))
    for h in range(N_HEADS):
        lo = h * hd
        wsh = jnp.where(causal, ws_ref[h], 0.0).astype(BF16)
        bias = bst_ref[:, h:h + 1]
        u = _dot(hn, win_ref[:, lo:lo + hd])
        z = _dot(hn, win_ref[:, 2 * exp_a + lo:2 * exp_a + lo + hd])
        s = jnp.concatenate(
            [_dot(wsh, vb[c * CHUNK:(c + 1) * CHUNK, lo:lo + hd]) + bias
             for c in range(rows // CHUNK)], axis=0)
        gated_ref[:, lo:lo + hd] = (u * s * _silu(z)).astype(BF16)
    out = _dot(gated_ref[...], wout_ref[...])
    o_ref[...] = x + _rmsnorm(out, npost_ref[...])


def _gmlp_sample_kernel(ws_ref, bs_ref, x_ref, npre_ref, npost_ref, lng_ref, lnb_ref, win_ref,
                        wout_ref, o_ref, v_ref, gated_ref, *, n_seq, seq_len, layer):
    exp_a = wout_ref.shape[0]
    hd = exp_a // N_HEADS
    x = x_ref[...]
    hn, v = _gmlp_front(x, npre_ref, lng_ref, lnb_ref, win_ref, exp_a)
    v_ref[...] = v
    for h in range(N_HEADS):
        lo = h * hd
        u = _dot(hn, win_ref[:, lo:lo + hd])
        z = _dot(hn, win_ref[:, 2 * exp_a + lo:2 * exp_a + lo + hd])
        vt = [v[t * n_seq:(t + 1) * n_seq, lo:lo + hd] for t in range(seq_len)]
        parts = []
        for t in range(seq_len):
            b_idx = (layer * N_HEADS + h) * seq_len + t
            s = ws_ref[b_idx * seq_len] * vt[0]
            for t2 in range(1, t + 1):
                s = s + ws_ref[b_idx * seq_len + t2] * vt[t2]
            parts.append(s + bs_ref[b_idx])
        s = jnp.concatenate(parts, axis=0)
        gated_ref[:, lo:lo + hd] = (u * s * _silu(z)).astype(BF16)
    out = _dot(gated_ref[...], wout_ref[...])
    o_ref[...] = x + _rmsnorm(out, npost_ref[...])


def _gmlp_prompt(x2d, layer_a, layer, npre, npost, lng, lnb, win, wout, ws, bst):
    rows, d = x2d.shape
    exp_a = wout.shape[1]
    footprint = (_layer_bytes(win, wout, ws) + 4 * ROW_TILE * d * 4
                 + ROW_TILE * exp_a * (2 + 4 + 2 + 8))
    return pl.pallas_call(
        _gmlp_prompt_kernel,
        grid=(rows // ROW_TILE,),
        in_specs=[
            _row_spec(ROW_TILE, d),
            _layer_spec(npre, layer), _layer_spec(npost, layer),
            _layer_spec(lng, layer_a), _layer_spec(lnb, layer_a),
            _layer_spec(win, layer_a), _layer_spec(wout, layer_a),
            _layer_spec(ws, layer_a), _layer_spec(bst, layer_a),
        ],
        out_specs=_row_spec(ROW_TILE, d),
        out_shape=jax.ShapeDtypeStruct((rows, d), F32),
        scratch_shapes=[pltpu.VMEM((ROW_TILE, exp_a), BF16)],
        compiler_params=pltpu.CompilerParams(
            dimension_semantics=("parallel",),
            vmem_limit_bytes=_vmem_limit(footprint)),
        name="gmlp_prompt",
    )(x2d, npre, npost, lng, lnb, win, wout, ws, bst)


def _gmlp_sample(xs, n_seq, seq_len, layer_a, layer, npre, npost, lng, lnb, win, wout,
                 ws_flat, bs_flat):
    rows, d = xs.shape
    exp_a = wout.shape[1]
    footprint = _layer_bytes(win, wout) + 4 * rows * d * 4 + rows * exp_a * (2 + 8 + 4 + 8)
    return pl.pallas_call(
        functools.partial(_gmlp_sample_kernel, n_seq=n_seq, seq_len=seq_len, layer=layer_a),
        grid=(1,),
        in_specs=[_SMEM, _SMEM, _row_spec(rows, d),
                  _layer_spec(npre, layer), _layer_spec(npost, layer),
                  _layer_spec(lng, layer_a), _layer_spec(lnb, layer_a),
                  _layer_spec(win, layer_a), _layer_spec(wout, layer_a)],
        out_specs=[_row_spec(rows, d), _row_spec(rows, exp_a)],
        out_shape=[jax.ShapeDtypeStruct((rows, d), F32),
                   jax.ShapeDtypeStruct((rows, exp_a), F32)],
        scratch_shapes=[pltpu.VMEM((rows, exp_a), BF16)],
        compiler_params=pltpu.CompilerParams(
            dimension_semantics=("arbitrary",),
            vmem_limit_bytes=_vmem_limit(footprint)),
        name="gmlp_sample",
    )(ws_flat, bs_flat, xs, npre, npost, lng, lnb, win, wout)


def _s5_prep_kernel(are_ref, aim_ref, ldt_ref, btr_ref, bti_ref, ctr_ref, cti_ref,
                    l4r_ref, l4i_ref, wur_ref, wui_ref, vc_ref, kloc_ref):
    dt = jnp.exp(ldt_ref[...])
    ar = are_ref[...]
    ai = aim_ref[...]
    mag = jnp.exp(dt * ar)
    ang = dt * ai
    abr = mag * jnp.cos(ang)
    abi = mag * jnp.sin(ang)
    nr = abr - 1.0
    ni = abi
    den = ar * ar + ai * ai
    cre = (nr * ar + ni * ai) / den
    cim = (ni * ar - nr * ai) / den
    btr = btr_ref[...]
    bti = bti_ref[...]
    bbr = cre * btr - cim * bti
    bbi = cre * bti + cim * btr

    def cmul(xr, xi, yr, yi):
        return xr * yr - xi * yi, xr * yi + xi * yr

    lam = [(jnp.ones_like(abr), jnp.zeros_like(abi)), (abr, abi)]
    for _ in range(BLOCK - 1):
        lam.append(cmul(*lam[-1], abr, abi))
    l4r_ref[...] = lam[BLOCK][0]
    l4i_ref[...] = lam[BLOCK][1]

    n_q, k_rows, n_cols = wur_ref.shape
    q_groups = n_cols // STATE_P
    q_rows = q_groups * SSM_GROUP
    same_q = (_div_pow2(lax.broadcasted_iota(jnp.int32, (q_rows, n_cols), 0), SSM_GROUP)
              == _div_pow2(lax.broadcasted_iota(jnp.int32, (q_rows, n_cols), 1), STATE_P))
    for ip in range(BLOCK):
        ur, ui = cmul(bbr, bbi, *lam[BLOCK - 1 - ip])
        for q in range(n_q):
            for src, dst in ((ur, wur_ref), (ui, wui_ref)):
                blk = jnp.concatenate([src[:, q * n_cols:(q + 1) * n_cols]] * q_groups, axis=0)
                dst[q, ip * q_rows:(ip + 1) * q_rows, :] = jnp.where(same_q, blk, 0.0).astype(BF16)

    n_v, n_pair, k_state, n_out = vc_ref.shape
    v_lanes = k_state // 2
    v_groups = v_lanes // STATE_P
    same_v = (_div_pow2(lax.broadcasted_iota(jnp.int32, (V7X_LANES, v_lanes), 0), SSM_GROUP)
              == _div_pow2(lax.broadcasted_iota(jnp.int32, (V7X_LANES, v_lanes), 1), STATE_P))
    ctr = ctr_ref[...]
    cti = cti_ref[...]
    cl = [cmul(ctr, cti, *lam[t]) for t in range(BLOCK + 1)]

    def block_rows(a, v):
        blk = jnp.concatenate([a[:, v * v_lanes:(v + 1) * v_lanes]] * v_groups, axis=0)
        return jnp.where(same_v, blk, 0.0)

    for v in range(n_v):
        clm = [(block_rows(c[0], v), block_rows(c[1], v)) for c in cl]
        for pr in range(n_pair):
            rows = [jnp.concatenate([clm[pr * 2 + il + 1][0], -clm[pr * 2 + il + 1][1]], axis=1)
                    for il in range(2)]
            vc_ref[v, pr] = jnp.concatenate(rows, axis=0).T.astype(BF16)
        bcat = jnp.concatenate([block_rows(bbr, v), -block_rows(bbi, v)], axis=1).astype(BF16)
        ccat = jnp.concatenate([jnp.concatenate(clm[t], axis=1) for t in range(BLOCK)], axis=0)
        kall = lax.dot_general(bcat, ccat.astype(BF16), (((1,), (1,)), ((), ())),
                               preferred_element_type=F32)
        kt = [kall[:, t * V7X_LANES:(t + 1) * V7X_LANES] for t in range(BLOCK)]
        zero = jnp.zeros_like(kt[0])
        for pr in range(n_pair):
            kloc_ref[v, pr] = jnp.concatenate(
                [jnp.concatenate([kt[i - ip] if i >= ip else zero for i in (2 * pr, 2 * pr + 1)],
                                 axis=1) for ip in range(BLOCK)], axis=0).astype(BF16)


def _s5_prep(are, aim, ldt, btr, bti, ctr, cti):
    n_layers, _, n_state = are.shape
    n_ch = n_state // STATE_P * SSM_GROUP
    n_q = n_state // V7X_MXU_DIM
    n_v = n_ch // V7X_LANES
    v_lanes = n_state // n_v

    def per_layer(shape):
        return pl.BlockSpec((None,) + shape, lambda l: (l,) + (0,) * len(shape))

    out_tails = [(1, n_state), (1, n_state),
                 (n_q, V7X_MXU_DIM, V7X_MXU_DIM), (n_q, V7X_MXU_DIM, V7X_MXU_DIM),
                 (n_v, BLOCK // 2, 2 * v_lanes, V7X_MXU_DIM),
                 (n_v, BLOCK // 2, BLOCK * V7X_LANES, V7X_MXU_DIM)]
    out_dtypes = [F32, F32, BF16, BF16, BF16, BF16]
    args = (are, aim, ldt, btr, bti, ctr, cti)
    return pl.pallas_call(
        _s5_prep_kernel,
        grid=(n_layers,),
        in_specs=[per_layer(a.shape[1:]) for a in args],
        out_specs=[per_layer(t) for t in out_tails],
        out_shape=[jax.ShapeDtypeStruct((n_layers,) + t, dt) for t, dt in zip(out_tails, out_dtypes)],
        compiler_params=pltpu.CompilerParams(
            dimension_semantics=("parallel",),
            vmem_limit_bytes=_vmem_limit(2 * sum(
                int(jnp.dtype(dt).itemsize) * functools.reduce(lambda a, b: a * b, t)
                for t, dt in zip(out_tails, out_dtypes)))),
        name="s5_prep",
    )(*args)


def _s5_in_kernel(x_ref, npre_ref, win_ref, xb_ref, z_ref):
    d_in = xb_ref.shape[-1]
    hn = _rmsnorm(x_ref[...], npre_ref[...]).astype(BF16)
    xb_ref[...] = _dot(hn, win_ref[:, :d_in])
    z_ref[...] = _dot(hn, win_ref[:, d_in:])


def _s5_in(x2d, tile, layer_b, layer, npre, win, name):
    rows, d = x2d.shape
    exp_b = win.shape[2] // 2
    footprint = _layer_bytes(win) + 6 * tile * d * 4 + 4 * tile * exp_b * 4
    return pl.pallas_call(
        _s5_in_kernel,
        grid=(rows // tile,),
        in_specs=[_row_spec(tile, d), _layer_spec(npre, layer), _layer_spec(win, layer_b)],
        out_specs=[_row_spec(tile, exp_b), _row_spec(tile, exp_b)],
        out_shape=[jax.ShapeDtypeStruct((rows, exp_b), F32)] * 2,
        compiler_params=pltpu.CompilerParams(
            dimension_semantics=("parallel",),
            vmem_limit_bytes=_vmem_limit(footprint)),
        name=name,
    )(x2d, npre, win)


def _s5_core_kernel(*refs, n_seq, n_steps, has_h0, swap_bt):
    if has_h0:
        (xb_ref, h0r_ref, h0i_ref, l4r_ref, l4i_ref, wur_ref, wui_ref, vc_ref, kloc_ref, dsk_ref,
         y_ref, str_ref, sti_ref, hpr_ref, hpi_ref, hr_ref, hi_ref) = refs
    else:
        (xb_ref, l4r_ref, l4i_ref, wur_ref, wui_ref, vc_ref, kloc_ref, dsk_ref,
         y_ref, str_ref, sti_ref, hpr_ref, hpi_ref, hr_ref, hi_ref) = refs
    step = pl.program_id(0)

    @pl.when(step == 0)
    def _():
        if has_h0:
            hr_ref[...] = h0r_ref[...]
            hi_ref[...] = h0i_ref[...]
        else:
            hr_ref[...] = jnp.zeros_like(hr_ref)
            hi_ref[...] = jnp.zeros_like(hi_ref)

    xb = xb_ref[...]
    if swap_bt:
        xb = jnp.swapaxes(xb, 0, 1)
    width = xb.shape[-1]
    n_blocks = n_steps // BLOCK
    rows = n_blocks * n_seq
    x4 = xb.reshape(n_blocks, BLOCK, n_seq, width)
    xi = [x4[:, i].reshape(rows, width) for i in range(BLOCK)]

    sub = V7X_SUBLANES
    pair = 2 * sub
    n_q, _, n_cols = wur_ref.shape
    n_v = width // V7X_LANES
    half = V7X_LANES // 2
    low = lax.broadcasted_iota(jnp.int32, (rows, V7X_LANES), 1) < half

    for v in range(n_v):
        cols = slice(v * V7X_LANES, (v + 1) * V7X_LANES)
        p = [x[:, cols] for x in xi]
        pr = [pltpu.roll(a, half, 1) for a in p]
        lhs_lo = jnp.concatenate([jnp.where(low, p[i], pr[i + 1]) for i in range(0, BLOCK, 2)], axis=1)
        lhs_hi = jnp.concatenate([jnp.where(low, pr[i], p[i + 1]) for i in range(0, BLOCK, 2)], axis=1)
        for q, lhs in ((2 * v, lhs_lo.astype(BF16)), (2 * v + 1, lhs_hi.astype(BF16))):
            lanes = pl.ds(q * n_cols, n_cols)
            wr = _dot(lhs, wur_ref[q])
            wi = _dot(lhs, wui_ref[q])
            if n_seq == sub:
                ar = jnp.broadcast_to(l4r_ref[:, lanes], (sub, n_cols))
                ai = jnp.broadcast_to(l4i_ref[:, lanes], (sub, n_cols))
                hr, hi = hr_ref[:, lanes], hi_ref[:, lanes]
                for k in range(n_blocks // 2):
                    prv_r, prv_i = [], []
                    for r0 in (2 * k * sub, (2 * k + 1) * sub):
                        prv_r.append(hr)
                        prv_i.append(hi)
                        hr, hi = (ar * hr - ai * hi + wr[r0:r0 + sub],
                                  ar * hi + ai * hr + wi[r0:r0 + sub])
                    hpr_ref[k * pair:(k + 1) * pair, lanes] = jnp.concatenate(prv_r, axis=0).astype(BF16)
                    hpi_ref[k * pair:(k + 1) * pair, lanes] = jnp.concatenate(prv_i, axis=0).astype(BF16)
                hr_ref[:, lanes] = hr
                hi_ref[:, lanes] = hi
            else:
                ar = jnp.broadcast_to(l4r_ref[:, lanes], (pair, n_cols))
                ai = jnp.broadcast_to(l4i_ref[:, lanes], (pair, n_cols))
                for m in range(n_seq // pair):
                    srows = pl.ds(m * pair, pair)
                    hr, hi = hr_ref[srows, lanes], hi_ref[srows, lanes]
                    for blk in range(n_blocks):
                        r0 = blk * n_seq + m * pair
                        hpr_ref[r0:r0 + pair, lanes] = hr.astype(BF16)
                        hpi_ref[r0:r0 + pair, lanes] = hi.astype(BF16)
                        hr, hi = (ar * hr - ai * hi + wr[r0:r0 + pair],
                                  ar * hi + ai * hr + wi[r0:r0 + pair])
                    hr_ref[srows, lanes] = hr
                    hi_ref[srows, lanes] = hi

    v_lanes = vc_ref.shape[2] // 2
    for v in range(n_v):
        cols = slice(v * V7X_LANES, (v + 1) * V7X_LANES)
        st = jnp.concatenate([hpr_ref[:, v * v_lanes:(v + 1) * v_lanes],
                              hpi_ref[:, v * v_lanes:(v + 1) * v_lanes]], axis=1)
        xloc = [x[:, cols].astype(BF16) for x in xi]
        ys = []
        for pr_ in range(BLOCK // 2):
            n_loc = 2 * pr_ + 2
            yy = (_dot(st, vc_ref[v, pr_])
                  + _dot(jnp.concatenate(xloc[:n_loc], axis=1), kloc_ref[v, pr_, :n_loc * V7X_LANES, :]))
            for il in range(2):
                i = 2 * pr_ + il
                ys.append(jax.nn.gelu(yy[:, il * V7X_LANES:(il + 1) * V7X_LANES]
                                      + dsk_ref[:, cols] * xi[i][:, cols]))
        y = jnp.stack([a.reshape(n_blocks, n_seq, V7X_LANES) for a in ys], axis=1)
        y = y.reshape(n_steps, n_seq, V7X_LANES)
        if swap_bt:
            y_ref[:, :, cols] = jnp.swapaxes(y, 0, 1).astype(BF16)
        else:
            y_ref[:, cols] = y.reshape(n_steps * n_seq, V7X_LANES).astype(BF16)

    @pl.when(step == pl.num_programs(0) - 1)
    def _():
        str_ref[...] = hr_ref[...]
        sti_ref[...] = hi_ref[...]


def _s5_core(xb, h0, layer_b, consts, n_seq, n_steps):
    swap_bt = xb.ndim == 3
    exp_b = xb.shape[-1]
    n_state = consts[0].shape[-1]
    tile = n_seq * n_steps
    n_tiles = xb.size // (tile * exp_b)
    rows = tile // BLOCK
    has_h0 = h0 is not None
    assert n_steps % BLOCK == 0 and (n_seq > V7X_SUBLANES or (n_steps // BLOCK) % 2 == 0)
    args = [xb] + (list(h0) if has_h0 else []) + list(consts)
    if swap_bt:
        io_spec = pl.BlockSpec((n_seq, n_steps, exp_b), lambda i: (0, i, 0))
    else:
        io_spec = pl.BlockSpec((tile, exp_b), lambda i: (i, 0))
    in_specs = [io_spec] + [_layer_spec(a, layer_b) for a in args[1:]]
    st_spec = pl.BlockSpec((n_seq, n_state), lambda i: (0, 0))
    st_shape = jax.ShapeDtypeStruct((n_seq, n_state), F32)
    footprint = (_layer_bytes(*args[1:]) + 2 * tile * exp_b * (4 + 2) + 2 * rows * n_state * 2
                 + 8 * n_seq * n_state * 4 + 3 * tile * exp_b * 4)
    return pl.pallas_call(
        functools.partial(_s5_core_kernel, n_seq=n_seq, n_steps=n_steps, has_h0=has_h0,
                          swap_bt=swap_bt),
        grid=(n_tiles,),
        in_specs=in_specs,
        out_specs=[io_spec, st_spec, st_spec],
        out_shape=[jax.ShapeDtypeStruct(xb.shape, BF16), st_shape, st_shape],
        scratch_shapes=[pltpu.VMEM((rows, n_state), BF16), pltpu.VMEM((rows, n_state), BF16),
                        pltpu.VMEM((n_seq, n_state), F32), pltpu.VMEM((n_seq, n_state), F32)],
        compiler_params=pltpu.CompilerParams(
            dimension_semantics=("arbitrary",),
            vmem_limit_bytes=_vmem_limit(footprint)),
        name="s5_core_sample" if has_h0 else "s5_core_prompt",
    )(*args)


def _s5_out_kernel(y_ref, z_ref, x_ref, w1_ref, b1_ref, w2_ref, b2_ref, wout_ref, npost_ref, o_ref):
    y = y_ref[...]
    g = (_dot(y, w1_ref[...]) + b1_ref[...]) * jax.nn.sigmoid(_dot(y, w2_ref[...]) + b2_ref[...])
    out = _dot((g * _silu(z_ref[...])).astype(BF16), wout_ref[...])
    o_ref[...] = x_ref[...] + _rmsnorm(out, npost_ref[...])


def _s5_out(y, z, x2d, tile, layer_b, layer, w1, b1, w2, b2, wout, npost, name):
    rows, d = x2d.shape
    exp_b = z.shape[1]
    footprint = (_layer_bytes(w1, w2, wout) + 2 * tile * (exp_b * 6 + d * 8) + 6 * tile * exp_b * 4)
    return pl.pallas_call(
        _s5_out_kernel,
        grid=(rows // tile,),
        in_specs=[_row_spec(tile, exp_b), _row_spec(tile, exp_b), _row_spec(tile, d),
                  _layer_spec(w1, layer_b), _layer_spec(b1, layer_b),
                  _layer_spec(w2, layer_b), _layer_spec(b2, layer_b),
                  _layer_spec(wout, layer_b), _layer_spec(npost, layer)],
        out_specs=_row_spec(tile, d),
        out_shape=jax.ShapeDtypeStruct((rows, d), F32),
        compiler_params=pltpu.CompilerParams(
            dimension_semantics=("parallel",),
            vmem_limit_bytes=_vmem_limit(footprint)),
        name=name,
    )(y, z, x2d, w1, b1, w2, b2, wout, npost)


def kernel(x_prompt, x_sample, state_ssm_re, state_ssm_im, norm_pre, norm_post,
           w_in_a, ln_v_g, ln_v_b, w_s, b_s, w_out_a,
           w_in_b, a_re, a_im, log_dt, b_re, b_im, c_re, c_im, d_skip,
           w_glu1, b_glu1, w_glu2, b_glu2, w_out_b):
    n_batch, seq_len, d = x_prompt.shape
    n_dec, dec_len, _ = x_sample.shape
    depth = norm_pre.shape[0]
    n_ssm, n_groups, state_p = a_re.shape
    n_state = n_groups * state_p
    assert seq_len % ROW_TILE == 0 and ROW_TILE % CHUNK == 0 and seq_len % SCAN_STEPS == 0
    assert (n_batch * seq_len) % S5_ROW_TILE == 0
    assert n_batch == V7X_SUBLANES and n_dec % (2 * V7X_SUBLANES) == 0
    assert dec_len <= CHUNK and dec_len % BLOCK == 0
    assert state_p == STATE_P and b_re.shape[-1] == SSM_GROUP and w_s.shape[1] == N_HEADS

    rows3 = lambda a: a.reshape(a.shape[0], 1, -1)
    npre, npost = rows3(norm_pre), rows3(norm_post)
    lng, lnb = rows3(ln_v_g), rows3(ln_v_b)
    w_in_a16, w_out_a16 = w_in_a.astype(BF16), w_out_a.astype(BF16)
    bst = jnp.swapaxes(b_s, 1, 2)
    ws_dec = w_s[:, :, :dec_len, :dec_len].reshape(-1)
    bs_dec = b_s[:, :, :dec_len].reshape(-1)

    w_in_b16 = w_in_b.astype(BF16)
    w1, w2, w_out_b16 = w_glu1.astype(BF16), w_glu2.astype(BF16), w_out_b.astype(BF16)
    b1, b2, dsk = rows3(b_glu1), rows3(b_glu2), rows3(d_skip)
    lanes_gp = lambda a, perm: jnp.transpose(a, perm).reshape(n_ssm, SSM_GROUP, n_state)
    l4r, l4i, wur, wui, vcw, kloc = _s5_prep(
        rows3(a_re), rows3(a_im), rows3(jnp.repeat(log_dt, state_p, axis=1)),
        lanes_gp(b_re, (0, 3, 1, 2)), lanes_gp(b_im, (0, 3, 1, 2)),
        lanes_gp(c_re, (0, 2, 1, 3)), lanes_gp(c_im, (0, 2, 1, 3)))
    core_consts = (l4r, l4i, wur, wui, vcw, kloc, dsk)
    h0 = (state_ssm_re.reshape(n_ssm, n_dec, n_state), state_ssm_im.reshape(n_ssm, n_dec, n_state))

    xp = x_prompt.reshape(n_batch * seq_len, d)
    xs = jnp.transpose(x_sample, (1, 0, 2)).reshape(dec_len * n_dec, d)
    n_dec_rows = dec_len * n_dec

    v_rows, st_p_re, st_p_im, st_s_re, st_s_im = [], [], [], [], []
    for i in range(depth):
        j = i // 2
        if i % 2 == 0:
            xp = _gmlp_prompt(xp, j, i, npre, npost, lng, lnb, w_in_a16, w_out_a16, w_s, bst)
            xs, v = _gmlp_sample(xs, n_dec, dec_len, j, i, npre, npost, lng, lnb,
                                 w_in_a16, w_out_a16, ws_dec, bs_dec)
            v_rows.append(v)
        else:
            exp_b = w_in_b.shape[2] // 2
            xb, z = _s5_in(xp, S5_ROW_TILE, j, i, npre, w_in_b16, "s5_in_prompt")
            y, sr, si = _s5_core(xb.reshape(n_batch, seq_len, exp_b), None, j, core_consts,
                                 n_batch, SCAN_STEPS)
            xp = _s5_out(y.reshape(n_batch * seq_len, exp_b), z, xp, S5_ROW_TILE, j, i,
                         w1, b1, w2, b2, w_out_b16, npost, "s5_out_prompt")
            st_p_re.append(sr)
            st_p_im.append(si)

            xbs, zs = _s5_in(xs, n_dec_rows, j, i, npre, w_in_b16, "s5_in_sample")
            ys, sr, si = _s5_core(xbs, h0, j, core_consts, n_dec, dec_len)
            xs = _s5_out(ys, zs, xs, n_dec_rows, j, i, w1, b1, w2, b2, w_out_b16, npost,
                         "s5_out_sample")
            st_s_re.append(sr)
            st_s_im.append(si)

    y_prompt = xp.reshape(n_batch, seq_len, d)
    y_sample = jnp.transpose(xs.reshape(dec_len, n_dec, d), (1, 0, 2))
    chunk_v = jnp.transpose(jnp.stack(v_rows).reshape(len(v_rows), dec_len, n_dec, -1), (0, 2, 1, 3))
    states = lambda parts, n: jnp.stack(parts).reshape(len(parts), n, n_groups, state_p)
    return (y_prompt, y_sample, chunk_v, states(st_p_re, n_batch), states(st_p_im, n_batch),
            states(st_s_re, n_dec), states(st_s_im, n_dec))
```

```python
import functools

import jax
import jax.numpy as jnp
from jax import lax
from jax.experimental import pallas as pl
from jax.experimental.pallas import tpu as pltpu

EPS = 1e-6
CHUNK = 128
N_HEADS = 8
SSM_GROUP = 16
STATE_P = 64

V7X_LANES = 128
V7X_SUBLANES = 8
V7X_MXU_DIM = 256
V7X_VMEM_BYTES = 64 * 1024 * 1024

ROW_TILE = 512
SCAN_STEPS = 64
BLOCK = 4

BF16 = jnp.bfloat16
F32 = jnp.float32


def _dot(a, b):
    return jnp.dot(a, b, preferred_element_type=F32)


def _rmsnorm(x, g):
    ms = jnp.mean(x * x, axis=-1, keepdims=True)
    return x * lax.rsqrt(ms + EPS) * g


def _layernorm(x, g, b):
    mu = jnp.mean(x, axis=-1, keepdims=True)
    xc = x - mu
    var = jnp.mean(xc * xc, axis=-1, keepdims=True)
    return xc * lax.rsqrt(var + EPS) * g + b


def _silu(z):
    return z * jax.nn.sigmoid(z)


def _div_pow2(x, n):
    assert n & (n - 1) == 0
    return lax.shift_right_logical(x, n.bit_length() - 1)


def _vmem_limit(nbytes):
    return int(min(V7X_VMEM_BYTES - (4 << 20), nbytes + (12 << 20)))


def _layer_spec(stacked, layer):
    tail = stacked.shape[1:]
    return pl.BlockSpec((None,) + tail, lambda *_: (layer,) + (0,) * len(tail),
                        pipeline_mode=pl.Buffered(1))


def _layer_bytes(*stacked):
    return sum(a[0].size * a.dtype.itemsize for a in stacked)


def _row_spec(rows, width):
    return pl.BlockSpec((rows, width), lambda i: (i, 0))


_SMEM = pl.BlockSpec(memory_space=pltpu.SMEM)


def _gmlp_front(x, npre_ref, lng_ref, lnb_ref, win_ref, exp_a):
    hn = _rmsnorm(x, npre_ref[...]).astype(BF16)
    v = _dot(hn, win_ref[:, exp_a:2 * exp_a])
    v = _layernorm(v, lng_ref[...], lnb_ref[...])
    return hn, v


def _gmlp_prompt_kernel(x_ref, npre_ref, npost_ref, lng_ref, lnb_ref, win_ref, wout_ref,
                        ws_ref, bst_ref, o_ref, gated_ref):
    rows = x_ref.shape[0]
    exp_a = wout_ref.shape[0]
    hd = exp_a // N_HEADS
    x = x_ref[...]
    hn, v = _gmlp_front(x, npre_ref, lng_ref, lnb_ref, win_ref, exp_a)
    vb = v.astype(BF16)
    causal = (lax.broadcasted_iota(jnp.int32, (CHUNK, CHUNK), 1)
              <= lax.broadcasted_iota(jnp.int32, (CHUNK, CHUNK), 0))
    for h in range(N_HEADS):
        lo = h * hd
        wsh = jnp.where(causal, ws_ref[h], 0.0).astype(BF16)
        bias = bst_ref[:, h:h + 1]
        u = _dot(hn, win_ref[:, lo:lo + hd])
        z = _dot(hn, win_ref[:, 2 * exp_a + lo:2 * exp_a + lo + hd])
        s = jnp.concatenate(
            [_dot(wsh, vb[c * CHUNK:(c + 1) * CHUNK, lo:lo + hd]) + bias
             for c in range(rows // CHUNK)], axis=0)
        gated_ref[:, lo:lo + hd] = (u * s * _silu(z)).astype(BF16)
    out = _dot(gated_ref[...], wout_ref[...])
    o_ref[...] = x + _rmsnorm(out, npost_ref[...])


def _gmlp_sample_kernel(ws_ref, bs_ref, x_ref, npre_ref, npost_ref, lng_ref, lnb_ref, win_ref,
                        wout_ref, o_ref, v_ref, gated_ref, *, n_seq, seq_len, layer):
    exp_a = wout_ref.shape[0]
    hd = exp_a // N_HEADS
    x = x_ref[...]
    hn, v = _gmlp_front(x, npre_ref, lng_ref, lnb_ref, win_ref, exp_a)
    v_ref[...] = v
    for h in range(N_HEADS):
        lo = h * hd
        u = _dot(hn, win_ref[:, lo:lo + hd])
        z = _dot(hn, win_ref[:, 2 * exp_a + lo:2 * exp_a + lo + hd])
        vt = [v[t * n_seq:(t + 1) * n_seq, lo:lo + hd] for t in range(seq_len)]
        parts = []
        for t in range(seq_len):
            b_idx = (layer * N_HEADS + h) * seq_len + t
            s = ws_ref[b_idx * seq_len] * vt[0]
            for t2 in range(1, t + 1):
                s = s + ws_ref[b_idx * seq_len + t2] * vt[t2]
            parts.append(s + bs_ref[b_idx])
        s = jnp.concatenate(parts, axis=0)
        gated_ref[:, lo:lo + hd] = (u * s * _silu(z)).astype(BF16)
    out = _dot(gated_ref[...], wout_ref[...])
    o_ref[...] = x + _rmsnorm(out, npost_ref[...])


def _gmlp_prompt(x2d, layer_a, layer, npre, npost, lng, lnb, win, wout, ws, bst):
    rows, d = x2d.shape
    exp_a = wout.shape[1]
    footprint = (_layer_bytes(win, wout, ws) + 4 * ROW_TILE * d * 4
                 + ROW_TILE * exp_a * (2 + 4 + 2 + 8))
    return pl.pallas_call(
        _gmlp_prompt_kernel,
        grid=(rows // ROW_TILE,),
        in_specs=[
            _row_spec(ROW_TILE, d),
            _layer_spec(npre, layer), _layer_spec(npost, layer),
            _layer_spec(lng, layer_a), _layer_spec(lnb, layer_a),
            _layer_spec(win, layer_a), _layer_spec(wout, layer_a),
            _layer_spec(ws, layer_a), _layer_spec(bst, layer_a),
        ],
        out_specs=_row_spec(ROW_TILE, d),
        out_shape=jax.ShapeDtypeStruct((rows, d), F32),
        scratch_shapes=[pltpu.VMEM((ROW_TILE, exp_a), BF16)],
        compiler_params=pltpu.CompilerParams(
            dimension_semantics=("parallel",),
            vmem_limit_bytes=_vmem_limit(footprint)),
        name="gmlp_prompt",
    )(x2d, npre, npost, lng, lnb, win, wout, ws, bst)


def _gmlp_sample(xs, n_seq, seq_len, layer_a, layer, npre, npost, lng, lnb, win, wout,
                 ws_flat, bs_flat):
    rows, d = xs.shape
    exp_a = wout.shape[1]
    footprint = _layer_bytes(win, wout) + 4 * rows * d * 4 + rows * exp_a * (2 + 8 + 4 + 8)
    return pl.pallas_call(
        functools.partial(_gmlp_sample_kernel, n_seq=n_seq, seq_len=seq_len, layer=layer_a),
        grid=(1,),
        in_specs=[_SMEM, _SMEM, _row_spec(rows, d),
                  _layer_spec(npre, layer), _layer_spec(npost, layer),
                  _layer_spec(lng, layer_a), _layer_spec(lnb, layer_a),
                  _layer_spec(win, layer_a), _layer_spec(wout, layer_a)],
        out_specs=[_row_spec(rows, d), _row_spec(rows, exp_a)],
        out_shape=[jax.ShapeDtypeStruct((rows, d), F32),
                   jax.ShapeDtypeStruct((rows, exp_a), F32)],
        scratch_shapes=[pltpu.VMEM((rows, exp_a), BF16)],
        compiler_params=pltpu.CompilerParams(
            dimension_semantics=("arbitrary",),
            vmem_limit_bytes=_vmem_limit(footprint)),
        name="gmlp_sample",
    )(ws_flat, bs_flat, xs, npre, npost, lng, lnb, win, wout)


def _s5_prep_kernel(are_ref, aim_ref, ldt_ref, btr_ref, bti_ref, ctr_ref, cti_ref,
                    l4r_ref, l4i_ref, wur_ref, wui_ref, vc_ref, kloc_ref):
    dt = jnp.exp(ldt_ref[...])
    ar = are_ref[...]
    ai = aim_ref[...]
    mag = jnp.exp(dt * ar)
    ang = dt * ai
    abr = mag * jnp.cos(ang)
    abi = mag * jnp.sin(ang)
    nr = abr - 1.0
    ni = abi
    den = ar * ar + ai * ai
    cre = (nr * ar + ni * ai) / den
    cim = (ni * ar - nr * ai) / den
    btr = btr_ref[...]
    bti = bti_ref[...]
    bbr = cre * btr - cim * bti
    bbi = cre * bti + cim * btr

    def cmul(xr, xi, yr, yi):
        return xr * yr - xi * yi, xr * yi + xi * yr

    lam = [(jnp.ones_like(abr), jnp.zeros_like(abi)), (abr, abi)]
    for _ in range(BLOCK - 1):
        lam.append(cmul(*lam[-1], abr, abi))
    l4r_ref[...] = lam[BLOCK][0]
    l4i_ref[...] = lam[BLOCK][1]

    n_q, k_rows, n_cols = wur_ref.shape
    q_groups = n_cols // STATE_P
    q_rows = q_groups * SSM_GROUP
    same_q = (_div_pow2(lax.broadcasted_iota(jnp.int32, (q_rows, n_cols), 0), SSM_GROUP)
              == _div_pow2(lax.broadcasted_iota(jnp.int32, (q_rows, n_cols), 1), STATE_P))
    for ip in range(BLOCK):
        ur, ui = cmul(bbr, bbi, *lam[BLOCK - 1 - ip])
        for q in range(n_q):
            for src, dst in ((ur, wur_ref), (ui, wui_ref)):
                blk = jnp.concatenate([src[:, q * n_cols:(q + 1) * n_cols]] * q_groups, axis=0)
                dst[q, ip * q_rows:(ip + 1) * q_rows, :] = jnp.where(same_q, blk, 0.0).astype(BF16)

    n_v, n_pair, k_state, n_out = vc_ref.shape
    v_lanes = k_state // 2
    v_groups = v_lanes // STATE_P
    same_v = (_div_pow2(lax.broadcasted_iota(jnp.int32, (V7X_LANES, v_lanes), 0), SSM_GROUP)
              == _div_pow2(lax.broadcasted_iota(jnp.int32, (V7X_LANES, v_lanes), 1), STATE_P))
    ctr = ctr_ref[...]
    cti = cti_ref[...]
    cl = [cmul(ctr, cti, *lam[t]) for t in range(BLOCK + 1)]

    def block_rows(a, v):
        blk = jnp.concatenate([a[:, v * v_lanes:(v + 1) * v_lanes]] * v_groups, axis=0)
        return jnp.where(same_v, blk, 0.0)

    for v in range(n_v):
        clm = [(block_rows(c[0], v), block_rows(c[1], v)) for c in cl]
        for pr in range(n_pair):
            rows = [jnp.concatenate([clm[pr * 2 + il + 1][0], -clm[pr * 2 + il + 1][1]], axis=1)
                    for il in range(2)]
            vc_ref[v, pr] = jnp.concatenate(rows, axis=0).T.astype(BF16)
        bcat = jnp.concatenate([block_rows(bbr, v), -block_rows(bbi, v)], axis=1).astype(BF16)
        ccat = jnp.concatenate([jnp.concatenate(clm[t], axis=1) for t in range(BLOCK)], axis=0)
        kall = lax.dot_general(bcat, ccat.astype(BF16), (((1,), (1,)), ((), ())),
                               preferred_element_type=F32)
        kt = [kall[:, t * V7X_LANES:(t + 1) * V7X_LANES] for t in range(BLOCK)]
        zero = jnp.zeros_like(kt[0])
        for pr in range(n_pair):
            kloc_ref[v, pr] = jnp.concatenate(
                [jnp.concatenate([kt[i - ip] if i >= ip else zero for i in (2 * pr, 2 * pr + 1)],
                                 axis=1) for ip in range(BLOCK)], axis=0).astype(BF16)


def _s5_prep(are, aim, ldt, btr, bti, ctr, cti):
    n_layers, _, n_state = are.shape
    n_ch = n_state // STATE_P * SSM_GROUP
    n_q = n_state // V7X_MXU_DIM
    n_v = n_ch // V7X_LANES
    v_lanes = n_state // n_v

    def per_layer(shape):
        return pl.BlockSpec((None,) + shape, lambda l: (l,) + (0,) * len(shape))

    out_tails = [(1, n_state), (1, n_state),
                 (n_q, V7X_MXU_DIM, V7X_MXU_DIM), (n_q, V7X_MXU_DIM, V7X_MXU_DIM),
                 (n_v, BLOCK // 2, 2 * v_lanes, V7X_MXU_DIM),
                 (n_v, BLOCK // 2, BLOCK * V7X_LANES, V7X_MXU_DIM)]
    out_dtypes = [F32, F32, BF16, BF16, BF16, BF16]
    args = (are, aim, ldt, btr, bti, ctr, cti)
    return pl.pallas_call(
        _s5_prep_kernel,
        grid=(n_layers,),
        in_specs=[per_layer(a.shape[1:]) for a in args],
        out_specs=[per_layer(t) for t in out_tails],
        out_shape=[jax.ShapeDtypeStruct((n_layers,) + t, dt) for t, dt in zip(out_tails, out_dtypes)],
        compiler_params=pltpu.CompilerParams(
            dimension_semantics=("parallel",),
            vmem_limit_bytes=_vmem_limit(2 * sum(
                int(jnp.dtype(dt).itemsize) * functools.reduce(lambda a, b: a * b, t)
                for t, dt in zip(out_tails, out_dtypes)))),
        name="s5_prep",
    )(*args)


def _s5_layer_kernel(*refs, n_seq, n_steps, has_h0, swap_bt):
    if has_h0:
        x_ref, h0r_ref, h0i_ref = refs[:3]
        refs = refs[3:]
    else:
        x_ref = refs[0]
        refs = refs[1:]
    (npre_ref, win_ref, l4r_ref, l4i_ref, wur_ref, wui_ref, vc_ref, kloc_ref, dsk_ref,
     w1_ref, b1_ref, w2_ref, b2_ref, wout_ref, npost_ref,
     o_ref, str_ref, sti_ref, hpr_ref, hpi_ref, hr_ref, hi_ref, y_ref) = refs
    step = pl.program_id(0)

    @pl.when(step == 0)
    def _():
        if has_h0:
            hr_ref[...] = h0r_ref[...]
            hi_ref[...] = h0i_ref[...]
        else:
            hr_ref[...] = jnp.zeros_like(hr_ref)
            hi_ref[...] = jnp.zeros_like(hi_ref)

    x = x_ref[...]
    if swap_bt:
        x = jnp.swapaxes(x, 0, 1)
    d = x.shape[-1]
    x = x.reshape(n_steps * n_seq, d)
    width = win_ref.shape[1] // 2
    hn = _rmsnorm(x, npre_ref[...]).astype(BF16)
    xb = _dot(hn, win_ref[:, :width])
    z = _dot(hn, win_ref[:, width:])

    n_blocks = n_steps // BLOCK
    rows = n_blocks * n_seq
    x4 = xb.reshape(n_blocks, BLOCK, n_seq, width)
    xi = [x4[:, i].reshape(rows, width) for i in range(BLOCK)]

    sub = V7X_SUBLANES
    pair = 2 * sub
    n_q, _, n_cols = wur_ref.shape
    n_v = width // V7X_LANES
    half = V7X_LANES // 2
    low = lax.broadcasted_iota(jnp.int32, (rows, V7X_LANES), 1) < half

    for v in range(n_v):
        cols = slice(v * V7X_LANES, (v + 1) * V7X_LANES)
        p = [a[:, cols] for a in xi]
        pr = [pltpu.roll(a, half, 1) for a in p]
        lhs_lo = jnp.concatenate([jnp.where(low, p[i], pr[i + 1]) for i in range(0, BLOCK, 2)], axis=1)
        lhs_hi = jnp.concatenate([jnp.where(low, pr[i], p[i + 1]) for i in range(0, BLOCK, 2)], axis=1)
        for q, lhs in ((2 * v, lhs_lo.astype(BF16)), (2 * v + 1, lhs_hi.astype(BF16))):
            lanes = pl.ds(q * n_cols, n_cols)
            wr = _dot(lhs, wur_ref[q])
            wi = _dot(lhs, wui_ref[q])
            if n_seq == sub:
                ar = jnp.broadcast_to(l4r_ref[:, lanes], (sub, n_cols))
                ai = jnp.broadcast_to(l4i_ref[:, lanes], (sub, n_cols))
                hr, hi = hr_ref[:, lanes], hi_ref[:, lanes]
                for k in range(n_blocks // 2):
                    prv_r, prv_i = [], []
                    for r0 in (2 * k * sub, (2 * k + 1) * sub):
                        prv_r.append(hr)
                        prv_i.append(hi)
                        hr, hi = (ar * hr - ai * hi + wr[r0:r0 + sub],
                                  ar * hi + ai * hr + wi[r0:r0 + sub])
                    hpr_ref[k * pair:(k + 1) * pair, lanes] = jnp.concatenate(prv_r, axis=0).astype(BF16)
                    hpi_ref[k * pair:(k + 1) * pair, lanes] = jnp.concatenate(prv_i, axis=0).astype(BF16)
                hr_ref[:, lanes] = hr
                hi_ref[:, lanes] = hi
            else:
                ar = jnp.broadcast_to(l4r_ref[:, lanes], (pair, n_cols))
                ai = jnp.broadcast_to(l4i_ref[:, lanes], (pair, n_cols))
                for m in range(n_seq // pair):
                    srows = pl.ds(m * pair, pair)
                    hr, hi = hr_ref[srows, lanes], hi_ref[srows, lanes]
                    for blk in range(n_blocks):
                        r0 = blk * n_seq + m * pair
                        hpr_ref[r0:r0 + pair, lanes] = hr.astype(BF16)
                        hpi_ref[r0:r0 + pair, lanes] = hi.astype(BF16)
                        hr, hi = (ar * hr - ai * hi + wr[r0:r0 + pair],
                                  ar * hi + ai * hr + wi[r0:r0 + pair])
                    hr_ref[srows, lanes] = hr
                    hi_ref[srows, lanes] = hi

    v_lanes = vc_ref.shape[2] // 2
    for v in range(n_v):
        cols = slice(v * V7X_LANES, (v + 1) * V7X_LANES)
        st = jnp.concatenate([hpr_ref[:, v * v_lanes:(v + 1) * v_lanes],
                              hpi_ref[:, v * v_lanes:(v + 1) * v_lanes]], axis=1)
        xloc = [a[:, cols].astype(BF16) for a in xi]
        ys = []
        for pr_ in range(BLOCK // 2):
            n_loc = 2 * pr_ + 2
            yy = (_dot(st, vc_ref[v, pr_])
                  + _dot(jnp.concatenate(xloc[:n_loc], axis=1), kloc_ref[v, pr_, :n_loc * V7X_LANES, :]))
            for il in range(2):
                i = 2 * pr_ + il
                ys.append(jax.nn.gelu(yy[:, il * V7X_LANES:(il + 1) * V7X_LANES]
                                      + dsk_ref[:, cols] * xi[i][:, cols]))
        y = jnp.stack([a.reshape(n_blocks, n_seq, V7X_LANES) for a in ys], axis=1)
        y_ref[:, cols] = y.reshape(n_steps * n_seq, V7X_LANES).astype(BF16)

    y = y_ref[...]
    g = (_dot(y, w1_ref[...]) + b1_ref[...]) * jax.nn.sigmoid(_dot(y, w2_ref[...]) + b2_ref[...])
    out = _dot((g * _silu(z)).astype(BF16), wout_ref[...])
    xn = x + _rmsnorm(out, npost_ref[...])
    if swap_bt:
        o_ref[...] = jnp.swapaxes(xn.reshape(n_steps, n_seq, d), 0, 1)
    else:
        o_ref[...] = xn

    @pl.when(step == pl.num_programs(0) - 1)
    def _():
        str_ref[...] = hr_ref[...]
        sti_ref[...] = hi_ref[...]


def _s5_layer(x, h0, layer_b, layer, npre, win, core_consts, out_consts, npost, n_seq, n_steps):
    swap_bt = x.ndim == 3
    d = x.shape[-1]
    exp_b = win.shape[2] // 2
    n_state = core_consts[0].shape[-1]
    tile = n_seq * n_steps
    n_tiles = x.size // (tile * d)
    rows = tile // BLOCK
    has_h0 = h0 is not None
    assert n_steps % BLOCK == 0 and (n_seq > V7X_SUBLANES or (n_steps // BLOCK) % 2 == 0)
    params_b = [win] + list(core_consts) + list(out_consts)
    args = [x] + (list(h0) if has_h0 else []) + [npre] + params_b + [npost]
    if swap_bt:
        io_spec = pl.BlockSpec((n_seq, n_steps, d), lambda i: (0, i, 0))
    else:
        io_spec = pl.BlockSpec((tile, d), lambda i: (i, 0))
    in_specs = ([io_spec] + [_layer_spec(a, layer_b) for a in (h0 if has_h0 else ())]
                + [_layer_spec(npre, layer)] + [_layer_spec(a, layer_b) for a in params_b]
                + [_layer_spec(npost, layer)])
    st_spec = pl.BlockSpec((n_seq, n_state), lambda i: (0, 0))
    st_shape = jax.ShapeDtypeStruct((n_seq, n_state), F32)
    footprint = (_layer_bytes(*params_b) + (_layer_bytes(*h0) if has_h0 else 0) + 4 * tile * d * 4
                 + 2 * rows * n_state * 2 + 8 * n_seq * n_state * 4 + tile * exp_b * (2 + 6 * 4))
    return pl.pallas_call(
        functools.partial(_s5_layer_kernel, n_seq=n_seq, n_steps=n_steps, has_h0=has_h0,
                          swap_bt=swap_bt),
        grid=(n_tiles,),
        in_specs=in_specs,
        out_specs=[io_spec, st_spec, st_spec],
        out_shape=[jax.ShapeDtypeStruct(x.shape, F32), st_shape, st_shape],
        scratch_shapes=[pltpu.VMEM((rows, n_state), BF16), pltpu.VMEM((rows, n_state), BF16),
                        pltpu.VMEM((n_seq, n_state), F32), pltpu.VMEM((n_seq, n_state), F32),
                        pltpu.VMEM((tile, exp_b), BF16)],
        compiler_params=pltpu.CompilerParams(
            dimension_semantics=("arbitrary",),
            vmem_limit_bytes=_vmem_limit(footprint)),
        name="s5_layer_sample" if has_h0 else "s5_layer_prompt",
    )(*args)


def kernel(x_prompt, x_sample, state_ssm_re, state_ssm_im, norm_pre, norm_post,
           w_in_a, ln_v_g, ln_v_b, w_s, b_s, w_out_a,
           w_in_b, a_re, a_im, log_dt, b_re, b_im, c_re, c_im, d_skip,
           w_glu1, b_glu1, w_glu2, b_glu2, w_out_b):
    n_batch, seq_len, d = x_prompt.shape
    n_dec, dec_len, _ = x_sample.shape
    depth = norm_pre.shape[0]
    n_ssm, n_groups, state_p = a_re.shape
    n_state = n_groups * state_p
    assert seq_len % ROW_TILE == 0 and ROW_TILE % CHUNK == 0 and seq_len % SCAN_STEPS == 0
    assert n_batch == V7X_SUBLANES and n_dec % (2 * V7X_SUBLANES) == 0
    assert dec_len <= CHUNK and dec_len % BLOCK == 0
    assert state_p == STATE_P and b_re.shape[-1] == SSM_GROUP and w_s.shape[1] == N_HEADS

    rows3 = lambda a: a.reshape(a.shape[0], 1, -1)
    npre, npost = rows3(norm_pre), rows3(norm_post)
    lng, lnb = rows3(ln_v_g), rows3(ln_v_b)
    w_in_a16, w_out_a16 = w_in_a.astype(BF16), w_out_a.astype(BF16)
    bst = jnp.swapaxes(b_s, 1, 2)
    ws_dec = w_s[:, :, :dec_len, :dec_len].reshape(-1)
    bs_dec = b_s[:, :, :dec_len].reshape(-1)

    w_in_b16 = w_in_b.astype(BF16)
    out_consts = (w_glu1.astype(BF16), rows3(b_glu1), w_glu2.astype(BF16), rows3(b_glu2),
                  w_out_b.astype(BF16))
    lanes_gp = lambda a, perm: jnp.transpose(a, perm).reshape(n_ssm, SSM_GROUP, n_state)
    l4r, l4i, wur, wui, vcw, kloc = _s5_prep(
        rows3(a_re), rows3(a_im), rows3(jnp.repeat(log_dt, state_p, axis=1)),
        lanes_gp(b_re, (0, 3, 1, 2)), lanes_gp(b_im, (0, 3, 1, 2)),
        lanes_gp(c_re, (0, 2, 1, 3)), lanes_gp(c_im, (0, 2, 1, 3)))
    core_consts = (l4r, l4i, wur, wui, vcw, kloc, rows3(d_skip))
    h0 = (state_ssm_re.reshape(n_ssm, n_dec, n_state), state_ssm_im.reshape(n_ssm, n_dec, n_state))

    xp = x_prompt.reshape(n_batch * seq_len, d)
    xs = jnp.transpose(x_sample, (1, 0, 2)).reshape(dec_len * n_dec, d)

    v_rows, st_p_re, st_p_im, st_s_re, st_s_im = [], [], [], [], []
    for i in range(depth):
        j = i // 2
        if i % 2 == 0:
            xp = _gmlp_prompt(xp, j, i, npre, npost, lng, lnb, w_in_a16, w_out_a16, w_s, bst)
            xs, v = _gmlp_sample(xs, n_dec, dec_len, j, i, npre, npost, lng, lnb,
                                 w_in_a16, w_out_a16, ws_dec, bs_dec)
            v_rows.append(v)
        else:
            xp3, sr, si = _s5_layer(xp.reshape(n_batch, seq_len, d), None, j, i, npre, w_in_b16,
                                    core_consts, out_consts, npost, n_batch, SCAN_STEPS)
            xp = xp3.reshape(n_batch * seq_len, d)
            st_p_re.append(sr)
            st_p_im.append(si)
            xs, sr, si = _s5_layer(xs, h0, j, i, npre, w_in_b16, core_consts, out_consts, npost,
                                   n_dec, dec_len)
            st_s_re.append(sr)
            st_s_im.append(si)

    y_prompt = xp.reshape(n_batch, seq_len, d)
    y_sample = jnp.transpose(xs.reshape(dec_len, n_dec, d), (1, 0, 2))
    chunk_v = jnp.transpose(jnp.stack(v_rows).reshape(len(v_rows), dec_len, n_dec, -1), (0, 2, 1, 3))
    states = lambda parts, n: jnp.stack(parts).reshape(len(parts), n, n_groups, state_p)
    return (y_prompt, y_sample, chunk_v, states(st_p_re, n_batch), states(st_p_im, n_batch),
            states(st_s_re, n_dec), states(st_s_im, n_dec))
```

```python
import functools

import jax
import jax.numpy as jnp
from jax import lax
from jax.experimental import pallas as pl
from jax.experimental.pallas import tpu as pltpu

EPS = 1e-6
CHUNK = 128
N_HEADS = 8
SSM_GROUP = 16
STATE_P = 64

V7X_LANES = 128
V7X_SUBLANES = 8
V7X_MXU_DIM = 256
V7X_VMEM_BYTES = 64 * 1024 * 1024

ROW_TILE = 512
SCAN_STEPS = 64
BLOCK = 4
ROW_CHUNKS = 2

BF16 = jnp.bfloat16
F32 = jnp.float32


def _dot(a, b):
    return jnp.dot(a, b, preferred_element_type=F32)


def _rmsnorm(x, g):
    ms = jnp.mean(x * x, axis=-1, keepdims=True)
    return x * lax.rsqrt(ms + EPS) * g


def _layernorm(x, g, b):
    mu = jnp.mean(x, axis=-1, keepdims=True)
    xc = x - mu
    var = jnp.mean(xc * xc, axis=-1, keepdims=True)
    return xc * lax.rsqrt(var + EPS) * g + b


def _silu(z):
    return z * jax.nn.sigmoid(z)


def _div_pow2(x, n):
    assert n & (n - 1) == 0
    return lax.shift_right_logical(x, n.bit_length() - 1)


def _vmem_limit(nbytes):
    return int(min(V7X_VMEM_BYTES - (4 << 20), nbytes + (12 << 20)))


def _layer_spec(stacked, layer):
    tail = stacked.shape[1:]
    return pl.BlockSpec((None,) + tail, lambda *_: (layer,) + (0,) * len(tail),
                        pipeline_mode=pl.Buffered(1))


def _layer_bytes(*stacked):
    return sum(a[0].size * a.dtype.itemsize for a in stacked)


def _row_spec(rows, width):
    return pl.BlockSpec((rows, width), lambda i: (i, 0))


_SMEM = pl.BlockSpec(memory_space=pltpu.SMEM)


def _gmlp_front(x, npre_ref, lng_ref, lnb_ref, win_ref, exp_a):
    c_rows = x.shape[0] // ROW_CHUNKS
    hn, v = [], []
    for c in range(ROW_CHUNKS):
        hn.append(_rmsnorm(x[c * c_rows:(c + 1) * c_rows], npre_ref[...]).astype(BF16))
        v.append(_layernorm(_dot(hn[-1], win_ref[:, exp_a:2 * exp_a]), lng_ref[...], lnb_ref[...]))
    return jnp.concatenate(hn, axis=0), jnp.concatenate(v, axis=0)


def _gmlp_back(x, gated_ref, wout_ref, npost_ref, o_ref):
    c_rows = x.shape[0] // ROW_CHUNKS
    for c in range(ROW_CHUNKS):
        r = slice(c * c_rows, (c + 1) * c_rows)
        out = _dot(gated_ref[r, :], wout_ref[...])
        o_ref[r, :] = x[r] + _rmsnorm(out, npost_ref[...])


def _gmlp_prompt_kernel(x_ref, npre_ref, npost_ref, lng_ref, lnb_ref, win_ref, wout_ref,
                        ws_ref, bst_ref, o_ref, gated_ref):
    rows = x_ref.shape[0]
    exp_a = wout_ref.shape[0]
    hd = exp_a // N_HEADS
    x = x_ref[...]
    hn, v = _gmlp_front(x, npre_ref, lng_ref, lnb_ref, win_ref, exp_a)
    vb = v.astype(BF16)
    causal = (lax.broadcasted_iota(jnp.int32, (CHUNK, CHUNK), 1)
              <= lax.broadcasted_iota(jnp.int32, (CHUNK, CHUNK), 0))
    for h in range(N_HEADS):
        lo = h * hd
        wsh = jnp.where(causal, ws_ref[h], 0.0).astype(BF16)
        bias = bst_ref[:, h:h + 1]
        u = _dot(hn, win_ref[:, lo:lo + hd])
        z = _dot(hn, win_ref[:, 2 * exp_a + lo:2 * exp_a + lo + hd])
        s = jnp.concatenate(
            [_dot(wsh, vb[c * CHUNK:(c + 1) * CHUNK, lo:lo + hd]) + bias
             for c in range(rows // CHUNK)], axis=0)
        gated_ref[:, lo:lo + hd] = (u * s * _silu(z)).astype(BF16)
    _gmlp_back(x, gated_ref, wout_ref, npost_ref, o_ref)


def _gmlp_sample_kernel(ws_ref, bs_ref, x_ref, npre_ref, npost_ref, lng_ref, lnb_ref, win_ref,
                        wout_ref, o_ref, v_ref, gated_ref, *, n_seq, seq_len, layer):
    exp_a = wout_ref.shape[0]
    hd = exp_a // N_HEADS
    x = x_ref[...]
    hn, v = _gmlp_front(x, npre_ref, lng_ref, lnb_ref, win_ref, exp_a)
    v_ref[...] = v
    for h in range(N_HEADS):
        lo = h * hd
        u = _dot(hn, win_ref[:, lo:lo + hd])
        z = _dot(hn, win_ref[:, 2 * exp_a + lo:2 * exp_a + lo + hd])
        vt = [v[t * n_seq:(t + 1) * n_seq, lo:lo + hd] for t in range(seq_len)]
        parts = []
        for t in range(seq_len):
            b_idx = (layer * N_HEADS + h) * seq_len + t
            s = ws_ref[b_idx * seq_len] * vt[0]
            for t2 in range(1, t + 1):
                s = s + ws_ref[b_idx * seq_len + t2] * vt[t2]
            parts.append(s + bs_ref[b_idx])
        s = jnp.concatenate(parts, axis=0)
        gated_ref[:, lo:lo + hd] = (u * s * _silu(z)).astype(BF16)
    _gmlp_back(x, gated_ref, wout_ref, npost_ref, o_ref)


def _gmlp_prompt(x2d, layer_a, layer, npre, npost, lng, lnb, win, wout, ws, bst):
    rows, d = x2d.shape
    exp_a = wout.shape[1]
    footprint = (_layer_bytes(win, wout, ws) + 4 * ROW_TILE * d * 4
                 + ROW_TILE * exp_a * (2 + 4 + 2 + 8))
    return pl.pallas_call(
        _gmlp_prompt_kernel,
        grid=(rows // ROW_TILE,),
        in_specs=[
            _row_spec(ROW_TILE, d),
            _layer_spec(npre, layer), _layer_spec(npost, layer),
            _layer_spec(lng, layer_a), _layer_spec(lnb, layer_a),
            _layer_spec(win, layer_a), _layer_spec(wout, layer_a),
            _layer_spec(ws, layer_a), _layer_spec(bst, layer_a),
        ],
        out_specs=_row_spec(ROW_TILE, d),
        out_shape=jax.ShapeDtypeStruct((rows, d), F32),
        scratch_shapes=[pltpu.VMEM((ROW_TILE, exp_a), BF16)],
        compiler_params=pltpu.CompilerParams(
            dimension_semantics=("parallel",),
            vmem_limit_bytes=_vmem_limit(footprint)),
        name="gmlp_prompt",
    )(x2d, npre, npost, lng, lnb, win, wout, ws, bst)


def _gmlp_sample(xs, n_seq, seq_len, layer_a, layer, npre, npost, lng, lnb, win, wout,
                 ws_flat, bs_flat):
    rows, d = xs.shape
    exp_a = wout.shape[1]
    footprint = _layer_bytes(win, wout) + 4 * rows * d * 4 + rows * exp_a * (2 + 8 + 4 + 8)
    return pl.pallas_call(
        functools.partial(_gmlp_sample_kernel, n_seq=n_seq, seq_len=seq_len, layer=layer_a),
        grid=(1,),
        in_specs=[_SMEM, _SMEM, _row_spec(rows, d),
                  _layer_spec(npre, layer), _layer_spec(npost, layer),
                  _layer_spec(lng, layer_a), _layer_spec(lnb, layer_a),
                  _layer_spec(win, layer_a), _layer_spec(wout, layer_a)],
        out_specs=[_row_spec(rows, d), _row_spec(rows, exp_a)],
        out_shape=[jax.ShapeDtypeStruct((rows, d), F32),
                   jax.ShapeDtypeStruct((rows, exp_a), F32)],
        scratch_shapes=[pltpu.VMEM((rows, exp_a), BF16)],
        compiler_params=pltpu.CompilerParams(
            dimension_semantics=("arbitrary",),
            vmem_limit_bytes=_vmem_limit(footprint)),
        name="gmlp_sample",
    )(ws_flat, bs_flat, xs, npre, npost, lng, lnb, win, wout)


def _s5_prep_kernel(are_ref, aim_ref, ldt_ref, btr_ref, bti_ref, ctr_ref, cti_ref,
                    l4r_ref, l4i_ref, wur_ref, wui_ref, vc_ref, kloc_ref):
    dt = jnp.exp(ldt_ref[...])
    ar = are_ref[...]
    ai = aim_ref[...]
    mag = jnp.exp(dt * ar)
    ang = dt * ai
    abr = mag * jnp.cos(ang)
    abi = mag * jnp.sin(ang)
    nr = abr - 1.0
    ni = abi
    den = ar * ar + ai * ai
    cre = (nr * ar + ni * ai) / den
    cim = (ni * ar - nr * ai) / den
    btr = btr_ref[...]
    bti = bti_ref[...]
    bbr = cre * btr - cim * bti
    bbi = cre * bti + cim * btr

    def cmul(xr, xi, yr, yi):
        return xr * yr - xi * yi, xr * yi + xi * yr

    lam = [(jnp.ones_like(abr), jnp.zeros_like(abi)), (abr, abi)]
    for _ in range(BLOCK - 1):
        lam.append(cmul(*lam[-1], abr, abi))
    l4r_ref[...] = lam[BLOCK][0]
    l4i_ref[...] = lam[BLOCK][1]

    n_q, k_rows, n_cols = wur_ref.shape
    q_groups = n_cols // STATE_P
    q_rows = q_groups * SSM_GROUP
    same_q = (_div_pow2(lax.broadcasted_iota(jnp.int32, (q_rows, n_cols), 0), SSM_GROUP)
              == _div_pow2(lax.broadcasted_iota(jnp.int32, (q_rows, n_cols), 1), STATE_P))
    for ip in range(BLOCK):
        ur, ui = cmul(bbr, bbi, *lam[BLOCK - 1 - ip])
        for q in range(n_q):
            for src, dst in ((ur, wur_ref), (ui, wui_ref)):
                blk = jnp.concatenate([src[:, q * n_cols:(q + 1) * n_cols]] * q_groups, axis=0)
                dst[q, ip * q_rows:(ip + 1) * q_rows, :] = jnp.where(same_q, blk, 0.0).astype(BF16)

    ctr = ctr_ref[...]
    cti = cti_ref[...]
    cl = [cmul(ctr, cti, *lam[t]) for t in range(BLOCK + 1)]

    def tile_rows(a, q):
        blk = jnp.concatenate([a[:, q * n_cols:(q + 1) * n_cols]] * q_groups, axis=0)
        return jnp.where(same_q, blk, 0.0)

    half = V7X_LANES // 2
    low = lax.broadcasted_iota(jnp.int32, (q_rows, V7X_LANES), 1) < half
    for q in range(n_q):
        clm = [(tile_rows(c[0], q), tile_rows(c[1], q)) for c in cl]
        rows = [jnp.concatenate([clm[i + 1][0], -clm[i + 1][1]], axis=1) for i in range(BLOCK)]
        vc_ref[q] = jnp.concatenate(rows, axis=0).T.astype(BF16)
        bcat = jnp.concatenate([tile_rows(bbr, q), -tile_rows(bbi, q)], axis=1).astype(BF16)
        ccat = jnp.concatenate([jnp.concatenate(clm[t], axis=1) for t in range(BLOCK)], axis=0)
        kall = lax.dot_general(bcat, ccat.astype(BF16), (((1,), (1,)), ((), ())),
                               preferred_element_type=F32)
        k0, k1 = kall[:, :V7X_LANES], kall[:, V7X_LANES:]
        r0, r1 = pltpu.roll(k0, half, 1), pltpu.roll(k1, half, 1)
        zero = jnp.zeros_like(k0)
        shifted = [(k0, k1),
                   (jnp.where(low, zero, r0), jnp.where(low, r0, r1)),
                   (zero, k0),
                   (zero, jnp.where(low, zero, r0))]
        kloc_ref[q] = jnp.concatenate([jnp.concatenate(sh, axis=1) for sh in shifted],
                                      axis=0).astype(BF16)


def _s5_prep(are, aim, ldt, btr, bti, ctr, cti):
    n_layers, _, n_state = are.shape
    n_q = n_state // V7X_MXU_DIM

    def per_layer(shape):
        return pl.BlockSpec((None,) + shape, lambda l: (l,) + (0,) * len(shape))

    out_tails = [(1, n_state), (1, n_state),
                 (n_q, V7X_MXU_DIM, V7X_MXU_DIM), (n_q, V7X_MXU_DIM, V7X_MXU_DIM),
                 (n_q, 2 * V7X_MXU_DIM, V7X_MXU_DIM), (n_q, V7X_MXU_DIM, V7X_MXU_DIM)]
    out_dtypes = [F32, F32, BF16, BF16, BF16, BF16]
    args = (are, aim, ldt, btr, bti, ctr, cti)
    return pl.pallas_call(
        _s5_prep_kernel,
        grid=(n_layers,),
        in_specs=[per_layer(a.shape[1:]) for a in args],
        out_specs=[per_layer(t) for t in out_tails],
        out_shape=[jax.ShapeDtypeStruct((n_layers,) + t, dt) for t, dt in zip(out_tails, out_dtypes)],
        compiler_params=pltpu.CompilerParams(
            dimension_semantics=("parallel",),
            vmem_limit_bytes=_vmem_limit(2 * sum(
                int(jnp.dtype(dt).itemsize) * functools.reduce(lambda a, b: a * b, t)
                for t, dt in zip(out_tails, out_dtypes)))),
        name="s5_prep",
    )(*args)


def _s5_layer_kernel(*refs, n_seq, n_steps, has_h0, swap_bt):
    if has_h0:
        x_ref, h0r_ref, h0i_ref = refs[:3]
        refs = refs[3:]
    else:
        x_ref = refs[0]
        refs = refs[1:]
    (npre_ref, win_ref, l4r_ref, l4i_ref, wur_ref, wui_ref, vc_ref, kloc_ref, dsk_ref,
     w1_ref, b1_ref, w2_ref, b2_ref, wout_ref, npost_ref,
     o_ref, str_ref, sti_ref, hpr_ref, hpi_ref, hr_ref, hi_ref, y_ref) = refs
    step = pl.program_id(0)

    @pl.when(step == 0)
    def _():
        if has_h0:
            hr_ref[...] = h0r_ref[...]
            hi_ref[...] = h0i_ref[...]
        else:
            hr_ref[...] = jnp.zeros_like(hr_ref)
            hi_ref[...] = jnp.zeros_like(hi_ref)

    d = x_ref.shape[-1]
    width = win_ref.shape[1] // 2
    c_steps = n_steps // ROW_CHUNKS
    c_rows = c_steps * n_seq
    x_c, xb_c, z_c = [], [], []
    for c in range(ROW_CHUNKS):
        if swap_bt:
            xc = jnp.swapaxes(x_ref[:, c * c_steps:(c + 1) * c_steps, :], 0, 1).reshape(c_rows, d)
        else:
            xc = x_ref[c * c_rows:(c + 1) * c_rows, :]
        hn = _rmsnorm(xc, npre_ref[...]).astype(BF16)
        x_c.append(xc)
        xb_c.append(_dot(hn, win_ref[:, :width]))
        z_c.append(_dot(hn, win_ref[:, width:]))
    xb = jnp.concatenate(xb_c, axis=0)

    n_blocks = n_steps // BLOCK
    rows = n_blocks * n_seq
    x4 = xb.reshape(n_blocks, BLOCK, n_seq, width)
    xi = [x4[:, i].reshape(rows, width) for i in range(BLOCK)]

    sub = V7X_SUBLANES
    pair = 2 * sub
    n_q, _, n_cols = wur_ref.shape
    n_v = width // V7X_LANES
    half = V7X_LANES // 2
    low = lax.broadcasted_iota(jnp.int32, (rows, V7X_LANES), 1) < half

    lhs_q = []
    for v in range(n_v):
        cols = slice(v * V7X_LANES, (v + 1) * V7X_LANES)
        p = [a[:, cols] for a in xi]
        pr = [pltpu.roll(a, half, 1) for a in p]
        lhs_lo = jnp.concatenate([jnp.where(low, p[i], pr[i + 1]) for i in range(0, BLOCK, 2)], axis=1)
        lhs_hi = jnp.concatenate([jnp.where(low, pr[i], p[i + 1]) for i in range(0, BLOCK, 2)], axis=1)
        for q, lhs in ((2 * v, lhs_lo.astype(BF16)), (2 * v + 1, lhs_hi.astype(BF16))):
            lanes = pl.ds(q * n_cols, n_cols)
            wr = _dot(lhs, wur_ref[q])
            wi = _dot(lhs, wui_ref[q])
            if n_seq == sub:
                ar = jnp.broadcast_to(l4r_ref[:, lanes], (sub, n_cols))
                ai = jnp.broadcast_to(l4i_ref[:, lanes], (sub, n_cols))
                hr, hi = hr_ref[:, lanes], hi_ref[:, lanes]
                for k in range(n_blocks // 2):
                    prv_r, prv_i = [], []
                    for r0 in (2 * k * sub, (2 * k + 1) * sub):
                        prv_r.append(hr)
                        prv_i.append(hi)
                        hr, hi = (ar * hr - ai * hi + wr[r0:r0 + sub],
                                  ar * hi + ai * hr + wi[r0:r0 + sub])
                    hpr_ref[k * pair:(k + 1) * pair, lanes] = jnp.concatenate(prv_r, axis=0).astype(BF16)
                    hpi_ref[k * pair:(k + 1) * pair, lanes] = jnp.concatenate(prv_i, axis=0).astype(BF16)
                hr_ref[:, lanes] = hr
                hi_ref[:, lanes] = hi
            else:
                ar = jnp.broadcast_to(l4r_ref[:, lanes], (pair, n_cols))
                ai = jnp.broadcast_to(l4i_ref[:, lanes], (pair, n_cols))
                for m in range(n_seq // pair):
                    srows = pl.ds(m * pair, pair)
                    hr, hi = hr_ref[srows, lanes], hi_ref[srows, lanes]
                    for blk in range(n_blocks):
                        r0 = blk * n_seq + m * pair
                        hpr_ref[r0:r0 + pair, lanes] = hr.astype(BF16)
                        hpi_ref[r0:r0 + pair, lanes] = hi.astype(BF16)
                        hr, hi = (ar * hr - ai * hi + wr[r0:r0 + pair],
                                  ar * hi + ai * hr + wi[r0:r0 + pair])
                    hr_ref[srows, lanes] = hr
                    hi_ref[srows, lanes] = hi
            lhs_q.append(lhs)

    for v in range(n_v):
        cols = slice(v * V7X_LANES, (v + 1) * V7X_LANES)
        yq = []
        for q in (2 * v, 2 * v + 1):
            lanes = pl.ds(q * n_cols, n_cols)
            st = jnp.concatenate([hpr_ref[:, lanes], hpi_ref[:, lanes]], axis=1)
            yq.append(_dot(st, vc_ref[q]) + _dot(lhs_q[q], kloc_ref[q]))
        ys = []
        for i in range(BLOCK):
            lo_q, hi_q = (a[:, (i // 2) * V7X_LANES:(i // 2 + 1) * V7X_LANES] for a in yq)
            if i % 2 == 0:
                yv = jnp.where(low, lo_q, pltpu.roll(hi_q, half, 1))
            else:
                yv = jnp.where(low, pltpu.roll(lo_q, half, 1), hi_q)
            ys.append(jax.nn.gelu(yv + dsk_ref[:, cols] * xi[i][:, cols]))
        y = jnp.stack([a.reshape(n_blocks, n_seq, V7X_LANES) for a in ys], axis=1)
        y_ref[:, cols] = y.reshape(n_steps * n_seq, V7X_LANES).astype(BF16)

    for c in range(ROW_CHUNKS):
        y = y_ref[c * c_rows:(c + 1) * c_rows, :]
        g = (_dot(y, w1_ref[...]) + b1_ref[...]) * jax.nn.sigmoid(_dot(y, w2_ref[...]) + b2_ref[...])
        out = _dot((g * _silu(z_c[c])).astype(BF16), wout_ref[...])
        xn = x_c[c] + _rmsnorm(out, npost_ref[...])
        if swap_bt:
            o_ref[:, c * c_steps:(c + 1) * c_steps, :] = jnp.swapaxes(
                xn.reshape(c_steps, n_seq, d), 0, 1)
        else:
            o_ref[c * c_rows:(c + 1) * c_rows, :] = xn

    @pl.when(step == pl.num_programs(0) - 1)
    def _():
        str_ref[...] = hr_ref[...]
        sti_ref[...] = hi_ref[...]


def _s5_layer(x, h0, layer_b, layer, npre, win, core_consts, out_consts, npost, n_seq, n_steps):
    swap_bt = x.ndim == 3
    d = x.shape[-1]
    exp_b = win.shape[2] // 2
    n_state = core_consts[0].shape[-1]
    tile = n_seq * n_steps
    n_tiles = x.size // (tile * d)
    rows = tile // BLOCK
    has_h0 = h0 is not None
    assert n_steps % BLOCK == 0 and (n_seq > V7X_SUBLANES or (n_steps // BLOCK) % 2 == 0)
    params_b = [win] + list(core_consts) + list(out_consts)
    args = [x] + (list(h0) if has_h0 else []) + [npre] + params_b + [npost]
    if swap_bt:
        io_spec = pl.BlockSpec((n_seq, n_steps, d), lambda i: (0, i, 0))
    else:
        io_spec = pl.BlockSpec((tile, d), lambda i: (i, 0))
    in_specs = ([io_spec] + [_layer_spec(a, layer_b) for a in (h0 if has_h0 else ())]
                + [_layer_spec(npre, layer)] + [_layer_spec(a, layer_b) for a in params_b]
                + [_layer_spec(npost, layer)])
    st_spec = pl.BlockSpec((n_seq, n_state), lambda i: (0, 0))
    st_shape = jax.ShapeDtypeStruct((n_seq, n_state), F32)
    footprint = (_layer_bytes(*params_b) + (_layer_bytes(*h0) if has_h0 else 0) + 4 * tile * d * 4
                 + 2 * rows * n_state * 2 + 8 * n_seq * n_state * 4 + tile * exp_b * (2 + 6 * 4))
    return pl.pallas_call(
        functools.partial(_s5_layer_kernel, n_seq=n_seq, n_steps=n_steps, has_h0=has_h0,
                          swap_bt=swap_bt),
        grid=(n_tiles,),
        in_specs=in_specs,
        out_specs=[io_spec, st_spec, st_spec],
        out_shape=[jax.ShapeDtypeStruct(x.shape, F32), st_shape, st_shape],
        scratch_shapes=[pltpu.VMEM((rows, n_state), BF16), pltpu.VMEM((rows, n_state), BF16),
                        pltpu.VMEM((n_seq, n_state), F32), pltpu.VMEM((n_seq, n_state), F32),
                        pltpu.VMEM((tile, exp_b), BF16)],
        compiler_params=pltpu.CompilerParams(
            dimension_semantics=("arbitrary",),
            vmem_limit_bytes=_vmem_limit(footprint)),
        name="s5_layer_sample" if has_h0 else "s5_layer_prompt",
    )(*args)


def kernel(x_prompt, x_sample, state_ssm_re, state_ssm_im, norm_pre, norm_post,
           w_in_a, ln_v_g, ln_v_b, w_s, b_s, w_out_a,
           w_in_b, a_re, a_im, log_dt, b_re, b_im, c_re, c_im, d_skip,
           w_glu1, b_glu1, w_glu2, b_glu2, w_out_b):
    n_batch, seq_len, d = x_prompt.shape
    n_dec, dec_len, _ = x_sample.shape
    depth = norm_pre.shape[0]
    n_ssm, n_groups, state_p = a_re.shape
    n_state = n_groups * state_p
    assert seq_len % ROW_TILE == 0 and ROW_TILE % CHUNK == 0 and seq_len % SCAN_STEPS == 0
    assert n_batch == V7X_SUBLANES and n_dec % (2 * V7X_SUBLANES) == 0
    assert dec_len <= CHUNK and dec_len % BLOCK == 0
    assert state_p == STATE_P and b_re.shape[-1] == SSM_GROUP and w_s.shape[1] == N_HEADS

    rows3 = lambda a: a.reshape(a.shape[0], 1, -1)
    npre, npost = rows3(norm_pre), rows3(norm_post)
    lng, lnb = rows3(ln_v_g), rows3(ln_v_b)
    w_in_a16, w_out_a16 = w_in_a.astype(BF16), w_out_a.astype(BF16)
    bst = jnp.swapaxes(b_s, 1, 2)
    ws_dec = w_s[:, :, :dec_len, :dec_len].reshape(-1)
    bs_dec = b_s[:, :, :dec_len].reshape(-1)

    w_in_b16 = w_in_b.astype(BF16)
    out_consts = (w_glu1.astype(BF16), rows3(b_glu1), w_glu2.astype(BF16), rows3(b_glu2),
                  w_out_b.astype(BF16))
    lanes_gp = lambda a, perm: jnp.transpose(a, perm).reshape(n_ssm, SSM_GROUP, n_state)
    l4r, l4i, wur, wui, vcw, kloc = _s5_prep(
        rows3(a_re), rows3(a_im), rows3(jnp.repeat(log_dt, state_p, axis=1)),
        lanes_gp(b_re, (0, 3, 1, 2)), lanes_gp(b_im, (0, 3, 1, 2)),
        lanes_gp(c_re, (0, 2, 1, 3)), lanes_gp(c_im, (0, 2, 1, 3)))
    core_consts = (l4r, l4i, wur, wui, vcw, kloc, rows3(d_skip))
    h0 = (state_ssm_re.reshape(n_ssm, n_dec, n_state), state_ssm_im.reshape(n_ssm, n_dec, n_state))

    xp = x_prompt.reshape(n_batch * seq_len, d)
    xs = jnp.transpose(x_sample, (1, 0, 2)).reshape(dec_len * n_dec, d)

    v_rows, st_p_re, st_p_im, st_s_re, st_s_im = [], [], [], [], []
    for i in range(depth):
        j = i // 2
        if i % 2 == 0:
            xp = _gmlp_prompt(xp, j, i, npre, npost, lng, lnb, w_in_a16, w_out_a16, w_s, bst)
            xs, v = _gmlp_sample(xs, n_dec, dec_len, j, i, npre, npost, lng, lnb,
                                 w_in_a16, w_out_a16, ws_dec, bs_dec)
            v_rows.append(v)
        else:
            xp3, sr, si = _s5_layer(xp.reshape(n_batch, seq_len, d), None, j, i, npre, w_in_b16,
                                    core_consts, out_consts, npost, n_batch, SCAN_STEPS)
            xp = xp3.reshape(n_batch * seq_len, d)
            st_p_re.append(sr)
            st_p_im.append(si)
            xs, sr, si = _s5_layer(xs, h0, j, i, npre, w_in_b16, core_consts, out_consts, npost,
                                   n_dec, dec_len)
            st_s_re.append(sr)
            st_s_im.append(si)

    y_prompt = xp.reshape(n_batch, seq_len, d)
    y_sample = jnp.transpose(xs.reshape(dec_len, n_dec, d), (1, 0, 2))
    chunk_v = jnp.transpose(jnp.stack(v_rows).reshape(len(v_rows), dec_len, n_dec, -1), (0, 2, 1, 3))
    states = lambda parts, n: jnp.stack(parts).reshape(len(parts), n, n_groups, state_p)
    return (y_prompt, y_sample, chunk_v, states(st_p_re, n_batch), states(st_p_im, n_batch),
            states(st_s_re, n_dec), states(st_s_im, n_dec))
```

```python
import functools

import jax
import jax.numpy as jnp
from jax import lax
from jax.experimental import pallas as pl
from jax.experimental.pallas import tpu as pltpu

EPS = 1e-6
CHUNK = 128
N_HEADS = 8
SSM_GROUP = 16
STATE_P = 64

V7X_LANES = 128
V7X_SUBLANES = 8
V7X_MXU_DIM = 256
V7X_VMEM_BYTES = 64 * 1024 * 1024

ROW_TILE = 1024
SCAN_STEPS = 64
BLOCK = 4
ROW_CHUNKS = 2

BF16 = jnp.bfloat16
F32 = jnp.float32


def _dot(a, b):
    return jnp.dot(a, b, preferred_element_type=F32)


def _rmsnorm(x, g):
    ms = jnp.mean(x * x, axis=-1, keepdims=True)
    return x * lax.rsqrt(ms + EPS) * g


def _layernorm(x, g, b):
    mu = jnp.mean(x, axis=-1, keepdims=True)
    xc = x - mu
    var = jnp.mean(xc * xc, axis=-1, keepdims=True)
    return xc * lax.rsqrt(var + EPS) * g + b


def _silu(z):
    return z * jax.nn.sigmoid(z)


def _div_pow2(x, n):
    assert n & (n - 1) == 0
    return lax.shift_right_logical(x, n.bit_length() - 1)


def _vmem_limit(nbytes):
    return int(min(V7X_VMEM_BYTES - (4 << 20), nbytes + (12 << 20)))


def _layer_spec(stacked, layer):
    tail = stacked.shape[1:]
    return pl.BlockSpec((None,) + tail, lambda *_: (layer,) + (0,) * len(tail),
                        pipeline_mode=pl.Buffered(1))


def _layer_bytes(*stacked):
    return sum(a[0].size * a.dtype.itemsize for a in stacked)


def _row_spec(rows, width):
    return pl.BlockSpec((rows, width), lambda i: (i, 0))


_SMEM = pl.BlockSpec(memory_space=pltpu.SMEM)


def _gmlp_front(x, npre_ref, lng_ref, lnb_ref, win_ref, exp_a):
    c_rows = x.shape[0] // ROW_CHUNKS
    hn, v = [], []
    for c in range(ROW_CHUNKS):
        hn.append(_rmsnorm(x[c * c_rows:(c + 1) * c_rows], npre_ref[...]).astype(BF16))
        v.append(_layernorm(_dot(hn[-1], win_ref[:, exp_a:2 * exp_a]), lng_ref[...], lnb_ref[...]))
    return jnp.concatenate(hn, axis=0), jnp.concatenate(v, axis=0)


def _gmlp_back(x, gated_ref, wout_ref, npost_ref, o_ref):
    c_rows = x.shape[0] // ROW_CHUNKS
    for c in range(ROW_CHUNKS):
        r = slice(c * c_rows, (c + 1) * c_rows)
        out = _dot(gated_ref[r, :], wout_ref[...])
        o_ref[r, :] = x[r] + _rmsnorm(out, npost_ref[...])


def _gmlp_prompt_kernel(x_ref, npre_ref, npost_ref, lng_ref, lnb_ref, win_ref, wout_ref,
                        ws_ref, bst_ref, o_ref, gated_ref):
    rows = x_ref.shape[0]
    exp_a = wout_ref.shape[0]
    hd = exp_a // N_HEADS
    x = x_ref[...]
    hn, v = _gmlp_front(x, npre_ref, lng_ref, lnb_ref, win_ref, exp_a)
    vb = v.astype(BF16)
    causal = (lax.broadcasted_iota(jnp.int32, (CHUNK, CHUNK), 1)
              <= lax.broadcasted_iota(jnp.int32, (CHUNK, CHUNK), 0))
    for h in range(N_HEADS):
        lo = h * hd
        wsh = jnp.where(causal, ws_ref[h], 0.0).astype(BF16)
        bias = bst_ref[:, h:h + 1]
        u = _dot(hn, win_ref[:, lo:lo + hd])
        z = _dot(hn, win_ref[:, 2 * exp_a + lo:2 * exp_a + lo + hd])
        s = jnp.concatenate(
            [_dot(wsh, vb[c * CHUNK:(c + 1) * CHUNK, lo:lo + hd]) + bias
             for c in range(rows // CHUNK)], axis=0)
        gated_ref[:, lo:lo + hd] = (u * s * _silu(z)).astype(BF16)
    _gmlp_back(x, gated_ref, wout_ref, npost_ref, o_ref)


def _gmlp_sample_kernel(ws_ref, bs_ref, x_ref, npre_ref, npost_ref, lng_ref, lnb_ref, win_ref,
                        wout_ref, o_ref, v_ref, gated_ref, *, n_seq, seq_len, layer):
    exp_a = wout_ref.shape[0]
    hd = exp_a // N_HEADS
    x = x_ref[...]
    hn, v = _gmlp_front(x, npre_ref, lng_ref, lnb_ref, win_ref, exp_a)
    v_ref[...] = v
    for h in range(N_HEADS):
        lo = h * hd
        u = _dot(hn, win_ref[:, lo:lo + hd])
        z = _dot(hn, win_ref[:, 2 * exp_a + lo:2 * exp_a + lo + hd])
        vt = [v[t * n_seq:(t + 1) * n_seq, lo:lo + hd] for t in range(seq_len)]
        parts = []
        for t in range(seq_len):
            b_idx = (layer * N_HEADS + h) * seq_len + t
            s = ws_ref[b_idx * seq_len] * vt[0]
            for t2 in range(1, t + 1):
                s = s + ws_ref[b_idx * seq_len + t2] * vt[t2]
            parts.append(s + bs_ref[b_idx])
        s = jnp.concatenate(parts, axis=0)
        gated_ref[:, lo:lo + hd] = (u * s * _silu(z)).astype(BF16)
    _gmlp_back(x, gated_ref, wout_ref, npost_ref, o_ref)


def _gmlp_prompt(x2d, layer_a, layer, npre, npost, lng, lnb, win, wout, ws, bst):
    rows, d = x2d.shape
    exp_a = wout.shape[1]
    footprint = (_layer_bytes(win, wout, ws) + 4 * ROW_TILE * d * 4
                 + ROW_TILE * exp_a * (2 + 4 + 2 + 8))
    return pl.pallas_call(
        _gmlp_prompt_kernel,
        grid=(rows // ROW_TILE,),
        in_specs=[
            _row_spec(ROW_TILE, d),
            _layer_spec(npre, layer), _layer_spec(npost, layer),
            _layer_spec(lng, layer_a), _layer_spec(lnb, layer_a),
            _layer_spec(win, layer_a), _layer_spec(wout, layer_a),
            _layer_spec(ws, layer_a), _layer_spec(bst, layer_a),
        ],
        out_specs=_row_spec(ROW_TILE, d),
        out_shape=jax.ShapeDtypeStruct((rows, d), F32),
        scratch_shapes=[pltpu.VMEM((ROW_TILE, exp_a), BF16)],
        compiler_params=pltpu.CompilerParams(
            dimension_semantics=("parallel",),
            vmem_limit_bytes=_vmem_limit(footprint)),
        name="gmlp_prompt",
    )(x2d, npre, npost, lng, lnb, win, wout, ws, bst)


def _gmlp_sample(xs, n_seq, seq_len, layer_a, layer, npre, npost, lng, lnb, win, wout,
                 ws_flat, bs_flat):
    rows, d = xs.shape
    exp_a = wout.shape[1]
    footprint = _layer_bytes(win, wout) + 4 * rows * d * 4 + rows * exp_a * (2 + 8 + 4 + 8)
    return pl.pallas_call(
        functools.partial(_gmlp_sample_kernel, n_seq=n_seq, seq_len=seq_len, layer=layer_a),
        grid=(1,),
        in_specs=[_SMEM, _SMEM, _row_spec(rows, d),
                  _layer_spec(npre, layer), _layer_spec(npost, layer),
                  _layer_spec(lng, layer_a), _layer_spec(lnb, layer_a),
                  _layer_spec(win, layer_a), _layer_spec(wout, layer_a)],
        out_specs=[_row_spec(rows, d), _row_spec(rows, exp_a)],
        out_shape=[jax.ShapeDtypeStruct((rows, d), F32),
                   jax.ShapeDtypeStruct((rows, exp_a), F32)],
        scratch_shapes=[pltpu.VMEM((rows, exp_a), BF16)],
        compiler_params=pltpu.CompilerParams(
            dimension_semantics=("arbitrary",),
            vmem_limit_bytes=_vmem_limit(footprint)),
        name="gmlp_sample",
    )(ws_flat, bs_flat, xs, npre, npost, lng, lnb, win, wout)


def _s5_prep_kernel(are_ref, aim_ref, ldt_ref, btr_ref, bti_ref, ctr_ref, cti_ref,
                    l4r_ref, l4i_ref, wur_ref, wui_ref, vc_ref, kloc_ref):
    dt = jnp.exp(ldt_ref[...])
    ar = are_ref[...]
    ai = aim_ref[...]
    mag = jnp.exp(dt * ar)
    ang = dt * ai
    abr = mag * jnp.cos(ang)
    abi = mag * jnp.sin(ang)
    nr = abr - 1.0
    ni = abi
    den = ar * ar + ai * ai
    cre = (nr * ar + ni * ai) / den
    cim = (ni * ar - nr * ai) / den
    btr = btr_ref[...]
    bti = bti_ref[...]
    bbr = cre * btr - cim * bti
    bbi = cre * bti + cim * btr

    def cmul(xr, xi, yr, yi):
        return xr * yr - xi * yi, xr * yi + xi * yr

    lam = [(jnp.ones_like(abr), jnp.zeros_like(abi)), (abr, abi)]
    for _ in range(BLOCK - 1):
        lam.append(cmul(*lam[-1], abr, abi))
    l4r_ref[...] = lam[BLOCK][0]
    l4i_ref[...] = lam[BLOCK][1]

    n_q, k_rows, n_cols = wur_ref.shape
    q_groups = n_cols // STATE_P
    q_rows = q_groups * SSM_GROUP
    same_q = (_div_pow2(lax.broadcasted_iota(jnp.int32, (q_rows, n_cols), 0), SSM_GROUP)
              == _div_pow2(lax.broadcasted_iota(jnp.int32, (q_rows, n_cols), 1), STATE_P))
    for ip in range(BLOCK):
        ur, ui = cmul(bbr, bbi, *lam[BLOCK - 1 - ip])
        for q in range(n_q):
            for src, dst in ((ur, wur_ref), (ui, wui_ref)):
                blk = jnp.concatenate([src[:, q * n_cols:(q + 1) * n_cols]] * q_groups, axis=0)
                dst[q, ip * q_rows:(ip + 1) * q_rows, :] = jnp.where(same_q, blk, 0.0).astype(BF16)

    ctr = ctr_ref[...]
    cti = cti_ref[...]
    cl = [cmul(ctr, cti, *lam[t]) for t in range(BLOCK + 1)]

    def tile_rows(a, q):
        blk = jnp.concatenate([a[:, q * n_cols:(q + 1) * n_cols]] * q_groups, axis=0)
        return jnp.where(same_q, blk, 0.0)

    half = V7X_LANES // 2
    low = lax.broadcasted_iota(jnp.int32, (q_rows, V7X_LANES), 1) < half
    for q in range(n_q):
        clm = [(tile_rows(c[0], q), tile_rows(c[1], q)) for c in cl]
        rows = [jnp.concatenate([clm[i + 1][0], -clm[i + 1][1]], axis=1) for i in range(BLOCK)]
        vc_ref[q] = jnp.concatenate(rows, axis=0).T.astype(BF16)
        bcat = jnp.concatenate([tile_rows(bbr, q), -tile_rows(bbi, q)], axis=1).astype(BF16)
        ccat = jnp.concatenate([jnp.concatenate(clm[t], axis=1) for t in range(BLOCK)], axis=0)
        kall = lax.dot_general(bcat, ccat.astype(BF16), (((1,), (1,)), ((), ())),
                               preferred_element_type=F32)
        k0, k1 = kall[:, :V7X_LANES], kall[:, V7X_LANES:]
        r0, r1 = pltpu.roll(k0, half, 1), pltpu.roll(k1, half, 1)
        zero = jnp.zeros_like(k0)
        shifted = [(k0, k1),
                   (jnp.where(low, zero, r0), jnp.where(low, r0, r1)),
                   (zero, k0),
                   (zero, jnp.where(low, zero, r0))]
        kloc_ref[q] = jnp.concatenate([jnp.concatenate(sh, axis=1) for sh in shifted],
                                      axis=0).astype(BF16)


def _s5_prep(are, aim, ldt, btr, bti, ctr, cti):
    n_layers, _, n_state = are.shape
    n_q = n_state // V7X_MXU_DIM

    def per_layer(shape):
        return pl.BlockSpec((None,) + shape, lambda l: (l,) + (0,) * len(shape))

    out_tails = [(1, n_state), (1, n_state),
                 (n_q, V7X_MXU_DIM, V7X_MXU_DIM), (n_q, V7X_MXU_DIM, V7X_MXU_DIM),
                 (n_q, 2 * V7X_MXU_DIM, V7X_MXU_DIM), (n_q, V7X_MXU_DIM, V7X_MXU_DIM)]
    out_dtypes = [F32, F32, BF16, BF16, BF16, BF16]
    args = (are, aim, ldt, btr, bti, ctr, cti)
    return pl.pallas_call(
        _s5_prep_kernel,
        grid=(n_layers,),
        in_specs=[per_layer(a.shape[1:]) for a in args],
        out_specs=[per_layer(t) for t in out_tails],
        out_shape=[jax.ShapeDtypeStruct((n_layers,) + t, dt) for t, dt in zip(out_tails, out_dtypes)],
        compiler_params=pltpu.CompilerParams(
            dimension_semantics=("parallel",),
            vmem_limit_bytes=_vmem_limit(2 * sum(
                int(jnp.dtype(dt).itemsize) * functools.reduce(lambda a, b: a * b, t)
                for t, dt in zip(out_tails, out_dtypes)))),
        name="s5_prep",
    )(*args)


def _s5_layer_kernel(*refs, n_seq, n_steps, has_h0, swap_bt):
    if has_h0:
        x_ref, h0r_ref, h0i_ref = refs[:3]
        refs = refs[3:]
    else:
        x_ref = refs[0]
        refs = refs[1:]
    (npre_ref, win_ref, l4r_ref, l4i_ref, wur_ref, wui_ref, vc_ref, kloc_ref, dsk_ref,
     w1_ref, b1_ref, w2_ref, b2_ref, wout_ref, npost_ref,
     o_ref, str_ref, sti_ref, hpr_ref, hpi_ref, hr_ref, hi_ref, y_ref) = refs
    step = pl.program_id(0)

    @pl.when(step == 0)
    def _():
        if has_h0:
            hr_ref[...] = h0r_ref[...]
            hi_ref[...] = h0i_ref[...]
        else:
            hr_ref[...] = jnp.zeros_like(hr_ref)
            hi_ref[...] = jnp.zeros_like(hi_ref)

    d = x_ref.shape[-1]
    width = win_ref.shape[1] // 2
    c_steps = n_steps // ROW_CHUNKS
    c_rows = c_steps * n_seq
    x_c, xb_c, z_c = [], [], []
    for c in range(ROW_CHUNKS):
        if swap_bt:
            xc = jnp.swapaxes(x_ref[:, c * c_steps:(c + 1) * c_steps, :], 0, 1).reshape(c_rows, d)
        else:
            xc = x_ref[c * c_rows:(c + 1) * c_rows, :]
        hn = _rmsnorm(xc, npre_ref[...]).astype(BF16)
        x_c.append(xc)
        xb_c.append(_dot(hn, win_ref[:, :width]))
        z_c.append(_dot(hn, win_ref[:, width:]))
    xb = jnp.concatenate(xb_c, axis=0)

    n_blocks = n_steps // BLOCK
    rows = n_blocks * n_seq
    x4 = xb.reshape(n_blocks, BLOCK, n_seq, width)
    xi = [x4[:, i].reshape(rows, width) for i in range(BLOCK)]

    sub = V7X_SUBLANES
    pair = 2 * sub
    n_q, _, n_cols = wur_ref.shape
    n_v = width // V7X_LANES
    half = V7X_LANES // 2
    low = lax.broadcasted_iota(jnp.int32, (rows, V7X_LANES), 1) < half

    lhs_q = []
    for v in range(n_v):
        cols = slice(v * V7X_LANES, (v + 1) * V7X_LANES)
        p = [a[:, cols] for a in xi]
        pr = [pltpu.roll(a, half, 1) for a in p]
        lhs_lo = jnp.concatenate([jnp.where(low, p[i], pr[i + 1]) for i in range(0, BLOCK, 2)], axis=1)
        lhs_hi = jnp.concatenate([jnp.where(low, pr[i], p[i + 1]) for i in range(0, BLOCK, 2)], axis=1)
        for q, lhs in ((2 * v, lhs_lo.astype(BF16)), (2 * v + 1, lhs_hi.astype(BF16))):
            lanes = pl.ds(q * n_cols, n_cols)
            wr = _dot(lhs, wur_ref[q])
            wi = _dot(lhs, wui_ref[q])
            if n_seq == sub:
                ar = jnp.broadcast_to(l4r_ref[:, lanes], (sub, n_cols))
                ai = jnp.broadcast_to(l4i_ref[:, lanes], (sub, n_cols))
                hr, hi = hr_ref[:, lanes], hi_ref[:, lanes]
                for k in range(n_blocks // 2):
                    prv_r, prv_i = [], []
                    for r0 in (2 * k * sub, (2 * k + 1) * sub):
                        prv_r.append(hr)
                        prv_i.append(hi)
                        hr, hi = (ar * hr - ai * hi + wr[r0:r0 + sub],
                                  ar * hi + ai * hr + wi[r0:r0 + sub])
                    hpr_ref[k * pair:(k + 1) * pair, lanes] = jnp.concatenate(prv_r, axis=0).astype(BF16)
                    hpi_ref[k * pair:(k + 1) * pair, lanes] = jnp.concatenate(prv_i, axis=0).astype(BF16)
                hr_ref[:, lanes] = hr
                hi_ref[:, lanes] = hi
            else:
                ar = jnp.broadcast_to(l4r_ref[:, lanes], (pair, n_cols))
                ai = jnp.broadcast_to(l4i_ref[:, lanes], (pair, n_cols))
                for m in range(n_seq // pair):
                    srows = pl.ds(m * pair, pair)
                    hr, hi = hr_ref[srows, lanes], hi_ref[srows, lanes]
                    for blk in range(n_blocks):
                        r0 = blk * n_seq + m * pair
                        hpr_ref[r0:r0 + pair, lanes] = hr.astype(BF16)
                        hpi_ref[r0:r0 + pair, lanes] = hi.astype(BF16)
                        hr, hi = (ar * hr - ai * hi + wr[r0:r0 + pair],
                                  ar * hi + ai * hr + wi[r0:r0 + pair])
                    hr_ref[srows, lanes] = hr
                    hi_ref[srows, lanes] = hi
            lhs_q.append(lhs)

    for v in range(n_v):
        cols = slice(v * V7X_LANES, (v + 1) * V7X_LANES)
        yq = []
        for q in (2 * v, 2 * v + 1):
            lanes = pl.ds(q * n_cols, n_cols)
            st = jnp.concatenate([hpr_ref[:, lanes], hpi_ref[:, lanes]], axis=1)
            yq.append(_dot(st, vc_ref[q]) + _dot(lhs_q[q], kloc_ref[q]))
        ys = []
        for i in range(BLOCK):
            lo_q, hi_q = (a[:, (i // 2) * V7X_LANES:(i // 2 + 1) * V7X_LANES] for a in yq)
            if i % 2 == 0:
                yv = jnp.where(low, lo_q, pltpu.roll(hi_q, half, 1))
            else:
                yv = jnp.where(low, pltpu.roll(lo_q, half, 1), hi_q)
            ys.append(jax.nn.gelu(yv + dsk_ref[:, cols] * xi[i][:, cols]))
        y = jnp.stack([a.reshape(n_blocks, n_seq, V7X_LANES) for a in ys], axis=1)
        y_ref[:, cols] = y.reshape(n_steps * n_seq, V7X_LANES).astype(BF16)

    for c in range(ROW_CHUNKS):
        y = y_ref[c * c_rows:(c + 1) * c_rows, :]
        g = (_dot(y, w1_ref[...]) + b1_ref[...]) * jax.nn.sigmoid(_dot(y, w2_ref[...]) + b2_ref[...])
        out = _dot((g * _silu(z_c[c])).astype(BF16), wout_ref[...])
        xn = x_c[c] + _rmsnorm(out, npost_ref[...])
        if swap_bt:
            o_ref[:, c * c_steps:(c + 1) * c_steps, :] = jnp.swapaxes(
                xn.reshape(c_steps, n_seq, d), 0, 1)
        else:
            o_ref[c * c_rows:(c + 1) * c_rows, :] = xn

    @pl.when(step == pl.num_programs(0) - 1)
    def _():
        str_ref[...] = hr_ref[...]
        sti_ref[...] = hi_ref[...]


def _s5_layer(x, h0, layer_b, layer, npre, win, core_consts, out_consts, npost, n_seq, n_steps):
    swap_bt = x.ndim == 3
    d = x.shape[-1]
    exp_b = win.shape[2] // 2
    n_state = core_consts[0].shape[-1]
    tile = n_seq * n_steps
    n_tiles = x.size // (tile * d)
    rows = tile // BLOCK
    has_h0 = h0 is not None
    assert n_steps % BLOCK == 0 and (n_seq > V7X_SUBLANES or (n_steps // BLOCK) % 2 == 0)
    params_b = [win] + list(core_consts) + list(out_consts)
    args = [x] + (list(h0) if has_h0 else []) + [npre] + params_b + [npost]
    if swap_bt:
        io_spec = pl.BlockSpec((n_seq, n_steps, d), lambda i: (0, i, 0))
    else:
        io_spec = pl.BlockSpec((tile, d), lambda i: (i, 0))
    in_specs = ([io_spec] + [_layer_spec(a, layer_b) for a in (h0 if has_h0 else ())]
                + [_layer_spec(npre, layer)] + [_layer_spec(a, layer_b) for a in params_b]
                + [_layer_spec(npost, layer)])
    st_spec = pl.BlockSpec((n_seq, n_state), lambda i: (0, 0))
    st_shape = jax.ShapeDtypeStruct((n_seq, n_state), F32)
    footprint = (_layer_bytes(*params_b) + (_layer_bytes(*h0) if has_h0 else 0) + 4 * tile * d * 4
                 + 2 * rows * n_state * 2 + 8 * n_seq * n_state * 4 + tile * exp_b * (2 + 6 * 4))
    return pl.pallas_call(
        functools.partial(_s5_layer_kernel, n_seq=n_seq, n_steps=n_steps, has_h0=has_h0,
                          swap_bt=swap_bt),
        grid=(n_tiles,),
        in_specs=in_specs,
        out_specs=[io_spec, st_spec, st_spec],
        out_shape=[jax.ShapeDtypeStruct(x.shape, F32), st_shape, st_shape],
        scratch_shapes=[pltpu.VMEM((rows, n_state), BF16), pltpu.VMEM((rows, n_state), BF16),
                        pltpu.VMEM((n_seq, n_state), F32), pltpu.VMEM((n_seq, n_state), F32),
                        pltpu.VMEM((tile, exp_b), BF16)],
        compiler_params=pltpu.CompilerParams(
            dimension_semantics=("arbitrary",),
            vmem_limit_bytes=_vmem_limit(footprint)),
        name="s5_layer_sample" if has_h0 else "s5_layer_prompt",
    )(*args)


def kernel(x_prompt, x_sample, state_ssm_re, state_ssm_im, norm_pre, norm_post,
           w_in_a, ln_v_g, ln_v_b, w_s, b_s, w_out_a,
           w_in_b, a_re, a_im, log_dt, b_re, b_im, c_re, c_im, d_skip,
           w_glu1, b_glu1, w_glu2, b_glu2, w_out_b):
    n_batch, seq_len, d = x_prompt.shape
    n_dec, dec_len, _ = x_sample.shape
    depth = norm_pre.shape[0]
    n_ssm, n_groups, state_p = a_re.shape
    n_state = n_groups * state_p
    assert seq_len % ROW_TILE == 0 and ROW_TILE % CHUNK == 0 and seq_len % SCAN_STEPS == 0
    assert n_batch == V7X_SUBLANES and n_dec % (2 * V7X_SUBLANES) == 0
    assert dec_len <= CHUNK and dec_len % BLOCK == 0
    assert state_p == STATE_P and b_re.shape[-1] == SSM_GROUP and w_s.shape[1] == N_HEADS

    rows3 = lambda a: a.reshape(a.shape[0], 1, -1)
    npre, npost = rows3(norm_pre), rows3(norm_post)
    lng, lnb = rows3(ln_v_g), rows3(ln_v_b)
    w_in_a16, w_out_a16 = w_in_a.astype(BF16), w_out_a.astype(BF16)
    bst = jnp.swapaxes(b_s, 1, 2)
    ws_dec = w_s[:, :, :dec_len, :dec_len].reshape(-1)
    bs_dec = b_s[:, :, :dec_len].reshape(-1)

    w_in_b16 = w_in_b.astype(BF16)
    out_consts = (w_glu1.astype(BF16), rows3(b_glu1), w_glu2.astype(BF16), rows3(b_glu2),
                  w_out_b.astype(BF16))
    lanes_gp = lambda a, perm: jnp.transpose(a, perm).reshape(n_ssm, SSM_GROUP, n_state)
    l4r, l4i, wur, wui, vcw, kloc = _s5_prep(
        rows3(a_re), rows3(a_im), rows3(jnp.repeat(log_dt, state_p, axis=1)),
        lanes_gp(b_re, (0, 3, 1, 2)), lanes_gp(b_im, (0, 3, 1, 2)),
        lanes_gp(c_re, (0, 2, 1, 3)), lanes_gp(c_im, (0, 2, 1, 3)))
    core_consts = (l4r, l4i, wur, wui, vcw, kloc, rows3(d_skip))
    h0 = (state_ssm_re.reshape(n_ssm, n_dec, n_state), state_ssm_im.reshape(n_ssm, n_dec, n_state))

    xp = x_prompt.reshape(n_batch * seq_len, d)
    xs = jnp.transpose(x_sample, (1, 0, 2)).reshape(dec_len * n_dec, d)

    v_rows, st_p_re, st_p_im, st_s_re, st_s_im = [], [], [], [], []
    for i in range(depth):
        j = i // 2
        if i % 2 == 0:
            xp = _gmlp_prompt(xp, j, i, npre, npost, lng, lnb, w_in_a16, w_out_a16, w_s, bst)
            xs, v = _gmlp_sample(xs, n_dec, dec_len, j, i, npre, npost, lng, lnb,
                                 w_in_a16, w_out_a16, ws_dec, bs_dec)
            v_rows.append(v)
        else:
            xp3, sr, si = _s5_layer(xp.reshape(n_batch, seq_len, d), None, j, i, npre, w_in_b16,
                                    core_consts, out_consts, npost, n_batch, SCAN_STEPS)
            xp = xp3.reshape(n_batch * seq_len, d)
            st_p_re.append(sr)
            st_p_im.append(si)
            xs, sr, si = _s5_layer(xs, h0, j, i, npre, w_in_b16, core_consts, out_consts, npost,
                                   n_dec, dec_len)
            st_s_re.append(sr)
            st_s_im.append(si)

    y_prompt = xp.reshape(n_batch, seq_len, d)
    y_sample = jnp.transpose(xs.reshape(dec_len, n_dec, d), (1, 0, 2))
    chunk_v = jnp.transpose(jnp.stack(v_rows).reshape(len(v_rows), dec_len, n_dec, -1), (0, 2, 1, 3))
    states = lambda parts, n: jnp.stack(parts).reshape(len(parts), n, n_groups, state_p)
    return (y_prompt, y_sample, chunk_v, states(st_p_re, n_batch), states(st_p_im, n_batch),
            states(st_s_re, n_dec), states(st_s_im, n_dec))
```

```python
import functools

import jax
import jax.numpy as jnp
from jax import lax
from jax.experimental import pallas as pl
from jax.experimental.pallas import tpu as pltpu

EPS = 1e-6
CHUNK = 128
N_HEADS = 8
SSM_GROUP = 16
STATE_P = 64

V7X_LANES = 128
V7X_SUBLANES = 8
V7X_MXU_DIM = 256
V7X_VMEM_BYTES = 64 * 1024 * 1024

ROW_TILE = 1024
SCAN_STEPS = 64
BLOCK = 4
PIECE_ROWS = 256

BF16 = jnp.bfloat16
F32 = jnp.float32


def _dot(a, b):
    return jnp.dot(a, b, preferred_element_type=F32)


def _rmsnorm(x, g):
    ms = jnp.mean(x * x, axis=-1, keepdims=True)
    return x * lax.rsqrt(ms + EPS) * g


def _layernorm(x, g, b):
    mu = jnp.mean(x, axis=-1, keepdims=True)
    xc = x - mu
    var = jnp.mean(xc * xc, axis=-1, keepdims=True)
    return xc * lax.rsqrt(var + EPS) * g + b


def _silu(z):
    return z * jax.nn.sigmoid(z)


def _div_pow2(x, n):
    assert n & (n - 1) == 0
    return lax.shift_right_logical(x, n.bit_length() - 1)


def _vmem_limit(nbytes):
    return int(min(V7X_VMEM_BYTES - (4 << 20), nbytes + (12 << 20)))


def _layer_spec(stacked, layer):
    tail = stacked.shape[1:]
    return pl.BlockSpec((None,) + tail, lambda *_: (layer,) + (0,) * len(tail),
                        pipeline_mode=pl.Buffered(1))


def _layer_bytes(*stacked):
    return sum(a[0].size * a.dtype.itemsize for a in stacked)


def _row_spec(rows, width):
    return pl.BlockSpec((rows, width), lambda i: (i, 0))


_SMEM = pl.BlockSpec(memory_space=pltpu.SMEM)


def _gmlp_front(x, npre_ref, lng_ref, lnb_ref, win_ref, exp_a):
    hn, v = [], []
    for r0 in range(0, x.shape[0], PIECE_ROWS):
        hn.append(_rmsnorm(x[r0:r0 + PIECE_ROWS], npre_ref[...]).astype(BF16))
        v.append(_layernorm(_dot(hn[-1], win_ref[:, exp_a:2 * exp_a]), lng_ref[...], lnb_ref[...]))
    return jnp.concatenate(hn, axis=0), jnp.concatenate(v, axis=0)


def _gmlp_back(x, gated_ref, wout_ref, npost_ref, o_ref):
    for r0 in range(0, x.shape[0], PIECE_ROWS):
        r = slice(r0, r0 + PIECE_ROWS)
        out = _dot(gated_ref[r, :], wout_ref[...])
        o_ref[r, :] = x[r] + _rmsnorm(out, npost_ref[...])


def _gmlp_prompt_kernel(x_ref, npre_ref, npost_ref, lng_ref, lnb_ref, win_ref, wout_ref,
                        ws_ref, bst_ref, o_ref, gated_ref):
    rows = x_ref.shape[0]
    exp_a = wout_ref.shape[0]
    hd = exp_a // N_HEADS
    x = x_ref[...]
    hn, v = _gmlp_front(x, npre_ref, lng_ref, lnb_ref, win_ref, exp_a)
    vb = v.astype(BF16)
    causal = (lax.broadcasted_iota(jnp.int32, (CHUNK, CHUNK), 1)
              <= lax.broadcasted_iota(jnp.int32, (CHUNK, CHUNK), 0))
    for h in range(N_HEADS):
        lo = h * hd
        wsh = jnp.where(causal, ws_ref[h], 0.0).astype(BF16)
        bias = bst_ref[:, h:h + 1]
        u = _dot(hn, win_ref[:, lo:lo + hd])
        z = _dot(hn, win_ref[:, 2 * exp_a + lo:2 * exp_a + lo + hd])
        s = jnp.concatenate(
            [_dot(wsh, vb[c * CHUNK:(c + 1) * CHUNK, lo:lo + hd]) + bias
             for c in range(rows // CHUNK)], axis=0)
        gated_ref[:, lo:lo + hd] = (u * s * _silu(z)).astype(BF16)
    _gmlp_back(x, gated_ref, wout_ref, npost_ref, o_ref)


def _gmlp_sample_kernel(ws_ref, bs_ref, x_ref, npre_ref, npost_ref, lng_ref, lnb_ref, win_ref,
                        wout_ref, o_ref, v_ref, gated_ref, *, n_seq, seq_len, layer):
    exp_a = wout_ref.shape[0]
    hd = exp_a // N_HEADS
    x = x_ref[...]
    hn, v = _gmlp_front(x, npre_ref, lng_ref, lnb_ref, win_ref, exp_a)
    v_ref[...] = v
    for h in range(N_HEADS):
        lo = h * hd
        u = _dot(hn, win_ref[:, lo:lo + hd])
        z = _dot(hn, win_ref[:, 2 * exp_a + lo:2 * exp_a + lo + hd])
        vt = [v[t * n_seq:(t + 1) * n_seq, lo:lo + hd] for t in range(seq_len)]
        parts = []
        for t in range(seq_len):
            b_idx = (layer * N_HEADS + h) * seq_len + t
            s = ws_ref[b_idx * seq_len] * vt[0]
            for t2 in range(1, t + 1):
                s = s + ws_ref[b_idx * seq_len + t2] * vt[t2]
            parts.append(s + bs_ref[b_idx])
        s = jnp.concatenate(parts, axis=0)
        gated_ref[:, lo:lo + hd] = (u * s * _silu(z)).astype(BF16)
    _gmlp_back(x, gated_ref, wout_ref, npost_ref, o_ref)


def _gmlp_prompt(x2d, layer_a, layer, npre, npost, lng, lnb, win, wout, ws, bst):
    rows, d = x2d.shape
    exp_a = wout.shape[1]
    footprint = (_layer_bytes(win, wout, ws) + 4 * ROW_TILE * d * 4
                 + ROW_TILE * exp_a * (2 + 4 + 2 + 8))
    return pl.pallas_call(
        _gmlp_prompt_kernel,
        grid=(rows // ROW_TILE,),
        in_specs=[
            _row_spec(ROW_TILE, d),
            _layer_spec(npre, layer), _layer_spec(npost, layer),
            _layer_spec(lng, layer_a), _layer_spec(lnb, layer_a),
            _layer_spec(win, layer_a), _layer_spec(wout, layer_a),
            _layer_spec(ws, layer_a), _layer_spec(bst, layer_a),
        ],
        out_specs=_row_spec(ROW_TILE, d),
        out_shape=jax.ShapeDtypeStruct((rows, d), F32),
        scratch_shapes=[pltpu.VMEM((ROW_TILE, exp_a), BF16)],
        compiler_params=pltpu.CompilerParams(
            dimension_semantics=("parallel",),
            vmem_limit_bytes=_vmem_limit(footprint)),
        name="gmlp_prompt",
    )(x2d, npre, npost, lng, lnb, win, wout, ws, bst)


def _gmlp_sample(xs, n_seq, seq_len, layer_a, layer, npre, npost, lng, lnb, win, wout,
                 ws_flat, bs_flat):
    rows, d = xs.shape
    exp_a = wout.shape[1]
    footprint = _layer_bytes(win, wout) + 4 * rows * d * 4 + rows * exp_a * (2 + 8 + 4 + 8)
    return pl.pallas_call(
        functools.partial(_gmlp_sample_kernel, n_seq=n_seq, seq_len=seq_len, layer=layer_a),
        grid=(1,),
        in_specs=[_SMEM, _SMEM, _row_spec(rows, d),
                  _layer_spec(npre, layer), _layer_spec(npost, layer),
                  _layer_spec(lng, layer_a), _layer_spec(lnb, layer_a),
                  _layer_spec(win, layer_a), _layer_spec(wout, layer_a)],
        out_specs=[_row_spec(rows, d), _row_spec(rows, exp_a)],
        out_shape=[jax.ShapeDtypeStruct((rows, d), F32),
                   jax.ShapeDtypeStruct((rows, exp_a), F32)],
        scratch_shapes=[pltpu.VMEM((rows, exp_a), BF16)],
        compiler_params=pltpu.CompilerParams(
            dimension_semantics=("arbitrary",),
            vmem_limit_bytes=_vmem_limit(footprint)),
        name="gmlp_sample",
    )(ws_flat, bs_flat, xs, npre, npost, lng, lnb, win, wout)


def _s5_prep_kernel(are_ref, aim_ref, ldt_ref, btr_ref, bti_ref, ctr_ref, cti_ref,
                    l4r_ref, l4i_ref, wur_ref, wui_ref, vc_ref, kloc_ref):
    dt = jnp.exp(ldt_ref[...])
    ar = are_ref[...]
    ai = aim_ref[...]
    mag = jnp.exp(dt * ar)
    ang = dt * ai
    abr = mag * jnp.cos(ang)
    abi = mag * jnp.sin(ang)
    nr = abr - 1.0
    ni = abi
    den = ar * ar + ai * ai
    cre = (nr * ar + ni * ai) / den
    cim = (ni * ar - nr * ai) / den
    btr = btr_ref[...]
    bti = bti_ref[...]
    bbr = cre * btr - cim * bti
    bbi = cre * bti + cim * btr

    def cmul(xr, xi, yr, yi):
        return xr * yr - xi * yi, xr * yi + xi * yr

    lam = [(jnp.ones_like(abr), jnp.zeros_like(abi)), (abr, abi)]
    for _ in range(BLOCK - 1):
        lam.append(cmul(*lam[-1], abr, abi))
    l4r_ref[...] = lam[BLOCK][0]
    l4i_ref[...] = lam[BLOCK][1]

    n_q, k_rows, n_cols = wur_ref.shape
    q_groups = n_cols // STATE_P
    q_rows = q_groups * SSM_GROUP
    same_q = (_div_pow2(lax.broadcasted_iota(jnp.int32, (q_rows, n_cols), 0), SSM_GROUP)
              == _div_pow2(lax.broadcasted_iota(jnp.int32, (q_rows, n_cols), 1), STATE_P))
    for ip in range(BLOCK):
        ur, ui = cmul(bbr, bbi, *lam[BLOCK - 1 - ip])
        for q in range(n_q):
            for src, dst in ((ur, wur_ref), (ui, wui_ref)):
                blk = jnp.concatenate([src[:, q * n_cols:(q + 1) * n_cols]] * q_groups, axis=0)
                dst[q, ip * q_rows:(ip + 1) * q_rows, :] = jnp.where(same_q, blk, 0.0).astype(BF16)

    ctr = ctr_ref[...]
    cti = cti_ref[...]
    cl = [cmul(ctr, cti, *lam[t]) for t in range(BLOCK + 1)]

    def tile_rows(a, q):
        blk = jnp.concatenate([a[:, q * n_cols:(q + 1) * n_cols]] * q_groups, axis=0)
        return jnp.where(same_q, blk, 0.0)

    half = V7X_LANES // 2
    low = lax.broadcasted_iota(jnp.int32, (q_rows, V7X_LANES), 1) < half
    for q in range(n_q):
        clm = [(tile_rows(c[0], q), tile_rows(c[1], q)) for c in cl]
        rows = [jnp.concatenate([clm[i + 1][0], -clm[i + 1][1]], axis=1) for i in range(BLOCK)]
        vc_ref[q] = jnp.concatenate(rows, axis=0).T.astype(BF16)
        bcat = jnp.concatenate([tile_rows(bbr, q), -tile_rows(bbi, q)], axis=1).astype(BF16)
        ccat = jnp.concatenate([jnp.concatenate(clm[t], axis=1) for t in range(BLOCK)], axis=0)
        kall = lax.dot_general(bcat, ccat.astype(BF16), (((1,), (1,)), ((), ())),
                               preferred_element_type=F32)
        k0, k1 = kall[:, :V7X_LANES], kall[:, V7X_LANES:]
        r0, r1 = pltpu.roll(k0, half, 1), pltpu.roll(k1, half, 1)
        zero = jnp.zeros_like(k0)
        shifted = [(k0, k1),
                   (jnp.where(low, zero, r0), jnp.where(low, r0, r1)),
                   (zero, k0),
                   (zero, jnp.where(low, zero, r0))]
        kloc_ref[q] = jnp.concatenate([jnp.concatenate(sh, axis=1) for sh in shifted],
                                      axis=0).astype(BF16)


def _s5_prep(are, aim, ldt, btr, bti, ctr, cti):
    n_layers, _, n_state = are.shape
    n_q = n_state // V7X_MXU_DIM

    def per_layer(shape):
        return pl.BlockSpec((None,) + shape, lambda l: (l,) + (0,) * len(shape))

    out_tails = [(1, n_state), (1, n_state),
                 (n_q, V7X_MXU_DIM, V7X_MXU_DIM), (n_q, V7X_MXU_DIM, V7X_MXU_DIM),
                 (n_q, 2 * V7X_MXU_DIM, V7X_MXU_DIM), (n_q, V7X_MXU_DIM, V7X_MXU_DIM)]
    out_dtypes = [F32, F32, BF16, BF16, BF16, BF16]
    args = (are, aim, ldt, btr, bti, ctr, cti)
    return pl.pallas_call(
        _s5_prep_kernel,
        grid=(n_layers,),
        in_specs=[per_layer(a.shape[1:]) for a in args],
        out_specs=[per_layer(t) for t in out_tails],
        out_shape=[jax.ShapeDtypeStruct((n_layers,) + t, dt) for t, dt in zip(out_tails, out_dtypes)],
        compiler_params=pltpu.CompilerParams(
            dimension_semantics=("parallel",),
            vmem_limit_bytes=_vmem_limit(2 * sum(
                int(jnp.dtype(dt).itemsize) * functools.reduce(lambda a, b: a * b, t)
                for t, dt in zip(out_tails, out_dtypes)))),
        name="s5_prep",
    )(*args)


def _s5_tile(x_ref, o_ref, params, hpr_ref, hpi_ref, hr_ref, hi_ref, y_ref, *, n_seq, n_steps,
             swap_bt):
    (npre_ref, win_ref, l4r_ref, l4i_ref, wur_ref, wui_ref, vc_ref, kloc_ref, dsk_ref,
     w1_ref, b1_ref, w2_ref, b2_ref, wout_ref, npost_ref) = params

    d = x_ref.shape[-1]
    width = win_ref.shape[1] // 2
    c_rows = PIECE_ROWS
    c_steps = c_rows // n_seq
    n_pieces = n_steps // c_steps
    x_c, xb_c, z_c = [], [], []
    for c in range(n_pieces):
        if swap_bt:
            xc = jnp.swapaxes(x_ref[:, c * c_steps:(c + 1) * c_steps, :], 0, 1).reshape(c_rows, d)
        else:
            xc = x_ref[c * c_rows:(c + 1) * c_rows, :]
        hn = _rmsnorm(xc, npre_ref[...]).astype(BF16)
        x_c.append(xc)
        xb_c.append(_dot(hn, win_ref[:, :width]))
        z_c.append(_dot(hn, win_ref[:, width:]))
    xb = jnp.concatenate(xb_c, axis=0)

    n_blocks = n_steps // BLOCK
    rows = n_blocks * n_seq
    x4 = xb.reshape(n_blocks, BLOCK, n_seq, width)
    xi = [x4[:, i].reshape(rows, width) for i in range(BLOCK)]

    sub = V7X_SUBLANES
    pair = 2 * sub
    n_q, _, n_cols = wur_ref.shape
    n_v = width // V7X_LANES
    half = V7X_LANES // 2
    low = lax.broadcasted_iota(jnp.int32, (rows, V7X_LANES), 1) < half

    lhs_q = []
    for v in range(n_v):
        cols = slice(v * V7X_LANES, (v + 1) * V7X_LANES)
        p = [a[:, cols] for a in xi]
        pr = [pltpu.roll(a, half, 1) for a in p]
        lhs_lo = jnp.concatenate([jnp.where(low, p[i], pr[i + 1]) for i in range(0, BLOCK, 2)], axis=1)
        lhs_hi = jnp.concatenate([jnp.where(low, pr[i], p[i + 1]) for i in range(0, BLOCK, 2)], axis=1)
        for q, lhs in ((2 * v, lhs_lo.astype(BF16)), (2 * v + 1, lhs_hi.astype(BF16))):
            lanes = pl.ds(q * n_cols, n_cols)
            wr = _dot(lhs, wur_ref[q])
            wi = _dot(lhs, wui_ref[q])
            if n_seq == sub:
                ar = jnp.broadcast_to(l4r_ref[:, lanes], (sub, n_cols))
                ai = jnp.broadcast_to(l4i_ref[:, lanes], (sub, n_cols))
                hr, hi = hr_ref[:, lanes], hi_ref[:, lanes]
                for k in range(n_blocks // 2):
                    prv_r, prv_i = [], []
                    for r0 in (2 * k * sub, (2 * k + 1) * sub):
                        prv_r.append(hr)
                        prv_i.append(hi)
                        hr, hi = (ar * hr - ai * hi + wr[r0:r0 + sub],
                                  ar * hi + ai * hr + wi[r0:r0 + sub])
                    hpr_ref[k * pair:(k + 1) * pair, lanes] = jnp.concatenate(prv_r, axis=0).astype(BF16)
                    hpi_ref[k * pair:(k + 1) * pair, lanes] = jnp.concatenate(prv_i, axis=0).astype(BF16)
                hr_ref[:, lanes] = hr
                hi_ref[:, lanes] = hi
            else:
                ar = jnp.broadcast_to(l4r_ref[:, lanes], (pair, n_cols))
                ai = jnp.broadcast_to(l4i_ref[:, lanes], (pair, n_cols))
                for m in range(n_seq // pair):
                    srows = pl.ds(m * pair, pair)
                    hr, hi = hr_ref[srows, lanes], hi_ref[srows, lanes]
                    for blk in range(n_blocks):
                        r0 = blk * n_seq + m * pair
                        hpr_ref[r0:r0 + pair, lanes] = hr.astype(BF16)
                        hpi_ref[r0:r0 + pair, lanes] = hi.astype(BF16)
                        hr, hi = (ar * hr - ai * hi + wr[r0:r0 + pair],
                                  ar * hi + ai * hr + wi[r0:r0 + pair])
                    hr_ref[srows, lanes] = hr
                    hi_ref[srows, lanes] = hi
            lhs_q.append(lhs)

    for v in range(n_v):
        cols = slice(v * V7X_LANES, (v + 1) * V7X_LANES)
        yq = []
        for q in (2 * v, 2 * v + 1):
            lanes = pl.ds(q * n_cols, n_cols)
            st = jnp.concatenate([hpr_ref[:, lanes], hpi_ref[:, lanes]], axis=1)
            yq.append(_dot(st, vc_ref[q]) + _dot(lhs_q[q], kloc_ref[q]))
        ys = []
        for i in range(BLOCK):
            lo_q, hi_q = (a[:, (i // 2) * V7X_LANES:(i // 2 + 1) * V7X_LANES] for a in yq)
            if i % 2 == 0:
                yv = jnp.where(low, lo_q, pltpu.roll(hi_q, half, 1))
            else:
                yv = jnp.where(low, pltpu.roll(lo_q, half, 1), hi_q)
            ys.append(jax.nn.gelu(yv + dsk_ref[:, cols] * xi[i][:, cols]))
        y = jnp.stack([a.reshape(n_blocks, n_seq, V7X_LANES) for a in ys], axis=1)
        y_ref[:, cols] = y.reshape(n_steps * n_seq, V7X_LANES).astype(BF16)

    for c in range(n_pieces):
        y = y_ref[c * c_rows:(c + 1) * c_rows, :]
        g = (_dot(y, w1_ref[...]) + b1_ref[...]) * jax.nn.sigmoid(_dot(y, w2_ref[...]) + b2_ref[...])
        out = _dot((g * _silu(z_c[c])).astype(BF16), wout_ref[...])
        xn = x_c[c] + _rmsnorm(out, npost_ref[...])
        if swap_bt:
            o_ref[:, c * c_steps:(c + 1) * c_steps, :] = jnp.swapaxes(
                xn.reshape(c_steps, n_seq, d), 0, 1)
        else:
            o_ref[c * c_rows:(c + 1) * c_rows, :] = xn


N_S5_PARAMS = 15


def _s5_layer_kernel(xp_ref, xs_ref, h0r_ref, h0i_ref, *refs, n_tiles, prompt_dims, sample_dims):
    params = refs[:N_S5_PARAMS]
    (op_ref, os_ref, spr_ref, spi_ref, ssr_ref, ssi_ref,
     hpr_ref, hpi_ref, hrp_ref, hip_ref, hrs_ref, his_ref, y_ref) = refs[N_S5_PARAMS:]
    step = pl.program_id(0)

    @pl.when(step == 0)
    def _():
        hrp_ref[...] = jnp.zeros_like(hrp_ref)
        hip_ref[...] = jnp.zeros_like(hip_ref)

    @pl.when(step < n_tiles)
    def _():
        _s5_tile(xp_ref, op_ref, params, hpr_ref, hpi_ref, hrp_ref, hip_ref, y_ref,
                 n_seq=prompt_dims[0], n_steps=prompt_dims[1], swap_bt=True)

    @pl.when(step == n_tiles - 1)
    def _():
        spr_ref[...] = hrp_ref[...]
        spi_ref[...] = hip_ref[...]

    @pl.when(step == n_tiles)
    def _():
        hrs_ref[...] = h0r_ref[...]
        his_ref[...] = h0i_ref[...]
        _s5_tile(xs_ref, os_ref, params, hpr_ref, hpi_ref, hrs_ref, his_ref, y_ref,
                 n_seq=sample_dims[0], n_steps=sample_dims[1], swap_bt=False)
        ssr_ref[...] = hrs_ref[...]
        ssi_ref[...] = his_ref[...]


def _s5_layer(xp, xs, h0, layer_b, layer, npre, win, core_consts, out_consts, npost, prompt_steps,
              sample_dims):
    n_seq, seq_len, d = xp.shape
    n_dec, dec_len = sample_dims
    exp_b = win.shape[2] // 2
    n_state = core_consts[0].shape[-1]
    tile = n_seq * prompt_steps
    n_tiles = seq_len // prompt_steps
    rows = tile // BLOCK
    assert xs.shape == (tile, d) and n_dec * dec_len == tile
    assert prompt_steps % (2 * BLOCK) == 0 and dec_len % BLOCK == 0
    assert tile % PIECE_ROWS == 0 and PIECE_ROWS % n_dec == 0 and PIECE_ROWS % n_seq == 0
    params_b = [win] + list(core_consts) + list(out_consts)
    assert len(params_b) + 2 == N_S5_PARAMS
    args = [xp, xs] + list(h0) + [npre] + params_b + [npost]
    xp_spec = pl.BlockSpec((n_seq, prompt_steps, d), lambda i: (0, jnp.minimum(i, n_tiles - 1), 0))
    xs_spec = pl.BlockSpec((tile, d), lambda i: (0, 0))
    in_specs = ([xp_spec, pl.BlockSpec((tile, d), lambda i: (0, 0), pipeline_mode=pl.Buffered(1))]
                + [_layer_spec(a, layer_b) for a in h0]
                + [_layer_spec(npre, layer)] + [_layer_spec(a, layer_b) for a in params_b]
                + [_layer_spec(npost, layer)])
    st = lambda n: (pl.BlockSpec((n, n_state), lambda i: (0, 0)), jax.ShapeDtypeStruct((n, n_state), F32))
    (sp_spec, sp_shape), (ss_spec, ss_shape) = st(n_seq), st(n_dec)
    footprint = (_layer_bytes(*params_b) + _layer_bytes(*h0) + 7 * tile * d * 4
                 + 2 * rows * n_state * 2 + 4 * (n_seq + n_dec) * n_state * 4
                 + tile * exp_b * (2 + 6 * 4))
    return pl.pallas_call(
        functools.partial(_s5_layer_kernel, n_tiles=n_tiles, prompt_dims=(n_seq, prompt_steps),
                          sample_dims=(n_dec, dec_len)),
        grid=(n_tiles + 1,),
        in_specs=in_specs,
        out_specs=[xp_spec, xs_spec, sp_spec, sp_spec, ss_spec, ss_spec],
        out_shape=[jax.ShapeDtypeStruct(xp.shape, F32), jax.ShapeDtypeStruct(xs.shape, F32),
                   sp_shape, sp_shape, ss_shape, ss_shape],
        scratch_shapes=[pltpu.VMEM((rows, n_state), BF16), pltpu.VMEM((rows, n_state), BF16),
                        pltpu.VMEM((n_seq, n_state), F32), pltpu.VMEM((n_seq, n_state), F32),
                        pltpu.VMEM((n_dec, n_state), F32), pltpu.VMEM((n_dec, n_state), F32),
                        pltpu.VMEM((tile, exp_b), BF16)],
        compiler_params=pltpu.CompilerParams(
            dimension_semantics=("arbitrary",),
            vmem_limit_bytes=_vmem_limit(footprint)),
        name="s5_layer",
    )(*args)


def kernel(x_prompt, x_sample, state_ssm_re, state_ssm_im, norm_pre, norm_post,
           w_in_a, ln_v_g, ln_v_b, w_s, b_s, w_out_a,
           w_in_b, a_re, a_im, log_dt, b_re, b_im, c_re, c_im, d_skip,
           w_glu1, b_glu1, w_glu2, b_glu2, w_out_b):
    n_batch, seq_len, d = x_prompt.shape
    n_dec, dec_len, _ = x_sample.shape
    depth = norm_pre.shape[0]
    n_ssm, n_groups, state_p = a_re.shape
    n_state = n_groups * state_p
    assert seq_len % ROW_TILE == 0 and ROW_TILE % CHUNK == 0 and seq_len % SCAN_STEPS == 0
    assert n_batch == V7X_SUBLANES and n_dec % (2 * V7X_SUBLANES) == 0
    assert dec_len <= CHUNK and dec_len % BLOCK == 0
    assert state_p == STATE_P and b_re.shape[-1] == SSM_GROUP and w_s.shape[1] == N_HEADS

    rows3 = lambda a: a.reshape(a.shape[0], 1, -1)
    npre, npost = rows3(norm_pre), rows3(norm_post)
    lng, lnb = rows3(ln_v_g), rows3(ln_v_b)
    w_in_a16, w_out_a16 = w_in_a.astype(BF16), w_out_a.astype(BF16)
    bst = jnp.swapaxes(b_s, 1, 2)
    ws_dec = w_s[:, :, :dec_len, :dec_len].reshape(-1)
    bs_dec = b_s[:, :, :dec_len].reshape(-1)

    w_in_b16 = w_in_b.astype(BF16)
    out_consts = (w_glu1.astype(BF16), rows3(b_glu1), w_glu2.astype(BF16), rows3(b_glu2),
                  w_out_b.astype(BF16))
    lanes_gp = lambda a, perm: jnp.transpose(a, perm).reshape(n_ssm, SSM_GROUP, n_state)
    l4r, l4i, wur, wui, vcw, kloc = _s5_prep(
        rows3(a_re), rows3(a_im), rows3(jnp.repeat(log_dt, state_p, axis=1)),
        lanes_gp(b_re, (0, 3, 1, 2)), lanes_gp(b_im, (0, 3, 1, 2)),
        lanes_gp(c_re, (0, 2, 1, 3)), lanes_gp(c_im, (0, 2, 1, 3)))
    core_consts = (l4r, l4i, wur, wui, vcw, kloc, rows3(d_skip))
    h0 = (state_ssm_re.reshape(n_ssm, n_dec, n_state), state_ssm_im.reshape(n_ssm, n_dec, n_state))

    xp = x_prompt.reshape(n_batch * seq_len, d)
    xs = jnp.transpose(x_sample, (1, 0, 2)).reshape(dec_len * n_dec, d)

    v_rows, st_p_re, st_p_im, st_s_re, st_s_im = [], [], [], [], []
    for i in range(depth):
        j = i // 2
        if i % 2 == 0:
            xp = _gmlp_prompt(xp, j, i, npre, npost, lng, lnb, w_in_a16, w_out_a16, w_s, bst)
            xs, v = _gmlp_sample(xs, n_dec, dec_len, j, i, npre, npost, lng, lnb,
                                 w_in_a16, w_out_a16, ws_dec, bs_dec)
            v_rows.append(v)
        else:
            xp3, xs, spr, spi, ssr, ssi = _s5_layer(
                xp.reshape(n_batch, seq_len, d), xs, h0, j, i, npre, w_in_b16, core_consts,
                out_consts, npost, SCAN_STEPS, (n_dec, dec_len))
            xp = xp3.reshape(n_batch * seq_len, d)
            st_p_re.append(spr)
            st_p_im.append(spi)
            st_s_re.append(ssr)
            st_s_im.append(ssi)

    y_prompt = xp.reshape(n_batch, seq_len, d)
    y_sample = jnp.transpose(xs.reshape(dec_len, n_dec, d), (1, 0, 2))
    chunk_v = jnp.transpose(jnp.stack(v_rows).reshape(len(v_rows), dec_len, n_dec, -1), (0, 2, 1, 3))
    states = lambda parts, n: jnp.stack(parts).reshape(len(parts), n, n_groups, state_p)
    return (y_prompt, y_sample, chunk_v, states(st_p_re, n_batch), states(st_p_im, n_batch),
            states(st_s_re, n_dec), states(st_s_im, n_dec))
```

```python
import functools

import jax
import jax.numpy as jnp
from jax import lax
from jax.experimental import pallas as pl
from jax.experimental.pallas import tpu as pltpu

EPS = 1e-6
CHUNK = 128
N_HEADS = 8
SSM_GROUP = 16
STATE_P = 64

V7X_LANES = 128
V7X_SUBLANES = 8
V7X_MXU_DIM = 256
V7X_VMEM_BYTES = 64 * 1024 * 1024

ROW_TILE = 1024
SCAN_STEPS = 64
BLOCK = 4
PIECE_ROWS = 256

BF16 = jnp.bfloat16
F32 = jnp.float32


def _dot(a, b):
    return jnp.dot(a, b, preferred_element_type=F32)


def _rmsnorm(x, g):
    ms = jnp.mean(x * x, axis=-1, keepdims=True)
    return x * lax.rsqrt(ms + EPS) * g


def _layernorm(x, g, b):
    mu = jnp.mean(x, axis=-1, keepdims=True)
    xc = x - mu
    var = jnp.mean(xc * xc, axis=-1, keepdims=True)
    return xc * lax.rsqrt(var + EPS) * g + b


def _silu(z):
    return z * jax.nn.sigmoid(z)


def _div_pow2(x, n):
    assert n & (n - 1) == 0
    return lax.shift_right_logical(x, n.bit_length() - 1)


def _vmem_limit(nbytes):
    return int(min(V7X_VMEM_BYTES - (4 << 20), nbytes + (12 << 20)))


def _layer_spec(stacked, layer):
    tail = stacked.shape[1:]
    return pl.BlockSpec((None,) + tail, lambda *_: (layer,) + (0,) * len(tail),
                        pipeline_mode=pl.Buffered(1))


def _layer_bytes(*stacked):
    return sum(a[0].size * a.dtype.itemsize for a in stacked)


def _cast_specs(jobs, n_steps):
    in_specs, out_specs, out_shapes = [], [], []
    for w, layer in jobs:
        _, rows, cols = w.shape
        slab = rows // n_steps
        assert rows % n_steps == 0 and slab % (2 * V7X_SUBLANES) == 0
        in_specs.append(pl.BlockSpec((None, slab, cols), lambda i, layer=layer: (layer, i, 0)))
        out_specs.append(pl.BlockSpec((None, slab, cols), lambda i: (0, i, 0)))
        out_shapes.append(jax.ShapeDtypeStruct((1, rows, cols), BF16))
    return in_specs, out_specs, out_shapes


def _cast_slabs(in_refs, out_refs):
    for src, dst in zip(in_refs, out_refs):
        dst[...] = src[...].astype(BF16)


def _row_spec(rows, width):
    return pl.BlockSpec((rows, width), lambda i: (i, 0))


_SMEM = pl.BlockSpec(memory_space=pltpu.SMEM)


def _gmlp_front(x, npre_ref, lng_ref, lnb_ref, win_ref, exp_a):
    hn, v = [], []
    for r0 in range(0, x.shape[0], PIECE_ROWS):
        hn.append(_rmsnorm(x[r0:r0 + PIECE_ROWS], npre_ref[...]).astype(BF16))
        v.append(_layernorm(_dot(hn[-1], win_ref[:, exp_a:2 * exp_a]), lng_ref[...], lnb_ref[...]))
    return jnp.concatenate(hn, axis=0), jnp.concatenate(v, axis=0)


def _gmlp_back(x, gated_ref, wout_ref, npost_ref, o_ref):
    for r0 in range(0, x.shape[0], PIECE_ROWS):
        r = slice(r0, r0 + PIECE_ROWS)
        out = _dot(gated_ref[r, :], wout_ref[...])
        o_ref[r, :] = x[r] + _rmsnorm(out, npost_ref[...])


def _gmlp_prompt_kernel(x_ref, npre_ref, npost_ref, lng_ref, lnb_ref, win_ref, wout_ref,
                        ws_ref, bst_ref, *refs, n_cast):
    o_ref, gated_ref = refs[n_cast], refs[-1]
    _cast_slabs(refs[:n_cast], refs[n_cast + 1:-1])
    rows = x_ref.shape[0]
    exp_a = wout_ref.shape[0]
    hd = exp_a // N_HEADS
    x = x_ref[...]
    hn, v = _gmlp_front(x, npre_ref, lng_ref, lnb_ref, win_ref, exp_a)
    vb = v.astype(BF16)
    causal = (lax.broadcasted_iota(jnp.int32, (CHUNK, CHUNK), 1)
              <= lax.broadcasted_iota(jnp.int32, (CHUNK, CHUNK), 0))
    for h in range(N_HEADS):
        lo = h * hd
        wsh = jnp.where(causal, ws_ref[h], 0.0).astype(BF16)
        bias = bst_ref[:, h:h + 1]
        u = _dot(hn, win_ref[:, lo:lo + hd])
        z = _dot(hn, win_ref[:, 2 * exp_a + lo:2 * exp_a + lo + hd])
        s = jnp.concatenate(
            [_dot(wsh, vb[c * CHUNK:(c + 1) * CHUNK, lo:lo + hd]) + bias
             for c in range(rows // CHUNK)], axis=0)
        gated_ref[:, lo:lo + hd] = (u * s * _silu(z)).astype(BF16)
    _gmlp_back(x, gated_ref, wout_ref, npost_ref, o_ref)


def _gmlp_sample_kernel(ws_ref, bs_ref, x_ref, npre_ref, npost_ref, lng_ref, lnb_ref, win_ref,
                        wout_ref, o_ref, v_ref, gated_ref, *, n_seq, seq_len, layer):
    exp_a = wout_ref.shape[0]
    hd = exp_a // N_HEADS
    x = x_ref[...]
    hn, v = _gmlp_front(x, npre_ref, lng_ref, lnb_ref, win_ref, exp_a)
    v_ref[...] = v
    for h in range(N_HEADS):
        lo = h * hd
        u = _dot(hn, win_ref[:, lo:lo + hd])
        z = _dot(hn, win_ref[:, 2 * exp_a + lo:2 * exp_a + lo + hd])
        vt = [v[t * n_seq:(t + 1) * n_seq, lo:lo + hd] for t in range(seq_len)]
        parts = []
        for t in range(seq_len):
            b_idx = (layer * N_HEADS + h) * seq_len + t
            s = ws_ref[b_idx * seq_len] * vt[0]
            for t2 in range(1, t + 1):
                s = s + ws_ref[b_idx * seq_len + t2] * vt[t2]
            parts.append(s + bs_ref[b_idx])
        s = jnp.concatenate(parts, axis=0)
        gated_ref[:, lo:lo + hd] = (u * s * _silu(z)).astype(BF16)
    _gmlp_back(x, gated_ref, wout_ref, npost_ref, o_ref)


def _gmlp_prompt(x2d, layer_a, layer, npre, npost, lng, lnb, win, wout, ws, bst, cast_jobs):
    rows, d = x2d.shape
    (win, win_l), (wout, wout_l) = win, wout
    exp_a = wout.shape[1]
    n_steps = rows // ROW_TILE
    c_in, c_out, c_shapes = _cast_specs(cast_jobs, n_steps)
    footprint = (_layer_bytes(win, wout, ws) + 4 * ROW_TILE * d * 4
                 + ROW_TILE * exp_a * (2 + 4 + 2 + 8)
                 + 6 * sum(w[0].size for w, _ in cast_jobs) // n_steps * 2)
    res = pl.pallas_call(
        functools.partial(_gmlp_prompt_kernel, n_cast=len(cast_jobs)),
        grid=(n_steps,),
        in_specs=[
            _row_spec(ROW_TILE, d),
            _layer_spec(npre, layer), _layer_spec(npost, layer),
            _layer_spec(lng, layer_a), _layer_spec(lnb, layer_a),
            _layer_spec(win, win_l), _layer_spec(wout, wout_l),
            _layer_spec(ws, layer_a), _layer_spec(bst, layer_a),
        ] + c_in,
        out_specs=[_row_spec(ROW_TILE, d)] + c_out,
        out_shape=[jax.ShapeDtypeStruct((rows, d), F32)] + c_shapes,
        scratch_shapes=[pltpu.VMEM((ROW_TILE, exp_a), BF16)],
        compiler_params=pltpu.CompilerParams(
            dimension_semantics=("parallel",),
            vmem_limit_bytes=_vmem_limit(footprint)),
        name="gmlp_prompt",
    )(x2d, npre, npost, lng, lnb, win, wout, ws, bst, *[w for w, _ in cast_jobs])
    return res[0], res[1:]


def _gmlp_sample(xs, n_seq, seq_len, layer_a, layer, npre, npost, lng, lnb, win, wout,
                 ws_flat, bs_flat):
    rows, d = xs.shape
    (win, win_l), (wout, wout_l) = win, wout
    exp_a = wout.shape[1]
    footprint = _layer_bytes(win, wout) + 4 * rows * d * 4 + rows * exp_a * (2 + 8 + 4 + 8)
    return pl.pallas_call(
        functools.partial(_gmlp_sample_kernel, n_seq=n_seq, seq_len=seq_len, layer=layer_a),
        grid=(1,),
        in_specs=[_SMEM, _SMEM, _row_spec(rows, d),
                  _layer_spec(npre, layer), _layer_spec(npost, layer),
                  _layer_spec(lng, layer_a), _layer_spec(lnb, layer_a),
                  _layer_spec(win, win_l), _layer_spec(wout, wout_l)],
        out_specs=[_row_spec(rows, d), _row_spec(rows, exp_a)],
        out_shape=[jax.ShapeDtypeStruct((rows, d), F32),
                   jax.ShapeDtypeStruct((rows, exp_a), F32)],
        scratch_shapes=[pltpu.VMEM((rows, exp_a), BF16)],
        compiler_params=pltpu.CompilerParams(
            dimension_semantics=("arbitrary",),
            vmem_limit_bytes=_vmem_limit(footprint)),
        name="gmlp_sample",
    )(ws_flat, bs_flat, xs, npre, npost, lng, lnb, win, wout)


def _s5_prep_kernel(are_ref, aim_ref, ldt_ref, btr_ref, bti_ref, ctr_ref, cti_ref,
                    l4r_ref, l4i_ref, wur_ref, wui_ref, vc_ref, kloc_ref):
    dt = jnp.exp(ldt_ref[...])
    ar = are_ref[...]
    ai = aim_ref[...]
    mag = jnp.exp(dt * ar)
    ang = dt * ai
    abr = mag * jnp.cos(ang)
    abi = mag * jnp.sin(ang)
    nr = abr - 1.0
    ni = abi
    den = ar * ar + ai * ai
    cre = (nr * ar + ni * ai) / den
    cim = (ni * ar - nr * ai) / den
    btr = btr_ref[...]
    bti = bti_ref[...]
    bbr = cre * btr - cim * bti
    bbi = cre * bti + cim * btr

    def cmul(xr, xi, yr, yi):
        return xr * yr - xi * yi, xr * yi + xi * yr

    lam = [(jnp.ones_like(abr), jnp.zeros_like(abi)), (abr, abi)]
    for _ in range(BLOCK - 1):
        lam.append(cmul(*lam[-1], abr, abi))
    l4r_ref[...] = lam[BLOCK][0]
    l4i_ref[...] = lam[BLOCK][1]

    n_q, k_rows, n_cols = wur_ref.shape
    q_groups = n_cols // STATE_P
    q_rows = q_groups * SSM_GROUP
    same_q = (_div_pow2(lax.broadcasted_iota(jnp.int32, (q_rows, n_cols), 0), SSM_GROUP)
              == _div_pow2(lax.broadcasted_iota(jnp.int32, (q_rows, n_cols), 1), STATE_P))
    for ip in range(BLOCK):
        ur, ui = cmul(bbr, bbi, *lam[BLOCK - 1 - ip])
        for q in range(n_q):
            for src, dst in ((ur, wur_ref), (ui, wui_ref)):
                blk = jnp.concatenate([src[:, q * n_cols:(q + 1) * n_cols]] * q_groups, axis=0)
                dst[q, ip * q_rows:(ip + 1) * q_rows, :] = jnp.where(same_q, blk, 0.0).astype(BF16)

    ctr = ctr_ref[...]
    cti = cti_ref[...]
    cl = [cmul(ctr, cti, *lam[t]) for t in range(BLOCK + 1)]

    def tile_rows(a, q):
        blk = jnp.concatenate([a[:, q * n_cols:(q + 1) * n_cols]] * q_groups, axis=0)
        return jnp.where(same_q, blk, 0.0)

    half = V7X_LANES // 2
    low = lax.broadcasted_iota(jnp.int32, (q_rows, V7X_LANES), 1) < half
    for q in range(n_q):
        clm = [(tile_rows(c[0], q), tile_rows(c[1], q)) for c in cl]
        rows = [jnp.concatenate([clm[i + 1][0], -clm[i + 1][1]], axis=1) for i in range(BLOCK)]
        vc_ref[q] = jnp.concatenate(rows, axis=0).T.astype(BF16)
        bcat = jnp.concatenate([tile_rows(bbr, q), -tile_rows(bbi, q)], axis=1).astype(BF16)
        ccat = jnp.concatenate([jnp.concatenate(clm[t], axis=1) for t in range(BLOCK)], axis=0)
        kall = lax.dot_general(bcat, ccat.astype(BF16), (((1,), (1,)), ((), ())),
                               preferred_element_type=F32)
        k0, k1 = kall[:, :V7X_LANES], kall[:, V7X_LANES:]
        r0, r1 = pltpu.roll(k0, half, 1), pltpu.roll(k1, half, 1)
        zero = jnp.zeros_like(k0)
        shifted = [(k0, k1),
                   (jnp.where(low, zero, r0), jnp.where(low, r0, r1)),
                   (zero, k0),
                   (zero, jnp.where(low, zero, r0))]
        kloc_ref[q] = jnp.concatenate([jnp.concatenate(sh, axis=1) for sh in shifted],
                                      axis=0).astype(BF16)


def _s5_prep(are, aim, ldt, btr, bti, ctr, cti):
    n_layers, _, n_state = are.shape
    n_q = n_state // V7X_MXU_DIM

    def per_layer(shape):
        return pl.BlockSpec((None,) + shape, lambda l: (l,) + (0,) * len(shape))

    out_tails = [(1, n_state), (1, n_state),
                 (n_q, V7X_MXU_DIM, V7X_MXU_DIM), (n_q, V7X_MXU_DIM, V7X_MXU_DIM),
                 (n_q, 2 * V7X_MXU_DIM, V7X_MXU_DIM), (n_q, V7X_MXU_DIM, V7X_MXU_DIM)]
    out_dtypes = [F32, F32, BF16, BF16, BF16, BF16]
    args = (are, aim, ldt, btr, bti, ctr, cti)
    return pl.pallas_call(
        _s5_prep_kernel,
        grid=(n_layers,),
        in_specs=[per_layer(a.shape[1:]) for a in args],
        out_specs=[per_layer(t) for t in out_tails],
        out_shape=[jax.ShapeDtypeStruct((n_layers,) + t, dt) for t, dt in zip(out_tails, out_dtypes)],
        compiler_params=pltpu.CompilerParams(
            dimension_semantics=("parallel",),
            vmem_limit_bytes=_vmem_limit(2 * sum(
                int(jnp.dtype(dt).itemsize) * functools.reduce(lambda a, b: a * b, t)
                for t, dt in zip(out_tails, out_dtypes)))),
        name="s5_prep",
    )(*args)


N_S5_PARAMS = 15


def _s5_layer_kernel(*refs, n_seq, n_steps, has_h0, swap_bt, n_cast):
    if has_h0:
        x_ref, h0r_ref, h0i_ref = refs[:3]
        refs = refs[3:]
    else:
        x_ref = refs[0]
        refs = refs[1:]
    (npre_ref, win_ref, l4r_ref, l4i_ref, wur_ref, wui_ref, vc_ref, kloc_ref, dsk_ref,
     w1_ref, b1_ref, w2_ref, b2_ref, wout_ref, npost_ref) = refs[:N_S5_PARAMS]
    refs = refs[N_S5_PARAMS:]
    o_ref, str_ref, sti_ref = refs[n_cast:n_cast + 3]
    hpr_ref, hpi_ref, hr_ref, hi_ref, y_ref = refs[-5:]
    _cast_slabs(refs[:n_cast], refs[n_cast + 3:-5])
    step = pl.program_id(0)

    @pl.when(step == 0)
    def _():
        if has_h0:
            hr_ref[...] = h0r_ref[...]
            hi_ref[...] = h0i_ref[...]
        else:
            hr_ref[...] = jnp.zeros_like(hr_ref)
            hi_ref[...] = jnp.zeros_like(hi_ref)

    d = x_ref.shape[-1]
    width = win_ref.shape[1] // 2
    c_rows = PIECE_ROWS
    c_steps = c_rows // n_seq
    n_pieces = n_steps // c_steps
    x_c, xb_c, z_c = [], [], []
    for c in range(n_pieces):
        if swap_bt:
            xc = jnp.swapaxes(x_ref[:, c * c_steps:(c + 1) * c_steps, :], 0, 1).reshape(c_rows, d)
        else:
            xc = x_ref[c * c_rows:(c + 1) * c_rows, :]
        hn = _rmsnorm(xc, npre_ref[...]).astype(BF16)
        x_c.append(xc)
        xb_c.append(_dot(hn, win_ref[:, :width]))
        z_c.append(_dot(hn, win_ref[:, width:]))
    xb = jnp.concatenate(xb_c, axis=0)

    n_blocks = n_steps // BLOCK
    rows = n_blocks * n_seq
    x4 = xb.reshape(n_blocks, BLOCK, n_seq, width)
    xi = [x4[:, i].reshape(rows, width) for i in range(BLOCK)]

    sub = V7X_SUBLANES
    pair = 2 * sub
    n_q, _, n_cols = wur_ref.shape
    n_v = width // V7X_LANES
    half = V7X_LANES // 2
    low = lax.broadcasted_iota(jnp.int32, (rows, V7X_LANES), 1) < half

    lhs_q = []
    for v in range(n_v):
        cols = slice(v * V7X_LANES, (v + 1) * V7X_LANES)
        p = [a[:, cols] for a in xi]
        pr = [pltpu.roll(a, half, 1) for a in p]
        lhs_lo = jnp.concatenate([jnp.where(low, p[i], pr[i + 1]) for i in range(0, BLOCK, 2)], axis=1)
        lhs_hi = jnp.concatenate([jnp.where(low, pr[i], p[i + 1]) for i in range(0, BLOCK, 2)], axis=1)
        for q, lhs in ((2 * v, lhs_lo.astype(BF16)), (2 * v + 1, lhs_hi.astype(BF16))):
            lanes = pl.ds(q * n_cols, n_cols)
            wr = _dot(lhs, wur_ref[q])
            wi = _dot(lhs, wui_ref[q])
            if n_seq == sub:
                ar = jnp.broadcast_to(l4r_ref[:, lanes], (sub, n_cols))
                ai = jnp.broadcast_to(l4i_ref[:, lanes], (sub, n_cols))
                hr, hi = hr_ref[:, lanes], hi_ref[:, lanes]
                for k in range(n_blocks // 2):
                    prv_r, prv_i = [], []
                    for r0 in (2 * k * sub, (2 * k + 1) * sub):
                        prv_r.append(hr)
                        prv_i.append(hi)
                        hr, hi = (ar * hr - ai * hi + wr[r0:r0 + sub],
                                  ar * hi + ai * hr + wi[r0:r0 + sub])
                    hpr_ref[k * pair:(k + 1) * pair, lanes] = jnp.concatenate(prv_r, axis=0).astype(BF16)
                    hpi_ref[k * pair:(k + 1) * pair, lanes] = jnp.concatenate(prv_i, axis=0).astype(BF16)
                hr_ref[:, lanes] = hr
                hi_ref[:, lanes] = hi
            else:
                ar = jnp.broadcast_to(l4r_ref[:, lanes], (pair, n_cols))
                ai = jnp.broadcast_to(l4i_ref[:, lanes], (pair, n_cols))
                for m in range(n_seq // pair):
                    srows = pl.ds(m * pair, pair)
                    hr, hi = hr_ref[srows, lanes], hi_ref[srows, lanes]
                    for blk in range(n_blocks):
                        r0 = blk * n_seq + m * pair
                        hpr_ref[r0:r0 + pair, lanes] = hr.astype(BF16)
                        hpi_ref[r0:r0 + pair, lanes] = hi.astype(BF16)
                        hr, hi = (ar * hr - ai * hi + wr[r0:r0 + pair],
                                  ar * hi + ai * hr + wi[r0:r0 + pair])
                    hr_ref[srows, lanes] = hr
                    hi_ref[srows, lanes] = hi
            lhs_q.append(lhs)

    for v in range(n_v):
        cols = slice(v * V7X_LANES, (v + 1) * V7X_LANES)
        yq = []
        for q in (2 * v, 2 * v + 1):
            lanes = pl.ds(q * n_cols, n_cols)
            st = jnp.concatenate([hpr_ref[:, lanes], hpi_ref[:, lanes]], axis=1)
            yq.append(_dot(st, vc_ref[q]) + _dot(lhs_q[q], kloc_ref[q]))
        ys = []
        for i in range(BLOCK):
            lo_q, hi_q = (a[:, (i // 2) * V7X_LANES:(i // 2 + 1) * V7X_LANES] for a in yq)
            if i % 2 == 0:
                yv = jnp.where(low, lo_q, pltpu.roll(hi_q, half, 1))
            else:
                yv = jnp.where(low, pltpu.roll(lo_q, half, 1), hi_q)
            ys.append(jax.nn.gelu(yv + dsk_ref[:, cols] * xi[i][:, cols]))
        y = jnp.stack([a.reshape(n_blocks, n_seq, V7X_LANES) for a in ys], axis=1)
        y_ref[:, cols] = y.reshape(n_steps * n_seq, V7X_LANES).astype(BF16)

    for c in range(n_pieces):
        y = y_ref[c * c_rows:(c + 1) * c_rows, :]
        g = (_dot(y, w1_ref[...]) + b1_ref[...]) * jax.nn.sigmoid(_dot(y, w2_ref[...]) + b2_ref[...])
        out = _dot((g * _silu(z_c[c])).astype(BF16), wout_ref[...])
        xn = x_c[c] + _rmsnorm(out, npost_ref[...])
        if swap_bt:
            o_ref[:, c * c_steps:(c + 1) * c_steps, :] = jnp.swapaxes(
                xn.reshape(c_steps, n_seq, d), 0, 1)
        else:
            o_ref[c * c_rows:(c + 1) * c_rows, :] = xn

    @pl.when(step == pl.num_programs(0) - 1)
    def _():
        str_ref[...] = hr_ref[...]
        sti_ref[...] = hi_ref[...]


def _s5_layer(x, h0, layer_b, layer, npre, win, core_consts, out_consts, npost, n_seq, n_steps,
              cast_jobs=()):
    swap_bt = x.ndim == 3
    d = x.shape[-1]
    exp_b = win[0].shape[2] // 2
    n_state = core_consts[0].shape[-1]
    tile = n_seq * n_steps
    n_tiles = x.size // (tile * d)
    rows = tile // BLOCK
    has_h0 = h0 is not None
    assert n_steps % BLOCK == 0 and (n_seq > V7X_SUBLANES or (n_steps // BLOCK) % 2 == 0)
    assert tile % PIECE_ROWS == 0 and PIECE_ROWS % n_seq == 0
    params_b = [win] + [(c, layer_b) for c in core_consts] + list(out_consts)
    assert len(params_b) + 2 == N_S5_PARAMS
    c_in, c_out, c_shapes = _cast_specs(cast_jobs, n_tiles)
    args = ([x] + (list(h0) if has_h0 else []) + [npre] + [a for a, _ in params_b] + [npost]
            + [w for w, _ in cast_jobs])
    if swap_bt:
        io_spec = pl.BlockSpec((n_seq, n_steps, d), lambda i: (0, i, 0))
    else:
        io_spec = pl.BlockSpec((tile, d), lambda i: (i, 0))
    in_specs = ([io_spec] + [_layer_spec(a, layer_b) for a in (h0 if has_h0 else ())]
                + [_layer_spec(npre, layer)] + [_layer_spec(a, l) for a, l in params_b]
                + [_layer_spec(npost, layer)] + c_in)
    st_spec = pl.BlockSpec((n_seq, n_state), lambda i: (0, 0))
    st_shape = jax.ShapeDtypeStruct((n_seq, n_state), F32)
    footprint = (_layer_bytes(*[a for a, _ in params_b]) + (_layer_bytes(*h0) if has_h0 else 0)
                 + 4 * tile * d * 4 + 2 * rows * n_state * 2 + 8 * n_seq * n_state * 4
                 + tile * exp_b * (2 + 6 * 4)
                 + 6 * sum(w[0].size for w, _ in cast_jobs) // n_tiles * 2)
    res = pl.pallas_call(
        functools.partial(_s5_layer_kernel, n_seq=n_seq, n_steps=n_steps, has_h0=has_h0,
                          swap_bt=swap_bt, n_cast=len(cast_jobs)),
        grid=(n_tiles,),
        in_specs=in_specs,
        out_specs=[io_spec, st_spec, st_spec] + c_out,
        out_shape=[jax.ShapeDtypeStruct(x.shape, F32), st_shape, st_shape] + c_shapes,
        scratch_shapes=[pltpu.VMEM((rows, n_state), BF16), pltpu.VMEM((rows, n_state), BF16),
                        pltpu.VMEM((n_seq, n_state), F32), pltpu.VMEM((n_seq, n_state), F32),
                        pltpu.VMEM((tile, exp_b), BF16)],
        compiler_params=pltpu.CompilerParams(
            dimension_semantics=("arbitrary",),
            vmem_limit_bytes=_vmem_limit(footprint)),
        name="s5_layer_sample" if has_h0 else "s5_layer_prompt",
    )(*args)
    return res[0], res[1], res[2], res[3:]


def kernel(x_prompt, x_sample, state_ssm_re, state_ssm_im, norm_pre, norm_post,
           w_in_a, ln_v_g, ln_v_b, w_s, b_s, w_out_a,
           w_in_b, a_re, a_im, log_dt, b_re, b_im, c_re, c_im, d_skip,
           w_glu1, b_glu1, w_glu2, b_glu2, w_out_b):
    n_batch, seq_len, d = x_prompt.shape
    n_dec, dec_len, _ = x_sample.shape
    depth = norm_pre.shape[0]
    n_ssm, n_groups, state_p = a_re.shape
    n_state = n_groups * state_p
    assert seq_len % ROW_TILE == 0 and ROW_TILE % CHUNK == 0 and seq_len % SCAN_STEPS == 0
    assert n_batch == V7X_SUBLANES and n_dec % (2 * V7X_SUBLANES) == 0
    assert dec_len <= CHUNK and dec_len % BLOCK == 0
    assert state_p == STATE_P and b_re.shape[-1] == SSM_GROUP and w_s.shape[1] == N_HEADS

    rows3 = lambda a: a.reshape(a.shape[0], 1, -1)
    npre, npost = rows3(norm_pre), rows3(norm_post)
    lng, lnb = rows3(ln_v_g), rows3(ln_v_b)
    wa16 = ((w_in_a[:1].astype(BF16), 0), (w_out_a[:1].astype(BF16), 0))
    bst = jnp.swapaxes(b_s, 1, 2)
    ws_dec = w_s[:, :, :dec_len, :dec_len].reshape(-1)
    bs_dec = b_s[:, :, :dec_len].reshape(-1)

    b1, b2 = rows3(b_glu1), rows3(b_glu2)
    lanes_gp = lambda a, perm: jnp.transpose(a, perm).reshape(n_ssm, SSM_GROUP, n_state)
    l4r, l4i, wur, wui, vcw, kloc = _s5_prep(
        rows3(a_re), rows3(a_im), rows3(jnp.repeat(log_dt, state_p, axis=1)),
        lanes_gp(b_re, (0, 3, 1, 2)), lanes_gp(b_im, (0, 3, 1, 2)),
        lanes_gp(c_re, (0, 2, 1, 3)), lanes_gp(c_im, (0, 2, 1, 3)))
    core_consts = (l4r, l4i, wur, wui, vcw, kloc, rows3(d_skip))
    h0 = (state_ssm_re.reshape(n_ssm, n_dec, n_state), state_ssm_im.reshape(n_ssm, n_dec, n_state))

    xp = x_prompt.reshape(n_batch * seq_len, d)
    xs = jnp.transpose(x_sample, (1, 0, 2)).reshape(dec_len * n_dec, d)

    v_rows, st_p_re, st_p_im, st_s_re, st_s_im = [], [], [], [], []
    for i in range(depth):
        j = i // 2
        if i % 2 == 0:
            jobs = [(w, j) for w in (w_in_b, w_glu1, w_glu2, w_out_b)] if i + 1 < depth else []
            xp, wb16 = _gmlp_prompt(xp, j, i, npre, npost, lng, lnb, *wa16, w_s, bst, jobs)
            xs, v = _gmlp_sample(xs, n_dec, dec_len, j, i, npre, npost, lng, lnb, *wa16,
                                 ws_dec, bs_dec)
            v_rows.append(v)
        else:
            win = (wb16[0], 0)
            out_consts = ((wb16[1], 0), (b1, j), (wb16[2], 0), (b2, j), (wb16[3], 0))
            jobs = [(w_in_a, j + 1), (w_out_a, j + 1)] if i + 1 < depth else []
            xp3, sr, si, wa16 = _s5_layer(xp.reshape(n_batch, seq_len, d), None, j, i, npre, win,
                                          core_consts, out_consts, npost, n_batch, SCAN_STEPS, jobs)
            wa16 = tuple((w, 0) for w in wa16)
            xp = xp3.reshape(n_batch * seq_len, d)
            st_p_re.append(sr)
            st_p_im.append(si)
            xs, sr, si, _ = _s5_layer(xs, h0, j, i, npre, win, core_consts, out_consts, npost,
                                      n_dec, dec_len)
            st_s_re.append(sr)
            st_s_im.append(si)

    y_prompt = xp.reshape(n_batch, seq_len, d)
    y_sample = jnp.transpose(xs.reshape(dec_len, n_dec, d), (1, 0, 2))
    chunk_v = jnp.transpose(jnp.stack(v_rows).reshape(len(v_rows), dec_len, n_dec, -1), (0, 2, 1, 3))
    states = lambda parts, n: jnp.stack(parts).reshape(len(parts), n, n_groups, state_p)
    return (y_prompt, y_sample, chunk_v, states(st_p_re, n_batch), states(st_p_im, n_batch),
            states(st_s_re, n_dec), states(st_s_im, n_dec))
```

```python
import functools

import jax
import jax.numpy as jnp
from jax import lax
from jax.experimental import pallas as pl
from jax.experimental.pallas import tpu as pltpu

EPS = 1e-6
CHUNK = 128
N_HEADS = 8
SSM_GROUP = 16
STATE_P = 64

V7X_LANES = 128
V7X_SUBLANES = 8
V7X_MXU_DIM = 256
V7X_VMEM_BYTES = 64 * 1024 * 1024

ROW_TILE = 1024
SCAN_STEPS = 64
BLOCK = 4
PIECE_ROWS = 256

BF16 = jnp.bfloat16
F32 = jnp.float32


def _dot(a, b):
    return jnp.dot(a, b, preferred_element_type=F32)


def _rmsnorm(x, g):
    ms = jnp.mean(x * x, axis=-1, keepdims=True)
    return x * lax.rsqrt(ms + EPS) * g


def _layernorm(x, g, b):
    mu = jnp.mean(x, axis=-1, keepdims=True)
    xc = x - mu
    var = jnp.mean(xc * xc, axis=-1, keepdims=True)
    return xc * lax.rsqrt(var + EPS) * g + b


def _silu(z):
    return z * jax.nn.sigmoid(z)


def _div_pow2(x, n):
    assert n & (n - 1) == 0
    return lax.shift_right_logical(x, n.bit_length() - 1)


def _vmem_limit(nbytes):
    return int(min(V7X_VMEM_BYTES - (4 << 20), nbytes + (12 << 20)))


def _layer_spec(stacked, layer):
    tail = stacked.shape[1:]
    return pl.BlockSpec((None,) + tail, lambda *_: (layer,) + (0,) * len(tail),
                        pipeline_mode=pl.Buffered(1))


def _layer_bytes(*stacked):
    return sum(a[0].size * a.dtype.itemsize for a in stacked)


def _cast_specs(jobs, n_steps):
    in_specs, out_specs, out_shapes = [], [], []
    for w, layer in jobs:
        _, rows, cols = w.shape
        slab = rows // n_steps
        assert rows % n_steps == 0 and slab % (2 * V7X_SUBLANES) == 0
        in_specs.append(pl.BlockSpec((None, slab, cols), lambda i, layer=layer: (layer, i, 0)))
        out_specs.append(pl.BlockSpec((None, slab, cols), lambda i: (0, i, 0)))
        out_shapes.append(jax.ShapeDtypeStruct((1, rows, cols), BF16))
    return in_specs, out_specs, out_shapes


def _cast_slabs(in_refs, out_refs):
    for src, dst in zip(in_refs, out_refs):
        dst[...] = src[...].astype(BF16)


def _row_spec(rows, width):
    return pl.BlockSpec((rows, width), lambda i: (i, 0))


_SMEM = pl.BlockSpec(memory_space=pltpu.SMEM)


def _gmlp_front(x, npre_ref, lng_ref, lnb_ref, win_ref, exp_a):
    hn, v = [], []
    for r0 in range(0, x.shape[0], PIECE_ROWS):
        hn.append(_rmsnorm(x[r0:r0 + PIECE_ROWS], npre_ref[...]).astype(BF16))
        v.append(_layernorm(_dot(hn[-1], win_ref[:, exp_a:2 * exp_a]), lng_ref[...], lnb_ref[...]))
    return jnp.concatenate(hn, axis=0), jnp.concatenate(v, axis=0)


def _gmlp_back(x, gated_ref, wout_ref, npost_ref, o_ref):
    for r0 in range(0, x.shape[0], PIECE_ROWS):
        r = slice(r0, r0 + PIECE_ROWS)
        out = _dot(gated_ref[r, :], wout_ref[...])
        o_ref[r, :] = x[r] + _rmsnorm(out, npost_ref[...])


def _gmlp_prompt_kernel(x_ref, npre_ref, npost_ref, lng_ref, lnb_ref, win_ref, wout_ref,
                        ws_ref, bst_ref, *refs, n_cast):
    o_ref, gated_ref = refs[n_cast], refs[-1]
    _cast_slabs(refs[:n_cast], refs[n_cast + 1:-1])
    rows = x_ref.shape[0]
    exp_a = wout_ref.shape[0]
    hd = exp_a // N_HEADS
    x = x_ref[...]
    hn, v = _gmlp_front(x, npre_ref, lng_ref, lnb_ref, win_ref, exp_a)
    vb = v.astype(BF16)
    causal = (lax.broadcasted_iota(jnp.int32, (CHUNK, CHUNK), 1)
              <= lax.broadcasted_iota(jnp.int32, (CHUNK, CHUNK), 0))
    for h in range(N_HEADS):
        lo = h * hd
        wsh = jnp.where(causal, ws_ref[h], 0.0).astype(BF16)
        bias = bst_ref[:, h:h + 1]
        u = _dot(hn, win_ref[:, lo:lo + hd])
        z = _dot(hn, win_ref[:, 2 * exp_a + lo:2 * exp_a + lo + hd])
        s = jnp.concatenate(
            [_dot(wsh, vb[c * CHUNK:(c + 1) * CHUNK, lo:lo + hd]) + bias
             for c in range(rows // CHUNK)], axis=0)
        gated_ref[:, lo:lo + hd] = (u * s * _silu(z)).astype(BF16)
    _gmlp_back(x, gated_ref, wout_ref, npost_ref, o_ref)


def _gmlp_sample_kernel(ws_ref, bs_ref, x_ref, npre_ref, npost_ref, lng_ref, lnb_ref, win_ref,
                        wout_ref, o_ref, v_ref, gated_ref, *, n_seq, seq_len, layer):
    exp_a = wout_ref.shape[0]
    hd = exp_a // N_HEADS
    x = x_ref[...]
    hn, v = _gmlp_front(x, npre_ref, lng_ref, lnb_ref, win_ref, exp_a)
    v_ref[...] = v
    for h in range(N_HEADS):
        lo = h * hd
        u = _dot(hn, win_ref[:, lo:lo + hd])
        z = _dot(hn, win_ref[:, 2 * exp_a + lo:2 * exp_a + lo + hd])
        vt = [v[t * n_seq:(t + 1) * n_seq, lo:lo + hd] for t in range(seq_len)]
        parts = []
        for t in range(seq_len):
            b_idx = (layer * N_HEADS + h) * seq_len + t
            s = ws_ref[b_idx * seq_len] * vt[0]
            for t2 in range(1, t + 1):
                s = s + ws_ref[b_idx * seq_len + t2] * vt[t2]
            parts.append(s + bs_ref[b_idx])
        s = jnp.concatenate(parts, axis=0)
        gated_ref[:, lo:lo + hd] = (u * s * _silu(z)).astype(BF16)
    _gmlp_back(x, gated_ref, wout_ref, npost_ref, o_ref)


def _gmlp_prompt(x2d, layer_a, layer, npre, npost, lng, lnb, win, wout, ws, bst, cast_jobs):
    rows, d = x2d.shape
    (win, win_l), (wout, wout_l) = win, wout
    exp_a = wout.shape[1]
    n_steps = rows // ROW_TILE
    c_in, c_out, c_shapes = _cast_specs(cast_jobs, n_steps)
    footprint = (_layer_bytes(win, wout, ws) + 4 * ROW_TILE * d * 4
                 + ROW_TILE * exp_a * (2 + 4 + 2 + 8)
                 + 6 * sum(w[0].size for w, _ in cast_jobs) // n_steps * 2)
    res = pl.pallas_call(
        functools.partial(_gmlp_prompt_kernel, n_cast=len(cast_jobs)),
        grid=(n_steps,),
        in_specs=[
            _row_spec(ROW_TILE, d),
            _layer_spec(npre, layer), _layer_spec(npost, layer),
            _layer_spec(lng, layer_a), _layer_spec(lnb, layer_a),
            _layer_spec(win, win_l), _layer_spec(wout, wout_l),
            _layer_spec(ws, layer_a), _layer_spec(bst, layer_a),
        ] + c_in,
        out_specs=[_row_spec(ROW_TILE, d)] + c_out,
        out_shape=[jax.ShapeDtypeStruct((rows, d), F32)] + c_shapes,
        scratch_shapes=[pltpu.VMEM((ROW_TILE, exp_a), BF16)],
        compiler_params=pltpu.CompilerParams(
            dimension_semantics=("parallel",),
            vmem_limit_bytes=_vmem_limit(footprint)),
        name="gmlp_prompt",
    )(x2d, npre, npost, lng, lnb, win, wout, ws, bst, *[w for w, _ in cast_jobs])
    return res[0], res[1:]


def _gmlp_sample(xs, n_seq, seq_len, layer_a, layer, npre, npost, lng, lnb, win, wout,
                 ws_flat, bs_flat):
    rows, d = xs.shape
    (win, win_l), (wout, wout_l) = win, wout
    exp_a = wout.shape[1]
    footprint = _layer_bytes(win, wout) + 4 * rows * d * 4 + rows * exp_a * (2 + 8 + 4 + 8)
    return pl.pallas_call(
        functools.partial(_gmlp_sample_kernel, n_seq=n_seq, seq_len=seq_len, layer=layer_a),
        grid=(1,),
        in_specs=[_SMEM, _SMEM, _row_spec(rows, d),
                  _layer_spec(npre, layer), _layer_spec(npost, layer),
                  _layer_spec(lng, layer_a), _layer_spec(lnb, layer_a),
                  _layer_spec(win, win_l), _layer_spec(wout, wout_l)],
        out_specs=[_row_spec(rows, d), _row_spec(rows, exp_a)],
        out_shape=[jax.ShapeDtypeStruct((rows, d), F32),
                   jax.ShapeDtypeStruct((rows, exp_a), F32)],
        scratch_shapes=[pltpu.VMEM((rows, exp_a), BF16)],
        compiler_params=pltpu.CompilerParams(
            dimension_semantics=("arbitrary",),
            vmem_limit_bytes=_vmem_limit(footprint)),
        name="gmlp_sample",
    )(ws_flat, bs_flat, xs, npre, npost, lng, lnb, win, wout)


def _s5_prep_kernel(are_ref, aim_ref, ldt_ref, btr_ref, bti_ref, ctr_ref, cti_ref,
                    l4r_ref, l4i_ref, wur_ref, wui_ref, vc_ref, kloc_ref):
    dt = jnp.exp(ldt_ref[...])
    ar = are_ref[...]
    ai = aim_ref[...]
    mag = jnp.exp(dt * ar)
    ang = dt * ai
    abr = mag * jnp.cos(ang)
    abi = mag * jnp.sin(ang)
    nr = abr - 1.0
    ni = abi
    den = ar * ar + ai * ai
    cre = (nr * ar + ni * ai) / den
    cim = (ni * ar - nr * ai) / den
    btr = btr_ref[...]
    bti = bti_ref[...]
    bbr = cre * btr - cim * bti
    bbi = cre * bti + cim * btr

    def cmul(xr, xi, yr, yi):
        return xr * yr - xi * yi, xr * yi + xi * yr

    lam = [(jnp.ones_like(abr), jnp.zeros_like(abi)), (abr, abi)]
    for _ in range(BLOCK - 1):
        lam.append(cmul(*lam[-1], abr, abi))
    l4r_ref[...] = lam[BLOCK][0]
    l4i_ref[...] = lam[BLOCK][1]

    n_q, k_rows, n_cols = wur_ref.shape
    q_groups = n_cols // STATE_P
    q_rows = q_groups * SSM_GROUP
    same_q = (_div_pow2(lax.broadcasted_iota(jnp.int32, (q_rows, n_cols), 0), SSM_GROUP)
              == _div_pow2(lax.broadcasted_iota(jnp.int32, (q_rows, n_cols), 1), STATE_P))
    for ip in range(BLOCK):
        ur, ui = cmul(bbr, bbi, *lam[BLOCK - 1 - ip])
        for q in range(n_q):
            for src, dst in ((ur, wur_ref), (ui, wui_ref)):
                blk = jnp.concatenate([src[:, q * n_cols:(q + 1) * n_cols]] * q_groups, axis=0)
                dst[q, ip * q_rows:(ip + 1) * q_rows, :] = jnp.where(same_q, blk, 0.0).astype(BF16)

    ctr = ctr_ref[...]
    cti = cti_ref[...]
    cl = [cmul(ctr, cti, *lam[t]) for t in range(BLOCK + 1)]

    def tile_rows(a, q):
        blk = jnp.concatenate([a[:, q * n_cols:(q + 1) * n_cols]] * q_groups, axis=0)
        return jnp.where(same_q, blk, 0.0)

    half = V7X_LANES // 2
    low = lax.broadcasted_iota(jnp.int32, (q_rows, V7X_LANES), 1) < half
    for q in range(n_q):
        clm = [(tile_rows(c[0], q), tile_rows(c[1], q)) for c in cl]
        rows = [jnp.concatenate([clm[i + 1][0], -clm[i + 1][1]], axis=1) for i in range(BLOCK)]
        vc_ref[q] = jnp.concatenate(rows, axis=0).T.astype(BF16)
        bcat = jnp.concatenate([tile_rows(bbr, q), -tile_rows(bbi, q)], axis=1).astype(BF16)
        ccat = jnp.concatenate([jnp.concatenate(clm[t], axis=1) for t in range(BLOCK)], axis=0)
        kall = lax.dot_general(bcat, ccat.astype(BF16), (((1,), (1,)), ((), ())),
                               preferred_element_type=F32)
        k0, k1 = kall[:, :V7X_LANES], kall[:, V7X_LANES:]
        r0, r1 = pltpu.roll(k0, half, 1), pltpu.roll(k1, half, 1)
        zero = jnp.zeros_like(k0)
        shifted = [(k0, k1),
                   (jnp.where(low, zero, r0), jnp.where(low, r0, r1)),
                   (zero, k0),
                   (zero, jnp.where(low, zero, r0))]
        kloc_ref[q] = jnp.concatenate([jnp.concatenate(sh, axis=1) for sh in shifted],
                                      axis=0).astype(BF16)


def _s5_prep(are, aim, ldt, btr, bti, ctr, cti):
    n_layers, _, n_state = are.shape
    n_q = n_state // V7X_MXU_DIM

    def per_layer(shape):
        return pl.BlockSpec((None,) + shape, lambda l: (l,) + (0,) * len(shape))

    out_tails = [(1, n_state), (1, n_state),
                 (n_q, V7X_MXU_DIM, V7X_MXU_DIM), (n_q, V7X_MXU_DIM, V7X_MXU_DIM),
                 (n_q, 2 * V7X_MXU_DIM, V7X_MXU_DIM), (n_q, V7X_MXU_DIM, V7X_MXU_DIM)]
    out_dtypes = [F32, F32, BF16, BF16, BF16, BF16]
    args = (are, aim, ldt, btr, bti, ctr, cti)
    return pl.pallas_call(
        _s5_prep_kernel,
        grid=(n_layers,),
        in_specs=[per_layer(a.shape[1:]) for a in args],
        out_specs=[per_layer(t) for t in out_tails],
        out_shape=[jax.ShapeDtypeStruct((n_layers,) + t, dt) for t, dt in zip(out_tails, out_dtypes)],
        compiler_params=pltpu.CompilerParams(
            dimension_semantics=("parallel",),
            vmem_limit_bytes=_vmem_limit(2 * sum(
                int(jnp.dtype(dt).itemsize) * functools.reduce(lambda a, b: a * b, t)
                for t, dt in zip(out_tails, out_dtypes)))),
        name="s5_prep",
    )(*args)


N_S5_PARAMS = 15


def _s5_layer_kernel(*refs, n_seq, n_steps, has_h0, swap_bt, n_cast):
    if has_h0:
        x_ref, h0r_ref, h0i_ref = refs[:3]
        refs = refs[3:]
    else:
        x_ref = refs[0]
        refs = refs[1:]
    (npre_ref, win_ref, l4r_ref, l4i_ref, wur_ref, wui_ref, vc_ref, kloc_ref, dsk_ref,
     w1_ref, b1_ref, w2_ref, b2_ref, wout_ref, npost_ref) = refs[:N_S5_PARAMS]
    refs = refs[N_S5_PARAMS:]
    o_ref, str_ref, sti_ref = refs[n_cast:n_cast + 3]
    n_scratch = 9 if swap_bt else 5
    hpr_ref, hpi_ref, hr_ref, hi_ref, y_ref = refs[-n_scratch:][:5]
    _cast_slabs(refs[:n_cast], refs[n_cast + 3:-n_scratch])
    step = pl.program_id(0)
    n_tiles = pl.num_programs(0)

    if swap_bt:
        xbuf_ref, obuf_ref, in_sem, out_sem = refs[-4:]
        slot = step % 2

        def tile_copies(tile, sl, fetch):
            seq_rows = pl.ds(tile * n_steps, n_steps)
            if fetch:
                return [pltpu.make_async_copy(x_ref.at[b, seq_rows, :], xbuf_ref.at[sl, :, b, :],
                                              in_sem.at[sl, b]) for b in range(n_seq)]
            return [pltpu.make_async_copy(obuf_ref.at[sl, :, b, :], o_ref.at[b, seq_rows, :],
                                          out_sem.at[sl, b]) for b in range(n_seq)]

        @pl.when(step == 0)
        def _():
            for cp in tile_copies(0, 0, True):
                cp.start()

        @pl.when(step + 1 < n_tiles)
        def _():
            for cp in tile_copies(step + 1, 1 - slot, True):
                cp.start()

        for cp in tile_copies(step, slot, True):
            cp.wait()

        @pl.when(step >= 2)
        def _():
            for cp in tile_copies(step - 2, slot, False):
                cp.wait()

    @pl.when(step == 0)
    def _():
        if has_h0:
            hr_ref[...] = h0r_ref[...]
            hi_ref[...] = h0i_ref[...]
        else:
            hr_ref[...] = jnp.zeros_like(hr_ref)
            hi_ref[...] = jnp.zeros_like(hi_ref)

    d = npre_ref.shape[-1]
    width = win_ref.shape[1] // 2
    c_rows = PIECE_ROWS
    c_steps = c_rows // n_seq
    n_pieces = n_steps // c_steps
    x_c, xb_c, z_c = [], [], []
    for c in range(n_pieces):
        if swap_bt:
            xc = xbuf_ref.at[slot][c * c_steps:(c + 1) * c_steps].reshape(c_rows, d)
        else:
            xc = x_ref[c * c_rows:(c + 1) * c_rows, :]
        hn = _rmsnorm(xc, npre_ref[...]).astype(BF16)
        x_c.append(xc)
        xb_c.append(_dot(hn, win_ref[:, :width]))
        z_c.append(_dot(hn, win_ref[:, width:]))
    xb = jnp.concatenate(xb_c, axis=0)

    n_blocks = n_steps // BLOCK
    rows = n_blocks * n_seq
    x4 = xb.reshape(n_blocks, BLOCK, n_seq, width)
    xi = [x4[:, i].reshape(rows, width) for i in range(BLOCK)]

    sub = V7X_SUBLANES
    pair = 2 * sub
    n_q, _, n_cols = wur_ref.shape
    n_v = width // V7X_LANES
    half = V7X_LANES // 2
    low = lax.broadcasted_iota(jnp.int32, (rows, V7X_LANES), 1) < half

    lhs_q = []
    for v in range(n_v):
        cols = slice(v * V7X_LANES, (v + 1) * V7X_LANES)
        p = [a[:, cols] for a in xi]
        pr = [pltpu.roll(a, half, 1) for a in p]
        lhs_lo = jnp.concatenate([jnp.where(low, p[i], pr[i + 1]) for i in range(0, BLOCK, 2)], axis=1)
        lhs_hi = jnp.concatenate([jnp.where(low, pr[i], p[i + 1]) for i in range(0, BLOCK, 2)], axis=1)
        for q, lhs in ((2 * v, lhs_lo.astype(BF16)), (2 * v + 1, lhs_hi.astype(BF16))):
            lanes = pl.ds(q * n_cols, n_cols)
            wr = _dot(lhs, wur_ref[q])
            wi = _dot(lhs, wui_ref[q])
            if n_seq == sub:
                ar = jnp.broadcast_to(l4r_ref[:, lanes], (sub, n_cols))
                ai = jnp.broadcast_to(l4i_ref[:, lanes], (sub, n_cols))
                hr, hi = hr_ref[:, lanes], hi_ref[:, lanes]
                for k in range(n_blocks // 2):
                    prv_r, prv_i = [], []
                    for r0 in (2 * k * sub, (2 * k + 1) * sub):
                        prv_r.append(hr)
                        prv_i.append(hi)
                        hr, hi = (ar * hr - ai * hi + wr[r0:r0 + sub],
                                  ar * hi + ai * hr + wi[r0:r0 + sub])
                    hpr_ref[k * pair:(k + 1) * pair, lanes] = jnp.concatenate(prv_r, axis=0).astype(BF16)
                    hpi_ref[k * pair:(k + 1) * pair, lanes] = jnp.concatenate(prv_i, axis=0).astype(BF16)
                hr_ref[:, lanes] = hr
                hi_ref[:, lanes] = hi
            else:
                ar = jnp.broadcast_to(l4r_ref[:, lanes], (pair, n_cols))
                ai = jnp.broadcast_to(l4i_ref[:, lanes], (pair, n_cols))
                for m in range(n_seq // pair):
                    srows = pl.ds(m * pair, pair)
                    hr, hi = hr_ref[srows, lanes], hi_ref[srows, lanes]
                    for blk in range(n_blocks):
                        r0 = blk * n_seq + m * pair
                        hpr_ref[r0:r0 + pair, lanes] = hr.astype(BF16)
                        hpi_ref[r0:r0 + pair, lanes] = hi.astype(BF16)
                        hr, hi = (ar * hr - ai * hi + wr[r0:r0 + pair],
                                  ar * hi + ai * hr + wi[r0:r0 + pair])
                    hr_ref[srows, lanes] = hr
                    hi_ref[srows, lanes] = hi
            lhs_q.append(lhs)

    for v in range(n_v):
        cols = slice(v * V7X_LANES, (v + 1) * V7X_LANES)
        yq = []
        for q in (2 * v, 2 * v + 1):
            lanes = pl.ds(q * n_cols, n_cols)
            st = jnp.concatenate([hpr_ref[:, lanes], hpi_ref[:, lanes]], axis=1)
            yq.append(_dot(st, vc_ref[q]) + _dot(lhs_q[q], kloc_ref[q]))
        ys = []
        for i in range(BLOCK):
            lo_q, hi_q = (a[:, (i // 2) * V7X_LANES:(i // 2 + 1) * V7X_LANES] for a in yq)
            if i % 2 == 0:
                yv = jnp.where(low, lo_q, pltpu.roll(hi_q, half, 1))
            else:
                yv = jnp.where(low, pltpu.roll(lo_q, half, 1), hi_q)
            ys.append(jax.nn.gelu(yv + dsk_ref[:, cols] * xi[i][:, cols]))
        y = jnp.stack([a.reshape(n_blocks, n_seq, V7X_LANES) for a in ys], axis=1)
        y_ref[:, cols] = y.reshape(n_steps * n_seq, V7X_LANES).astype(BF16)

    for c in range(n_pieces):
        y = y_ref[c * c_rows:(c + 1) * c_rows, :]
        g = (_dot(y, w1_ref[...]) + b1_ref[...]) * jax.nn.sigmoid(_dot(y, w2_ref[...]) + b2_ref[...])
        out = _dot((g * _silu(z_c[c])).astype(BF16), wout_ref[...])
        xn = x_c[c] + _rmsnorm(out, npost_ref[...])
        if swap_bt:
            obuf_ref.at[slot][c * c_steps:(c + 1) * c_steps] = xn.reshape(c_steps, n_seq, d)
        else:
            o_ref[c * c_rows:(c + 1) * c_rows, :] = xn

    if swap_bt:
        for cp in tile_copies(step, slot, False):
            cp.start()

    @pl.when(step == n_tiles - 1)
    def _():
        str_ref[...] = hr_ref[...]
        sti_ref[...] = hi_ref[...]
        if swap_bt:
            for cp in tile_copies(step, slot, False):
                cp.wait()

    if swap_bt:
        @pl.when(jnp.logical_and(step == n_tiles - 1, n_tiles >= 2))
        def _():
            for cp in tile_copies(step - 1, 1 - slot, False):
                cp.wait()


def _s5_layer(x, h0, layer_b, layer, npre, win, core_consts, out_consts, npost, n_seq, n_steps,
              cast_jobs=()):
    swap_bt = x.ndim == 3
    d = x.shape[-1]
    exp_b = win[0].shape[2] // 2
    n_state = core_consts[0].shape[-1]
    tile = n_seq * n_steps
    n_tiles = x.size // (tile * d)
    rows = tile // BLOCK
    has_h0 = h0 is not None
    assert n_steps % BLOCK == 0 and (n_seq > V7X_SUBLANES or (n_steps // BLOCK) % 2 == 0)
    assert tile % PIECE_ROWS == 0 and PIECE_ROWS % n_seq == 0
    params_b = [win] + [(c, layer_b) for c in core_consts] + list(out_consts)
    assert len(params_b) + 2 == N_S5_PARAMS
    c_in, c_out, c_shapes = _cast_specs(cast_jobs, n_tiles)
    args = ([x] + (list(h0) if has_h0 else []) + [npre] + [a for a, _ in params_b] + [npost]
            + [w for w, _ in cast_jobs])
    if swap_bt:
        io_spec = pl.BlockSpec(memory_space=pl.ANY)
        io_scratch = [pltpu.VMEM((2, n_steps, n_seq, d), F32), pltpu.VMEM((2, n_steps, n_seq, d), F32),
                      pltpu.SemaphoreType.DMA((2, n_seq)), pltpu.SemaphoreType.DMA((2, n_seq))]
    else:
        io_spec = pl.BlockSpec((tile, d), lambda i: (i, 0))
        io_scratch = []
    in_specs = ([io_spec] + [_layer_spec(a, layer_b) for a in (h0 if has_h0 else ())]
                + [_layer_spec(npre, layer)] + [_layer_spec(a, l) for a, l in params_b]
                + [_layer_spec(npost, layer)] + c_in)
    st_spec = pl.BlockSpec((n_seq, n_state), lambda i: (0, 0))
    st_shape = jax.ShapeDtypeStruct((n_seq, n_state), F32)
    footprint = (_layer_bytes(*[a for a, _ in params_b]) + (_layer_bytes(*h0) if has_h0 else 0)
                 + 4 * tile * d * 4 + 2 * rows * n_state * 2 + 8 * n_seq * n_state * 4
                 + tile * exp_b * (2 + 6 * 4)
                 + 6 * sum(w[0].size for w, _ in cast_jobs) // n_tiles * 2)
    res = pl.pallas_call(
        functools.partial(_s5_layer_kernel, n_seq=n_seq, n_steps=n_steps, has_h0=has_h0,
                          swap_bt=swap_bt, n_cast=len(cast_jobs)),
        grid=(n_tiles,),
        in_specs=in_specs,
        out_specs=[io_spec, st_spec, st_spec] + c_out,
        out_shape=[jax.ShapeDtypeStruct(x.shape, F32), st_shape, st_shape] + c_shapes,
        scratch_shapes=[pltpu.VMEM((rows, n_state), BF16), pltpu.VMEM((rows, n_state), BF16),
                        pltpu.VMEM((n_seq, n_state), F32), pltpu.VMEM((n_seq, n_state), F32),
                        pltpu.VMEM((tile, exp_b), BF16)] + io_scratch,
        compiler_params=pltpu.CompilerParams(
            dimension_semantics=("arbitrary",),
            vmem_limit_bytes=_vmem_limit(footprint)),
        name="s5_layer_sample" if has_h0 else "s5_layer_prompt",
    )(*args)
    return res[0], res[1], res[2], res[3:]


def kernel(x_prompt, x_sample, state_ssm_re, state_ssm_im, norm_pre, norm_post,
           w_in_a, ln_v_g, ln_v_b, w_s, b_s, w_out_a,
           w_in_b, a_re, a_im, log_dt, b_re, b_im, c_re, c_im, d_skip,
           w_glu1, b_glu1, w_glu2, b_glu2, w_out_b):
    n_batch, seq_len, d = x_prompt.shape
    n_dec, dec_len, _ = x_sample.shape
    depth = norm_pre.shape[0]
    n_ssm, n_groups, state_p = a_re.shape
    n_state = n_groups * state_p
    assert seq_len % ROW_TILE == 0 and ROW_TILE % CHUNK == 0 and seq_len % SCAN_STEPS == 0
    assert n_batch == V7X_SUBLANES and n_dec % (2 * V7X_SUBLANES) == 0
    assert dec_len <= CHUNK and dec_len % BLOCK == 0
    assert state_p == STATE_P and b_re.shape[-1] == SSM_GROUP and w_s.shape[1] == N_HEADS

    rows3 = lambda a: a.reshape(a.shape[0], 1, -1)
    npre, npost = rows3(norm_pre), rows3(norm_post)
    lng, lnb = rows3(ln_v_g), rows3(ln_v_b)
    wa16 = ((w_in_a[:1].astype(BF16), 0), (w_out_a[:1].astype(BF16), 0))
    bst = jnp.swapaxes(b_s, 1, 2)
    ws_dec = w_s[:, :, :dec_len, :dec_len].reshape(-1)
    bs_dec = b_s[:, :, :dec_len].reshape(-1)

    b1, b2 = rows3(b_glu1), rows3(b_glu2)
    lanes_gp = lambda a, perm: jnp.transpose(a, perm).reshape(n_ssm, SSM_GROUP, n_state)
    l4r, l4i, wur, wui, vcw, kloc = _s5_prep(
        rows3(a_re), rows3(a_im), rows3(jnp.repeat(log_dt, state_p, axis=1)),
        lanes_gp(b_re, (0, 3, 1, 2)), lanes_gp(b_im, (0, 3, 1, 2)),
        lanes_gp(c_re, (0, 2, 1, 3)), lanes_gp(c_im, (0, 2, 1, 3)))
    core_consts = (l4r, l4i, wur, wui, vcw, kloc, rows3(d_skip))
    h0 = (state_ssm_re.reshape(n_ssm, n_dec, n_state), state_ssm_im.reshape(n_ssm, n_dec, n_state))

    xp = x_prompt.reshape(n_batch * seq_len, d)
    xs = jnp.transpose(x_sample, (1, 0, 2)).reshape(dec_len * n_dec, d)

    v_rows, st_p_re, st_p_im, st_s_re, st_s_im = [], [], [], [], []
    for i in range(depth):
        j = i // 2
        if i % 2 == 0:
            jobs = [(w, j) for w in (w_in_b, w_glu1, w_glu2, w_out_b)] if i + 1 < depth else []
            xp, wb16 = _gmlp_prompt(xp, j, i, npre, npost, lng, lnb, *wa16, w_s, bst, jobs)
            xs, v = _gmlp_sample(xs, n_dec, dec_len, j, i, npre, npost, lng, lnb, *wa16,
                                 ws_dec, bs_dec)
            v_rows.append(v)
        else:
            win = (wb16[0], 0)
            out_consts = ((wb16[1], 0), (b1, j), (wb16[2], 0), (b2, j), (wb16[3], 0))
            jobs = [(w_in_a, j + 1), (w_out_a, j + 1)] if i + 1 < depth else []
            xp3, sr, si, wa16 = _s5_layer(xp.reshape(n_batch, seq_len, d), None, j, i, npre, win,
                                          core_consts, out_consts, npost, n_batch, SCAN_STEPS, jobs)
            wa16 = tuple((w, 0) for w in wa16)
            xp = xp3.reshape(n_batch * seq_len, d)
            st_p_re.append(sr)
            st_p_im.append(si)
            xs, sr, si, _ = _s5_layer(xs, h0, j, i, npre, win, core_consts, out_consts, npost,
                                      n_dec, dec_len)
            st_s_re.append(sr)
            st_s_im.append(si)

    y_prompt = xp.reshape(n_batch, seq_len, d)
    y_sample = jnp.transpose(xs.reshape(dec_len, n_dec, d), (1, 0, 2))
    chunk_v = jnp.transpose(jnp.stack(v_rows).reshape(len(v_rows), dec_len, n_dec, -1), (0, 2, 1, 3))
    states = lambda parts, n: jnp.stack(parts).reshape(len(parts), n, n_groups, state_p)
    return (y_prompt, y_sample, chunk_v, states(st_p_re, n_batch), states(st_p_im, n_batch),
            states(st_s_re, n_dec), states(st_s_im, n_dec))
```

```python
import functools

import jax
import jax.numpy as jnp
from jax import lax
from jax.experimental import pallas as pl
from jax.experimental.pallas import tpu as pltpu

EPS = 1e-6
CHUNK = 128
N_HEADS = 8
SSM_GROUP = 16
STATE_P = 64

V7X_LANES = 128
V7X_SUBLANES = 8
V7X_MXU_DIM = 256
V7X_VMEM_BYTES = 64 * 1024 * 1024

ROW_TILE = 1024
SCAN_STEPS = 64
BLOCK = 4
PIECE_ROWS = 256

BF16 = jnp.bfloat16
F32 = jnp.float32


def _dot(a, b):
    return jnp.dot(a, b, preferred_element_type=F32)


def _rmsnorm(x, g):
    ms = jnp.mean(x * x, axis=-1, keepdims=True)
    return x * lax.rsqrt(ms + EPS) * g


def _layernorm(x, g, b):
    mu = jnp.mean(x, axis=-1, keepdims=True)
    xc = x - mu
    var = jnp.mean(xc * xc, axis=-1, keepdims=True)
    return xc * lax.rsqrt(var + EPS) * g + b


def _silu(z):
    return z * jax.nn.sigmoid(z)


def _div_pow2(x, n):
    assert n & (n - 1) == 0
    return lax.shift_right_logical(x, n.bit_length() - 1)


def _vmem_limit(nbytes):
    return int(min(V7X_VMEM_BYTES - (4 << 20), nbytes + (12 << 20)))


def _layer_spec(stacked, layer):
    tail = stacked.shape[1:]
    return pl.BlockSpec((None,) + tail, lambda *_: (layer,) + (0,) * len(tail),
                        pipeline_mode=pl.Buffered(1))


def _layer_bytes(*stacked):
    return sum(a[0].size * a.dtype.itemsize for a in stacked)


def _cast_specs(jobs, n_steps):
    in_specs, out_specs, out_shapes = [], [], []
    for w, layer in jobs:
        _, rows, cols = w.shape
        slab = rows // n_steps
        assert rows % n_steps == 0 and slab % (2 * V7X_SUBLANES) == 0
        in_specs.append(pl.BlockSpec((None, slab, cols), lambda i, layer=layer: (layer, i, 0)))
        out_specs.append(pl.BlockSpec((None, slab, cols), lambda i: (0, i, 0)))
        out_shapes.append(jax.ShapeDtypeStruct((1, rows, cols), BF16))
    return in_specs, out_specs, out_shapes


def _cast_slabs(in_refs, out_refs):
    for src, dst in zip(in_refs, out_refs):
        dst[...] = src[...].astype(BF16)


def _row_spec(rows, width):
    return pl.BlockSpec((rows, width), lambda i: (i, 0))


_SMEM = pl.BlockSpec(memory_space=pltpu.SMEM)


def _gmlp_front(x, npre_ref, lng_ref, lnb_ref, win_ref, exp_a):
    hn, v = [], []
    for r0 in range(0, x.shape[0], PIECE_ROWS):
        hn.append(_rmsnorm(x[r0:r0 + PIECE_ROWS], npre_ref[...]).astype(BF16))
        v.append(_layernorm(_dot(hn[-1], win_ref[:, exp_a:2 * exp_a]), lng_ref[...], lnb_ref[...]))
    return jnp.concatenate(hn, axis=0), jnp.concatenate(v, axis=0)


def _gmlp_back(x, gated_ref, wout_ref, npost_ref, o_ref):
    for r0 in range(0, x.shape[0], PIECE_ROWS):
        r = slice(r0, r0 + PIECE_ROWS)
        out = _dot(gated_ref[r, :], wout_ref[...])
        o_ref[r, :] = x[r] + _rmsnorm(out, npost_ref[...])


def _gmlp_prompt_kernel(x_ref, npre_ref, npost_ref, lng_ref, lnb_ref, win_ref, wout_ref,
                        ws_ref, bst_ref, *refs, n_cast):
    o_ref, gated_ref = refs[n_cast], refs[-1]
    _cast_slabs(refs[:n_cast], refs[n_cast + 1:-1])
    rows = x_ref.shape[0]
    exp_a = wout_ref.shape[0]
    hd = exp_a // N_HEADS
    x = x_ref[...]
    hn, v = _gmlp_front(x, npre_ref, lng_ref, lnb_ref, win_ref, exp_a)
    vb = v.astype(BF16)
    causal = (lax.broadcasted_iota(jnp.int32, (CHUNK, CHUNK), 1)
              <= lax.broadcasted_iota(jnp.int32, (CHUNK, CHUNK), 0))
    for h in range(N_HEADS):
        lo = h * hd
        wsh = jnp.where(causal, ws_ref[h], 0.0).astype(BF16)
        bias = bst_ref[:, h:h + 1]
        u = _dot(hn, win_ref[:, lo:lo + hd])
        z = _dot(hn, win_ref[:, 2 * exp_a + lo:2 * exp_a + lo + hd])
        s = jnp.concatenate(
            [_dot(wsh, vb[c * CHUNK:(c + 1) * CHUNK, lo:lo + hd]) + bias
             for c in range(rows // CHUNK)], axis=0)
        gated_ref[:, lo:lo + hd] = (u * s * _silu(z)).astype(BF16)
    _gmlp_back(x, gated_ref, wout_ref, npost_ref, o_ref)


def _gmlp_sample_kernel(ws_ref, bs_ref, x_ref, npre_ref, npost_ref, lng_ref, lnb_ref, win_hbm,
                        wout_hbm, o_ref, v_ref, gated_ref, win_ref, wout_ref, w_sem, *, n_seq,
                        seq_len, layer, w_layers):
    exp_a = wout_ref.shape[0]
    hd = exp_a // N_HEADS
    col_parts = [slice(exp_a, 2 * exp_a), slice(0, exp_a), slice(2 * exp_a, 3 * exp_a)]
    copies = [pltpu.make_async_copy(win_hbm.at[w_layers[0], :, p], win_ref.at[:, p], w_sem.at[k])
              for k, p in enumerate(col_parts)]
    copies.append(pltpu.make_async_copy(wout_hbm.at[w_layers[1]], wout_ref, w_sem.at[len(col_parts)]))
    for cp in copies:
        cp.start()
    x = x_ref[...]
    copies[0].wait()
    hn, v = _gmlp_front(x, npre_ref, lng_ref, lnb_ref, win_ref, exp_a)
    v_ref[...] = v
    copies[1].wait()
    copies[2].wait()
    for h in range(N_HEADS):
        lo = h * hd
        u = _dot(hn, win_ref[:, lo:lo + hd])
        z = _dot(hn, win_ref[:, 2 * exp_a + lo:2 * exp_a + lo + hd])
        vt = [v[t * n_seq:(t + 1) * n_seq, lo:lo + hd] for t in range(seq_len)]
        parts = []
        for t in range(seq_len):
            b_idx = (layer * N_HEADS + h) * seq_len + t
            s = ws_ref[b_idx * seq_len] * vt[0]
            for t2 in range(1, t + 1):
                s = s + ws_ref[b_idx * seq_len + t2] * vt[t2]
            parts.append(s + bs_ref[b_idx])
        s = jnp.concatenate(parts, axis=0)
        gated_ref[:, lo:lo + hd] = (u * s * _silu(z)).astype(BF16)
    copies[3].wait()
    _gmlp_back(x, gated_ref, wout_ref, npost_ref, o_ref)


def _gmlp_prompt(x2d, layer_a, layer, npre, npost, lng, lnb, win, wout, ws, bst, cast_jobs):
    rows, d = x2d.shape
    (win, win_l), (wout, wout_l) = win, wout
    exp_a = wout.shape[1]
    n_steps = rows // ROW_TILE
    c_in, c_out, c_shapes = _cast_specs(cast_jobs, n_steps)
    footprint = (_layer_bytes(win, wout, ws) + 4 * ROW_TILE * d * 4
                 + ROW_TILE * exp_a * (2 + 4 + 2 + 8)
                 + 6 * sum(w[0].size for w, _ in cast_jobs) // n_steps * 2)
    res = pl.pallas_call(
        functools.partial(_gmlp_prompt_kernel, n_cast=len(cast_jobs)),
        grid=(n_steps,),
        in_specs=[
            _row_spec(ROW_TILE, d),
            _layer_spec(npre, layer), _layer_spec(npost, layer),
            _layer_spec(lng, layer_a), _layer_spec(lnb, layer_a),
            _layer_spec(win, win_l), _layer_spec(wout, wout_l),
            _layer_spec(ws, layer_a), _layer_spec(bst, layer_a),
        ] + c_in,
        out_specs=[_row_spec(ROW_TILE, d)] + c_out,
        out_shape=[jax.ShapeDtypeStruct((rows, d), F32)] + c_shapes,
        scratch_shapes=[pltpu.VMEM((ROW_TILE, exp_a), BF16)],
        compiler_params=pltpu.CompilerParams(
            dimension_semantics=("parallel",),
            vmem_limit_bytes=_vmem_limit(footprint)),
        name="gmlp_prompt",
    )(x2d, npre, npost, lng, lnb, win, wout, ws, bst, *[w for w, _ in cast_jobs])
    return res[0], res[1:]


def _gmlp_sample(xs, n_seq, seq_len, layer_a, layer, npre, npost, lng, lnb, win, wout,
                 ws_flat, bs_flat):
    rows, d = xs.shape
    (win, win_l), (wout, wout_l) = win, wout
    exp_a = wout.shape[1]
    footprint = _layer_bytes(win, wout) + 4 * rows * d * 4 + rows * exp_a * (2 + 8 + 4 + 8)
    hbm = pl.BlockSpec(memory_space=pl.ANY)
    return pl.pallas_call(
        functools.partial(_gmlp_sample_kernel, n_seq=n_seq, seq_len=seq_len, layer=layer_a,
                          w_layers=(win_l, wout_l)),
        grid=(1,),
        in_specs=[_SMEM, _SMEM, _row_spec(rows, d),
                  _layer_spec(npre, layer), _layer_spec(npost, layer),
                  _layer_spec(lng, layer_a), _layer_spec(lnb, layer_a), hbm, hbm],
        out_specs=[_row_spec(rows, d), _row_spec(rows, exp_a)],
        out_shape=[jax.ShapeDtypeStruct((rows, d), F32),
                   jax.ShapeDtypeStruct((rows, exp_a), F32)],
        scratch_shapes=[pltpu.VMEM((rows, exp_a), BF16), pltpu.VMEM(win.shape[1:], BF16),
                        pltpu.VMEM(wout.shape[1:], BF16), pltpu.SemaphoreType.DMA((4,))],
        compiler_params=pltpu.CompilerParams(
            dimension_semantics=("arbitrary",),
            vmem_limit_bytes=_vmem_limit(footprint)),
        name="gmlp_sample",
    )(ws_flat, bs_flat, xs, npre, npost, lng, lnb, win, wout)


def _s5_prep_kernel(are_ref, aim_ref, ldt_ref, btr_ref, bti_ref, ctr_ref, cti_ref,
                    l4r_ref, l4i_ref, wur_ref, wui_ref, vc_ref, kloc_ref):
    dt = jnp.exp(ldt_ref[...])
    ar = are_ref[...]
    ai = aim_ref[...]
    mag = jnp.exp(dt * ar)
    ang = dt * ai
    abr = mag * jnp.cos(ang)
    abi = mag * jnp.sin(ang)
    nr = abr - 1.0
    ni = abi
    den = ar * ar + ai * ai
    cre = (nr * ar + ni * ai) / den
    cim = (ni * ar - nr * ai) / den
    btr = btr_ref[...]
    bti = bti_ref[...]
    bbr = cre * btr - cim * bti
    bbi = cre * bti + cim * btr

    def cmul(xr, xi, yr, yi):
        return xr * yr - xi * yi, xr * yi + xi * yr

    lam = [(jnp.ones_like(abr), jnp.zeros_like(abi)), (abr, abi)]
    for _ in range(BLOCK - 1):
        lam.append(cmul(*lam[-1], abr, abi))
    l4r_ref[...] = lam[BLOCK][0]
    l4i_ref[...] = lam[BLOCK][1]

    n_q, k_rows, n_cols = wur_ref.shape
    q_groups = n_cols // STATE_P
    q_rows = q_groups * SSM_GROUP
    same_q = (_div_pow2(lax.broadcasted_iota(jnp.int32, (q_rows, n_cols), 0), SSM_GROUP)
              == _div_pow2(lax.broadcasted_iota(jnp.int32, (q_rows, n_cols), 1), STATE_P))
    for ip in range(BLOCK):
        ur, ui = cmul(bbr, bbi, *lam[BLOCK - 1 - ip])
        for q in range(n_q):
            for src, dst in ((ur, wur_ref), (ui, wui_ref)):
                blk = jnp.concatenate([src[:, q * n_cols:(q + 1) * n_cols]] * q_groups, axis=0)
                dst[q, ip * q_rows:(ip + 1) * q_rows, :] = jnp.where(same_q, blk, 0.0).astype(BF16)

    ctr = ctr_ref[...]
    cti = cti_ref[...]
    cl = [cmul(ctr, cti, *lam[t]) for t in range(BLOCK + 1)]

    def tile_rows(a, q):
        blk = jnp.concatenate([a[:, q * n_cols:(q + 1) * n_cols]] * q_groups, axis=0)
        return jnp.where(same_q, blk, 0.0)

    half = V7X_LANES // 2
    low = lax.broadcasted_iota(jnp.int32, (q_rows, V7X_LANES), 1) < half
    for q in range(n_q):
        clm = [(tile_rows(c[0], q), tile_rows(c[1], q)) for c in cl]
        rows = [jnp.concatenate([clm[i + 1][0], -clm[i + 1][1]], axis=1) for i in range(BLOCK)]
        vc_ref[q] = jnp.concatenate(rows, axis=0).T.astype(BF16)
        bcat = jnp.concatenate([tile_rows(bbr, q), -tile_rows(bbi, q)], axis=1).astype(BF16)
        ccat = jnp.concatenate([jnp.concatenate(clm[t], axis=1) for t in range(BLOCK)], axis=0)
        kall = lax.dot_general(bcat, ccat.astype(BF16), (((1,), (1,)), ((), ())),
                               preferred_element_type=F32)
        k0, k1 = kall[:, :V7X_LANES], kall[:, V7X_LANES:]
        r0, r1 = pltpu.roll(k0, half, 1), pltpu.roll(k1, half, 1)
        zero = jnp.zeros_like(k0)
        shifted = [(k0, k1),
                   (jnp.where(low, zero, r0), jnp.where(low, r0, r1)),
                   (zero, k0),
                   (zero, jnp.where(low, zero, r0))]
        kloc_ref[q] = jnp.concatenate([jnp.concatenate(sh, axis=1) for sh in shifted],
                                      axis=0).astype(BF16)


def _s5_prep(are, aim, ldt, btr, bti, ctr, cti):
    n_layers, _, n_state = are.shape
    n_q = n_state // V7X_MXU_DIM

    def per_layer(shape):
        return pl.BlockSpec((None,) + shape, lambda l: (l,) + (0,) * len(shape))

    out_tails = [(1, n_state), (1, n_state),
                 (n_q, V7X_MXU_DIM, V7X_MXU_DIM), (n_q, V7X_MXU_DIM, V7X_MXU_DIM),
                 (n_q, 2 * V7X_MXU_DIM, V7X_MXU_DIM), (n_q, V7X_MXU_DIM, V7X_MXU_DIM)]
    out_dtypes = [F32, F32, BF16, BF16, BF16, BF16]
    args = (are, aim, ldt, btr, bti, ctr, cti)
    return pl.pallas_call(
        _s5_prep_kernel,
        grid=(n_layers,),
        in_specs=[per_layer(a.shape[1:]) for a in args],
        out_specs=[per_layer(t) for t in out_tails],
        out_shape=[jax.ShapeDtypeStruct((n_layers,) + t, dt) for t, dt in zip(out_tails, out_dtypes)],
        compiler_params=pltpu.CompilerParams(
            dimension_semantics=("parallel",),
            vmem_limit_bytes=_vmem_limit(2 * sum(
                int(jnp.dtype(dt).itemsize) * functools.reduce(lambda a, b: a * b, t)
                for t, dt in zip(out_tails, out_dtypes)))),
        name="s5_prep",
    )(*args)


N_S5_PARAMS = 15
S5_WEIGHTS = (1, 4, 5, 6, 7, 9, 11, 13)


def _s5_layer_kernel(*refs, n_seq, n_steps, has_h0, swap_bt, n_cast, lazy_layers):
    if has_h0:
        x_ref, h0r_ref, h0i_ref = refs[:3]
        refs = refs[3:]
    else:
        x_ref = refs[0]
        refs = refs[1:]
    params = list(refs[:N_S5_PARAMS])
    refs = refs[N_S5_PARAMS:]
    o_ref, str_ref, sti_ref = refs[n_cast:n_cast + 3]
    n_lazy = len(S5_WEIGHTS) + 1 if lazy_layers else 0
    n_scratch = 5 + (4 if swap_bt else 0) + n_lazy
    scratch = refs[-n_scratch:]
    hpr_ref, hpi_ref, hr_ref, hi_ref, y_ref = scratch[:5]
    _cast_slabs(refs[:n_cast], refs[n_cast + 3:-n_scratch])
    step = pl.program_id(0)
    n_tiles = pl.num_programs(0)

    w_copies = []
    if lazy_layers:
        w_sem = scratch[-1]
        for k, (p, layer) in enumerate(zip(S5_WEIGHTS, lazy_layers)):
            w_copies.append(pltpu.make_async_copy(params[p].at[layer], scratch[-n_lazy + k],
                                                  w_sem.at[k]))
            params[p] = scratch[-n_lazy + k]
        for cp in w_copies:
            cp.start()

    def need(*which):
        for k in which:
            if w_copies:
                w_copies[k].wait()

    (npre_ref, win_ref, l4r_ref, l4i_ref, wur_ref, wui_ref, vc_ref, kloc_ref, dsk_ref,
     w1_ref, b1_ref, w2_ref, b2_ref, wout_ref, npost_ref) = params

    if swap_bt:
        xbuf_ref, obuf_ref, in_sem, out_sem = scratch[5:9]
        slot = step % 2

        def tile_copies(tile, sl, fetch):
            seq_rows = pl.ds(tile * n_steps, n_steps)
            if fetch:
                return [pltpu.make_async_copy(x_ref.at[b, seq_rows, :], xbuf_ref.at[sl, :, b, :],
                                              in_sem.at[sl, b]) for b in range(n_seq)]
            return [pltpu.make_async_copy(obuf_ref.at[sl, :, b, :], o_ref.at[b, seq_rows, :],
                                          out_sem.at[sl, b]) for b in range(n_seq)]

        @pl.when(step == 0)
        def _():
            for cp in tile_copies(0, 0, True):
                cp.start()

        @pl.when(step + 1 < n_tiles)
        def _():
            for cp in tile_copies(step + 1, 1 - slot, True):
                cp.start()

        for cp in tile_copies(step, slot, True):
            cp.wait()

        @pl.when(step >= 2)
        def _():
            for cp in tile_copies(step - 2, slot, False):
                cp.wait()

    @pl.when(step == 0)
    def _():
        if has_h0:
            hr_ref[...] = h0r_ref[...]
            hi_ref[...] = h0i_ref[...]
        else:
            hr_ref[...] = jnp.zeros_like(hr_ref)
            hi_ref[...] = jnp.zeros_like(hi_ref)

    d = npre_ref.shape[-1]
    width = win_ref.shape[1] // 2
    c_rows = PIECE_ROWS
    c_steps = c_rows // n_seq
    n_pieces = n_steps // c_steps
    x_c, xb_c, z_c = [], [], []
    need(0)
    for c in range(n_pieces):
        if swap_bt:
            xc = xbuf_ref.at[slot][c * c_steps:(c + 1) * c_steps].reshape(c_rows, d)
        else:
            xc = x_ref[c * c_rows:(c + 1) * c_rows, :]
        hn = _rmsnorm(xc, npre_ref[...]).astype(BF16)
        x_c.append(xc)
        xb_c.append(_dot(hn, win_ref[:, :width]))
        z_c.append(_dot(hn, win_ref[:, width:]))
    xb = jnp.concatenate(xb_c, axis=0)

    n_blocks = n_steps // BLOCK
    rows = n_blocks * n_seq
    x4 = xb.reshape(n_blocks, BLOCK, n_seq, width)
    xi = [x4[:, i].reshape(rows, width) for i in range(BLOCK)]

    sub = V7X_SUBLANES
    pair = 2 * sub
    n_q, _, n_cols = wur_ref.shape
    n_v = width // V7X_LANES
    half = V7X_LANES // 2
    low = lax.broadcasted_iota(jnp.int32, (rows, V7X_LANES), 1) < half

    lhs_q = []
    need(1, 2)
    for v in range(n_v):
        cols = slice(v * V7X_LANES, (v + 1) * V7X_LANES)
        p = [a[:, cols] for a in xi]
        pr = [pltpu.roll(a, half, 1) for a in p]
        lhs_lo = jnp.concatenate([jnp.where(low, p[i], pr[i + 1]) for i in range(0, BLOCK, 2)], axis=1)
        lhs_hi = jnp.concatenate([jnp.where(low, pr[i], p[i + 1]) for i in range(0, BLOCK, 2)], axis=1)
        for q, lhs in ((2 * v, lhs_lo.astype(BF16)), (2 * v + 1, lhs_hi.astype(BF16))):
            lanes = pl.ds(q * n_cols, n_cols)
            wr = _dot(lhs, wur_ref[q])
            wi = _dot(lhs, wui_ref[q])
            if n_seq == sub:
                ar = jnp.broadcast_to(l4r_ref[:, lanes], (sub, n_cols))
                ai = jnp.broadcast_to(l4i_ref[:, lanes], (sub, n_cols))
                hr, hi = hr_ref[:, lanes], hi_ref[:, lanes]
                for k in range(n_blocks // 2):
                    prv_r, prv_i = [], []
                    for r0 in (2 * k * sub, (2 * k + 1) * sub):
                        prv_r.append(hr)
                        prv_i.append(hi)
                        hr, hi = (ar * hr - ai * hi + wr[r0:r0 + sub],
                                  ar * hi + ai * hr + wi[r0:r0 + sub])
                    hpr_ref[k * pair:(k + 1) * pair, lanes] = jnp.concatenate(prv_r, axis=0).astype(BF16)
                    hpi_ref[k * pair:(k + 1) * pair, lanes] = jnp.concatenate(prv_i, axis=0).astype(BF16)
                hr_ref[:, lanes] = hr
                hi_ref[:, lanes] = hi
            else:
                ar = jnp.broadcast_to(l4r_ref[:, lanes], (pair, n_cols))
                ai = jnp.broadcast_to(l4i_ref[:, lanes], (pair, n_cols))
                for m in range(n_seq // pair):
                    srows = pl.ds(m * pair, pair)
                    hr, hi = hr_ref[srows, lanes], hi_ref[srows, lanes]
                    for blk in range(n_blocks):
                        r0 = blk * n_seq + m * pair
                        hpr_ref[r0:r0 + pair, lanes] = hr.astype(BF16)
                        hpi_ref[r0:r0 + pair, lanes] = hi.astype(BF16)
                        hr, hi = (ar * hr - ai * hi + wr[r0:r0 + pair],
                                  ar * hi + ai * hr + wi[r0:r0 + pair])
                    hr_ref[srows, lanes] = hr
                    hi_ref[srows, lanes] = hi
            lhs_q.append(lhs)

    need(3, 4)
    for v in range(n_v):
        cols = slice(v * V7X_LANES, (v + 1) * V7X_LANES)
        yq = []
        for q in (2 * v, 2 * v + 1):
            lanes = pl.ds(q * n_cols, n_cols)
            st = jnp.concatenate([hpr_ref[:, lanes], hpi_ref[:, lanes]], axis=1)
            yq.append(_dot(st, vc_ref[q]) + _dot(lhs_q[q], kloc_ref[q]))
        ys = []
        for i in range(BLOCK):
            lo_q, hi_q = (a[:, (i // 2) * V7X_LANES:(i // 2 + 1) * V7X_LANES] for a in yq)
            if i % 2 == 0:
                yv = jnp.where(low, lo_q, pltpu.roll(hi_q, half, 1))
            else:
                yv = jnp.where(low, pltpu.roll(lo_q, half, 1), hi_q)
            ys.append(jax.nn.gelu(yv + dsk_ref[:, cols] * xi[i][:, cols]))
        y = jnp.stack([a.reshape(n_blocks, n_seq, V7X_LANES) for a in ys], axis=1)
        y_ref[:, cols] = y.reshape(n_steps * n_seq, V7X_LANES).astype(BF16)

    need(5, 6, 7)
    for c in range(n_pieces):
        y = y_ref[c * c_rows:(c + 1) * c_rows, :]
        g = (_dot(y, w1_ref[...]) + b1_ref[...]) * jax.nn.sigmoid(_dot(y, w2_ref[...]) + b2_ref[...])
        out = _dot((g * _silu(z_c[c])).astype(BF16), wout_ref[...])
        xn = x_c[c] + _rmsnorm(out, npost_ref[...])
        if swap_bt:
            obuf_ref.at[slot][c * c_steps:(c + 1) * c_steps] = xn.reshape(c_steps, n_seq, d)
        else:
            o_ref[c * c_rows:(c + 1) * c_rows, :] = xn

    if swap_bt:
        for cp in tile_copies(step, slot, False):
            cp.start()

    @pl.when(step == n_tiles - 1)
    def _():
        str_ref[...] = hr_ref[...]
        sti_ref[...] = hi_ref[...]
        if swap_bt:
            for cp in tile_copies(step, slot, False):
                cp.wait()

    if swap_bt:
        @pl.when(jnp.logical_and(step == n_tiles - 1, n_tiles >= 2))
        def _():
            for cp in tile_copies(step - 1, 1 - slot, False):
                cp.wait()


def _s5_layer(x, h0, layer_b, layer, npre, win, core_consts, out_consts, npost, n_seq, n_steps,
              cast_jobs=()):
    swap_bt = x.ndim == 3
    d = x.shape[-1]
    exp_b = win[0].shape[2] // 2
    n_state = core_consts[0].shape[-1]
    tile = n_seq * n_steps
    n_tiles = x.size // (tile * d)
    rows = tile // BLOCK
    has_h0 = h0 is not None
    assert n_steps % BLOCK == 0 and (n_seq > V7X_SUBLANES or (n_steps // BLOCK) % 2 == 0)
    assert tile % PIECE_ROWS == 0 and PIECE_ROWS % n_seq == 0
    params_b = [win] + [(c, layer_b) for c in core_consts] + list(out_consts)
    assert len(params_b) + 2 == N_S5_PARAMS
    c_in, c_out, c_shapes = _cast_specs(cast_jobs, n_tiles)
    args = ([x] + (list(h0) if has_h0 else []) + [npre] + [a for a, _ in params_b] + [npost]
            + [w for w, _ in cast_jobs])
    if swap_bt:
        io_spec = pl.BlockSpec(memory_space=pl.ANY)
        io_scratch = [pltpu.VMEM((2, n_steps, n_seq, d), F32), pltpu.VMEM((2, n_steps, n_seq, d), F32),
                      pltpu.SemaphoreType.DMA((2, n_seq)), pltpu.SemaphoreType.DMA((2, n_seq))]
    else:
        io_spec = pl.BlockSpec((tile, d), lambda i: (i, 0))
        io_scratch = []
    lazy = n_tiles == 1
    lazy_idx = [p - 1 for p in S5_WEIGHTS]
    hbm = pl.BlockSpec(memory_space=pl.ANY)
    in_specs = ([io_spec] + [_layer_spec(a, layer_b) for a in (h0 if has_h0 else ())]
                + [_layer_spec(npre, layer)]
                + [hbm if lazy and k in lazy_idx else _layer_spec(a, l)
                   for k, (a, l) in enumerate(params_b)]
                + [_layer_spec(npost, layer)] + c_in)
    lazy_scratch = ([pltpu.VMEM(params_b[k][0].shape[1:], params_b[k][0].dtype) for k in lazy_idx]
                    + [pltpu.SemaphoreType.DMA((len(lazy_idx),))]) if lazy else []
    st_spec = pl.BlockSpec((n_seq, n_state), lambda i: (0, 0))
    st_shape = jax.ShapeDtypeStruct((n_seq, n_state), F32)
    footprint = (_layer_bytes(*[a for a, _ in params_b]) + (_layer_bytes(*h0) if has_h0 else 0)
                 + 4 * tile * d * 4 + 2 * rows * n_state * 2 + 8 * n_seq * n_state * 4
                 + tile * exp_b * (2 + 6 * 4)
                 + 6 * sum(w[0].size for w, _ in cast_jobs) // n_tiles * 2)
    res = pl.pallas_call(
        functools.partial(_s5_layer_kernel, n_seq=n_seq, n_steps=n_steps, has_h0=has_h0,
                          swap_bt=swap_bt, n_cast=len(cast_jobs),
                          lazy_layers=tuple(params_b[k][1] for k in lazy_idx) if lazy else ()),
        grid=(n_tiles,),
        in_specs=in_specs,
        out_specs=[io_spec, st_spec, st_spec] + c_out,
        out_shape=[jax.ShapeDtypeStruct(x.shape, F32), st_shape, st_shape] + c_shapes,
        scratch_shapes=[pltpu.VMEM((rows, n_state), BF16), pltpu.VMEM((rows, n_state), BF16),
                        pltpu.VMEM((n_seq, n_state), F32), pltpu.VMEM((n_seq, n_state), F32),
                        pltpu.VMEM((tile, exp_b), BF16)] + io_scratch + lazy_scratch,
        compiler_params=pltpu.CompilerParams(
            dimension_semantics=("arbitrary",),
            vmem_limit_bytes=_vmem_limit(footprint)),
        name="s5_layer_sample" if has_h0 else "s5_layer_prompt",
    )(*args)
    return res[0], res[1], res[2], res[3:]


def kernel(x_prompt, x_sample, state_ssm_re, state_ssm_im, norm_pre, norm_post,
           w_in_a, ln_v_g, ln_v_b, w_s, b_s, w_out_a,
           w_in_b, a_re, a_im, log_dt, b_re, b_im, c_re, c_im, d_skip,
           w_glu1, b_glu1, w_glu2, b_glu2, w_out_b):
    n_batch, seq_len, d = x_prompt.shape
    n_dec, dec_len, _ = x_sample.shape
    depth = norm_pre.shape[0]
    n_ssm, n_groups, state_p = a_re.shape
    n_state = n_groups * state_p
    assert seq_len % ROW_TILE == 0 and ROW_TILE % CHUNK == 0 and seq_len % SCAN_STEPS == 0
    assert n_batch == V7X_SUBLANES and n_dec % (2 * V7X_SUBLANES) == 0
    assert dec_len <= CHUNK and dec_len % BLOCK == 0
    assert state_p == STATE_P and b_re.shape[-1] == SSM_GROUP and w_s.shape[1] == N_HEADS

    rows3 = lambda a: a.reshape(a.shape[0], 1, -1)
    npre, npost = rows3(norm_pre), rows3(norm_post)
    lng, lnb = rows3(ln_v_g), rows3(ln_v_b)
    wa16 = ((w_in_a[:1].astype(BF16), 0), (w_out_a[:1].astype(BF16), 0))
    bst = jnp.swapaxes(b_s, 1, 2)
    ws_dec = w_s[:, :, :dec_len, :dec_len].reshape(-1)
    bs_dec = b_s[:, :, :dec_len].reshape(-1)

    b1, b2 = rows3(b_glu1), rows3(b_glu2)
    lanes_gp = lambda a, perm: jnp.transpose(a, perm).reshape(n_ssm, SSM_GROUP, n_state)
    l4r, l4i, wur, wui, vcw, kloc = _s5_prep(
        rows3(a_re), rows3(a_im), rows3(jnp.repeat(log_dt, state_p, axis=1)),
        lanes_gp(b_re, (0, 3, 1, 2)), lanes_gp(b_im, (0, 3, 1, 2)),
        lanes_gp(c_re, (0, 2, 1, 3)), lanes_gp(c_im, (0, 2, 1, 3)))
    core_consts = (l4r, l4i, wur, wui, vcw, kloc, rows3(d_skip))
    h0 = (state_ssm_re.reshape(n_ssm, n_dec, n_state), state_ssm_im.reshape(n_ssm, n_dec, n_state))

    xp = x_prompt.reshape(n_batch * seq_len, d)
    xs = jnp.transpose(x_sample, (1, 0, 2)).reshape(dec_len * n_dec, d)

    v_rows, st_p_re, st_p_im, st_s_re, st_s_im = [], [], [], [], []
    for i in range(depth):
        j = i // 2
        if i % 2 == 0:
            jobs = [(w, j) for w in (w_in_b, w_glu1, w_glu2, w_out_b)] if i + 1 < depth else []
            xp, wb16 = _gmlp_prompt(xp, j, i, npre, npost, lng, lnb, *wa16, w_s, bst, jobs)
            xs, v = _gmlp_sample(xs, n_dec, dec_len, j, i, npre, npost, lng, lnb, *wa16,
                                 ws_dec, bs_dec)
            v_rows.append(v)
        else:
            win = (wb16[0], 0)
            out_consts = ((wb16[1], 0), (b1, j), (wb16[2], 0), (b2, j), (wb16[3], 0))
            jobs = [(w_in_a, j + 1), (w_out_a, j + 1)] if i + 1 < depth else []
            xp3, sr, si, wa16 = _s5_layer(xp.reshape(n_batch, seq_len, d), None, j, i, npre, win,
                                          core_consts, out_consts, npost, n_batch, SCAN_STEPS, jobs)
            wa16 = tuple((w, 0) for w in wa16)
            xp = xp3.reshape(n_batch * seq_len, d)
            st_p_re.append(sr)
            st_p_im.append(si)
            xs, sr, si, _ = _s5_layer(xs, h0, j, i, npre, win, core_consts, out_consts, npost,
                                      n_dec, dec_len)
            st_s_re.append(sr)
            st_s_im.append(si)

    y_prompt = xp.reshape(n_batch, seq_len, d)
    y_sample = jnp.transpose(xs.reshape(dec_len, n_dec, d), (1, 0, 2))
    chunk_v = jnp.transpose(jnp.stack(v_rows).reshape(len(v_rows), dec_len, n_dec, -1), (0, 2, 1, 3))
    states = lambda parts, n: jnp.stack(parts).reshape(len(parts), n, n_groups, state_p)
    return (y_prompt, y_sample, chunk_v, states(st_p_re, n_batch), states(st_p_im, n_batch),
            states(st_s_re, n_dec), states(st_s_im, n_dec))
```

```python
import functools

import jax
import jax.numpy as jnp
from jax import lax
from jax.experimental import pallas as pl
from jax.experimental.pallas import tpu as pltpu

EPS = 1e-6
CHUNK = 128
N_HEADS = 8
SSM_GROUP = 16
STATE_P = 64

V7X_LANES = 128
V7X_SUBLANES = 8
V7X_MXU_DIM = 256
V7X_VMEM_BYTES = 64 * 1024 * 1024

ROW_TILE = 1024
SCAN_STEPS = 64
BLOCK = 4
PIECE_ROWS = 256

BF16 = jnp.bfloat16
F32 = jnp.float32


def _dot(a, b):
    return jnp.dot(a, b, preferred_element_type=F32)


def _rmsnorm(x, g):
    ms = jnp.mean(x * x, axis=-1, keepdims=True)
    return x * lax.rsqrt(ms + EPS) * g


def _layernorm(x, g, b):
    mu = jnp.mean(x, axis=-1, keepdims=True)
    xc = x - mu
    var = jnp.mean(xc * xc, axis=-1, keepdims=True)
    return xc * lax.rsqrt(var + EPS) * g + b


def _silu(z):
    return z * jax.nn.sigmoid(z)


def _div_pow2(x, n):
    assert n & (n - 1) == 0
    return lax.shift_right_logical(x, n.bit_length() - 1)


def _vmem_limit(nbytes):
    return int(min(V7X_VMEM_BYTES - (4 << 20), nbytes + (12 << 20)))


def _layer_spec(stacked, layer):
    tail = stacked.shape[1:]
    return pl.BlockSpec((None,) + tail, lambda *_: (layer,) + (0,) * len(tail),
                        pipeline_mode=pl.Buffered(1))


def _layer_bytes(*stacked):
    return sum(a[0].size * a.dtype.itemsize for a in stacked)


def _cast_specs(jobs, n_steps):
    in_specs, out_specs, out_shapes = [], [], []
    for w, layer in jobs:
        _, rows, cols = w.shape
        slab = rows // n_steps
        assert rows % n_steps == 0 and slab % (2 * V7X_SUBLANES) == 0
        in_specs.append(pl.BlockSpec((None, slab, cols), lambda i, layer=layer: (layer, i, 0)))
        out_specs.append(pl.BlockSpec((None, slab, cols), lambda i: (0, i, 0)))
        out_shapes.append(jax.ShapeDtypeStruct((1, rows, cols), BF16))
    return in_specs, out_specs, out_shapes


def _cast_slabs(in_refs, out_refs):
    for src, dst in zip(in_refs, out_refs):
        dst[...] = src[...].astype(BF16)


def _row_spec(rows, width):
    return pl.BlockSpec((rows, width), lambda i: (i, 0))


_SMEM = pl.BlockSpec(memory_space=pltpu.SMEM)


def _gmlp_front(x, npre_ref, lng_ref, lnb_ref, win_ref, exp_a):
    hn, v = [], []
    for r0 in range(0, x.shape[0], PIECE_ROWS):
        hn.append(_rmsnorm(x[r0:r0 + PIECE_ROWS], npre_ref[...]).astype(BF16))
        v.append(_layernorm(_dot(hn[-1], win_ref[:, exp_a:2 * exp_a]), lng_ref[...], lnb_ref[...]))
    return jnp.concatenate(hn, axis=0), jnp.concatenate(v, axis=0)


def _gmlp_back(x, gated_ref, wout_ref, npost_ref, o_ref):
    for r0 in range(0, x.shape[0], PIECE_ROWS):
        r = slice(r0, r0 + PIECE_ROWS)
        out = _dot(gated_ref[r, :], wout_ref[...])
        o_ref[r, :] = x[r] + _rmsnorm(out, npost_ref[...])


def _gmlp_prompt_kernel(x_ref, npre_ref, npost_ref, lng_ref, lnb_ref, win_ref, wout_ref,
                        ws_ref, bst_ref, *refs, n_cast):
    o_ref, gated_ref = refs[n_cast], refs[-1]
    _cast_slabs(refs[:n_cast], refs[n_cast + 1:-1])
    rows = x_ref.shape[0]
    exp_a = wout_ref.shape[0]
    hd = exp_a // N_HEADS
    x = x_ref[...]
    hn, v = _gmlp_front(x, npre_ref, lng_ref, lnb_ref, win_ref, exp_a)
    vb = v.astype(BF16)
    causal = (lax.broadcasted_iota(jnp.int32, (CHUNK, CHUNK), 1)
              <= lax.broadcasted_iota(jnp.int32, (CHUNK, CHUNK), 0))
    for h in range(N_HEADS):
        lo = h * hd
        wsh = jnp.where(causal, ws_ref[h], 0.0).astype(BF16)
        bias = bst_ref[:, h:h + 1]
        u = _dot(hn, win_ref[:, lo:lo + hd])
        z = _dot(hn, win_ref[:, 2 * exp_a + lo:2 * exp_a + lo + hd])
        s = jnp.concatenate(
            [_dot(wsh, vb[c * CHUNK:(c + 1) * CHUNK, lo:lo + hd]) + bias
             for c in range(rows // CHUNK)], axis=0)
        gated_ref[:, lo:lo + hd] = (u * s * _silu(z)).astype(BF16)
    _gmlp_back(x, gated_ref, wout_ref, npost_ref, o_ref)


def _gmlp_sample_kernel(ws_ref, bs_ref, x_ref, npre_ref, npost_ref, lng_ref, lnb_ref, win_hbm,
                        wout_hbm, o_ref, v_hbm, gated_ref, win_ref, wout_ref, w_sem, v_ref, v_sem,
                        *, n_seq, seq_len, layer, w_layers):
    exp_a = wout_ref.shape[0]
    hd = exp_a // N_HEADS
    col_parts = [slice(exp_a, 2 * exp_a), slice(0, exp_a), slice(2 * exp_a, 3 * exp_a)]
    copies = [pltpu.make_async_copy(win_hbm.at[w_layers[0], :, p], win_ref.at[:, p], w_sem.at[k])
              for k, p in enumerate(col_parts)]
    copies.append(pltpu.make_async_copy(wout_hbm.at[w_layers[1]], wout_ref, w_sem.at[len(col_parts)]))
    for cp in copies:
        cp.start()
    x = x_ref[...]
    copies[0].wait()
    hn, v = _gmlp_front(x, npre_ref, lng_ref, lnb_ref, win_ref, exp_a)
    v_ref[...] = v
    v_copies = [pltpu.make_async_copy(v_ref.at[pl.ds(t * n_seq, n_seq), :], v_hbm.at[:, t, :],
                                      v_sem.at[t]) for t in range(seq_len)]
    for cp in v_copies:
        cp.start()
    copies[1].wait()
    copies[2].wait()
    for h in range(N_HEADS):
        lo = h * hd
        u = _dot(hn, win_ref[:, lo:lo + hd])
        z = _dot(hn, win_ref[:, 2 * exp_a + lo:2 * exp_a + lo + hd])
        vt = [v[t * n_seq:(t + 1) * n_seq, lo:lo + hd] for t in range(seq_len)]
        parts = []
        for t in range(seq_len):
            b_idx = (layer * N_HEADS + h) * seq_len + t
            s = ws_ref[b_idx * seq_len] * vt[0]
            for t2 in range(1, t + 1):
                s = s + ws_ref[b_idx * seq_len + t2] * vt[t2]
            parts.append(s + bs_ref[b_idx])
        s = jnp.concatenate(parts, axis=0)
        gated_ref[:, lo:lo + hd] = (u * s * _silu(z)).astype(BF16)
    copies[3].wait()
    _gmlp_back(x, gated_ref, wout_ref, npost_ref, o_ref)
    for cp in v_copies:
        cp.wait()


def _gmlp_prompt(x2d, layer_a, layer, npre, npost, lng, lnb, win, wout, ws, bst, cast_jobs):
    rows, d = x2d.shape
    (win, win_l), (wout, wout_l) = win, wout
    exp_a = wout.shape[1]
    n_steps = rows // ROW_TILE
    c_in, c_out, c_shapes = _cast_specs(cast_jobs, n_steps)
    footprint = (_layer_bytes(win, wout, ws) + 4 * ROW_TILE * d * 4
                 + ROW_TILE * exp_a * (2 + 4 + 2 + 8)
                 + 6 * sum(w[0].size for w, _ in cast_jobs) // n_steps * 2)
    res = pl.pallas_call(
        functools.partial(_gmlp_prompt_kernel, n_cast=len(cast_jobs)),
        grid=(n_steps,),
        in_specs=[
            _row_spec(ROW_TILE, d),
            _layer_spec(npre, layer), _layer_spec(npost, layer),
            _layer_spec(lng, layer_a), _layer_spec(lnb, layer_a),
            _layer_spec(win, win_l), _layer_spec(wout, wout_l),
            _layer_spec(ws, layer_a), _layer_spec(bst, layer_a),
        ] + c_in,
        out_specs=[_row_spec(ROW_TILE, d)] + c_out,
        out_shape=[jax.ShapeDtypeStruct((rows, d), F32)] + c_shapes,
        scratch_shapes=[pltpu.VMEM((ROW_TILE, exp_a), BF16)],
        compiler_params=pltpu.CompilerParams(
            dimension_semantics=("parallel",),
            vmem_limit_bytes=_vmem_limit(footprint)),
        name="gmlp_prompt",
    )(x2d, npre, npost, lng, lnb, win, wout, ws, bst, *[w for w, _ in cast_jobs])
    return res[0], res[1:]


def _gmlp_sample(xs, n_seq, seq_len, layer_a, layer, npre, npost, lng, lnb, win, wout,
                 ws_flat, bs_flat):
    rows, d = xs.shape
    (win, win_l), (wout, wout_l) = win, wout
    exp_a = wout.shape[1]
    footprint = _layer_bytes(win, wout) + 4 * rows * d * 4 + rows * exp_a * (2 + 8 + 4 + 8)
    hbm = pl.BlockSpec(memory_space=pl.ANY)
    return pl.pallas_call(
        functools.partial(_gmlp_sample_kernel, n_seq=n_seq, seq_len=seq_len, layer=layer_a,
                          w_layers=(win_l, wout_l)),
        grid=(1,),
        in_specs=[_SMEM, _SMEM, _row_spec(rows, d),
                  _layer_spec(npre, layer), _layer_spec(npost, layer),
                  _layer_spec(lng, layer_a), _layer_spec(lnb, layer_a), hbm, hbm],
        out_specs=[_row_spec(rows, d), hbm],
        out_shape=[jax.ShapeDtypeStruct((rows, d), F32),
                   jax.ShapeDtypeStruct((n_seq, seq_len, exp_a), F32)],
        scratch_shapes=[pltpu.VMEM((rows, exp_a), BF16), pltpu.VMEM(win.shape[1:], BF16),
                        pltpu.VMEM(wout.shape[1:], BF16), pltpu.SemaphoreType.DMA((4,)),
                        pltpu.VMEM((rows, exp_a), F32), pltpu.SemaphoreType.DMA((seq_len,))],
        compiler_params=pltpu.CompilerParams(
            dimension_semantics=("arbitrary",),
            vmem_limit_bytes=_vmem_limit(footprint)),
        name="gmlp_sample",
    )(ws_flat, bs_flat, xs, npre, npost, lng, lnb, win, wout)


def _s5_prep_kernel(are_ref, aim_ref, ldt_ref, btr_ref, bti_ref, ctr_ref, cti_ref,
                    l4r_ref, l4i_ref, wur_ref, wui_ref, vc_ref, kloc_ref):
    dt = jnp.exp(ldt_ref[...])
    ar = are_ref[...]
    ai = aim_ref[...]
    mag = jnp.exp(dt * ar)
    ang = dt * ai
    abr = mag * jnp.cos(ang)
    abi = mag * jnp.sin(ang)
    nr = abr - 1.0
    ni = abi
    den = ar * ar + ai * ai
    cre = (nr * ar + ni * ai) / den
    cim = (ni * ar - nr * ai) / den
    btr = btr_ref[...]
    bti = bti_ref[...]
    bbr = cre * btr - cim * bti
    bbi = cre * bti + cim * btr

    def cmul(xr, xi, yr, yi):
        return xr * yr - xi * yi, xr * yi + xi * yr

    lam = [(jnp.ones_like(abr), jnp.zeros_like(abi)), (abr, abi)]
    for _ in range(BLOCK - 1):
        lam.append(cmul(*lam[-1], abr, abi))
    l4r_ref[...] = lam[BLOCK][0]
    l4i_ref[...] = lam[BLOCK][1]

    n_q, k_rows, n_cols = wur_ref.shape
    q_groups = n_cols // STATE_P
    q_rows = q_groups * SSM_GROUP
    same_q = (_div_pow2(lax.broadcasted_iota(jnp.int32, (q_rows, n_cols), 0), SSM_GROUP)
              == _div_pow2(lax.broadcasted_iota(jnp.int32, (q_rows, n_cols), 1), STATE_P))
    for ip in range(BLOCK):
        ur, ui = cmul(bbr, bbi, *lam[BLOCK - 1 - ip])
        for q in range(n_q):
            for src, dst in ((ur, wur_ref), (ui, wui_ref)):
                blk = jnp.concatenate([src[:, q * n_cols:(q + 1) * n_cols]] * q_groups, axis=0)
                dst[q, ip * q_rows:(ip + 1) * q_rows, :] = jnp.where(same_q, blk, 0.0).astype(BF16)

    ctr = ctr_ref[...]
    cti = cti_ref[...]
    cl = [cmul(ctr, cti, *lam[t]) for t in range(BLOCK + 1)]

    def tile_rows(a, q):
        blk = jnp.concatenate([a[:, q * n_cols:(q + 1) * n_cols]] * q_groups, axis=0)
        return jnp.where(same_q, blk, 0.0)

    half = V7X_LANES // 2
    low = lax.broadcasted_iota(jnp.int32, (q_rows, V7X_LANES), 1) < half
    for q in range(n_q):
        clm = [(tile_rows(c[0], q), tile_rows(c[1], q)) for c in cl]
        rows = [jnp.concatenate([clm[i + 1][0], -clm[i + 1][1]], axis=1) for i in range(BLOCK)]
        vc_ref[q] = jnp.concatenate(rows, axis=0).T.astype(BF16)
        bcat = jnp.concatenate([tile_rows(bbr, q), -tile_rows(bbi, q)], axis=1).astype(BF16)
        ccat = jnp.concatenate([jnp.concatenate(clm[t], axis=1) for t in range(BLOCK)], axis=0)
        kall = lax.dot_general(bcat, ccat.astype(BF16), (((1,), (1,)), ((), ())),
                               preferred_element_type=F32)
        k0, k1 = kall[:, :V7X_LANES], kall[:, V7X_LANES:]
        r0, r1 = pltpu.roll(k0, half, 1), pltpu.roll(k1, half, 1)
        zero = jnp.zeros_like(k0)
        shifted = [(k0, k1),
                   (jnp.where(low, zero, r0), jnp.where(low, r0, r1)),
                   (zero, k0),
                   (zero, jnp.where(low, zero, r0))]
        kloc_ref[q] = jnp.concatenate([jnp.concatenate(sh, axis=1) for sh in shifted],
                                      axis=0).astype(BF16)


def _s5_prep(are, aim, ldt, btr, bti, ctr, cti):
    n_layers, _, n_state = are.shape
    n_q = n_state // V7X_MXU_DIM

    def per_layer(shape):
        return pl.BlockSpec((None,) + shape, lambda l: (l,) + (0,) * len(shape))

    out_tails = [(1, n_state), (1, n_state),
                 (n_q, V7X_MXU_DIM, V7X_MXU_DIM), (n_q, V7X_MXU_DIM, V7X_MXU_DIM),
                 (n_q, 2 * V7X_MXU_DIM, V7X_MXU_DIM), (n_q, V7X_MXU_DIM, V7X_MXU_DIM)]
    out_dtypes = [F32, F32, BF16, BF16, BF16, BF16]
    args = (are, aim, ldt, btr, bti, ctr, cti)
    return pl.pallas_call(
        _s5_prep_kernel,
        grid=(n_layers,),
        in_specs=[per_layer(a.shape[1:]) for a in args],
        out_specs=[per_layer(t) for t in out_tails],
        out_shape=[jax.ShapeDtypeStruct((n_layers,) + t, dt) for t, dt in zip(out_tails, out_dtypes)],
        compiler_params=pltpu.CompilerParams(
            dimension_semantics=("parallel",),
            vmem_limit_bytes=_vmem_limit(2 * sum(
                int(jnp.dtype(dt).itemsize) * functools.reduce(lambda a, b: a * b, t)
                for t, dt in zip(out_tails, out_dtypes)))),
        name="s5_prep",
    )(*args)


N_S5_PARAMS = 15
S5_WEIGHTS = (1, 4, 5, 6, 7, 9, 11, 13)


def _s5_layer_kernel(*refs, n_seq, n_steps, has_h0, swap_bt, n_cast, lazy_layers):
    if has_h0:
        x_ref, h0r_ref, h0i_ref = refs[:3]
        refs = refs[3:]
    else:
        x_ref = refs[0]
        refs = refs[1:]
    params = list(refs[:N_S5_PARAMS])
    refs = refs[N_S5_PARAMS:]
    o_ref, str_ref, sti_ref = refs[n_cast:n_cast + 3]
    n_lazy = len(S5_WEIGHTS) + 1 if lazy_layers else 0
    n_scratch = 5 + (4 if swap_bt else 0) + n_lazy
    scratch = refs[-n_scratch:]
    hpr_ref, hpi_ref, hr_ref, hi_ref, y_ref = scratch[:5]
    _cast_slabs(refs[:n_cast], refs[n_cast + 3:-n_scratch])
    step = pl.program_id(0)
    n_tiles = pl.num_programs(0)

    w_copies = []
    if lazy_layers:
        w_sem = scratch[-1]
        for k, (p, layer) in enumerate(zip(S5_WEIGHTS, lazy_layers)):
            w_copies.append(pltpu.make_async_copy(params[p].at[layer], scratch[-n_lazy + k],
                                                  w_sem.at[k]))
            params[p] = scratch[-n_lazy + k]
        for cp in w_copies:
            cp.start()

    def need(*which):
        for k in which:
            if w_copies:
                w_copies[k].wait()

    (npre_ref, win_ref, l4r_ref, l4i_ref, wur_ref, wui_ref, vc_ref, kloc_ref, dsk_ref,
     w1_ref, b1_ref, w2_ref, b2_ref, wout_ref, npost_ref) = params

    if swap_bt:
        xbuf_ref, obuf_ref, in_sem, out_sem = scratch[5:9]
        slot = step % 2

        def tile_copies(tile, sl, fetch):
            seq_rows = pl.ds(tile * n_steps, n_steps)
            if fetch:
                return [pltpu.make_async_copy(x_ref.at[b, seq_rows, :], xbuf_ref.at[sl, :, b, :],
                                              in_sem.at[sl, b]) for b in range(n_seq)]
            return [pltpu.make_async_copy(obuf_ref.at[sl, :, b, :], o_ref.at[b, seq_rows, :],
                                          out_sem.at[sl, b]) for b in range(n_seq)]

        @pl.when(step == 0)
        def _():
            for cp in tile_copies(0, 0, True):
                cp.start()

        @pl.when(step + 1 < n_tiles)
        def _():
            for cp in tile_copies(step + 1, 1 - slot, True):
                cp.start()

        for cp in tile_copies(step, slot, True):
            cp.wait()

        @pl.when(step >= 2)
        def _():
            for cp in tile_copies(step - 2, slot, False):
                cp.wait()

    @pl.when(step == 0)
    def _():
        if has_h0:
            hr_ref[...] = h0r_ref[...]
            hi_ref[...] = h0i_ref[...]
        else:
            hr_ref[...] = jnp.zeros_like(hr_ref)
            hi_ref[...] = jnp.zeros_like(hi_ref)

    d = npre_ref.shape[-1]
    width = win_ref.shape[1] // 2
    c_rows = PIECE_ROWS
    c_steps = c_rows // n_seq
    n_pieces = n_steps // c_steps
    x_c, xb_c, z_c = [], [], []
    need(0)
    for c in range(n_pieces):
        if swap_bt:
            xc = xbuf_ref.at[slot][c * c_steps:(c + 1) * c_steps].reshape(c_rows, d)
        else:
            xc = x_ref[c * c_rows:(c + 1) * c_rows, :]
        hn = _rmsnorm(xc, npre_ref[...]).astype(BF16)
        x_c.append(xc)
        xb_c.append(_dot(hn, win_ref[:, :width]))
        z_c.append(_dot(hn, win_ref[:, width:]))
    xb = jnp.concatenate(xb_c, axis=0)

    n_blocks = n_steps // BLOCK
    rows = n_blocks * n_seq
    x4 = xb.reshape(n_blocks, BLOCK, n_seq, width)
    xi = [x4[:, i].reshape(rows, width) for i in range(BLOCK)]

    sub = V7X_SUBLANES
    pair = 2 * sub
    n_q, _, n_cols = wur_ref.shape
    n_v = width // V7X_LANES
    half = V7X_LANES // 2
    low = lax.broadcasted_iota(jnp.int32, (rows, V7X_LANES), 1) < half

    lhs_q = []
    need(1, 2)
    for v in range(n_v):
        cols = slice(v * V7X_LANES, (v + 1) * V7X_LANES)
        p = [a[:, cols] for a in xi]
        pr = [pltpu.roll(a, half, 1) for a in p]
        lhs_lo = jnp.concatenate([jnp.where(low, p[i], pr[i + 1]) for i in range(0, BLOCK, 2)], axis=1)
        lhs_hi = jnp.concatenate([jnp.where(low, pr[i], p[i + 1]) for i in range(0, BLOCK, 2)], axis=1)
        for q, lhs in ((2 * v, lhs_lo.astype(BF16)), (2 * v + 1, lhs_hi.astype(BF16))):
            lanes = pl.ds(q * n_cols, n_cols)
            wr = _dot(lhs, wur_ref[q])
            wi = _dot(lhs, wui_ref[q])
            if n_seq == sub:
                ar = jnp.broadcast_to(l4r_ref[:, lanes], (sub, n_cols))
                ai = jnp.broadcast_to(l4i_ref[:, lanes], (sub, n_cols))
                hr, hi = hr_ref[:, lanes], hi_ref[:, lanes]
                for k in range(n_blocks // 2):
                    prv_r, prv_i = [], []
                    for r0 in (2 * k * sub, (2 * k + 1) * sub):
                        prv_r.append(hr)
                        prv_i.append(hi)
                        hr, hi = (ar * hr - ai * hi + wr[r0:r0 + sub],
                                  ar * hi + ai * hr + wi[r0:r0 + sub])
                    hpr_ref[k * pair:(k + 1) * pair, lanes] = jnp.concatenate(prv_r, axis=0).astype(BF16)
                    hpi_ref[k * pair:(k + 1) * pair, lanes] = jnp.concatenate(prv_i, axis=0).astype(BF16)
                hr_ref[:, lanes] = hr
                hi_ref[:, lanes] = hi
            else:
                ar = jnp.broadcast_to(l4r_ref[:, lanes], (pair, n_cols))
                ai = jnp.broadcast_to(l4i_ref[:, lanes], (pair, n_cols))
                for m in range(n_seq // pair):
                    srows = pl.ds(m * pair, pair)
                    hr, hi = hr_ref[srows, lanes], hi_ref[srows, lanes]
                    for blk in range(n_blocks):
                        r0 = blk * n_seq + m * pair
                        hpr_ref[r0:r0 + pair, lanes] = hr.astype(BF16)
                        hpi_ref[r0:r0 + pair, lanes] = hi.astype(BF16)
                        hr, hi = (ar * hr - ai * hi + wr[r0:r0 + pair],
                                  ar * hi + ai * hr + wi[r0:r0 + pair])
                    hr_ref[srows, lanes] = hr
                    hi_ref[srows, lanes] = hi
            lhs_q.append(lhs)

    need(3, 4)
    for v in range(n_v):
        cols = slice(v * V7X_LANES, (v + 1) * V7X_LANES)
        yq = []
        for q in (2 * v, 2 * v + 1):
            lanes = pl.ds(q * n_cols, n_cols)
            st = jnp.concatenate([hpr_ref[:, lanes], hpi_ref[:, lanes]], axis=1)
            yq.append(_dot(st, vc_ref[q]) + _dot(lhs_q[q], kloc_ref[q]))
        ys = []
        for i in range(BLOCK):
            lo_q, hi_q = (a[:, (i // 2) * V7X_LANES:(i // 2 + 1) * V7X_LANES] for a in yq)
            if i % 2 == 0:
                yv = jnp.where(low, lo_q, pltpu.roll(hi_q, half, 1))
            else:
                yv = jnp.where(low, pltpu.roll(lo_q, half, 1), hi_q)
            ys.append(jax.nn.gelu(yv + dsk_ref[:, cols] * xi[i][:, cols]))
        y = jnp.stack([a.reshape(n_blocks, n_seq, V7X_LANES) for a in ys], axis=1)
        y_ref[:, cols] = y.reshape(n_steps * n_seq, V7X_LANES).astype(BF16)

    need(5, 6, 7)
    for c in range(n_pieces):
        y = y_ref[c * c_rows:(c + 1) * c_rows, :]
        g = (_dot(y, w1_ref[...]) + b1_ref[...]) * jax.nn.sigmoid(_dot(y, w2_ref[...]) + b2_ref[...])
        out = _dot((g * _silu(z_c[c])).astype(BF16), wout_ref[...])
        xn = x_c[c] + _rmsnorm(out, npost_ref[...])
        if swap_bt:
            obuf_ref.at[slot][c * c_steps:(c + 1) * c_steps] = xn.reshape(c_steps, n_seq, d)
        else:
            o_ref[c * c_rows:(c + 1) * c_rows, :] = xn

    if swap_bt:
        for cp in tile_copies(step, slot, False):
            cp.start()

    @pl.when(step == n_tiles - 1)
    def _():
        str_ref[...] = hr_ref[...]
        sti_ref[...] = hi_ref[...]
        if swap_bt:
            for cp in tile_copies(step, slot, False):
                cp.wait()

    if swap_bt:
        @pl.when(jnp.logical_and(step == n_tiles - 1, n_tiles >= 2))
        def _():
            for cp in tile_copies(step - 1, 1 - slot, False):
                cp.wait()


def _s5_layer(x, h0, layer_b, layer, npre, win, core_consts, out_consts, npost, n_seq, n_steps,
              cast_jobs=()):
    swap_bt = x.ndim == 3
    d = x.shape[-1]
    exp_b = win[0].shape[2] // 2
    n_state = core_consts[0].shape[-1]
    tile = n_seq * n_steps
    n_tiles = x.size // (tile * d)
    rows = tile // BLOCK
    has_h0 = h0 is not None
    assert n_steps % BLOCK == 0 and (n_seq > V7X_SUBLANES or (n_steps // BLOCK) % 2 == 0)
    assert tile % PIECE_ROWS == 0 and PIECE_ROWS % n_seq == 0
    params_b = [win] + [(c, layer_b) for c in core_consts] + list(out_consts)
    assert len(params_b) + 2 == N_S5_PARAMS
    c_in, c_out, c_shapes = _cast_specs(cast_jobs, n_tiles)
    args = ([x] + (list(h0) if has_h0 else []) + [npre] + [a for a, _ in params_b] + [npost]
            + [w for w, _ in cast_jobs])
    if swap_bt:
        io_spec = pl.BlockSpec(memory_space=pl.ANY)
        io_scratch = [pltpu.VMEM((2, n_steps, n_seq, d), F32), pltpu.VMEM((2, n_steps, n_seq, d), F32),
                      pltpu.SemaphoreType.DMA((2, n_seq)), pltpu.SemaphoreType.DMA((2, n_seq))]
    else:
        io_spec = pl.BlockSpec((tile, d), lambda i: (i, 0))
        io_scratch = []
    lazy = n_tiles == 1
    lazy_idx = [p - 1 for p in S5_WEIGHTS]
    hbm = pl.BlockSpec(memory_space=pl.ANY)
    in_specs = ([io_spec] + [_layer_spec(a, layer_b) for a in (h0 if has_h0 else ())]
                + [_layer_spec(npre, layer)]
                + [hbm if lazy and k in lazy_idx else _layer_spec(a, l)
                   for k, (a, l) in enumerate(params_b)]
                + [_layer_spec(npost, layer)] + c_in)
    lazy_scratch = ([pltpu.VMEM(params_b[k][0].shape[1:], params_b[k][0].dtype) for k in lazy_idx]
                    + [pltpu.SemaphoreType.DMA((len(lazy_idx),))]) if lazy else []
    st_spec = pl.BlockSpec((n_seq, n_state), lambda i: (0, 0))
    st_shape = jax.ShapeDtypeStruct((n_seq, n_state), F32)
    footprint = (_layer_bytes(*[a for a, _ in params_b]) + (_layer_bytes(*h0) if has_h0 else 0)
                 + 4 * tile * d * 4 + 2 * rows * n_state * 2 + 8 * n_seq * n_state * 4
                 + tile * exp_b * (2 + 6 * 4)
                 + 6 * sum(w[0].size for w, _ in cast_jobs) // n_tiles * 2)
    res = pl.pallas_call(
        functools.partial(_s5_layer_kernel, n_seq=n_seq, n_steps=n_steps, has_h0=has_h0,
                          swap_bt=swap_bt, n_cast=len(cast_jobs),
                          lazy_layers=tuple(params_b[k][1] for k in lazy_idx) if lazy else ()),
        grid=(n_tiles,),
        in_specs=in_specs,
        out_specs=[io_spec, st_spec, st_spec] + c_out,
        out_shape=[jax.ShapeDtypeStruct(x.shape, F32), st_shape, st_shape] + c_shapes,
        scratch_shapes=[pltpu.VMEM((rows, n_state), BF16), pltpu.VMEM((rows, n_state), BF16),
                        pltpu.VMEM((n_seq, n_state), F32), pltpu.VMEM((n_seq, n_state), F32),
                        pltpu.VMEM((tile, exp_b), BF16)] + io_scratch + lazy_scratch,
        compiler_params=pltpu.CompilerParams(
            dimension_semantics=("arbitrary",),
            vmem_limit_bytes=_vmem_limit(footprint)),
        name="s5_layer_sample" if has_h0 else "s5_layer_prompt",
    )(*args)
    return res[0], res[1], res[2], res[3:]


def kernel(x_prompt, x_sample, state_ssm_re, state_ssm_im, norm_pre, norm_post,
           w_in_a, ln_v_g, ln_v_b, w_s, b_s, w_out_a,
           w_in_b, a_re, a_im, log_dt, b_re, b_im, c_re, c_im, d_skip,
           w_glu1, b_glu1, w_glu2, b_glu2, w_out_b):
    n_batch, seq_len, d = x_prompt.shape
    n_dec, dec_len, _ = x_sample.shape
    depth = norm_pre.shape[0]
    n_ssm, n_groups, state_p = a_re.shape
    n_state = n_groups * state_p
    assert seq_len % ROW_TILE == 0 and ROW_TILE % CHUNK == 0 and seq_len % SCAN_STEPS == 0
    assert n_batch == V7X_SUBLANES and n_dec % (2 * V7X_SUBLANES) == 0
    assert dec_len <= CHUNK and dec_len % BLOCK == 0
    assert state_p == STATE_P and b_re.shape[-1] == SSM_GROUP and w_s.shape[1] == N_HEADS

    rows3 = lambda a: a.reshape(a.shape[0], 1, -1)
    npre, npost = rows3(norm_pre), rows3(norm_post)
    lng, lnb = rows3(ln_v_g), rows3(ln_v_b)
    wa16 = ((w_in_a[:1].astype(BF16), 0), (w_out_a[:1].astype(BF16), 0))
    bst = jnp.swapaxes(b_s, 1, 2)
    ws_dec = w_s[:, :, :dec_len, :dec_len].reshape(-1)
    bs_dec = b_s[:, :, :dec_len].reshape(-1)

    b1, b2 = rows3(b_glu1), rows3(b_glu2)
    lanes_gp = lambda a, perm: jnp.transpose(a, perm).reshape(n_ssm, SSM_GROUP, n_state)
    l4r, l4i, wur, wui, vcw, kloc = _s5_prep(
        rows3(a_re), rows3(a_im), rows3(jnp.repeat(log_dt, state_p, axis=1)),
        lanes_gp(b_re, (0, 3, 1, 2)), lanes_gp(b_im, (0, 3, 1, 2)),
        lanes_gp(c_re, (0, 2, 1, 3)), lanes_gp(c_im, (0, 2, 1, 3)))
    core_consts = (l4r, l4i, wur, wui, vcw, kloc, rows3(d_skip))
    h0 = (state_ssm_re.reshape(n_ssm, n_dec, n_state), state_ssm_im.reshape(n_ssm, n_dec, n_state))

    xp = x_prompt.reshape(n_batch * seq_len, d)
    xs = jnp.transpose(x_sample, (1, 0, 2)).reshape(dec_len * n_dec, d)

    v_rows, st_p_re, st_p_im, st_s_re, st_s_im = [], [], [], [], []
    for i in range(depth):
        j = i // 2
        if i % 2 == 0:
            jobs = [(w, j) for w in (w_in_b, w_glu1, w_glu2, w_out_b)] if i + 1 < depth else []
            xp, wb16 = _gmlp_prompt(xp, j, i, npre, npost, lng, lnb, *wa16, w_s, bst, jobs)
            xs, v = _gmlp_sample(xs, n_dec, dec_len, j, i, npre, npost, lng, lnb, *wa16,
                                 ws_dec, bs_dec)
            v_rows.append(v)
        else:
            win = (wb16[0], 0)
            out_consts = ((wb16[1], 0), (b1, j), (wb16[2], 0), (b2, j), (wb16[3], 0))
            jobs = [(w_in_a, j + 1), (w_out_a, j + 1)] if i + 1 < depth else []
            xp3, sr, si, wa16 = _s5_layer(xp.reshape(n_batch, seq_len, d), None, j, i, npre, win,
                                          core_consts, out_consts, npost, n_batch, SCAN_STEPS, jobs)
            wa16 = tuple((w, 0) for w in wa16)
            xp = xp3.reshape(n_batch * seq_len, d)
            st_p_re.append(sr)
            st_p_im.append(si)
            xs, sr, si, _ = _s5_layer(xs, h0, j, i, npre, win, core_consts, out_consts, npost,
                                      n_dec, dec_len)
            st_s_re.append(sr)
            st_s_im.append(si)

    y_prompt = xp.reshape(n_batch, seq_len, d)
    y_sample = jnp.transpose(xs.reshape(dec_len, n_dec, d), (1, 0, 2))
    chunk_v = jnp.stack(v_rows)
    states = lambda parts, n: jnp.stack(parts).reshape(len(parts), n, n_groups, state_p)
    return (y_prompt, y_sample, chunk_v, states(st_p_re, n_batch), states(st_p_im, n_batch),
            states(st_s_re, n_dec), states(st_s_im, n_dec))
```

```python
import functools

import jax
import jax.numpy as jnp
from jax import lax
from jax.experimental import pallas as pl
from jax.experimental.pallas import tpu as pltpu

EPS = 1e-6
CHUNK = 128
N_HEADS = 8
SSM_GROUP = 16
STATE_P = 64

V7X_LANES = 128
V7X_SUBLANES = 8
V7X_MXU_DIM = 256
V7X_VMEM_BYTES = 64 * 1024 * 1024

ROW_TILE = 1024
SCAN_STEPS = 64
BLOCK = 4
PIECE_ROWS = 256

BF16 = jnp.bfloat16
F32 = jnp.float32


def _dot(a, b):
    return jnp.dot(a, b, preferred_element_type=F32)


def _rmsnorm(x, g):
    ms = jnp.mean(x * x, axis=-1, keepdims=True)
    return x * lax.rsqrt(ms + EPS) * g


def _layernorm(x, g, b):
    mu = jnp.mean(x, axis=-1, keepdims=True)
    xc = x - mu
    var = jnp.mean(xc * xc, axis=-1, keepdims=True)
    return xc * lax.rsqrt(var + EPS) * g + b


def _silu(z):
    return z * jax.nn.sigmoid(z)


def _div_pow2(x, n):
    assert n & (n - 1) == 0
    return lax.shift_right_logical(x, n.bit_length() - 1)


def _vmem_limit(nbytes):
    return int(min(V7X_VMEM_BYTES - (4 << 20), nbytes + (12 << 20)))


def _layer_spec(stacked, layer):
    tail = stacked.shape[1:]
    return pl.BlockSpec((None,) + tail, lambda *_: (layer,) + (0,) * len(tail),
                        pipeline_mode=pl.Buffered(1))


def _layer_bytes(*stacked):
    return sum(a[0].size * a.dtype.itemsize for a in stacked)


def _cast_specs(jobs, n_steps):
    in_specs, out_specs, out_shapes = [], [], []
    for w, layer in jobs:
        _, rows, cols = w.shape
        slab = rows // n_steps
        assert rows % n_steps == 0 and slab % (2 * V7X_SUBLANES) == 0
        in_specs.append(pl.BlockSpec((None, slab, cols), lambda i, layer=layer: (layer, i, 0)))
        out_specs.append(pl.BlockSpec((None, slab, cols), lambda i: (0, i, 0)))
        out_shapes.append(jax.ShapeDtypeStruct((1, rows, cols), BF16))
    return in_specs, out_specs, out_shapes


def _cast_slabs(in_refs, out_refs):
    for src, dst in zip(in_refs, out_refs):
        dst[...] = src[...].astype(BF16)


def _row_spec(rows, width):
    return pl.BlockSpec((rows, width), lambda i: (i, 0))


_SMEM = pl.BlockSpec(memory_space=pltpu.SMEM)


def _gmlp_front(x, npre_ref, lng_ref, lnb_ref, win_ref, exp_a):
    hn, v = [], []
    for r0 in range(0, x.shape[0], PIECE_ROWS):
        hn.append(_rmsnorm(x[r0:r0 + PIECE_ROWS], npre_ref[...]).astype(BF16))
        v.append(_layernorm(_dot(hn[-1], win_ref[:, exp_a:2 * exp_a]), lng_ref[...], lnb_ref[...]))
    return jnp.concatenate(hn, axis=0), jnp.concatenate(v, axis=0)


def _gmlp_back(x, gated_ref, wout_ref, npost_ref, o_ref):
    for r0 in range(0, x.shape[0], PIECE_ROWS):
        r = slice(r0, r0 + PIECE_ROWS)
        out = _dot(gated_ref[r, :], wout_ref[...])
        o_ref[r, :] = x[r] + _rmsnorm(out, npost_ref[...])


def _gmlp_prompt_kernel(x_ref, npre_ref, npost_ref, lng_ref, lnb_ref, win_ref, wout_ref,
                        ws_ref, bst_ref, *refs, n_cast):
    o_ref, gated_ref = refs[n_cast], refs[-1]
    _cast_slabs(refs[:n_cast], refs[n_cast + 1:-1])
    rows = x_ref.shape[0]
    exp_a = wout_ref.shape[0]
    hd = exp_a // N_HEADS
    x = x_ref[...]
    hn, v = _gmlp_front(x, npre_ref, lng_ref, lnb_ref, win_ref, exp_a)
    vb = v.astype(BF16)
    causal = (lax.broadcasted_iota(jnp.int32, (CHUNK, CHUNK), 1)
              <= lax.broadcasted_iota(jnp.int32, (CHUNK, CHUNK), 0))
    for h in range(N_HEADS):
        lo = h * hd
        wsh = jnp.where(causal, ws_ref[h], 0.0).astype(BF16)
        bias = bst_ref[:, h:h + 1]
        u = _dot(hn, win_ref[:, lo:lo + hd])
        z = _dot(hn, win_ref[:, 2 * exp_a + lo:2 * exp_a + lo + hd])
        s = jnp.concatenate(
            [_dot(wsh, vb[c * CHUNK:(c + 1) * CHUNK, lo:lo + hd]) + bias
             for c in range(rows // CHUNK)], axis=0)
        gated_ref[:, lo:lo + hd] = (u * s * _silu(z)).astype(BF16)
    _gmlp_back(x, gated_ref, wout_ref, npost_ref, o_ref)


def _gmlp_sample_kernel(ws_ref, bs_ref, x_ref, npre_ref, npost_ref, lng_ref, lnb_ref, win_hbm,
                        wout_hbm, *refs, n_seq, seq_len, layer, w_layers, n_prev):
    vprev_hbm = refs[0] if n_prev else None
    o_ref, v_hbm, gated_ref, win_ref, wout_ref, w_sem, v_ref, v_sem = refs[-8:]
    exp_a = wout_ref.shape[0]
    hd = exp_a // N_HEADS
    col_parts = [slice(exp_a, 2 * exp_a), slice(0, exp_a), slice(2 * exp_a, 3 * exp_a)]
    copies = [pltpu.make_async_copy(win_hbm.at[w_layers[0], :, p], win_ref.at[:, p], w_sem.at[k])
              for k, p in enumerate(col_parts)]
    copies.append(pltpu.make_async_copy(wout_hbm.at[w_layers[1]], wout_ref, w_sem.at[len(col_parts)]))
    for cp in copies:
        cp.start()
    x = x_ref[...]
    copies[0].wait()
    hn, v = _gmlp_front(x, npre_ref, lng_ref, lnb_ref, win_ref, exp_a)
    v_ref[...] = v
    v_copies = [pltpu.make_async_copy(v_ref.at[pl.ds(t * n_seq, n_seq), :],
                                      v_hbm.at[n_prev, :, t, :], v_sem.at[t])
                for t in range(seq_len)]
    if n_prev:
        v_copies.append(pltpu.make_async_copy(vprev_hbm, v_hbm.at[pl.ds(0, n_prev)],
                                              v_sem.at[seq_len]))
    for cp in v_copies:
        cp.start()
    copies[1].wait()
    copies[2].wait()
    for h in range(N_HEADS):
        lo = h * hd
        u = _dot(hn, win_ref[:, lo:lo + hd])
        z = _dot(hn, win_ref[:, 2 * exp_a + lo:2 * exp_a + lo + hd])
        vt = [v[t * n_seq:(t + 1) * n_seq, lo:lo + hd] for t in range(seq_len)]
        parts = []
        for t in range(seq_len):
            b_idx = (layer * N_HEADS + h) * seq_len + t
            s = ws_ref[b_idx * seq_len] * vt[0]
            for t2 in range(1, t + 1):
                s = s + ws_ref[b_idx * seq_len + t2] * vt[t2]
            parts.append(s + bs_ref[b_idx])
        s = jnp.concatenate(parts, axis=0)
        gated_ref[:, lo:lo + hd] = (u * s * _silu(z)).astype(BF16)
    copies[3].wait()
    _gmlp_back(x, gated_ref, wout_ref, npost_ref, o_ref)
    for cp in v_copies:
        cp.wait()


def _gmlp_prompt(x2d, layer_a, layer, npre, npost, lng, lnb, win, wout, ws, bst, cast_jobs):
    rows, d = x2d.shape
    (win, win_l), (wout, wout_l) = win, wout
    exp_a = wout.shape[1]
    n_steps = rows // ROW_TILE
    c_in, c_out, c_shapes = _cast_specs(cast_jobs, n_steps)
    footprint = (_layer_bytes(win, wout, ws) + 4 * ROW_TILE * d * 4
                 + ROW_TILE * exp_a * (2 + 4 + 2 + 8)
                 + 6 * sum(w[0].size for w, _ in cast_jobs) // n_steps * 2)
    res = pl.pallas_call(
        functools.partial(_gmlp_prompt_kernel, n_cast=len(cast_jobs)),
        grid=(n_steps,),
        in_specs=[
            _row_spec(ROW_TILE, d),
            _layer_spec(npre, layer), _layer_spec(npost, layer),
            _layer_spec(lng, layer_a), _layer_spec(lnb, layer_a),
            _layer_spec(win, win_l), _layer_spec(wout, wout_l),
            _layer_spec(ws, layer_a), _layer_spec(bst, layer_a),
        ] + c_in,
        out_specs=[_row_spec(ROW_TILE, d)] + c_out,
        out_shape=[jax.ShapeDtypeStruct((rows, d), F32)] + c_shapes,
        scratch_shapes=[pltpu.VMEM((ROW_TILE, exp_a), BF16)],
        compiler_params=pltpu.CompilerParams(
            dimension_semantics=("parallel",),
            vmem_limit_bytes=_vmem_limit(footprint)),
        name="gmlp_prompt",
    )(x2d, npre, npost, lng, lnb, win, wout, ws, bst, *[w for w, _ in cast_jobs])
    return res[0], res[1:]


def _gmlp_sample(xs, n_seq, seq_len, layer_a, layer, npre, npost, lng, lnb, win, wout,
                 ws_flat, bs_flat, v_prev):
    rows, d = xs.shape
    n_prev = 0 if v_prev is None else v_prev.shape[0]
    (win, win_l), (wout, wout_l) = win, wout
    exp_a = wout.shape[1]
    footprint = _layer_bytes(win, wout) + 4 * rows * d * 4 + rows * exp_a * (2 + 8 + 4 + 8)
    hbm = pl.BlockSpec(memory_space=pl.ANY)
    return pl.pallas_call(
        functools.partial(_gmlp_sample_kernel, n_seq=n_seq, seq_len=seq_len, layer=layer_a,
                          w_layers=(win_l, wout_l), n_prev=n_prev),
        grid=(1,),
        in_specs=[_SMEM, _SMEM, _row_spec(rows, d),
                  _layer_spec(npre, layer), _layer_spec(npost, layer),
                  _layer_spec(lng, layer_a), _layer_spec(lnb, layer_a), hbm, hbm]
        + ([hbm] if n_prev else []),
        out_specs=[_row_spec(rows, d), hbm],
        out_shape=[jax.ShapeDtypeStruct((rows, d), F32),
                   jax.ShapeDtypeStruct((n_prev + 1, n_seq, seq_len, exp_a), F32)],
        scratch_shapes=[pltpu.VMEM((rows, exp_a), BF16), pltpu.VMEM(win.shape[1:], BF16),
                        pltpu.VMEM(wout.shape[1:], BF16), pltpu.SemaphoreType.DMA((4,)),
                        pltpu.VMEM((rows, exp_a), F32), pltpu.SemaphoreType.DMA((seq_len + 1,))],
        compiler_params=pltpu.CompilerParams(
            dimension_semantics=("arbitrary",),
            vmem_limit_bytes=_vmem_limit(footprint)),
        name="gmlp_sample",
    )(ws_flat, bs_flat, xs, npre, npost, lng, lnb, win, wout, *([v_prev] if n_prev else []))


def _s5_prep_kernel(are_ref, aim_ref, ldt_ref, btr_ref, bti_ref, ctr_ref, cti_ref,
                    l4r_ref, l4i_ref, wur_ref, wui_ref, vc_ref, kloc_ref):
    dt = jnp.exp(ldt_ref[...])
    ar = are_ref[...]
    ai = aim_ref[...]
    mag = jnp.exp(dt * ar)
    ang = dt * ai
    abr = mag * jnp.cos(ang)
    abi = mag * jnp.sin(ang)
    nr = abr - 1.0
    ni = abi
    den = ar * ar + ai * ai
    cre = (nr * ar + ni * ai) / den
    cim = (ni * ar - nr * ai) / den
    btr = btr_ref[...]
    bti = bti_ref[...]
    bbr = cre * btr - cim * bti
    bbi = cre * bti + cim * btr

    def cmul(xr, xi, yr, yi):
        return xr * yr - xi * yi, xr * yi + xi * yr

    lam = [(jnp.ones_like(abr), jnp.zeros_like(abi)), (abr, abi)]
    for _ in range(BLOCK - 1):
        lam.append(cmul(*lam[-1], abr, abi))
    l4r_ref[...] = lam[BLOCK][0]
    l4i_ref[...] = lam[BLOCK][1]

    n_q, k_rows, n_cols = wur_ref.shape
    q_groups = n_cols // STATE_P
    q_rows = q_groups * SSM_GROUP
    same_q = (_div_pow2(lax.broadcasted_iota(jnp.int32, (q_rows, n_cols), 0), SSM_GROUP)
              == _div_pow2(lax.broadcasted_iota(jnp.int32, (q_rows, n_cols), 1), STATE_P))
    for ip in range(BLOCK):
        ur, ui = cmul(bbr, bbi, *lam[BLOCK - 1 - ip])
        for q in range(n_q):
            for src, dst in ((ur, wur_ref), (ui, wui_ref)):
                blk = jnp.concatenate([src[:, q * n_cols:(q + 1) * n_cols]] * q_groups, axis=0)
                dst[q, ip * q_rows:(ip + 1) * q_rows, :] = jnp.where(same_q, blk, 0.0).astype(BF16)

    ctr = ctr_ref[...]
    cti = cti_ref[...]
    cl = [cmul(ctr, cti, *lam[t]) for t in range(BLOCK + 1)]

    def tile_rows(a, q):
        blk = jnp.concatenate([a[:, q * n_cols:(q + 1) * n_cols]] * q_groups, axis=0)
        return jnp.where(same_q, blk, 0.0)

    half = V7X_LANES // 2
    low = lax.broadcasted_iota(jnp.int32, (q_rows, V7X_LANES), 1) < half
    for q in range(n_q):
        clm = [(tile_rows(c[0], q), tile_rows(c[1], q)) for c in cl]
        rows = [jnp.concatenate([clm[i + 1][0], -clm[i + 1][1]], axis=1) for i in range(BLOCK)]
        vc_ref[q] = jnp.concatenate(rows, axis=0).T.astype(BF16)
        bcat = jnp.concatenate([tile_rows(bbr, q), -tile_rows(bbi, q)], axis=1).astype(BF16)
        ccat = jnp.concatenate([jnp.concatenate(clm[t], axis=1) for t in range(BLOCK)], axis=0)
        kall = lax.dot_general(bcat, ccat.astype(BF16), (((1,), (1,)), ((), ())),
                               preferred_element_type=F32)
        k0, k1 = kall[:, :V7X_LANES], kall[:, V7X_LANES:]
        r0, r1 = pltpu.roll(k0, half, 1), pltpu.roll(k1, half, 1)
        zero = jnp.zeros_like(k0)
        shifted = [(k0, k1),
                   (jnp.where(low, zero, r0), jnp.where(low, r0, r1)),
                   (zero, k0),
                   (zero, jnp.where(low, zero, r0))]
        kloc_ref[q] = jnp.concatenate([jnp.concatenate(sh, axis=1) for sh in shifted],
                                      axis=0).astype(BF16)


def _s5_prep(are, aim, ldt, btr, bti, ctr, cti):
    n_layers, _, n_state = are.shape
    n_q = n_state // V7X_MXU_DIM

    def per_layer(shape):
        return pl.BlockSpec((None,) + shape, lambda l: (l,) + (0,) * len(shape))

    out_tails = [(1, n_state), (1, n_state),
                 (n_q, V7X_MXU_DIM, V7X_MXU_DIM), (n_q, V7X_MXU_DIM, V7X_MXU_DIM),
                 (n_q, 2 * V7X_MXU_DIM, V7X_MXU_DIM), (n_q, V7X_MXU_DIM, V7X_MXU_DIM)]
    out_dtypes = [F32, F32, BF16, BF16, BF16, BF16]
    args = (are, aim, ldt, btr, bti, ctr, cti)
    return pl.pallas_call(
        _s5_prep_kernel,
        grid=(n_layers,),
        in_specs=[per_layer(a.shape[1:]) for a in args],
        out_specs=[per_layer(t) for t in out_tails],
        out_shape=[jax.ShapeDtypeStruct((n_layers,) + t, dt) for t, dt in zip(out_tails, out_dtypes)],
        compiler_params=pltpu.CompilerParams(
            dimension_semantics=("parallel",),
            vmem_limit_bytes=_vmem_limit(2 * sum(
                int(jnp.dtype(dt).itemsize) * functools.reduce(lambda a, b: a * b, t)
                for t, dt in zip(out_tails, out_dtypes)))),
        name="s5_prep",
    )(*args)


N_S5_PARAMS = 15
S5_WEIGHTS = (1, 4, 5, 6, 7, 9, 11, 13)


def _s5_layer_kernel(*refs, n_seq, n_steps, has_h0, swap_bt, n_cast, lazy_layers):
    if has_h0:
        x_ref, h0r_ref, h0i_ref = refs[:3]
        refs = refs[3:]
    else:
        x_ref = refs[0]
        refs = refs[1:]
    params = list(refs[:N_S5_PARAMS])
    refs = refs[N_S5_PARAMS:]
    o_ref, str_ref, sti_ref = refs[n_cast:n_cast + 3]
    n_lazy = len(S5_WEIGHTS) + 1 if lazy_layers else 0
    n_scratch = 5 + (4 if swap_bt else 0) + n_lazy
    scratch = refs[-n_scratch:]
    hpr_ref, hpi_ref, hr_ref, hi_ref, y_ref = scratch[:5]
    _cast_slabs(refs[:n_cast], refs[n_cast + 3:-n_scratch])
    step = pl.program_id(0)
    n_tiles = pl.num_programs(0)

    w_copies = []
    if lazy_layers:
        w_sem = scratch[-1]
        for k, (p, layer) in enumerate(zip(S5_WEIGHTS, lazy_layers)):
            w_copies.append(pltpu.make_async_copy(params[p].at[layer], scratch[-n_lazy + k],
                                                  w_sem.at[k]))
            params[p] = scratch[-n_lazy + k]
        for cp in w_copies:
            cp.start()

    def need(*which):
        for k in which:
            if w_copies:
                w_copies[k].wait()

    (npre_ref, win_ref, l4r_ref, l4i_ref, wur_ref, wui_ref, vc_ref, kloc_ref, dsk_ref,
     w1_ref, b1_ref, w2_ref, b2_ref, wout_ref, npost_ref) = params

    if swap_bt:
        xbuf_ref, obuf_ref, in_sem, out_sem = scratch[5:9]
        slot = step % 2

        def tile_copies(tile, sl, fetch):
            seq_rows = pl.ds(tile * n_steps, n_steps)
            if fetch:
                return [pltpu.make_async_copy(x_ref.at[b, seq_rows, :], xbuf_ref.at[sl, :, b, :],
                                              in_sem.at[sl, b]) for b in range(n_seq)]
            return [pltpu.make_async_copy(obuf_ref.at[sl, :, b, :], o_ref.at[b, seq_rows, :],
                                          out_sem.at[sl, b]) for b in range(n_seq)]

        @pl.when(step == 0)
        def _():
            for cp in tile_copies(0, 0, True):
                cp.start()

        @pl.when(step + 1 < n_tiles)
        def _():
            for cp in tile_copies(step + 1, 1 - slot, True):
                cp.start()

        for cp in tile_copies(step, slot, True):
            cp.wait()

        @pl.when(step >= 2)
        def _():
            for cp in tile_copies(step - 2, slot, False):
                cp.wait()

    @pl.when(step == 0)
    def _():
        if has_h0:
            hr_ref[...] = h0r_ref[...]
            hi_ref[...] = h0i_ref[...]
        else:
            hr_ref[...] = jnp.zeros_like(hr_ref)
            hi_ref[...] = jnp.zeros_like(hi_ref)

    d = npre_ref.shape[-1]
    width = win_ref.shape[1] // 2
    c_rows = PIECE_ROWS
    c_steps = c_rows // n_seq
    n_pieces = n_steps // c_steps
    x_c, xb_c, z_c = [], [], []
    need(0)
    for c in range(n_pieces):
        if swap_bt:
            xc = xbuf_ref.at[slot][c * c_steps:(c + 1) * c_steps].reshape(c_rows, d)
        else:
            xc = x_ref[c * c_rows:(c + 1) * c_rows, :]
        hn = _rmsnorm(xc, npre_ref[...]).astype(BF16)
        x_c.append(xc)
        xb_c.append(_dot(hn, win_ref[:, :width]))
        z_c.append(_dot(hn, win_ref[:, width:]))
    xb = jnp.concatenate(xb_c, axis=0)

    n_blocks = n_steps // BLOCK
    rows = n_blocks * n_seq
    x4 = xb.reshape(n_blocks, BLOCK, n_seq, width)
    xi = [x4[:, i].reshape(rows, width) for i in range(BLOCK)]

    sub = V7X_SUBLANES
    pair = 2 * sub
    n_q, _, n_cols = wur_ref.shape
    n_v = width // V7X_LANES
    half = V7X_LANES // 2
    low = lax.broadcasted_iota(jnp.int32, (rows, V7X_LANES), 1) < half

    lhs_q = []
    need(1, 2)
    for v in range(n_v):
        cols = slice(v * V7X_LANES, (v + 1) * V7X_LANES)
        p = [a[:, cols] for a in xi]
        pr = [pltpu.roll(a, half, 1) for a in p]
        lhs_lo = jnp.concatenate([jnp.where(low, p[i], pr[i + 1]) for i in range(0, BLOCK, 2)], axis=1)
        lhs_hi = jnp.concatenate([jnp.where(low, pr[i], p[i + 1]) for i in range(0, BLOCK, 2)], axis=1)
        for q, lhs in ((2 * v, lhs_lo.astype(BF16)), (2 * v + 1, lhs_hi.astype(BF16))):
            lanes = pl.ds(q * n_cols, n_cols)
            wr = _dot(lhs, wur_ref[q])
            wi = _dot(lhs, wui_ref[q])
            if n_seq == sub:
                ar = jnp.broadcast_to(l4r_ref[:, lanes], (sub, n_cols))
                ai = jnp.broadcast_to(l4i_ref[:, lanes], (sub, n_cols))
                hr, hi = hr_ref[:, lanes], hi_ref[:, lanes]
                for k in range(n_blocks // 2):
                    prv_r, prv_i = [], []
                    for r0 in (2 * k * sub, (2 * k + 1) * sub):
                        prv_r.append(hr)
                        prv_i.append(hi)
                        hr, hi = (ar * hr - ai * hi + wr[r0:r0 + sub],
                                  ar * hi + ai * hr + wi[r0:r0 + sub])
                    hpr_ref[k * pair:(k + 1) * pair, lanes] = jnp.concatenate(prv_r, axis=0).astype(BF16)
                    hpi_ref[k * pair:(k + 1) * pair, lanes] = jnp.concatenate(prv_i, axis=0).astype(BF16)
                hr_ref[:, lanes] = hr
                hi_ref[:, lanes] = hi
            else:
                ar = jnp.broadcast_to(l4r_ref[:, lanes], (pair, n_cols))
                ai = jnp.broadcast_to(l4i_ref[:, lanes], (pair, n_cols))
                for m in range(n_seq // pair):
                    srows = pl.ds(m * pair, pair)
                    hr, hi = hr_ref[srows, lanes], hi_ref[srows, lanes]
                    for blk in range(n_blocks):
                        r0 = blk * n_seq + m * pair
                        hpr_ref[r0:r0 + pair, lanes] = hr.astype(BF16)
                        hpi_ref[r0:r0 + pair, lanes] = hi.astype(BF16)
                        hr, hi = (ar * hr - ai * hi + wr[r0:r0 + pair],
                                  ar * hi + ai * hr + wi[r0:r0 + pair])
                    hr_ref[srows, lanes] = hr
                    hi_ref[srows, lanes] = hi
            lhs_q.append(lhs)

    need(3, 4)
    for v in range(n_v):
        cols = slice(v * V7X_LANES, (v + 1) * V7X_LANES)
        yq = []
        for q in (2 * v, 2 * v + 1):
            lanes = pl.ds(q * n_cols, n_cols)
            st = jnp.concatenate([hpr_ref[:, lanes], hpi_ref[:, lanes]], axis=1)
            yq.append(_dot(st, vc_ref[q]) + _dot(lhs_q[q], kloc_ref[q]))
        ys = []
        for i in range(BLOCK):
            lo_q, hi_q = (a[:, (i // 2) * V7X_LANES:(i // 2 + 1) * V7X_LANES] for a in yq)
            if i % 2 == 0:
                yv = jnp.where(low, lo_q, pltpu.roll(hi_q, half, 1))
            else:
                yv = jnp.where(low, pltpu.roll(lo_q, half, 1), hi_q)
            ys.append(jax.nn.gelu(yv + dsk_ref[:, cols] * xi[i][:, cols]))
        y = jnp.stack([a.reshape(n_blocks, n_seq, V7X_LANES) for a in ys], axis=1)
        y_ref[:, cols] = y.reshape(n_steps * n_seq, V7X_LANES).astype(BF16)

    need(5, 6, 7)
    for c in range(n_pieces):
        y = y_ref[c * c_rows:(c + 1) * c_rows, :]
        g = (_dot(y, w1_ref[...]) + b1_ref[...]) * jax.nn.sigmoid(_dot(y, w2_ref[...]) + b2_ref[...])
        out = _dot((g * _silu(z_c[c])).astype(BF16), wout_ref[...])
        xn = x_c[c] + _rmsnorm(out, npost_ref[...])
        if swap_bt:
            obuf_ref.at[slot][c * c_steps:(c + 1) * c_steps] = xn.reshape(c_steps, n_seq, d)
        else:
            o_ref[c * c_rows:(c + 1) * c_rows, :] = xn

    if swap_bt:
        for cp in tile_copies(step, slot, False):
            cp.start()

    @pl.when(step == n_tiles - 1)
    def _():
        str_ref[...] = hr_ref[...]
        sti_ref[...] = hi_ref[...]
        if swap_bt:
            for cp in tile_copies(step, slot, False):
                cp.wait()

    if swap_bt:
        @pl.when(jnp.logical_and(step == n_tiles - 1, n_tiles >= 2))
        def _():
            for cp in tile_copies(step - 1, 1 - slot, False):
                cp.wait()


def _s5_layer(x, h0, layer_b, layer, npre, win, core_consts, out_consts, npost, n_seq, n_steps,
              cast_jobs=()):
    swap_bt = x.ndim == 3
    d = x.shape[-1]
    exp_b = win[0].shape[2] // 2
    n_state = core_consts[0].shape[-1]
    tile = n_seq * n_steps
    n_tiles = x.size // (tile * d)
    rows = tile // BLOCK
    has_h0 = h0 is not None
    assert n_steps % BLOCK == 0 and (n_seq > V7X_SUBLANES or (n_steps // BLOCK) % 2 == 0)
    assert tile % PIECE_ROWS == 0 and PIECE_ROWS % n_seq == 0
    params_b = [win] + [(c, layer_b) for c in core_consts] + list(out_consts)
    assert len(params_b) + 2 == N_S5_PARAMS
    c_in, c_out, c_shapes = _cast_specs(cast_jobs, n_tiles)
    args = ([x] + (list(h0) if has_h0 else []) + [npre] + [a for a, _ in params_b] + [npost]
            + [w for w, _ in cast_jobs])
    if swap_bt:
        io_spec = pl.BlockSpec(memory_space=pl.ANY)
        io_scratch = [pltpu.VMEM((2, n_steps, n_seq, d), F32), pltpu.VMEM((2, n_steps, n_seq, d), F32),
                      pltpu.SemaphoreType.DMA((2, n_seq)), pltpu.SemaphoreType.DMA((2, n_seq))]
    else:
        io_spec = pl.BlockSpec((tile, d), lambda i: (i, 0))
        io_scratch = []
    lazy = n_tiles == 1
    lazy_idx = [p - 1 for p in S5_WEIGHTS]
    hbm = pl.BlockSpec(memory_space=pl.ANY)
    in_specs = ([io_spec] + [_layer_spec(a, layer_b) for a in (h0 if has_h0 else ())]
                + [_layer_spec(npre, layer)]
                + [hbm if lazy and k in lazy_idx else _layer_spec(a, l)
                   for k, (a, l) in enumerate(params_b)]
                + [_layer_spec(npost, layer)] + c_in)
    lazy_scratch = ([pltpu.VMEM(params_b[k][0].shape[1:], params_b[k][0].dtype) for k in lazy_idx]
                    + [pltpu.SemaphoreType.DMA((len(lazy_idx),))]) if lazy else []
    st_spec = pl.BlockSpec((n_seq, n_state), lambda i: (0, 0))
    st_shape = jax.ShapeDtypeStruct((n_seq, n_state), F32)
    footprint = (_layer_bytes(*[a for a, _ in params_b]) + (_layer_bytes(*h0) if has_h0 else 0)
                 + 4 * tile * d * 4 + 2 * rows * n_state * 2 + 8 * n_seq * n_state * 4
                 + tile * exp_b * (2 + 6 * 4)
                 + 6 * sum(w[0].size for w, _ in cast_jobs) // n_tiles * 2)
    res = pl.pallas_call(
        functools.partial(_s5_layer_kernel, n_seq=n_seq, n_steps=n_steps, has_h0=has_h0,
                          swap_bt=swap_bt, n_cast=len(cast_jobs),
                          lazy_layers=tuple(params_b[k][1] for k in lazy_idx) if lazy else ()),
        grid=(n_tiles,),
        in_specs=in_specs,
        out_specs=[io_spec, st_spec, st_spec] + c_out,
        out_shape=[jax.ShapeDtypeStruct(x.shape, F32), st_shape, st_shape] + c_shapes,
        scratch_shapes=[pltpu.VMEM((rows, n_state), BF16), pltpu.VMEM((rows, n_state), BF16),
                        pltpu.VMEM((n_seq, n_state), F32), pltpu.VMEM((n_seq, n_state), F32),
                        pltpu.VMEM((tile, exp_b), BF16)] + io_scratch + lazy_scratch,
        compiler_params=pltpu.CompilerParams(
            dimension_semantics=("arbitrary",),
            vmem_limit_bytes=_vmem_limit(footprint)),
        name="s5_layer_sample" if has_h0 else "s5_layer_prompt",
    )(*args)
    return res[0], res[1], res[2], res[3:]


def kernel(x_prompt, x_sample, state_ssm_re, state_ssm_im, norm_pre, norm_post,
           w_in_a, ln_v_g, ln_v_b, w_s, b_s, w_out_a,
           w_in_b, a_re, a_im, log_dt, b_re, b_im, c_re, c_im, d_skip,
           w_glu1, b_glu1, w_glu2, b_glu2, w_out_b):
    n_batch, seq_len, d = x_prompt.shape
    n_dec, dec_len, _ = x_sample.shape
    depth = norm_pre.shape[0]
    n_ssm, n_groups, state_p = a_re.shape
    n_state = n_groups * state_p
    assert seq_len % ROW_TILE == 0 and ROW_TILE % CHUNK == 0 and seq_len % SCAN_STEPS == 0
    assert n_batch == V7X_SUBLANES and n_dec % (2 * V7X_SUBLANES) == 0
    assert dec_len <= CHUNK and dec_len % BLOCK == 0
    assert state_p == STATE_P and b_re.shape[-1] == SSM_GROUP and w_s.shape[1] == N_HEADS

    rows3 = lambda a: a.reshape(a.shape[0], 1, -1)
    npre, npost = rows3(norm_pre), rows3(norm_post)
    lng, lnb = rows3(ln_v_g), rows3(ln_v_b)
    wa16 = ((w_in_a[:1].astype(BF16), 0), (w_out_a[:1].astype(BF16), 0))
    bst = jnp.swapaxes(b_s, 1, 2)
    ws_dec = w_s[:, :, :dec_len, :dec_len].reshape(-1)
    bs_dec = b_s[:, :, :dec_len].reshape(-1)

    b1, b2 = rows3(b_glu1), rows3(b_glu2)
    lanes_gp = lambda a, perm: jnp.transpose(a, perm).reshape(n_ssm, SSM_GROUP, n_state)
    l4r, l4i, wur, wui, vcw, kloc = _s5_prep(
        rows3(a_re), rows3(a_im), rows3(jnp.repeat(log_dt, state_p, axis=1)),
        lanes_gp(b_re, (0, 3, 1, 2)), lanes_gp(b_im, (0, 3, 1, 2)),
        lanes_gp(c_re, (0, 2, 1, 3)), lanes_gp(c_im, (0, 2, 1, 3)))
    core_consts = (l4r, l4i, wur, wui, vcw, kloc, rows3(d_skip))
    h0 = (state_ssm_re.reshape(n_ssm, n_dec, n_state), state_ssm_im.reshape(n_ssm, n_dec, n_state))

    xp = x_prompt.reshape(n_batch * seq_len, d)
    xs = jnp.transpose(x_sample, (1, 0, 2)).reshape(dec_len * n_dec, d)

    chunk_v, st_p_re, st_p_im, st_s_re, st_s_im = None, [], [], [], []
    for i in range(depth):
        j = i // 2
        if i % 2 == 0:
            jobs = [(w, j) for w in (w_in_b, w_glu1, w_glu2, w_out_b)] if i + 1 < depth else []
            xp, wb16 = _gmlp_prompt(xp, j, i, npre, npost, lng, lnb, *wa16, w_s, bst, jobs)
            xs, chunk_v = _gmlp_sample(xs, n_dec, dec_len, j, i, npre, npost, lng, lnb, *wa16,
                                       ws_dec, bs_dec, chunk_v)
        else:
            win = (wb16[0], 0)
            out_consts = ((wb16[1], 0), (b1, j), (wb16[2], 0), (b2, j), (wb16[3], 0))
            jobs = [(w_in_a, j + 1), (w_out_a, j + 1)] if i + 1 < depth else []
            xp3, sr, si, wa16 = _s5_layer(xp.reshape(n_batch, seq_len, d), None, j, i, npre, win,
                                          core_consts, out_consts, npost, n_batch, SCAN_STEPS, jobs)
            wa16 = tuple((w, 0) for w in wa16)
            xp = xp3.reshape(n_batch * seq_len, d)
            st_p_re.append(sr)
            st_p_im.append(si)
            xs, sr, si, _ = _s5_layer(xs, h0, j, i, npre, win, core_consts, out_consts, npost,
                                      n_dec, dec_len)
            st_s_re.append(sr)
            st_s_im.append(si)

    y_prompt = xp.reshape(n_batch, seq_len, d)
    y_sample = jnp.transpose(xs.reshape(dec_len, n_dec, d), (1, 0, 2))
    states = lambda parts, n: jnp.stack(parts).reshape(len(parts), n, n_groups, state_p)
    return (y_prompt, y_sample, chunk_v, states(st_p_re, n_batch), states(st_p_im, n_batch),
            states(st_s_re, n_dec), states(st_s_im, n_dec))
```

```python
import functools

import jax
import jax.numpy as jnp
from jax import lax
from jax.experimental import pallas as pl
from jax.experimental.pallas import tpu as pltpu

EPS = 1e-6
CHUNK = 128
N_HEADS = 8
SSM_GROUP = 16
STATE_P = 64

V7X_LANES = 128
V7X_SUBLANES = 8
V7X_MXU_DIM = 256
V7X_VMEM_BYTES = 64 * 1024 * 1024

ROW_TILE = 1024
SCAN_STEPS = 64
BLOCK = 4
PIECE_ROWS = 256

BF16 = jnp.bfloat16
F32 = jnp.float32


def _dot(a, b):
    return jnp.dot(a, b, preferred_element_type=F32)


def _rmsnorm(x, g):
    ms = jnp.mean(x * x, axis=-1, keepdims=True)
    return x * lax.rsqrt(ms + EPS) * g


def _layernorm(x, g, b):
    mu = jnp.mean(x, axis=-1, keepdims=True)
    xc = x - mu
    var = jnp.mean(xc * xc, axis=-1, keepdims=True)
    return xc * lax.rsqrt(var + EPS) * g + b


def _silu(z):
    return z * jax.nn.sigmoid(z)


def _div_pow2(x, n):
    assert n & (n - 1) == 0
    return lax.shift_right_logical(x, n.bit_length() - 1)


def _vmem_limit(nbytes):
    return int(min(V7X_VMEM_BYTES - (4 << 20), nbytes + (12 << 20)))


def _layer_spec(stacked, layer):
    tail = stacked.shape[1:]
    return pl.BlockSpec((None,) + tail, lambda *_: (layer,) + (0,) * len(tail),
                        pipeline_mode=pl.Buffered(1))


def _layer_bytes(*stacked):
    return sum(a[0].size * a.dtype.itemsize for a in stacked)


def _cast_specs(jobs, n_steps):
    in_specs, out_specs, out_shapes = [], [], []
    for w, layer in jobs:
        _, rows, cols = w.shape
        slab = rows // n_steps
        assert rows % n_steps == 0 and slab % (2 * V7X_SUBLANES) == 0
        in_specs.append(pl.BlockSpec((None, slab, cols), lambda i, layer=layer: (layer, i, 0)))
        out_specs.append(pl.BlockSpec((None, slab, cols), lambda i: (0, i, 0)))
        out_shapes.append(jax.ShapeDtypeStruct((1, rows, cols), BF16))
    return in_specs, out_specs, out_shapes


def _cast_slabs(in_refs, out_refs):
    for src, dst in zip(in_refs, out_refs):
        dst[...] = src[...].astype(BF16)


def _row_spec(rows, width):
    return pl.BlockSpec((rows, width), lambda i: (i, 0))


_SMEM = pl.BlockSpec(memory_space=pltpu.SMEM)


def _gmlp_front(x, npre_ref, lng_ref, lnb_ref, win_ref, exp_a):
    hn, v = [], []
    for r0 in range(0, x.shape[0], PIECE_ROWS):
        hn.append(_rmsnorm(x[r0:r0 + PIECE_ROWS], npre_ref[...]).astype(BF16))
        v.append(_layernorm(_dot(hn[-1], win_ref[:, exp_a:2 * exp_a]), lng_ref[...], lnb_ref[...]))
    return jnp.concatenate(hn, axis=0), jnp.concatenate(v, axis=0)


def _gmlp_back(x, gated_ref, wout_ref, npost_ref, o_ref):
    for r0 in range(0, x.shape[0], PIECE_ROWS):
        r = slice(r0, r0 + PIECE_ROWS)
        out = _dot(gated_ref[r, :], wout_ref[...])
        o_ref[r, :] = x[r] + _rmsnorm(out, npost_ref[...])


def _gmlp_prompt_kernel(x_ref, npre_ref, npost_ref, lng_ref, lnb_ref, win_ref, wout_ref,
                        ws_ref, bst_ref, *refs, n_cast):
    o_ref, gated_ref = refs[n_cast], refs[-1]
    _cast_slabs(refs[:n_cast], refs[n_cast + 1:-1])
    rows = x_ref.shape[0]
    exp_a = wout_ref.shape[0]
    hd = exp_a // N_HEADS
    x = x_ref[...]
    hn, v = _gmlp_front(x, npre_ref, lng_ref, lnb_ref, win_ref, exp_a)
    vb = v.astype(BF16)
    causal = (lax.broadcasted_iota(jnp.int32, (CHUNK, CHUNK), 1)
              <= lax.broadcasted_iota(jnp.int32, (CHUNK, CHUNK), 0))
    for h in range(N_HEADS):
        lo = h * hd
        wsh = jnp.where(causal, ws_ref[h], 0.0).astype(BF16)
        bias = bst_ref[:, h:h + 1]
        u = _dot(hn, win_ref[:, lo:lo + hd])
        z = _dot(hn, win_ref[:, 2 * exp_a + lo:2 * exp_a + lo + hd])
        s = jnp.concatenate(
            [_dot(wsh, vb[c * CHUNK:(c + 1) * CHUNK, lo:lo + hd]) + bias
             for c in range(rows // CHUNK)], axis=0)
        gated_ref[:, lo:lo + hd] = (u * s * _silu(z)).astype(BF16)
    _gmlp_back(x, gated_ref, wout_ref, npost_ref, o_ref)


def _gmlp_sample_kernel(ws_ref, bs_ref, x_ref, npre_ref, npost_ref, lng_ref, lnb_ref, win_hbm,
                        wout_hbm, o_ref, v_hbm, gated_ref, win_ref, wout_ref, w_sem, v_ref, v_sem,
                        *, n_seq, seq_len, layer, w_layers):
    exp_a = wout_ref.shape[0]
    hd = exp_a // N_HEADS
    col_parts = [slice(exp_a, 2 * exp_a), slice(0, exp_a), slice(2 * exp_a, 3 * exp_a)]
    copies = [pltpu.make_async_copy(win_hbm.at[w_layers[0], :, p], win_ref.at[:, p], w_sem.at[k])
              for k, p in enumerate(col_parts)]
    copies.append(pltpu.make_async_copy(wout_hbm.at[w_layers[1]], wout_ref, w_sem.at[len(col_parts)]))
    for cp in copies:
        cp.start()
    x = x_ref[...]
    copies[0].wait()
    hn, v = _gmlp_front(x, npre_ref, lng_ref, lnb_ref, win_ref, exp_a)
    v_ref[...] = v
    v_copies = [pltpu.make_async_copy(v_ref.at[pl.ds(t * n_seq, n_seq), :], v_hbm.at[:, t, :],
                                      v_sem.at[t]) for t in range(seq_len)]
    for cp in v_copies:
        cp.start()
    copies[1].wait()
    copies[2].wait()
    for h in range(N_HEADS):
        lo = h * hd
        u = _dot(hn, win_ref[:, lo:lo + hd])
        z = _dot(hn, win_ref[:, 2 * exp_a + lo:2 * exp_a + lo + hd])
        vt = [v[t * n_seq:(t + 1) * n_seq, lo:lo + hd] for t in range(seq_len)]
        parts = []
        for t in range(seq_len):
            b_idx = (layer * N_HEADS + h) * seq_len + t
            s = ws_ref[b_idx * seq_len] * vt[0]
            for t2 in range(1, t + 1):
                s = s + ws_ref[b_idx * seq_len + t2] * vt[t2]
            parts.append(s + bs_ref[b_idx])
        s = jnp.concatenate(parts, axis=0)
        gated_ref[:, lo:lo + hd] = (u * s * _silu(z)).astype(BF16)
    copies[3].wait()
    _gmlp_back(x, gated_ref, wout_ref, npost_ref, o_ref)
    for cp in v_copies:
        cp.wait()


def _gmlp_prompt(x2d, layer_a, layer, npre, npost, lng, lnb, win, wout, ws, bst, cast_jobs):
    rows, d = x2d.shape
    (win, win_l), (wout, wout_l) = win, wout
    exp_a = wout.shape[1]
    n_steps = rows // ROW_TILE
    c_in, c_out, c_shapes = _cast_specs(cast_jobs, n_steps)
    footprint = (_layer_bytes(win, wout, ws) + 4 * ROW_TILE * d * 4
                 + ROW_TILE * exp_a * (2 + 4 + 2 + 8)
                 + 6 * sum(w[0].size for w, _ in cast_jobs) // n_steps * 2)
    res = pl.pallas_call(
        functools.partial(_gmlp_prompt_kernel, n_cast=len(cast_jobs)),
        grid=(n_steps,),
        in_specs=[
            _row_spec(ROW_TILE, d),
            _layer_spec(npre, layer), _layer_spec(npost, layer),
            _layer_spec(lng, layer_a), _layer_spec(lnb, layer_a),
            _layer_spec(win, win_l), _layer_spec(wout, wout_l),
            _layer_spec(ws, layer_a), _layer_spec(bst, layer_a),
        ] + c_in,
        out_specs=[_row_spec(ROW_TILE, d)] + c_out,
        out_shape=[jax.ShapeDtypeStruct((rows, d), F32)] + c_shapes,
        scratch_shapes=[pltpu.VMEM((ROW_TILE, exp_a), BF16)],
        compiler_params=pltpu.CompilerParams(
            dimension_semantics=("parallel",),
            vmem_limit_bytes=_vmem_limit(footprint)),
        name="gmlp_prompt",
    )(x2d, npre, npost, lng, lnb, win, wout, ws, bst, *[w for w, _ in cast_jobs])
    return res[0], res[1:]


def _gmlp_sample(xs, n_seq, seq_len, layer_a, layer, npre, npost, lng, lnb, win, wout,
                 ws_flat, bs_flat):
    rows, d = xs.shape
    (win, win_l), (wout, wout_l) = win, wout
    exp_a = wout.shape[1]
    footprint = _layer_bytes(win, wout) + 4 * rows * d * 4 + rows * exp_a * (2 + 8 + 4 + 8)
    hbm = pl.BlockSpec(memory_space=pl.ANY)
    return pl.pallas_call(
        functools.partial(_gmlp_sample_kernel, n_seq=n_seq, seq_len=seq_len, layer=layer_a,
                          w_layers=(win_l, wout_l)),
        grid=(1,),
        in_specs=[_SMEM, _SMEM, _row_spec(rows, d),
                  _layer_spec(npre, layer), _layer_spec(npost, layer),
                  _layer_spec(lng, layer_a), _layer_spec(lnb, layer_a), hbm, hbm],
        out_specs=[_row_spec(rows, d), hbm],
        out_shape=[jax.ShapeDtypeStruct((rows, d), F32),
                   jax.ShapeDtypeStruct((n_seq, seq_len, exp_a), F32)],
        scratch_shapes=[pltpu.VMEM((rows, exp_a), BF16), pltpu.VMEM(win.shape[1:], BF16),
                        pltpu.VMEM(wout.shape[1:], BF16), pltpu.SemaphoreType.DMA((4,)),
                        pltpu.VMEM((rows, exp_a), F32), pltpu.SemaphoreType.DMA((seq_len,))],
        compiler_params=pltpu.CompilerParams(
            dimension_semantics=("arbitrary",),
            vmem_limit_bytes=_vmem_limit(footprint)),
        name="gmlp_sample",
    )(ws_flat, bs_flat, xs, npre, npost, lng, lnb, win, wout)


def _s5_prep_kernel(are_ref, aim_ref, ldt_ref, btr_ref, bti_ref, ctr_ref, cti_ref,
                    l4r_ref, l4i_ref, wur_ref, wui_ref, vc_ref, kloc_ref):
    dt = jnp.exp(ldt_ref[...])
    ar = are_ref[...]
    ai = aim_ref[...]
    mag = jnp.exp(dt * ar)
    ang = dt * ai
    abr = mag * jnp.cos(ang)
    abi = mag * jnp.sin(ang)
    nr = abr - 1.0
    ni = abi
    den = ar * ar + ai * ai
    cre = (nr * ar + ni * ai) / den
    cim = (ni * ar - nr * ai) / den
    btr = btr_ref[...]
    bti = bti_ref[...]
    bbr = cre * btr - cim * bti
    bbi = cre * bti + cim * btr

    def cmul(xr, xi, yr, yi):
        return xr * yr - xi * yi, xr * yi + xi * yr

    lam = [(jnp.ones_like(abr), jnp.zeros_like(abi)), (abr, abi)]
    for _ in range(BLOCK - 1):
        lam.append(cmul(*lam[-1], abr, abi))
    l4r_ref[...] = lam[BLOCK][0]
    l4i_ref[...] = lam[BLOCK][1]

    n_q, k_rows, n_cols = wur_ref.shape
    q_groups = n_cols // STATE_P
    q_rows = q_groups * SSM_GROUP
    same_q = (_div_pow2(lax.broadcasted_iota(jnp.int32, (q_rows, n_cols), 0), SSM_GROUP)
              == _div_pow2(lax.broadcasted_iota(jnp.int32, (q_rows, n_cols), 1), STATE_P))
    for ip in range(BLOCK):
        ur, ui = cmul(bbr, bbi, *lam[BLOCK - 1 - ip])
        for q in range(n_q):
            for src, dst in ((ur, wur_ref), (ui, wui_ref)):
                blk = jnp.concatenate([src[:, q * n_cols:(q + 1) * n_cols]] * q_groups, axis=0)
                dst[q, ip * q_rows:(ip + 1) * q_rows, :] = jnp.where(same_q, blk, 0.0).astype(BF16)

    ctr = ctr_ref[...]
    cti = cti_ref[...]
    cl = [cmul(ctr, cti, *lam[t]) for t in range(BLOCK + 1)]

    def tile_rows(a, q):
        blk = jnp.concatenate([a[:, q * n_cols:(q + 1) * n_cols]] * q_groups, axis=0)
        return jnp.where(same_q, blk, 0.0)

    half = V7X_LANES // 2
    low = lax.broadcasted_iota(jnp.int32, (q_rows, V7X_LANES), 1) < half
    for q in range(n_q):
        clm = [(tile_rows(c[0], q), tile_rows(c[1], q)) for c in cl]
        rows = [jnp.concatenate([clm[i + 1][0], -clm[i + 1][1]], axis=1) for i in range(BLOCK)]
        vc_ref[q] = jnp.concatenate(rows, axis=0).T.astype(BF16)
        bcat = jnp.concatenate([tile_rows(bbr, q), -tile_rows(bbi, q)], axis=1).astype(BF16)
        ccat = jnp.concatenate([jnp.concatenate(clm[t], axis=1) for t in range(BLOCK)], axis=0)
        kall = lax.dot_general(bcat, ccat.astype(BF16), (((1,), (1,)), ((), ())),
                               preferred_element_type=F32)
        k0, k1 = kall[:, :V7X_LANES], kall[:, V7X_LANES:]
        r0, r1 = pltpu.roll(k0, half, 1), pltpu.roll(k1, half, 1)
        zero = jnp.zeros_like(k0)
        shifted = [(k0, k1),
                   (jnp.where(low, zero, r0), jnp.where(low, r0, r1)),
                   (zero, k0),
                   (zero, jnp.where(low, zero, r0))]
        kloc_ref[q] = jnp.concatenate([jnp.concatenate(sh, axis=1) for sh in shifted],
                                      axis=0).astype(BF16)


def _s5_prep(are, aim, ldt, btr, bti, ctr, cti):
    n_layers, _, n_state = are.shape
    n_q = n_state // V7X_MXU_DIM

    def per_layer(shape):
        return pl.BlockSpec((None,) + shape, lambda l: (l,) + (0,) * len(shape))

    out_tails = [(1, n_state), (1, n_state),
                 (n_q, V7X_MXU_DIM, V7X_MXU_DIM), (n_q, V7X_MXU_DIM, V7X_MXU_DIM),
                 (n_q, 2 * V7X_MXU_DIM, V7X_MXU_DIM), (n_q, V7X_MXU_DIM, V7X_MXU_DIM)]
    out_dtypes = [F32, F32, BF16, BF16, BF16, BF16]
    args = (are, aim, ldt, btr, bti, ctr, cti)
    return pl.pallas_call(
        _s5_prep_kernel,
        grid=(n_layers,),
        in_specs=[per_layer(a.shape[1:]) for a in args],
        out_specs=[per_layer(t) for t in out_tails],
        out_shape=[jax.ShapeDtypeStruct((n_layers,) + t, dt) for t, dt in zip(out_tails, out_dtypes)],
        compiler_params=pltpu.CompilerParams(
            dimension_semantics=("parallel",),
            vmem_limit_bytes=_vmem_limit(2 * sum(
                int(jnp.dtype(dt).itemsize) * functools.reduce(lambda a, b: a * b, t)
                for t, dt in zip(out_tails, out_dtypes)))),
        name="s5_prep",
    )(*args)


N_S5_PARAMS = 15
S5_WEIGHTS = (1, 4, 5, 6, 7, 9, 11, 13)


def _s5_layer_kernel(*refs, n_seq, n_steps, has_h0, swap_bt, n_cast, lazy_layers):
    if has_h0:
        x_ref, h0r_ref, h0i_ref = refs[:3]
        refs = refs[3:]
    else:
        x_ref = refs[0]
        refs = refs[1:]
    params = list(refs[:N_S5_PARAMS])
    refs = refs[N_S5_PARAMS:]
    o_ref, str_ref, sti_ref = refs[n_cast:n_cast + 3]
    n_lazy = len(S5_WEIGHTS) + 1 if lazy_layers else 0
    n_scratch = 5 + (4 if swap_bt else 0) + n_lazy
    scratch = refs[-n_scratch:]
    hpr_ref, hpi_ref, hr_ref, hi_ref, y_ref = scratch[:5]
    _cast_slabs(refs[:n_cast], refs[n_cast + 3:-n_scratch])
    step = pl.program_id(0)
    n_tiles = pl.num_programs(0)

    w_copies = []
    if lazy_layers:
        w_sem = scratch[-1]
        for k, (p, layer) in enumerate(zip(S5_WEIGHTS, lazy_layers)):
            w_copies.append(pltpu.make_async_copy(params[p].at[layer], scratch[-n_lazy + k],
                                                  w_sem.at[k]))
            params[p] = scratch[-n_lazy + k]
        for cp in w_copies:
            cp.start()

    def need(*which):
        for k in which:
            if w_copies:
                w_copies[k].wait()

    (npre_ref, win_ref, l4r_ref, l4i_ref, wur_ref, wui_ref, vc_ref, kloc_ref, dsk_ref,
     w1_ref, b1_ref, w2_ref, b2_ref, wout_ref, npost_ref) = params

    if swap_bt:
        xbuf_ref, obuf_ref, in_sem, out_sem = scratch[5:9]
        slot = step % 2

        def tile_copies(tile, sl, fetch):
            seq_rows = pl.ds(tile * n_steps, n_steps)
            if fetch:
                return [pltpu.make_async_copy(x_ref.at[b, seq_rows, :], xbuf_ref.at[sl, :, b, :],
                                              in_sem.at[sl, b]) for b in range(n_seq)]
            return [pltpu.make_async_copy(obuf_ref.at[sl, :, b, :], o_ref.at[b, seq_rows, :],
                                          out_sem.at[sl, b]) for b in range(n_seq)]

        @pl.when(step == 0)
        def _():
            for cp in tile_copies(0, 0, True):
                cp.start()

        @pl.when(step + 1 < n_tiles)
        def _():
            for cp in tile_copies(step + 1, 1 - slot, True):
                cp.start()

        for cp in tile_copies(step, slot, True):
            cp.wait()

        @pl.when(step >= 2)
        def _():
            for cp in tile_copies(step - 2, slot, False):
                cp.wait()

    @pl.when(step == 0)
    def _():
        if has_h0:
            hr_ref[...] = h0r_ref[...]
            hi_ref[...] = h0i_ref[...]
        else:
            hr_ref[...] = jnp.zeros_like(hr_ref)
            hi_ref[...] = jnp.zeros_like(hi_ref)

    d = npre_ref.shape[-1]
    width = win_ref.shape[1] // 2
    c_rows = PIECE_ROWS
    c_steps = c_rows // n_seq
    n_pieces = n_steps // c_steps
    x_c, xb_c, z_c = [], [], []
    need(0)
    for c in range(n_pieces):
        if swap_bt:
            xc = xbuf_ref.at[slot][c * c_steps:(c + 1) * c_steps].reshape(c_rows, d)
        else:
            xc = x_ref[c * c_rows:(c + 1) * c_rows, :]
        hn = _rmsnorm(xc, npre_ref[...]).astype(BF16)
        x_c.append(xc)
        xb_c.append(_dot(hn, win_ref[:, :width]))
        z_c.append(_dot(hn, win_ref[:, width:]))
    xb = jnp.concatenate(xb_c, axis=0)

    n_blocks = n_steps // BLOCK
    rows = n_blocks * n_seq
    x4 = xb.reshape(n_blocks, BLOCK, n_seq, width)
    xi = [x4[:, i].reshape(rows, width) for i in range(BLOCK)]

    sub = V7X_SUBLANES
    pair = 2 * sub
    n_q, _, n_cols = wur_ref.shape
    n_v = width // V7X_LANES
    half = V7X_LANES // 2
    low = lax.broadcasted_iota(jnp.int32, (rows, V7X_LANES), 1) < half

    lhs_q = []
    need(1, 2)
    for v in range(n_v):
        cols = slice(v * V7X_LANES, (v + 1) * V7X_LANES)
        p = [a[:, cols] for a in xi]
        pr = [pltpu.roll(a, half, 1) for a in p]
        lhs_lo = jnp.concatenate([jnp.where(low, p[i], pr[i + 1]) for i in range(0, BLOCK, 2)], axis=1)
        lhs_hi = jnp.concatenate([jnp.where(low, pr[i], p[i + 1]) for i in range(0, BLOCK, 2)], axis=1)
        for q, lhs in ((2 * v, lhs_lo.astype(BF16)), (2 * v + 1, lhs_hi.astype(BF16))):
            lanes = pl.ds(q * n_cols, n_cols)
            wr = _dot(lhs, wur_ref[q])
            wi = _dot(lhs, wui_ref[q])
            if n_seq == sub:
                ar = jnp.broadcast_to(l4r_ref[:, lanes], (sub, n_cols))
                ai = jnp.broadcast_to(l4i_ref[:, lanes], (sub, n_cols))
                hr, hi = hr_ref[:, lanes], hi_ref[:, lanes]
                for k in range(n_blocks // 2):
                    prv_r, prv_i = [], []
                    for r0 in (2 * k * sub, (2 * k + 1) * sub):
                        prv_r.append(hr)
                        prv_i.append(hi)
                        hr, hi = (ar * hr - ai * hi + wr[r0:r0 + sub],
                                  ar * hi + ai * hr + wi[r0:r0 + sub])
                    hpr_ref[k * pair:(k + 1) * pair, lanes] = jnp.concatenate(prv_r, axis=0).astype(BF16)
                    hpi_ref[k * pair:(k + 1) * pair, lanes] = jnp.concatenate(prv_i, axis=0).astype(BF16)
                hr_ref[:, lanes] = hr
                hi_ref[:, lanes] = hi
            else:
                ar = jnp.broadcast_to(l4r_ref[:, lanes], (pair, n_cols))
                ai = jnp.broadcast_to(l4i_ref[:, lanes], (pair, n_cols))
                for m in range(n_seq // pair):
                    srows = pl.ds(m * pair, pair)
                    hr, hi = hr_ref[srows, lanes], hi_ref[srows, lanes]
                    for blk in range(n_blocks):
                        r0 = blk * n_seq + m * pair
                        hpr_ref[r0:r0 + pair, lanes] = hr.astype(BF16)
                        hpi_ref[r0:r0 + pair, lanes] = hi.astype(BF16)
                        hr, hi = (ar * hr - ai * hi + wr[r0:r0 + pair],
                                  ar * hi + ai * hr + wi[r0:r0 + pair])
                    hr_ref[srows, lanes] = hr
                    hi_ref[srows, lanes] = hi
            lhs_q.append(lhs)

    need(3, 4)
    for v in range(n_v):
        cols = slice(v * V7X_LANES, (v + 1) * V7X_LANES)
        yq = []
        for q in (2 * v, 2 * v + 1):
            lanes = pl.ds(q * n_cols, n_cols)
            st = jnp.concatenate([hpr_ref[:, lanes], hpi_ref[:, lanes]], axis=1)
            yq.append(_dot(st, vc_ref[q]) + _dot(lhs_q[q], kloc_ref[q]))
        ys = []
        for i in range(BLOCK):
            lo_q, hi_q = (a[:, (i // 2) * V7X_LANES:(i // 2 + 1) * V7X_LANES] for a in yq)
            if i % 2 == 0:
                yv = jnp.where(low, lo_q, pltpu.roll(hi_q, half, 1))
            else:
                yv = jnp.where(low, pltpu.roll(lo_q, half, 1), hi_q)
            ys.append(jax.nn.gelu(yv + dsk_ref[:, cols] * xi[i][:, cols]))
        y = jnp.stack([a.reshape(n_blocks, n_seq, V7X_LANES) for a in ys], axis=1)
        y_ref[:, cols] = y.reshape(n_steps * n_seq, V7X_LANES).astype(BF16)

    need(5, 6, 7)
    for c in range(n_pieces):
        y = y_ref[c * c_rows:(c + 1) * c_rows, :]
        g = (_dot(y, w1_ref[...]) + b1_ref[...]) * jax.nn.sigmoid(_dot(y, w2_ref[...]) + b2_ref[...])
        out = _dot((g * _silu(z_c[c])).astype(BF16), wout_ref[...])
        xn = x_c[c] + _rmsnorm(out, npost_ref[...])
        if swap_bt:
            obuf_ref.at[slot][c * c_steps:(c + 1) * c_steps] = xn.reshape(c_steps, n_seq, d)
        else:
            o_ref[c * c_rows:(c + 1) * c_rows, :] = xn

    if swap_bt:
        for cp in tile_copies(step, slot, False):
            cp.start()

    @pl.when(step == n_tiles - 1)
    def _():
        str_ref[...] = hr_ref[...]
        sti_ref[...] = hi_ref[...]
        if swap_bt:
            for cp in tile_copies(step, slot, False):
                cp.wait()

    if swap_bt:
        @pl.when(jnp.logical_and(step == n_tiles - 1, n_tiles >= 2))
        def _():
            for cp in tile_copies(step - 1, 1 - slot, False):
                cp.wait()


def _s5_layer(x, h0, layer_b, layer, npre, win, core_consts, out_consts, npost, n_seq, n_steps,
              cast_jobs=()):
    swap_bt = x.ndim == 3
    d = x.shape[-1]
    exp_b = win[0].shape[2] // 2
    n_state = core_consts[0].shape[-1]
    tile = n_seq * n_steps
    n_tiles = x.size // (tile * d)
    rows = tile // BLOCK
    has_h0 = h0 is not None
    assert n_steps % BLOCK == 0 and (n_seq > V7X_SUBLANES or (n_steps // BLOCK) % 2 == 0)
    assert tile % PIECE_ROWS == 0 and PIECE_ROWS % n_seq == 0
    params_b = [win] + [(c, layer_b) for c in core_consts] + list(out_consts)
    assert len(params_b) + 2 == N_S5_PARAMS
    c_in, c_out, c_shapes = _cast_specs(cast_jobs, n_tiles)
    args = ([x] + (list(h0) if has_h0 else []) + [npre] + [a for a, _ in params_b] + [npost]
            + [w for w, _ in cast_jobs])
    if swap_bt:
        io_spec = pl.BlockSpec(memory_space=pl.ANY)
        io_scratch = [pltpu.VMEM((2, n_steps, n_seq, d), F32), pltpu.VMEM((2, n_steps, n_seq, d), F32),
                      pltpu.SemaphoreType.DMA((2, n_seq)), pltpu.SemaphoreType.DMA((2, n_seq))]
    else:
        io_spec = pl.BlockSpec((tile, d), lambda i: (i, 0))
        io_scratch = []
    lazy = n_tiles == 1
    lazy_idx = [p - 1 for p in S5_WEIGHTS]
    hbm = pl.BlockSpec(memory_space=pl.ANY)
    in_specs = ([io_spec] + [_layer_spec(a, layer_b) for a in (h0 if has_h0 else ())]
                + [_layer_spec(npre, layer)]
                + [hbm if lazy and k in lazy_idx else _layer_spec(a, l)
                   for k, (a, l) in enumerate(params_b)]
                + [_layer_spec(npost, layer)] + c_in)
    lazy_scratch = ([pltpu.VMEM(params_b[k][0].shape[1:], params_b[k][0].dtype) for k in lazy_idx]
                    + [pltpu.SemaphoreType.DMA((len(lazy_idx),))]) if lazy else []
    st_spec = pl.BlockSpec((n_seq, n_state), lambda i: (0, 0))
    st_shape = jax.ShapeDtypeStruct((n_seq, n_state), F32)
    footprint = (_layer_bytes(*[a for a, _ in params_b]) + (_layer_bytes(*h0) if has_h0 else 0)
                 + 4 * tile * d * 4 + 2 * rows * n_state * 2 + 8 * n_seq * n_state * 4
                 + tile * exp_b * (2 + 6 * 4)
                 + 6 * sum(w[0].size for w, _ in cast_jobs) // n_tiles * 2)
    res = pl.pallas_call(
        functools.partial(_s5_layer_kernel, n_seq=n_seq, n_steps=n_steps, has_h0=has_h0,
                          swap_bt=swap_bt, n_cast=len(cast_jobs),
                          lazy_layers=tuple(params_b[k][1] for k in lazy_idx) if lazy else ()),
        grid=(n_tiles,),
        in_specs=in_specs,
        out_specs=[io_spec, st_spec, st_spec] + c_out,
        out_shape=[jax.ShapeDtypeStruct(x.shape, F32), st_shape, st_shape] + c_shapes,
        scratch_shapes=[pltpu.VMEM((rows, n_state), BF16), pltpu.VMEM((rows, n_state), BF16),
                        pltpu.VMEM((n_seq, n_state), F32), pltpu.VMEM((n_seq, n_state), F32),
                        pltpu.VMEM((tile, exp_b), BF16)] + io_scratch + lazy_scratch,
        compiler_params=pltpu.CompilerParams(
            dimension_semantics=("arbitrary",),
            vmem_limit_bytes=_vmem_limit(footprint)),
        name="s5_layer_sample" if has_h0 else "s5_layer_prompt",
    )(*args)
    return res[0], res[1], res[2], res[3:]


def kernel(x_prompt, x_sample, state_ssm_re, state_ssm_im, norm_pre, norm_post,
           w_in_a, ln_v_g, ln_v_b, w_s, b_s, w_out_a,
           w_in_b, a_re, a_im, log_dt, b_re, b_im, c_re, c_im, d_skip,
           w_glu1, b_glu1, w_glu2, b_glu2, w_out_b):
    n_batch, seq_len, d = x_prompt.shape
    n_dec, dec_len, _ = x_sample.shape
    depth = norm_pre.shape[0]
    n_ssm, n_groups, state_p = a_re.shape
    n_state = n_groups * state_p
    assert seq_len % ROW_TILE == 0 and ROW_TILE % CHUNK == 0 and seq_len % SCAN_STEPS == 0
    assert n_batch == V7X_SUBLANES and n_dec % (2 * V7X_SUBLANES) == 0
    assert dec_len <= CHUNK and dec_len % BLOCK == 0
    assert state_p == STATE_P and b_re.shape[-1] == SSM_GROUP and w_s.shape[1] == N_HEADS

    rows3 = lambda a: a.reshape(a.shape[0], 1, -1)
    npre, npost = rows3(norm_pre), rows3(norm_post)
    lng, lnb = rows3(ln_v_g), rows3(ln_v_b)
    wa16 = ((w_in_a[:1].astype(BF16), 0), (w_out_a[:1].astype(BF16), 0))
    bst = jnp.swapaxes(b_s, 1, 2)
    ws_dec = w_s[:, :, :dec_len, :dec_len].reshape(-1)
    bs_dec = b_s[:, :, :dec_len].reshape(-1)

    b1, b2 = rows3(b_glu1), rows3(b_glu2)
    lanes_gp = lambda a, perm: jnp.transpose(a, perm).reshape(n_ssm, SSM_GROUP, n_state)
    l4r, l4i, wur, wui, vcw, kloc = _s5_prep(
        rows3(a_re), rows3(a_im), rows3(jnp.repeat(log_dt, state_p, axis=1)),
        lanes_gp(b_re, (0, 3, 1, 2)), lanes_gp(b_im, (0, 3, 1, 2)),
        lanes_gp(c_re, (0, 2, 1, 3)), lanes_gp(c_im, (0, 2, 1, 3)))
    core_consts = (l4r, l4i, wur, wui, vcw, kloc, rows3(d_skip))
    h0 = (state_ssm_re.reshape(n_ssm, n_dec, n_state), state_ssm_im.reshape(n_ssm, n_dec, n_state))

    xp = x_prompt.reshape(n_batch * seq_len, d)
    xs = jnp.transpose(x_sample, (1, 0, 2)).reshape(dec_len * n_dec, d)

    v_rows, st_p_re, st_p_im, st_s_re, st_s_im = [], [], [], [], []
    for i in range(depth):
        j = i // 2
        if i % 2 == 0:
            jobs = [(w, j) for w in (w_in_b, w_glu1, w_glu2, w_out_b)] if i + 1 < depth else []
            xp, wb16 = _gmlp_prompt(xp, j, i, npre, npost, lng, lnb, *wa16, w_s, bst, jobs)
            xs, v = _gmlp_sample(xs, n_dec, dec_len, j, i, npre, npost, lng, lnb, *wa16,
                                 ws_dec, bs_dec)
            v_rows.append(v)
        else:
            win = (wb16[0], 0)
            out_consts = ((wb16[1], 0), (b1, j), (wb16[2], 0), (b2, j), (wb16[3], 0))
            jobs = [(w_in_a, j + 1), (w_out_a, j + 1)] if i + 1 < depth else []
            xp3, sr, si, wa16 = _s5_layer(xp.reshape(n_batch, seq_len, d), None, j, i, npre, win,
                                          core_consts, out_consts, npost, n_batch, SCAN_STEPS, jobs)
            wa16 = tuple((w, 0) for w in wa16)
            xp = xp3.reshape(n_batch * seq_len, d)
            st_p_re.append(sr)
            st_p_im.append(si)
            xs, sr, si, _ = _s5_layer(xs, h0, j, i, npre, win, core_consts, out_consts, npost,
                                      n_dec, dec_len)
            st_s_re.append(sr)
            st_s_im.append(si)

    y_prompt = xp.reshape(n_batch, seq_len, d)
    y_sample = jnp.transpose(xs.reshape(dec_len, n_dec, d), (1, 0, 2))
    chunk_v = jnp.stack(v_rows)
    states = lambda parts, n: jnp.stack(parts).reshape(len(parts), n, n_groups, state_p)
    return (y_prompt, y_sample, chunk_v, states(st_p_re, n_batch), states(st_p_im, n_batch),
            states(st_s_re, n_dec), states(st_s_im, n_dec))
```

```python
import functools

import jax
import jax.numpy as jnp
from jax import lax
from jax.experimental import pallas as pl
from jax.experimental.pallas import tpu as pltpu

EPS = 1e-6
CHUNK = 128
N_HEADS = 8
SSM_GROUP = 16
STATE_P = 64

V7X_LANES = 128
V7X_SUBLANES = 8
V7X_MXU_DIM = 256
V7X_VMEM_BYTES = 64 * 1024 * 1024

ROW_TILE = 1024
SCAN_STEPS = 64
BLOCK = 4
PIECE_ROWS = 256

BF16 = jnp.bfloat16
F32 = jnp.float32


def _dot(a, b):
    return jnp.dot(a, b, preferred_element_type=F32)


def _rmsnorm(x, g):
    ms = jnp.mean(x * x, axis=-1, keepdims=True)
    return x * lax.rsqrt(ms + EPS) * g


def _layernorm(x, g, b):
    mu = jnp.mean(x, axis=-1, keepdims=True)
    xc = x - mu
    var = jnp.mean(xc * xc, axis=-1, keepdims=True)
    return xc * lax.rsqrt(var + EPS) * g + b


def _silu(z):
    return z * jax.nn.sigmoid(z)


def _div_pow2(x, n):
    assert n & (n - 1) == 0
    return lax.shift_right_logical(x, n.bit_length() - 1)


def _vmem_limit(nbytes):
    return int(min(V7X_VMEM_BYTES - (4 << 20), nbytes + (12 << 20)))


def _layer_spec(stacked, layer):
    tail = stacked.shape[1:]
    return pl.BlockSpec((None,) + tail, lambda *_: (layer,) + (0,) * len(tail),
                        pipeline_mode=pl.Buffered(1))


def _layer_bytes(*stacked):
    return sum(a[0].size * a.dtype.itemsize for a in stacked)


def _cast_specs(jobs, n_steps):
    in_specs, out_specs, out_shapes = [], [], []
    for w, layer in jobs:
        _, rows, cols = w.shape
        slab = rows // n_steps
        assert rows % n_steps == 0 and slab % (2 * V7X_SUBLANES) == 0
        in_specs.append(pl.BlockSpec((None, slab, cols), lambda i, layer=layer: (layer, i, 0)))
        out_specs.append(pl.BlockSpec((None, slab, cols), lambda i: (0, i, 0)))
        out_shapes.append(jax.ShapeDtypeStruct((1, rows, cols), BF16))
    return in_specs, out_specs, out_shapes


def _cast_slabs(in_refs, out_refs):
    for src, dst in zip(in_refs, out_refs):
        dst[...] = src[...].astype(BF16)


def _row_spec(rows, width):
    return pl.BlockSpec((rows, width), lambda i: (i, 0))


_SMEM = pl.BlockSpec(memory_space=pltpu.SMEM)


def _gmlp_front(x, npre_ref, lng_ref, lnb_ref, win_ref, exp_a):
    hn, v = [], []
    for r0 in range(0, x.shape[0], PIECE_ROWS):
        hn.append(_rmsnorm(x[r0:r0 + PIECE_ROWS], npre_ref[...]).astype(BF16))
        v.append(_layernorm(_dot(hn[-1], win_ref[:, exp_a:2 * exp_a]), lng_ref[...], lnb_ref[...]))
    return jnp.concatenate(hn, axis=0), jnp.concatenate(v, axis=0)


def _gmlp_back(x, gated_ref, wout_ref, npost_ref, o_ref):
    for r0 in range(0, x.shape[0], PIECE_ROWS):
        r = slice(r0, r0 + PIECE_ROWS)
        out = _dot(gated_ref[r, :], wout_ref[...])
        o_ref[r, :] = x[r] + _rmsnorm(out, npost_ref[...])


def _gmlp_prompt_kernel(x_ref, npre_ref, npost_ref, lng_ref, lnb_ref, win_ref, wout_ref,
                        ws_ref, bst_ref, *refs, n_cast, own_layer):
    o_ref = refs[n_cast]
    if own_layer is None:
        gated_ref = refs[-1]
        _cast_slabs(refs[:n_cast], refs[n_cast + 1:-1])
    else:
        gated_ref, win_v, wout_v, stage_w, stage_o, sem = refs[-6:]
        win16_hbm, wout16_hbm = refs[-8:-6]
        _cast_slabs(refs[:n_cast], refs[n_cast + 1:-8])
        step = pl.program_id(0)
        out_copies = [pltpu.make_async_copy(win_v, win16_hbm.at[0], sem.at[4]),
                      pltpu.make_async_copy(wout_v, wout16_hbm.at[0], sem.at[5])]

        def stream(src_of, dst_of, stage, sem0, n_chunks):
            cps = [pltpu.make_async_copy(src_of(k), stage.at[k % 2], sem.at[sem0 + k % 2])
                   for k in range(n_chunks)]
            cps[0].start()
            for k in range(n_chunks):
                if k + 1 < n_chunks:
                    cps[k + 1].start()
                cps[k].wait()
                dst_of(k)[...] = stage[k % 2].astype(BF16)

        @pl.when(step == 0)
        def _():
            cw, ro = stage_w.shape[2], stage_o.shape[1]
            stream(lambda k: win_ref.at[own_layer, :, pl.ds(k * cw, cw)],
                   lambda k: win_v.at[:, pl.ds(k * cw, cw)], stage_w, 0, win_v.shape[1] // cw)
            stream(lambda k: wout_ref.at[own_layer, pl.ds(k * ro, ro), :],
                   lambda k: wout_v.at[pl.ds(k * ro, ro), :], stage_o, 2, wout_v.shape[0] // ro)
            for cp in out_copies:
                cp.start()

        @pl.when(step == pl.num_programs(0) - 1)
        def _():
            for cp in out_copies:
                cp.wait()

        win_ref, wout_ref = win_v, wout_v
    rows = x_ref.shape[0]
    exp_a = wout_ref.shape[0]
    hd = exp_a // N_HEADS
    x = x_ref[...]
    hn, v = _gmlp_front(x, npre_ref, lng_ref, lnb_ref, win_ref, exp_a)
    vb = v.astype(BF16)
    causal = (lax.broadcasted_iota(jnp.int32, (CHUNK, CHUNK), 1)
              <= lax.broadcasted_iota(jnp.int32, (CHUNK, CHUNK), 0))
    for h in range(N_HEADS):
        lo = h * hd
        wsh = jnp.where(causal, ws_ref[h], 0.0).astype(BF16)
        bias = bst_ref[:, h:h + 1]
        u = _dot(hn, win_ref[:, lo:lo + hd])
        z = _dot(hn, win_ref[:, 2 * exp_a + lo:2 * exp_a + lo + hd])
        s = jnp.concatenate(
            [_dot(wsh, vb[c * CHUNK:(c + 1) * CHUNK, lo:lo + hd]) + bias
             for c in range(rows // CHUNK)], axis=0)
        gated_ref[:, lo:lo + hd] = (u * s * _silu(z)).astype(BF16)
    _gmlp_back(x, gated_ref, wout_ref, npost_ref, o_ref)


def _gmlp_sample_kernel(ws_ref, bs_ref, x_ref, npre_ref, npost_ref, lng_ref, lnb_ref, win_hbm,
                        wout_hbm, o_ref, v_hbm, gated_ref, win_ref, wout_ref, w_sem, v_ref, v_sem,
                        *, n_seq, seq_len, layer, w_layers):
    exp_a = wout_ref.shape[0]
    hd = exp_a // N_HEADS
    col_parts = [slice(exp_a, 2 * exp_a), slice(0, exp_a), slice(2 * exp_a, 3 * exp_a)]
    copies = [pltpu.make_async_copy(win_hbm.at[w_layers[0], :, p], win_ref.at[:, p], w_sem.at[k])
              for k, p in enumerate(col_parts)]
    copies.append(pltpu.make_async_copy(wout_hbm.at[w_layers[1]], wout_ref, w_sem.at[len(col_parts)]))
    for cp in copies:
        cp.start()
    x = x_ref[...]
    copies[0].wait()
    hn, v = _gmlp_front(x, npre_ref, lng_ref, lnb_ref, win_ref, exp_a)
    v_ref[...] = v
    v_copies = [pltpu.make_async_copy(v_ref.at[pl.ds(t * n_seq, n_seq), :], v_hbm.at[:, t, :],
                                      v_sem.at[t]) for t in range(seq_len)]
    for cp in v_copies:
        cp.start()
    copies[1].wait()
    copies[2].wait()
    for h in range(N_HEADS):
        lo = h * hd
        u = _dot(hn, win_ref[:, lo:lo + hd])
        z = _dot(hn, win_ref[:, 2 * exp_a + lo:2 * exp_a + lo + hd])
        vt = [v[t * n_seq:(t + 1) * n_seq, lo:lo + hd] for t in range(seq_len)]
        parts = []
        for t in range(seq_len):
            b_idx = (layer * N_HEADS + h) * seq_len + t
            s = ws_ref[b_idx * seq_len] * vt[0]
            for t2 in range(1, t + 1):
                s = s + ws_ref[b_idx * seq_len + t2] * vt[t2]
            parts.append(s + bs_ref[b_idx])
        s = jnp.concatenate(parts, axis=0)
        gated_ref[:, lo:lo + hd] = (u * s * _silu(z)).astype(BF16)
    copies[3].wait()
    _gmlp_back(x, gated_ref, wout_ref, npost_ref, o_ref)
    for cp in v_copies:
        cp.wait()


def _gmlp_prompt(x2d, layer_a, layer, npre, npost, lng, lnb, win, wout, ws, bst, cast_jobs):
    rows, d = x2d.shape
    (win, win_l), (wout, wout_l) = win, wout
    own = win.dtype != BF16
    exp_a = wout.shape[1]
    n_steps = rows // ROW_TILE
    c_in, c_out, c_shapes = _cast_specs(cast_jobs, n_steps)
    hbm = pl.BlockSpec(memory_space=pl.ANY)
    stage_cols, stage_rows = 4 * V7X_LANES, 4 * V7X_LANES
    own_scratch = [pltpu.VMEM(win.shape[1:], BF16), pltpu.VMEM(wout.shape[1:], BF16),
                   pltpu.VMEM((2, win.shape[1], stage_cols), F32),
                   pltpu.VMEM((2, stage_rows, wout.shape[2]), F32),
                   pltpu.SemaphoreType.DMA((6,))] if own else []
    own_shapes = [jax.ShapeDtypeStruct((1,) + w.shape[1:], BF16) for w in (win, wout)] if own else []
    footprint = ((win[0].size + wout[0].size) * 2 + _layer_bytes(ws) + 4 * ROW_TILE * d * 4
                 + ROW_TILE * exp_a * (2 + 4 + 2 + 8)
                 + 6 * sum(w[0].size for w, _ in cast_jobs) // n_steps * 2
                 + (2 * (win.shape[1] * stage_cols + stage_rows * wout.shape[2]) * 4 if own else 0))
    res = pl.pallas_call(
        functools.partial(_gmlp_prompt_kernel, n_cast=len(cast_jobs),
                          own_layer=win_l if own else None),
        grid=(n_steps,),
        in_specs=[
            _row_spec(ROW_TILE, d),
            _layer_spec(npre, layer), _layer_spec(npost, layer),
            _layer_spec(lng, layer_a), _layer_spec(lnb, layer_a),
            hbm if own else _layer_spec(win, win_l), hbm if own else _layer_spec(wout, wout_l),
            _layer_spec(ws, layer_a), _layer_spec(bst, layer_a),
        ] + c_in,
        out_specs=[_row_spec(ROW_TILE, d)] + c_out + [hbm] * len(own_shapes),
        out_shape=[jax.ShapeDtypeStruct((rows, d), F32)] + c_shapes + own_shapes,
        scratch_shapes=[pltpu.VMEM((ROW_TILE, exp_a), BF16)] + own_scratch,
        compiler_params=pltpu.CompilerParams(
            dimension_semantics=("arbitrary",) if own else ("parallel",),
            vmem_limit_bytes=_vmem_limit(footprint)),
        name="gmlp_prompt",
    )(x2d, npre, npost, lng, lnb, win, wout, ws, bst, *[w for w, _ in cast_jobs])
    n_jobs = len(cast_jobs)
    return res[0], res[1:1 + n_jobs], res[1 + n_jobs:]


def _gmlp_sample(xs, n_seq, seq_len, layer_a, layer, npre, npost, lng, lnb, win, wout,
                 ws_flat, bs_flat):
    rows, d = xs.shape
    (win, win_l), (wout, wout_l) = win, wout
    exp_a = wout.shape[1]
    footprint = _layer_bytes(win, wout) + 4 * rows * d * 4 + rows * exp_a * (2 + 8 + 4 + 8)
    hbm = pl.BlockSpec(memory_space=pl.ANY)
    return pl.pallas_call(
        functools.partial(_gmlp_sample_kernel, n_seq=n_seq, seq_len=seq_len, layer=layer_a,
                          w_layers=(win_l, wout_l)),
        grid=(1,),
        in_specs=[_SMEM, _SMEM, _row_spec(rows, d),
                  _layer_spec(npre, layer), _layer_spec(npost, layer),
                  _layer_spec(lng, layer_a), _layer_spec(lnb, layer_a), hbm, hbm],
        out_specs=[_row_spec(rows, d), hbm],
        out_shape=[jax.ShapeDtypeStruct((rows, d), F32),
                   jax.ShapeDtypeStruct((n_seq, seq_len, exp_a), F32)],
        scratch_shapes=[pltpu.VMEM((rows, exp_a), BF16), pltpu.VMEM(win.shape[1:], BF16),
                        pltpu.VMEM(wout.shape[1:], BF16), pltpu.SemaphoreType.DMA((4,)),
                        pltpu.VMEM((rows, exp_a), F32), pltpu.SemaphoreType.DMA((seq_len,))],
        compiler_params=pltpu.CompilerParams(
            dimension_semantics=("arbitrary",),
            vmem_limit_bytes=_vmem_limit(footprint)),
        name="gmlp_sample",
    )(ws_flat, bs_flat, xs, npre, npost, lng, lnb, win, wout)


def _s5_prep_kernel(are_ref, aim_ref, ldt_ref, btr_ref, bti_ref, ctr_ref, cti_ref,
                    l4r_ref, l4i_ref, wur_ref, wui_ref, vc_ref, kloc_ref):
    dt = jnp.exp(ldt_ref[...])
    ar = are_ref[...]
    ai = aim_ref[...]
    mag = jnp.exp(dt * ar)
    ang = dt * ai
    abr = mag * jnp.cos(ang)
    abi = mag * jnp.sin(ang)
    nr = abr - 1.0
    ni = abi
    den = ar * ar + ai * ai
    cre = (nr * ar + ni * ai) / den
    cim = (ni * ar - nr * ai) / den
    btr = btr_ref[...]
    bti = bti_ref[...]
    bbr = cre * btr - cim * bti
    bbi = cre * bti + cim * btr

    def cmul(xr, xi, yr, yi):
        return xr * yr - xi * yi, xr * yi + xi * yr

    lam = [(jnp.ones_like(abr), jnp.zeros_like(abi)), (abr, abi)]
    for _ in range(BLOCK - 1):
        lam.append(cmul(*lam[-1], abr, abi))
    l4r_ref[...] = lam[BLOCK][0]
    l4i_ref[...] = lam[BLOCK][1]

    n_q, k_rows, n_cols = wur_ref.shape
    q_groups = n_cols // STATE_P
    q_rows = q_groups * SSM_GROUP
    same_q = (_div_pow2(lax.broadcasted_iota(jnp.int32, (q_rows, n_cols), 0), SSM_GROUP)
              == _div_pow2(lax.broadcasted_iota(jnp.int32, (q_rows, n_cols), 1), STATE_P))
    for ip in range(BLOCK):
        ur, ui = cmul(bbr, bbi, *lam[BLOCK - 1 - ip])
        for q in range(n_q):
            for src, dst in ((ur, wur_ref), (ui, wui_ref)):
                blk = jnp.concatenate([src[:, q * n_cols:(q + 1) * n_cols]] * q_groups, axis=0)
                dst[q, ip * q_rows:(ip + 1) * q_rows, :] = jnp.where(same_q, blk, 0.0).astype(BF16)

    ctr = ctr_ref[...]
    cti = cti_ref[...]
    cl = [cmul(ctr, cti, *lam[t]) for t in range(BLOCK + 1)]

    def tile_rows(a, q):
        blk = jnp.concatenate([a[:, q * n_cols:(q + 1) * n_cols]] * q_groups, axis=0)
        return jnp.where(same_q, blk, 0.0)

    half = V7X_LANES // 2
    low = lax.broadcasted_iota(jnp.int32, (q_rows, V7X_LANES), 1) < half
    for q in range(n_q):
        clm = [(tile_rows(c[0], q), tile_rows(c[1], q)) for c in cl]
        rows = [jnp.concatenate([clm[i + 1][0], -clm[i + 1][1]], axis=1) for i in range(BLOCK)]
        vc_ref[q] = jnp.concatenate(rows, axis=0).T.astype(BF16)
        bcat = jnp.concatenate([tile_rows(bbr, q), -tile_rows(bbi, q)], axis=1).astype(BF16)
        ccat = jnp.concatenate([jnp.concatenate(clm[t], axis=1) for t in range(BLOCK)], axis=0)
        kall = lax.dot_general(bcat, ccat.astype(BF16), (((1,), (1,)), ((), ())),
                               preferred_element_type=F32)
        k0, k1 = kall[:, :V7X_LANES], kall[:, V7X_LANES:]
        r0, r1 = pltpu.roll(k0, half, 1), pltpu.roll(k1, half, 1)
        zero = jnp.zeros_like(k0)
        shifted = [(k0, k1),
                   (jnp.where(low, zero, r0), jnp.where(low, r0, r1)),
                   (zero, k0),
                   (zero, jnp.where(low, zero, r0))]
        kloc_ref[q] = jnp.concatenate([jnp.concatenate(sh, axis=1) for sh in shifted],
                                      axis=0).astype(BF16)


def _s5_prep(are, aim, ldt, btr, bti, ctr, cti):
    n_layers, _, n_state = are.shape
    n_q = n_state // V7X_MXU_DIM

    def per_layer(shape):
        return pl.BlockSpec((None,) + shape, lambda l: (l,) + (0,) * len(shape))

    out_tails = [(1, n_state), (1, n_state),
                 (n_q, V7X_MXU_DIM, V7X_MXU_DIM), (n_q, V7X_MXU_DIM, V7X_MXU_DIM),
                 (n_q, 2 * V7X_MXU_DIM, V7X_MXU_DIM), (n_q, V7X_MXU_DIM, V7X_MXU_DIM)]
    out_dtypes = [F32, F32, BF16, BF16, BF16, BF16]
    args = (are, aim, ldt, btr, bti, ctr, cti)
    return pl.pallas_call(
        _s5_prep_kernel,
        grid=(n_layers,),
        in_specs=[per_layer(a.shape[1:]) for a in args],
        out_specs=[per_layer(t) for t in out_tails],
        out_shape=[jax.ShapeDtypeStruct((n_layers,) + t, dt) for t, dt in zip(out_tails, out_dtypes)],
        compiler_params=pltpu.CompilerParams(
            dimension_semantics=("parallel",),
            vmem_limit_bytes=_vmem_limit(2 * sum(
                int(jnp.dtype(dt).itemsize) * functools.reduce(lambda a, b: a * b, t)
                for t, dt in zip(out_tails, out_dtypes)))),
        name="s5_prep",
    )(*args)


N_S5_PARAMS = 15
S5_WEIGHTS = (1, 4, 5, 6, 7, 9, 11, 13)


def _s5_layer_kernel(*refs, n_seq, n_steps, has_h0, swap_bt, n_cast, lazy_layers):
    if has_h0:
        x_ref, h0r_ref, h0i_ref = refs[:3]
        refs = refs[3:]
    else:
        x_ref = refs[0]
        refs = refs[1:]
    params = list(refs[:N_S5_PARAMS])
    refs = refs[N_S5_PARAMS:]
    o_ref, str_ref, sti_ref = refs[n_cast:n_cast + 3]
    n_lazy = len(S5_WEIGHTS) + 1 if lazy_layers else 0
    n_scratch = 5 + (4 if swap_bt else 0) + n_lazy
    scratch = refs[-n_scratch:]
    hpr_ref, hpi_ref, hr_ref, hi_ref, y_ref = scratch[:5]
    _cast_slabs(refs[:n_cast], refs[n_cast + 3:-n_scratch])
    step = pl.program_id(0)
    n_tiles = pl.num_programs(0)

    w_copies = []
    if lazy_layers:
        w_sem = scratch[-1]
        for k, (p, layer) in enumerate(zip(S5_WEIGHTS, lazy_layers)):
            w_copies.append(pltpu.make_async_copy(params[p].at[layer], scratch[-n_lazy + k],
                                                  w_sem.at[k]))
            params[p] = scratch[-n_lazy + k]
        for cp in w_copies:
            cp.start()

    def need(*which):
        for k in which:
            if w_copies:
                w_copies[k].wait()

    (npre_ref, win_ref, l4r_ref, l4i_ref, wur_ref, wui_ref, vc_ref, kloc_ref, dsk_ref,
     w1_ref, b1_ref, w2_ref, b2_ref, wout_ref, npost_ref) = params

    if swap_bt:
        xbuf_ref, obuf_ref, in_sem, out_sem = scratch[5:9]
        slot = step % 2

        def tile_copies(tile, sl, fetch):
            seq_rows = pl.ds(tile * n_steps, n_steps)
            if fetch:
                return [pltpu.make_async_copy(x_ref.at[b, seq_rows, :], xbuf_ref.at[sl, :, b, :],
                                              in_sem.at[sl, b]) for b in range(n_seq)]
            return [pltpu.make_async_copy(obuf_ref.at[sl, :, b, :], o_ref.at[b, seq_rows, :],
                                          out_sem.at[sl, b]) for b in range(n_seq)]

        @pl.when(step == 0)
        def _():
            for cp in tile_copies(0, 0, True):
                cp.start()

        @pl.when(step + 1 < n_tiles)
        def _():
            for cp in tile_copies(step + 1, 1 - slot, True):
                cp.start()

        for cp in tile_copies(step, slot, True):
            cp.wait()

        @pl.when(step >= 2)
        def _():
            for cp in tile_copies(step - 2, slot, False):
                cp.wait()

    @pl.when(step == 0)
    def _():
        if has_h0:
            hr_ref[...] = h0r_ref[...]
            hi_ref[...] = h0i_ref[...]
        else:
            hr_ref[...] = jnp.zeros_like(hr_ref)
            hi_ref[...] = jnp.zeros_like(hi_ref)

    d = npre_ref.shape[-1]
    width = win_ref.shape[1] // 2
    c_rows = PIECE_ROWS
    c_steps = c_rows // n_seq
    n_pieces = n_steps // c_steps
    x_c, xb_c, z_c = [], [], []
    need(0)
    for c in range(n_pieces):
        if swap_bt:
            xc = xbuf_ref.at[slot][c * c_steps:(c + 1) * c_steps].reshape(c_rows, d)
        else:
            xc = x_ref[c * c_rows:(c + 1) * c_rows, :]
        hn = _rmsnorm(xc, npre_ref[...]).astype(BF16)
        x_c.append(xc)
        xb_c.append(_dot(hn, win_ref[:, :width]))
        z_c.append(_dot(hn, win_ref[:, width:]))
    xb = jnp.concatenate(xb_c, axis=0)

    n_blocks = n_steps // BLOCK
    rows = n_blocks * n_seq
    x4 = xb.reshape(n_blocks, BLOCK, n_seq, width)
    xi = [x4[:, i].reshape(rows, width) for i in range(BLOCK)]

    sub = V7X_SUBLANES
    pair = 2 * sub
    n_q, _, n_cols = wur_ref.shape
    n_v = width // V7X_LANES
    half = V7X_LANES // 2
    low = lax.broadcasted_iota(jnp.int32, (rows, V7X_LANES), 1) < half

    lhs_q = []
    need(1, 2)
    for v in range(n_v):
        cols = slice(v * V7X_LANES, (v + 1) * V7X_LANES)
        p = [a[:, cols] for a in xi]
        pr = [pltpu.roll(a, half, 1) for a in p]
        lhs_lo = jnp.concatenate([jnp.where(low, p[i], pr[i + 1]) for i in range(0, BLOCK, 2)], axis=1)
        lhs_hi = jnp.concatenate([jnp.where(low, pr[i], p[i + 1]) for i in range(0, BLOCK, 2)], axis=1)
        for q, lhs in ((2 * v, lhs_lo.astype(BF16)), (2 * v + 1, lhs_hi.astype(BF16))):
            lanes = pl.ds(q * n_cols, n_cols)
            wr = _dot(lhs, wur_ref[q])
            wi = _dot(lhs, wui_ref[q])
            if n_seq == sub:
                ar = jnp.broadcast_to(l4r_ref[:, lanes], (sub, n_cols))
                ai = jnp.broadcast_to(l4i_ref[:, lanes], (sub, n_cols))
                hr, hi = hr_ref[:, lanes], hi_ref[:, lanes]
                for k in range(n_blocks // 2):
                    prv_r, prv_i = [], []
                    for r0 in (2 * k * sub, (2 * k + 1) * sub):
                        prv_r.append(hr)
                        prv_i.append(hi)
                        hr, hi = (ar * hr - ai * hi + wr[r0:r0 + sub],
                                  ar * hi + ai * hr + wi[r0:r0 + sub])
                    hpr_ref[k * pair:(k + 1) * pair, lanes] = jnp.concatenate(prv_r, axis=0).astype(BF16)
                    hpi_ref[k * pair:(k + 1) * pair, lanes] = jnp.concatenate(prv_i, axis=0).astype(BF16)
                hr_ref[:, lanes] = hr
                hi_ref[:, lanes] = hi
            else:
                ar = jnp.broadcast_to(l4r_ref[:, lanes], (pair, n_cols))
                ai = jnp.broadcast_to(l4i_ref[:, lanes], (pair, n_cols))
                for m in range(n_seq // pair):
                    srows = pl.ds(m * pair, pair)
                    hr, hi = hr_ref[srows, lanes], hi_ref[srows, lanes]
                    for blk in range(n_blocks):
                        r0 = blk * n_seq + m * pair
                        hpr_ref[r0:r0 + pair, lanes] = hr.astype(BF16)
                        hpi_ref[r0:r0 + pair, lanes] = hi.astype(BF16)
                        hr, hi = (ar * hr - ai * hi + wr[r0:r0 + pair],
                                  ar * hi + ai * hr + wi[r0:r0 + pair])
                    hr_ref[srows, lanes] = hr
                    hi_ref[srows, lanes] = hi
            lhs_q.append(lhs)

    need(3, 4)
    for v in range(n_v):
        cols = slice(v * V7X_LANES, (v + 1) * V7X_LANES)
        yq = []
        for q in (2 * v, 2 * v + 1):
            lanes = pl.ds(q * n_cols, n_cols)
            st = jnp.concatenate([hpr_ref[:, lanes], hpi_ref[:, lanes]], axis=1)
            yq.append(_dot(st, vc_ref[q]) + _dot(lhs_q[q], kloc_ref[q]))
        ys = []
        for i in range(BLOCK):
            lo_q, hi_q = (a[:, (i // 2) * V7X_LANES:(i // 2 + 1) * V7X_LANES] for a in yq)
            if i % 2 == 0:
                yv = jnp.where(low, lo_q, pltpu.roll(hi_q, half, 1))
            else:
                yv = jnp.where(low, pltpu.roll(lo_q, half, 1), hi_q)
            ys.append(jax.nn.gelu(yv + dsk_ref[:, cols] * xi[i][:, cols]))
        y = jnp.stack([a.reshape(n_blocks, n_seq, V7X_LANES) for a in ys], axis=1)
        y_ref[:, cols] = y.reshape(n_steps * n_seq, V7X_LANES).astype(BF16)

    need(5, 6, 7)
    for c in range(n_pieces):
        y = y_ref[c * c_rows:(c + 1) * c_rows, :]
        g = (_dot(y, w1_ref[...]) + b1_ref[...]) * jax.nn.sigmoid(_dot(y, w2_ref[...]) + b2_ref[...])
        out = _dot((g * _silu(z_c[c])).astype(BF16), wout_ref[...])
        xn = x_c[c] + _rmsnorm(out, npost_ref[...])
        if swap_bt:
            obuf_ref.at[slot][c * c_steps:(c + 1) * c_steps] = xn.reshape(c_steps, n_seq, d)
        else:
            o_ref[c * c_rows:(c + 1) * c_rows, :] = xn

    if swap_bt:
        for cp in tile_copies(step, slot, False):
            cp.start()

    @pl.when(step == n_tiles - 1)
    def _():
        str_ref[...] = hr_ref[...]
        sti_ref[...] = hi_ref[...]
        if swap_bt:
            for cp in tile_copies(step, slot, False):
                cp.wait()

    if swap_bt:
        @pl.when(jnp.logical_and(step == n_tiles - 1, n_tiles >= 2))
        def _():
            for cp in tile_copies(step - 1, 1 - slot, False):
                cp.wait()


def _s5_layer(x, h0, layer_b, layer, npre, win, core_consts, out_consts, npost, n_seq, n_steps,
              cast_jobs=()):
    swap_bt = x.ndim == 3
    d = x.shape[-1]
    exp_b = win[0].shape[2] // 2
    n_state = core_consts[0].shape[-1]
    tile = n_seq * n_steps
    n_tiles = x.size // (tile * d)
    rows = tile // BLOCK
    has_h0 = h0 is not None
    assert n_steps % BLOCK == 0 and (n_seq > V7X_SUBLANES or (n_steps // BLOCK) % 2 == 0)
    assert tile % PIECE_ROWS == 0 and PIECE_ROWS % n_seq == 0
    params_b = [win] + [(c, layer_b) for c in core_consts] + list(out_consts)
    assert len(params_b) + 2 == N_S5_PARAMS
    c_in, c_out, c_shapes = _cast_specs(cast_jobs, n_tiles)
    args = ([x] + (list(h0) if has_h0 else []) + [npre] + [a for a, _ in params_b] + [npost]
            + [w for w, _ in cast_jobs])
    if swap_bt:
        io_spec = pl.BlockSpec(memory_space=pl.ANY)
        io_scratch = [pltpu.VMEM((2, n_steps, n_seq, d), F32), pltpu.VMEM((2, n_steps, n_seq, d), F32),
                      pltpu.SemaphoreType.DMA((2, n_seq)), pltpu.SemaphoreType.DMA((2, n_seq))]
    else:
        io_spec = pl.BlockSpec((tile, d), lambda i: (i, 0))
        io_scratch = []
    lazy = n_tiles == 1
    lazy_idx = [p - 1 for p in S5_WEIGHTS]
    hbm = pl.BlockSpec(memory_space=pl.ANY)
    in_specs = ([io_spec] + [_layer_spec(a, layer_b) for a in (h0 if has_h0 else ())]
                + [_layer_spec(npre, layer)]
                + [hbm if lazy and k in lazy_idx else _layer_spec(a, l)
                   for k, (a, l) in enumerate(params_b)]
                + [_layer_spec(npost, layer)] + c_in)
    lazy_scratch = ([pltpu.VMEM(params_b[k][0].shape[1:], params_b[k][0].dtype) for k in lazy_idx]
                    + [pltpu.SemaphoreType.DMA((len(lazy_idx),))]) if lazy else []
    st_spec = pl.BlockSpec((n_seq, n_state), lambda i: (0, 0))
    st_shape = jax.ShapeDtypeStruct((n_seq, n_state), F32)
    footprint = (_layer_bytes(*[a for a, _ in params_b]) + (_layer_bytes(*h0) if has_h0 else 0)
                 + 4 * tile * d * 4 + 2 * rows * n_state * 2 + 8 * n_seq * n_state * 4
                 + tile * exp_b * (2 + 6 * 4)
                 + 6 * sum(w[0].size for w, _ in cast_jobs) // n_tiles * 2)
    res = pl.pallas_call(
        functools.partial(_s5_layer_kernel, n_seq=n_seq, n_steps=n_steps, has_h0=has_h0,
                          swap_bt=swap_bt, n_cast=len(cast_jobs),
                          lazy_layers=tuple(params_b[k][1] for k in lazy_idx) if lazy else ()),
        grid=(n_tiles,),
        in_specs=in_specs,
        out_specs=[io_spec, st_spec, st_spec] + c_out,
        out_shape=[jax.ShapeDtypeStruct(x.shape, F32), st_shape, st_shape] + c_shapes,
        scratch_shapes=[pltpu.VMEM((rows, n_state), BF16), pltpu.VMEM((rows, n_state), BF16),
                        pltpu.VMEM((n_seq, n_state), F32), pltpu.VMEM((n_seq, n_state), F32),
                        pltpu.VMEM((tile, exp_b), BF16)] + io_scratch + lazy_scratch,
        compiler_params=pltpu.CompilerParams(
            dimension_semantics=("arbitrary",),
            vmem_limit_bytes=_vmem_limit(footprint)),
        name="s5_layer_sample" if has_h0 else "s5_layer_prompt",
    )(*args)
    return res[0], res[1], res[2], res[3:]


def kernel(x_prompt, x_sample, state_ssm_re, state_ssm_im, norm_pre, norm_post,
           w_in_a, ln_v_g, ln_v_b, w_s, b_s, w_out_a,
           w_in_b, a_re, a_im, log_dt, b_re, b_im, c_re, c_im, d_skip,
           w_glu1, b_glu1, w_glu2, b_glu2, w_out_b):
    n_batch, seq_len, d = x_prompt.shape
    n_dec, dec_len, _ = x_sample.shape
    depth = norm_pre.shape[0]
    n_ssm, n_groups, state_p = a_re.shape
    n_state = n_groups * state_p
    assert seq_len % ROW_TILE == 0 and ROW_TILE % CHUNK == 0 and seq_len % SCAN_STEPS == 0
    assert n_batch == V7X_SUBLANES and n_dec % (2 * V7X_SUBLANES) == 0
    assert dec_len <= CHUNK and dec_len % BLOCK == 0
    assert state_p == STATE_P and b_re.shape[-1] == SSM_GROUP and w_s.shape[1] == N_HEADS

    rows3 = lambda a: a.reshape(a.shape[0], 1, -1)
    npre, npost = rows3(norm_pre), rows3(norm_post)
    lng, lnb = rows3(ln_v_g), rows3(ln_v_b)
    wa16 = ((w_in_a, 0), (w_out_a, 0))
    bst = jnp.swapaxes(b_s, 1, 2)
    ws_dec = w_s[:, :, :dec_len, :dec_len].reshape(-1)
    bs_dec = b_s[:, :, :dec_len].reshape(-1)

    b1, b2 = rows3(b_glu1), rows3(b_glu2)
    lanes_gp = lambda a, perm: jnp.transpose(a, perm).reshape(n_ssm, SSM_GROUP, n_state)
    l4r, l4i, wur, wui, vcw, kloc = _s5_prep(
        rows3(a_re), rows3(a_im), rows3(jnp.repeat(log_dt, state_p, axis=1)),
        lanes_gp(b_re, (0, 3, 1, 2)), lanes_gp(b_im, (0, 3, 1, 2)),
        lanes_gp(c_re, (0, 2, 1, 3)), lanes_gp(c_im, (0, 2, 1, 3)))
    core_consts = (l4r, l4i, wur, wui, vcw, kloc, rows3(d_skip))
    h0 = (state_ssm_re.reshape(n_ssm, n_dec, n_state), state_ssm_im.reshape(n_ssm, n_dec, n_state))

    xp = x_prompt.reshape(n_batch * seq_len, d)
    xs = jnp.transpose(x_sample, (1, 0, 2)).reshape(dec_len * n_dec, d)

    v_rows, st_p_re, st_p_im, st_s_re, st_s_im = [], [], [], [], []
    for i in range(depth):
        j = i // 2
        if i % 2 == 0:
            jobs = [(w, j) for w in (w_in_b, w_glu1, w_glu2, w_out_b)] if i + 1 < depth else []
            xp, wb16, own16 = _gmlp_prompt(xp, j, i, npre, npost, lng, lnb, *wa16, w_s, bst, jobs)
            if own16:
                wa16 = tuple((w, 0) for w in own16)
            xs, v = _gmlp_sample(xs, n_dec, dec_len, j, i, npre, npost, lng, lnb, *wa16,
                                 ws_dec, bs_dec)
            v_rows.append(v)
        else:
            win = (wb16[0], 0)
            out_consts = ((wb16[1], 0), (b1, j), (wb16[2], 0), (b2, j), (wb16[3], 0))
            jobs = [(w_in_a, j + 1), (w_out_a, j + 1)] if i + 1 < depth else []
            xp3, sr, si, wa16 = _s5_layer(xp.reshape(n_batch, seq_len, d), None, j, i, npre, win,
                                          core_consts, out_consts, npost, n_batch, SCAN_STEPS, jobs)
            wa16 = tuple((w, 0) for w in wa16)
            xp = xp3.reshape(n_batch * seq_len, d)
            st_p_re.append(sr)
            st_p_im.append(si)
            xs, sr, si, _ = _s5_layer(xs, h0, j, i, npre, win, core_consts, out_consts, npost,
                                      n_dec, dec_len)
            st_s_re.append(sr)
            st_s_im.append(si)

    y_prompt = xp.reshape(n_batch, seq_len, d)
    y_sample = jnp.transpose(xs.reshape(dec_len, n_dec, d), (1, 0, 2))
    chunk_v = jnp.stack(v_rows)
    states = lambda parts, n: jnp.stack(parts).reshape(len(parts), n, n_groups, state_p)
    return (y_prompt, y_sample, chunk_v, states(st_p_re, n_batch), states(st_p_im, n_batch),
            states(st_s_re, n_dec), states(st_s_im, n_dec))
```

```python
import functools

import jax
import jax.numpy as jnp
from jax import lax
from jax.experimental import pallas as pl
from jax.experimental.pallas import tpu as pltpu

EPS = 1e-6
CHUNK = 128
N_HEADS = 8
SSM_GROUP = 16
STATE_P = 64

V7X_LANES = 128
V7X_SUBLANES = 8
V7X_MXU_DIM = 256
V7X_VMEM_BYTES = 64 * 1024 * 1024

ROW_TILE = 1024
SCAN_STEPS = 64
BLOCK = 4
PIECE_ROWS = 256

BF16 = jnp.bfloat16
F32 = jnp.float32


def _dot(a, b):
    return jnp.dot(a, b, preferred_element_type=F32)


def _rmsnorm(x, g):
    ms = jnp.mean(x * x, axis=-1, keepdims=True)
    return x * lax.rsqrt(ms + EPS) * g


def _layernorm(x, g, b):
    mu = jnp.mean(x, axis=-1, keepdims=True)
    xc = x - mu
    var = jnp.mean(xc * xc, axis=-1, keepdims=True)
    return xc * lax.rsqrt(var + EPS) * g + b


def _silu(z):
    return z * jax.nn.sigmoid(z)


def _div_pow2(x, n):
    assert n & (n - 1) == 0
    return lax.shift_right_logical(x, n.bit_length() - 1)


def _vmem_limit(nbytes):
    return int(min(V7X_VMEM_BYTES - (4 << 20), nbytes + (12 << 20)))


def _layer_spec(stacked, layer):
    tail = stacked.shape[1:]
    return pl.BlockSpec((None,) + tail, lambda *_: (layer,) + (0,) * len(tail),
                        pipeline_mode=pl.Buffered(1))


def _layer_bytes(*stacked):
    return sum(a[0].size * a.dtype.itemsize for a in stacked)


def _cast_specs(jobs, n_steps):
    in_specs, out_specs, out_shapes = [], [], []
    for w, layer in jobs:
        _, rows, cols = w.shape
        slab = rows // n_steps
        assert rows % n_steps == 0 and slab % (2 * V7X_SUBLANES) == 0
        in_specs.append(pl.BlockSpec((None, slab, cols), lambda i, layer=layer: (layer, i, 0)))
        out_specs.append(pl.BlockSpec((None, slab, cols), lambda i: (0, i, 0)))
        out_shapes.append(jax.ShapeDtypeStruct((1, rows, cols), BF16))
    return in_specs, out_specs, out_shapes


def _cast_slabs(in_refs, out_refs):
    for src, dst in zip(in_refs, out_refs):
        dst[...] = src[...].astype(BF16)


def _row_spec(rows, width):
    return pl.BlockSpec((rows, width), lambda i: (i, 0))


_SMEM = pl.BlockSpec(memory_space=pltpu.SMEM)


def _gmlp_front(x, npre_ref, lng_ref, lnb_ref, win_ref, exp_a):
    hn, v = [], []
    for r0 in range(0, x.shape[0], PIECE_ROWS):
        hn.append(_rmsnorm(x[r0:r0 + PIECE_ROWS], npre_ref[...]).astype(BF16))
        v.append(_layernorm(_dot(hn[-1], win_ref[:, exp_a:2 * exp_a]), lng_ref[...], lnb_ref[...]))
    return jnp.concatenate(hn, axis=0), jnp.concatenate(v, axis=0)


def _gmlp_back(x, gated_ref, wout_ref, npost_ref, o_ref):
    for r0 in range(0, x.shape[0], PIECE_ROWS):
        r = slice(r0, r0 + PIECE_ROWS)
        out = _dot(gated_ref[r, :], wout_ref[...])
        o_ref[r, :] = x[r] + _rmsnorm(out, npost_ref[...])


def _gmlp_prompt_kernel(x_ref, npre_ref, npost_ref, lng_ref, lnb_ref, win_ref, wout_ref,
                        ws_ref, bst_ref, *refs, n_cast, own_layer):
    o_ref = refs[n_cast]
    if own_layer is None:
        gated_ref = refs[-1]
        _cast_slabs(refs[:n_cast], refs[n_cast + 1:-1])
    else:
        gated_ref, win_v, wout_v, stage_w, stage_o, sem = refs[-6:]
        win16_hbm, wout16_hbm = refs[-8:-6]
        _cast_slabs(refs[:n_cast], refs[n_cast + 1:-8])
        step = pl.program_id(0)
        out_copies = [pltpu.make_async_copy(win_v, win16_hbm.at[0], sem.at[4]),
                      pltpu.make_async_copy(wout_v, wout16_hbm.at[0], sem.at[5])]

        def stream(src_of, dst_of, stage, sem0, n_chunks):
            cps = [pltpu.make_async_copy(src_of(k), stage.at[k % 2], sem.at[sem0 + k % 2])
                   for k in range(n_chunks)]
            cps[0].start()
            for k in range(n_chunks):
                if k + 1 < n_chunks:
                    cps[k + 1].start()
                cps[k].wait()
                dst_of(k)[...] = stage[k % 2].astype(BF16)

        @pl.when(step == 0)
        def _():
            cw, ro = stage_w.shape[2], stage_o.shape[1]
            stream(lambda k: win_ref.at[own_layer, :, pl.ds(k * cw, cw)],
                   lambda k: win_v.at[:, pl.ds(k * cw, cw)], stage_w, 0, win_v.shape[1] // cw)
            stream(lambda k: wout_ref.at[own_layer, pl.ds(k * ro, ro), :],
                   lambda k: wout_v.at[pl.ds(k * ro, ro), :], stage_o, 2, wout_v.shape[0] // ro)
            for cp in out_copies:
                cp.start()

        @pl.when(step == pl.num_programs(0) - 1)
        def _():
            for cp in out_copies:
                cp.wait()

        win_ref, wout_ref = win_v, wout_v
    rows = x_ref.shape[0]
    exp_a = wout_ref.shape[0]
    hd = exp_a // N_HEADS
    x = x_ref[...]
    hn, v = _gmlp_front(x, npre_ref, lng_ref, lnb_ref, win_ref, exp_a)
    vb = v.astype(BF16)
    causal = (lax.broadcasted_iota(jnp.int32, (CHUNK, CHUNK), 1)
              <= lax.broadcasted_iota(jnp.int32, (CHUNK, CHUNK), 0))
    for h in range(N_HEADS):
        lo = h * hd
        wsh = jnp.where(causal, ws_ref[h], 0.0).astype(BF16)
        bias = bst_ref[:, h:h + 1]
        u = _dot(hn, win_ref[:, lo:lo + hd])
        z = _dot(hn, win_ref[:, 2 * exp_a + lo:2 * exp_a + lo + hd])
        s = jnp.concatenate(
            [_dot(wsh, vb[c * CHUNK:(c + 1) * CHUNK, lo:lo + hd]) + bias
             for c in range(rows // CHUNK)], axis=0)
        gated_ref[:, lo:lo + hd] = (u * s * _silu(z)).astype(BF16)
    _gmlp_back(x, gated_ref, wout_ref, npost_ref, o_ref)


def _gmlp_sample_kernel(ws_ref, bs_ref, x_ref, npre_ref, npost_ref, lng_ref, lnb_ref, win_hbm,
                        wout_hbm, o_ref, v_hbm, gated_ref, win_ref, wout_ref, w_sem, v_ref, v_sem,
                        *, n_seq, seq_len, layer, w_layers):
    exp_a = wout_ref.shape[0]
    hd = exp_a // N_HEADS
    col_parts = [slice(exp_a, 2 * exp_a), slice(0, exp_a), slice(2 * exp_a, 3 * exp_a)]
    copies = [pltpu.make_async_copy(win_hbm.at[w_layers[0], :, p], win_ref.at[:, p], w_sem.at[k])
              for k, p in enumerate(col_parts)]
    copies.append(pltpu.make_async_copy(wout_hbm.at[w_layers[1]], wout_ref, w_sem.at[len(col_parts)]))
    for cp in copies:
        cp.start()
    x = x_ref[...]
    copies[0].wait()
    hn, v = _gmlp_front(x, npre_ref, lng_ref, lnb_ref, win_ref, exp_a)
    v_ref[...] = v
    v_copies = [pltpu.make_async_copy(v_ref.at[pl.ds(t * n_seq, n_seq), :], v_hbm.at[:, t, :],
                                      v_sem.at[t]) for t in range(seq_len)]
    for cp in v_copies:
        cp.start()
    copies[1].wait()
    copies[2].wait()
    for h in range(N_HEADS):
        lo = h * hd
        u = _dot(hn, win_ref[:, lo:lo + hd])
        z = _dot(hn, win_ref[:, 2 * exp_a + lo:2 * exp_a + lo + hd])
        vt = [v[t * n_seq:(t + 1) * n_seq, lo:lo + hd] for t in range(seq_len)]
        parts = []
        for t in range(seq_len):
            b_idx = (layer * N_HEADS + h) * seq_len + t
            s = ws_ref[b_idx * seq_len] * vt[0]
            for t2 in range(1, t + 1):
                s = s + ws_ref[b_idx * seq_len + t2] * vt[t2]
            parts.append(s + bs_ref[b_idx])
        s = jnp.concatenate(parts, axis=0)
        gated_ref[:, lo:lo + hd] = (u * s * _silu(z)).astype(BF16)
    copies[3].wait()
    _gmlp_back(x, gated_ref, wout_ref, npost_ref, o_ref)
    for cp in v_copies:
        cp.wait()


def _gmlp_prompt(x2d, layer_a, layer, npre, npost, lng, lnb, win, wout, ws, bst, cast_jobs):
    rows, d = x2d.shape
    (win, win_l), (wout, wout_l) = win, wout
    own = win.dtype != BF16
    exp_a = wout.shape[1]
    n_steps = rows // ROW_TILE
    c_in, c_out, c_shapes = _cast_specs(cast_jobs, n_steps)
    hbm = pl.BlockSpec(memory_space=pl.ANY)
    stage_cols, stage_rows = 4 * V7X_LANES, 4 * V7X_LANES
    own_scratch = [pltpu.VMEM(win.shape[1:], BF16), pltpu.VMEM(wout.shape[1:], BF16),
                   pltpu.VMEM((2, win.shape[1], stage_cols), F32),
                   pltpu.VMEM((2, stage_rows, wout.shape[2]), F32),
                   pltpu.SemaphoreType.DMA((6,))] if own else []
    own_shapes = [jax.ShapeDtypeStruct((1,) + w.shape[1:], BF16) for w in (win, wout)] if own else []
    footprint = ((win[0].size + wout[0].size) * 2 + _layer_bytes(ws) + 4 * ROW_TILE * d * 4
                 + ROW_TILE * exp_a * (2 + 4 + 2 + 8)
                 + 6 * sum(w[0].size for w, _ in cast_jobs) // n_steps * 2
                 + (2 * (win.shape[1] * stage_cols + stage_rows * wout.shape[2]) * 4 if own else 0))
    res = pl.pallas_call(
        functools.partial(_gmlp_prompt_kernel, n_cast=len(cast_jobs),
                          own_layer=win_l if own else None),
        grid=(n_steps,),
        in_specs=[
            _row_spec(ROW_TILE, d),
            _layer_spec(npre, layer), _layer_spec(npost, layer),
            _layer_spec(lng, layer_a), _layer_spec(lnb, layer_a),
            hbm if own else _layer_spec(win, win_l), hbm if own else _layer_spec(wout, wout_l),
            _layer_spec(ws, layer_a), _layer_spec(bst, layer_a),
        ] + c_in,
        out_specs=[_row_spec(ROW_TILE, d)] + c_out + [hbm] * len(own_shapes),
        out_shape=[jax.ShapeDtypeStruct((rows, d), F32)] + c_shapes + own_shapes,
        scratch_shapes=[pltpu.VMEM((ROW_TILE, exp_a), BF16)] + own_scratch,
        compiler_params=pltpu.CompilerParams(
            dimension_semantics=("arbitrary",) if own else ("parallel",),
            vmem_limit_bytes=_vmem_limit(footprint)),
        name="gmlp_prompt",
    )(x2d, npre, npost, lng, lnb, win, wout, ws, bst, *[w for w, _ in cast_jobs])
    n_jobs = len(cast_jobs)
    return res[0], res[1:1 + n_jobs], res[1 + n_jobs:]


def _gmlp_sample(xs, n_seq, seq_len, layer_a, layer, npre, npost, lng, lnb, win, wout,
                 ws_flat, bs_flat):
    rows, d = xs.shape
    (win, win_l), (wout, wout_l) = win, wout
    exp_a = wout.shape[1]
    footprint = _layer_bytes(win, wout) + 4 * rows * d * 4 + rows * exp_a * (2 + 8 + 4 + 8)
    hbm = pl.BlockSpec(memory_space=pl.ANY)
    return pl.pallas_call(
        functools.partial(_gmlp_sample_kernel, n_seq=n_seq, seq_len=seq_len, layer=layer_a,
                          w_layers=(win_l, wout_l)),
        grid=(1,),
        in_specs=[_SMEM, _SMEM, _row_spec(rows, d),
                  _layer_spec(npre, layer), _layer_spec(npost, layer),
                  _layer_spec(lng, layer_a), _layer_spec(lnb, layer_a), hbm, hbm],
        out_specs=[_row_spec(rows, d), hbm],
        out_shape=[jax.ShapeDtypeStruct((rows, d), F32),
                   jax.ShapeDtypeStruct((n_seq, seq_len, exp_a), F32)],
        scratch_shapes=[pltpu.VMEM((rows, exp_a), BF16), pltpu.VMEM(win.shape[1:], BF16),
                        pltpu.VMEM(wout.shape[1:], BF16), pltpu.SemaphoreType.DMA((4,)),
                        pltpu.VMEM((rows, exp_a), F32), pltpu.SemaphoreType.DMA((seq_len,))],
        compiler_params=pltpu.CompilerParams(
            dimension_semantics=("arbitrary",),
            vmem_limit_bytes=_vmem_limit(footprint)),
        name="gmlp_sample",
    )(ws_flat, bs_flat, xs, npre, npost, lng, lnb, win, wout)


def _s5_prep_kernel(are_ref, aim_ref, ldt_ref, btr_ref, bti_ref, ctr_ref, cti_ref,
                    l4r_ref, l4i_ref, wur_ref, wui_ref, vc_ref, kloc_ref):
    dt = jnp.exp(ldt_ref[...])
    ar = are_ref[...]
    ai = aim_ref[...]
    mag = jnp.exp(dt * ar)
    ang = dt * ai
    abr = mag * jnp.cos(ang)
    abi = mag * jnp.sin(ang)
    nr = abr - 1.0
    ni = abi
    den = ar * ar + ai * ai
    cre = (nr * ar + ni * ai) / den
    cim = (ni * ar - nr * ai) / den
    btr = btr_ref[...]
    bti = bti_ref[...]
    bbr = cre * btr - cim * bti
    bbi = cre * bti + cim * btr

    def cmul(xr, xi, yr, yi):
        return xr * yr - xi * yi, xr * yi + xi * yr

    lam = [(jnp.ones_like(abr), jnp.zeros_like(abi)), (abr, abi)]
    for _ in range(BLOCK - 1):
        lam.append(cmul(*lam[-1], abr, abi))
    l4r_ref[...] = lam[BLOCK][0]
    l4i_ref[...] = lam[BLOCK][1]

    n_q, k_rows, n_cols = wur_ref.shape
    q_groups = n_cols // STATE_P
    q_rows = q_groups * SSM_GROUP
    same_q = (_div_pow2(lax.broadcasted_iota(jnp.int32, (q_rows, n_cols), 0), SSM_GROUP)
              == _div_pow2(lax.broadcasted_iota(jnp.int32, (q_rows, n_cols), 1), STATE_P))
    for ip in range(BLOCK):
        ur, ui = cmul(bbr, bbi, *lam[BLOCK - 1 - ip])
        for q in range(n_q):
            for src, dst in ((ur, wur_ref), (ui, wui_ref)):
                blk = jnp.concatenate([src[:, q * n_cols:(q + 1) * n_cols]] * q_groups, axis=0)
                dst[q, ip * q_rows:(ip + 1) * q_rows, :] = jnp.where(same_q, blk, 0.0).astype(BF16)

    ctr = ctr_ref[...]
    cti = cti_ref[...]
    cl = [cmul(ctr, cti, *lam[t]) for t in range(BLOCK + 1)]

    def tile_rows(a, q):
        blk = jnp.concatenate([a[:, q * n_cols:(q + 1) * n_cols]] * q_groups, axis=0)
        return jnp.where(same_q, blk, 0.0)

    half = V7X_LANES // 2
    low = lax.broadcasted_iota(jnp.int32, (q_rows, V7X_LANES), 1) < half
    for q in range(n_q):
        clm = [(tile_rows(c[0], q), tile_rows(c[1], q)) for c in cl]
        rows = [jnp.concatenate([clm[i + 1][0], -clm[i + 1][1]], axis=1) for i in range(BLOCK)]
        vc_ref[q] = jnp.concatenate(rows, axis=0).T.astype(BF16)
        bcat = jnp.concatenate([tile_rows(bbr, q), -tile_rows(bbi, q)], axis=1).astype(BF16)
        ccat = jnp.concatenate([jnp.concatenate(clm[t], axis=1) for t in range(BLOCK)], axis=0)
        kall = lax.dot_general(bcat, ccat.astype(BF16), (((1,), (1,)), ((), ())),
                               preferred_element_type=F32)
        k0, k1 = kall[:, :V7X_LANES], kall[:, V7X_LANES:]
        r0, r1 = pltpu.roll(k0, half, 1), pltpu.roll(k1, half, 1)
        zero = jnp.zeros_like(k0)
        shifted = [(k0, k1),
                   (jnp.where(low, zero, r0), jnp.where(low, r0, r1)),
                   (zero, k0),
                   (zero, jnp.where(low, zero, r0))]
        kloc_ref[q] = jnp.concatenate([jnp.concatenate(sh, axis=1) for sh in shifted],
                                      axis=0).astype(BF16)


def _s5_prep(are, aim, ldt, btr, bti, ctr, cti):
    n_layers, _, n_state = are.shape
    n_q = n_state // V7X_MXU_DIM

    def per_layer(shape):
        return pl.BlockSpec((None,) + shape, lambda l: (l,) + (0,) * len(shape))

    out_tails = [(1, n_state), (1, n_state),
                 (n_q, V7X_MXU_DIM, V7X_MXU_DIM), (n_q, V7X_MXU_DIM, V7X_MXU_DIM),
                 (n_q, 2 * V7X_MXU_DIM, V7X_MXU_DIM), (n_q, V7X_MXU_DIM, V7X_MXU_DIM)]
    out_dtypes = [F32, F32, BF16, BF16, BF16, BF16]
    args = (are, aim, ldt, btr, bti, ctr, cti)
    return pl.pallas_call(
        _s5_prep_kernel,
        grid=(n_layers,),
        in_specs=[per_layer(a.shape[1:]) for a in args],
        out_specs=[per_layer(t) for t in out_tails],
        out_shape=[jax.ShapeDtypeStruct((n_layers,) + t, dt) for t, dt in zip(out_tails, out_dtypes)],
        compiler_params=pltpu.CompilerParams(
            dimension_semantics=("parallel",),
            vmem_limit_bytes=_vmem_limit(2 * sum(
                int(jnp.dtype(dt).itemsize) * functools.reduce(lambda a, b: a * b, t)
                for t, dt in zip(out_tails, out_dtypes)))),
        name="s5_prep",
    )(*args)


N_S5_PARAMS = 15
S5_WEIGHTS = (1, 4, 5, 6, 7, 9, 11, 13)


def _s5_layer_kernel(*refs, n_seq, n_steps, has_h0, swap_bt, n_cast, lazy_layers):
    if has_h0:
        x_ref, h0r_ref, h0i_ref = refs[:3]
        refs = refs[3:]
    else:
        x_ref = refs[0]
        refs = refs[1:]
    params = list(refs[:N_S5_PARAMS])
    refs = refs[N_S5_PARAMS:]
    o_ref, str_ref, sti_ref = refs[n_cast:n_cast + 3]
    n_lazy = len(S5_WEIGHTS) + 1 if lazy_layers else 0
    n_scratch = 5 + (4 if swap_bt else 0) + n_lazy
    scratch = refs[-n_scratch:]
    hpr_ref, hpi_ref, hr_ref, hi_ref, y_ref = scratch[:5]
    _cast_slabs(refs[:n_cast], refs[n_cast + 3:-n_scratch])
    step = pl.program_id(0)
    n_tiles = pl.num_programs(0)

    w_copies = []
    if lazy_layers:
        w_sem = scratch[-1]
        for k, (p, layer) in enumerate(zip(S5_WEIGHTS, lazy_layers)):
            w_copies.append(pltpu.make_async_copy(params[p].at[layer], scratch[-n_lazy + k],
                                                  w_sem.at[k]))
            params[p] = scratch[-n_lazy + k]
        for cp in w_copies:
            cp.start()

    def need(*which):
        for k in which:
            if w_copies:
                w_copies[k].wait()

    (npre_ref, win_ref, l4r_ref, l4i_ref, wur_ref, wui_ref, vc_ref, kloc_ref, dsk_ref,
     w1_ref, b1_ref, w2_ref, b2_ref, wout_ref, npost_ref) = params

    if swap_bt:
        xbuf_ref, obuf_ref, in_sem, out_sem = scratch[5:9]
        slot = step % 2

        def tile_copies(tile, sl, fetch):
            seq_rows = pl.ds(tile * n_steps, n_steps)
            if fetch:
                return [pltpu.make_async_copy(x_ref.at[b, seq_rows, :], xbuf_ref.at[sl, :, b, :],
                                              in_sem.at[sl, b]) for b in range(n_seq)]
            return [pltpu.make_async_copy(obuf_ref.at[sl, :, b, :], o_ref.at[b, seq_rows, :],
                                          out_sem.at[sl, b]) for b in range(n_seq)]

        @pl.when(step == 0)
        def _():
            for cp in tile_copies(0, 0, True):
                cp.start()

        @pl.when(step + 1 < n_tiles)
        def _():
            for cp in tile_copies(step + 1, 1 - slot, True):
                cp.start()

        for cp in tile_copies(step, slot, True):
            cp.wait()

        @pl.when(step >= 2)
        def _():
            for cp in tile_copies(step - 2, slot, False):
                cp.wait()

    @pl.when(step == 0)
    def _():
        if has_h0:
            hr_ref[...] = h0r_ref[...]
            hi_ref[...] = h0i_ref[...]
        else:
            hr_ref[...] = jnp.zeros_like(hr_ref)
            hi_ref[...] = jnp.zeros_like(hi_ref)

    d = npre_ref.shape[-1]
    width = win_ref.shape[1] // 2
    c_rows = PIECE_ROWS
    c_steps = c_rows // n_seq
    n_pieces = n_steps // c_steps
    x_c, xb_c, z_c = [], [], []
    need(0)
    for c in range(n_pieces):
        if swap_bt:
            xc = xbuf_ref.at[slot][c * c_steps:(c + 1) * c_steps].reshape(c_rows, d)
        else:
            xc = x_ref[c * c_rows:(c + 1) * c_rows, :]
        hn = _rmsnorm(xc, npre_ref[...]).astype(BF16)
        x_c.append(xc)
        xb_c.append(_dot(hn, win_ref[:, :width]))
        z_c.append(_dot(hn, win_ref[:, width:]))
    xb = jnp.concatenate(xb_c, axis=0)

    n_blocks = n_steps // BLOCK
    rows = n_blocks * n_seq
    x4 = xb.reshape(n_blocks, BLOCK, n_seq, width)
    xi = [x4[:, i].reshape(rows, width) for i in range(BLOCK)]

    sub = V7X_SUBLANES
    pair = 2 * sub
    n_q, _, n_cols = wur_ref.shape
    n_v = width // V7X_LANES
    half = V7X_LANES // 2
    low = lax.broadcasted_iota(jnp.int32, (rows, V7X_LANES), 1) < half

    lhs_q = []
    need(1, 2)
    for v in range(n_v):
        cols = slice(v * V7X_LANES, (v + 1) * V7X_LANES)
        p = [a[:, cols] for a in xi]
        pr = [pltpu.roll(a, half, 1) for a in p]
        lhs_lo = jnp.concatenate([jnp.where(low, p[i], pr[i + 1]) for i in range(0, BLOCK, 2)], axis=1)
        lhs_hi = jnp.concatenate([jnp.where(low, pr[i], p[i + 1]) for i in range(0, BLOCK, 2)], axis=1)
        for q, lhs in ((2 * v, lhs_lo.astype(BF16)), (2 * v + 1, lhs_hi.astype(BF16))):
            lanes = pl.ds(q * n_cols, n_cols)
            wr = _dot(lhs, wur_ref[q])
            wi = _dot(lhs, wui_ref[q])
            if n_seq == sub:
                ar = jnp.broadcast_to(l4r_ref[:, lanes], (sub, n_cols))
                ai = jnp.broadcast_to(l4i_ref[:, lanes], (sub, n_cols))
                hr, hi = hr_ref[:, lanes], hi_ref[:, lanes]
                for k in range(n_blocks // 2):
                    prv_r, prv_i = [], []
                    for r0 in (2 * k * sub, (2 * k + 1) * sub):
                        prv_r.append(hr)
                        prv_i.append(hi)
                        hr, hi = (ar * hr - ai * hi + wr[r0:r0 + sub],
                                  ar * hi + ai * hr + wi[r0:r0 + sub])
                    hpr_ref[k * pair:(k + 1) * pair, lanes] = jnp.concatenate(prv_r, axis=0).astype(BF16)
                    hpi_ref[k * pair:(k + 1) * pair, lanes] = jnp.concatenate(prv_i, axis=0).astype(BF16)
                hr_ref[:, lanes] = hr
                hi_ref[:, lanes] = hi
            else:
                ar = jnp.broadcast_to(l4r_ref[:, lanes], (pair, n_cols))
                ai = jnp.broadcast_to(l4i_ref[:, lanes], (pair, n_cols))
                for m in range(n_seq // pair):
                    srows = pl.ds(m * pair, pair)
                    hr, hi = hr_ref[srows, lanes], hi_ref[srows, lanes]
                    for blk in range(n_blocks):
                        r0 = blk * n_seq + m * pair
                        hpr_ref[r0:r0 + pair, lanes] = hr.astype(BF16)
                        hpi_ref[r0:r0 + pair, lanes] = hi.astype(BF16)
                        hr, hi = (ar * hr - ai * hi + wr[r0:r0 + pair],
                                  ar * hi + ai * hr + wi[r0:r0 + pair])
                    hr_ref[srows, lanes] = hr
                    hi_ref[srows, lanes] = hi
            lhs_q.append(lhs)

    need(3, 4)
    for v in range(n_v):
        cols = slice(v * V7X_LANES, (v + 1) * V7X_LANES)
        yq = []
        for q in (2 * v, 2 * v + 1):
            lanes = pl.ds(q * n_cols, n_cols)
            st = jnp.concatenate([hpr_ref[:, lanes], hpi_ref[:, lanes]], axis=1)
            yq.append(_dot(st, vc_ref[q]) + _dot(lhs_q[q], kloc_ref[q]))
        ys = []
        for i in range(BLOCK):
            lo_q, hi_q = (a[:, (i // 2) * V7X_LANES:(i // 2 + 1) * V7X_LANES] for a in yq)
            if i % 2 == 0:
                yv = jnp.where(low, lo_q, pltpu.roll(hi_q, half, 1))
            else:
                yv = jnp.where(low, pltpu.roll(lo_q, half, 1), hi_q)
            ys.append(jax.nn.gelu(yv + dsk_ref[:, cols] * xi[i][:, cols]))
        y = jnp.stack([a.reshape(n_blocks, n_seq, V7X_LANES) for a in ys], axis=1)
        y_ref[:, cols] = y.reshape(n_steps * n_seq, V7X_LANES).astype(BF16)

    need(5, 6, 7)
    for c in range(n_pieces):
        y = y_ref[c * c_rows:(c + 1) * c_rows, :]
        g = (_dot(y, w1_ref[...]) + b1_ref[...]) * jax.nn.sigmoid(_dot(y, w2_ref[...]) + b2_ref[...])
        out = _dot((g * _silu(z_c[c])).astype(BF16), wout_ref[...])
        xn = x_c[c] + _rmsnorm(out, npost_ref[...])
        if swap_bt:
            obuf_ref.at[slot][c * c_steps:(c + 1) * c_steps] = xn.reshape(c_steps, n_seq, d)
        else:
            o_ref[c * c_rows:(c + 1) * c_rows, :] = xn

    if swap_bt:
        for cp in tile_copies(step, slot, False):
            cp.start()

    @pl.when(step == n_tiles - 1)
    def _():
        str_ref[...] = hr_ref[...]
        sti_ref[...] = hi_ref[...]
        if swap_bt:
            for cp in tile_copies(step, slot, False):
                cp.wait()

    if swap_bt:
        @pl.when(jnp.logical_and(step == n_tiles - 1, n_tiles >= 2))
        def _():
            for cp in tile_copies(step - 1, 1 - slot, False):
                cp.wait()


def _s5_layer(x, h0, layer_b, layer, npre, win, core_consts, out_consts, npost, n_seq, n_steps,
              cast_jobs=()):
    swap_bt = x.ndim == 3
    d = x.shape[-1]
    exp_b = win[0].shape[2] // 2
    n_state = core_consts[0].shape[-1]
    tile = n_seq * n_steps
    n_tiles = x.size // (tile * d)
    rows = tile // BLOCK
    has_h0 = h0 is not None
    assert n_steps % BLOCK == 0 and (n_seq > V7X_SUBLANES or (n_steps // BLOCK) % 2 == 0)
    assert tile % PIECE_ROWS == 0 and PIECE_ROWS % n_seq == 0
    params_b = [win] + [(c, layer_b) for c in core_consts] + list(out_consts)
    assert len(params_b) + 2 == N_S5_PARAMS
    c_in, c_out, c_shapes = _cast_specs(cast_jobs, n_tiles)
    args = ([x] + (list(h0) if has_h0 else []) + [npre] + [a for a, _ in params_b] + [npost]
            + [w for w, _ in cast_jobs])
    if swap_bt:
        io_spec = pl.BlockSpec(memory_space=pl.ANY)
        io_scratch = [pltpu.VMEM((2, n_steps, n_seq, d), F32), pltpu.VMEM((2, n_steps, n_seq, d), F32),
                      pltpu.SemaphoreType.DMA((2, n_seq)), pltpu.SemaphoreType.DMA((2, n_seq))]
    else:
        io_spec = pl.BlockSpec((tile, d), lambda i: (i, 0))
        io_scratch = []
    lazy = n_tiles == 1
    lazy_idx = [p - 1 for p in S5_WEIGHTS]
    hbm = pl.BlockSpec(memory_space=pl.ANY)
    in_specs = ([io_spec] + [_layer_spec(a, layer_b) for a in (h0 if has_h0 else ())]
                + [_layer_spec(npre, layer)]
                + [hbm if lazy and k in lazy_idx else _layer_spec(a, l)
                   for k, (a, l) in enumerate(params_b)]
                + [_layer_spec(npost, layer)] + c_in)
    lazy_scratch = ([pltpu.VMEM(params_b[k][0].shape[1:], params_b[k][0].dtype) for k in lazy_idx]
                    + [pltpu.SemaphoreType.DMA((len(lazy_idx),))]) if lazy else []
    st_spec = pl.BlockSpec((n_seq, n_state), lambda i: (0, 0))
    st_shape = jax.ShapeDtypeStruct((n_seq, n_state), F32)
    footprint = (_layer_bytes(*[a for a, _ in params_b]) + (_layer_bytes(*h0) if has_h0 else 0)
                 + 4 * tile * d * 4 + 2 * rows * n_state * 2 + 8 * n_seq * n_state * 4
                 + tile * exp_b * (2 + 6 * 4)
                 + 6 * sum(w[0].size for w, _ in cast_jobs) // n_tiles * 2)
    res = pl.pallas_call(
        functools.partial(_s5_layer_kernel, n_seq=n_seq, n_steps=n_steps, has_h0=has_h0,
                          swap_bt=swap_bt, n_cast=len(cast_jobs),
                          lazy_layers=tuple(params_b[k][1] for k in lazy_idx) if lazy else ()),
        grid=(n_tiles,),
        in_specs=in_specs,
        out_specs=[io_spec, st_spec, st_spec] + c_out,
        out_shape=[jax.ShapeDtypeStruct(x.shape, F32), st_shape, st_shape] + c_shapes,
        scratch_shapes=[pltpu.VMEM((rows, n_state), BF16), pltpu.VMEM((rows, n_state), BF16),
                        pltpu.VMEM((n_seq, n_state), F32), pltpu.VMEM((n_seq, n_state), F32),
                        pltpu.VMEM((tile, exp_b), BF16)] + io_scratch + lazy_scratch,
        compiler_params=pltpu.CompilerParams(
            dimension_semantics=("arbitrary",),
            vmem_limit_bytes=_vmem_limit(footprint)),
        name="s5_layer_sample" if has_h0 else "s5_layer_prompt",
    )(*args)
    return res[0], res[1], res[2], res[3:]


def kernel(x_prompt, x_sample, state_ssm_re, state_ssm_im, norm_pre, norm_post,
           w_in_a, ln_v_g, ln_v_b, w_s, b_s, w_out_a,
           w_in_b, a_re, a_im, log_dt, b_re, b_im, c_re, c_im, d_skip,
           w_glu1, b_glu1, w_glu2, b_glu2, w_out_b):
    n_batch, seq_len, d = x_prompt.shape
    n_dec, dec_len, _ = x_sample.shape
    depth = norm_pre.shape[0]
    n_ssm, n_groups, state_p = a_re.shape
    n_state = n_groups * state_p
    assert seq_len % ROW_TILE == 0 and ROW_TILE % CHUNK == 0 and seq_len % SCAN_STEPS == 0
    assert n_batch == V7X_SUBLANES and n_dec % (2 * V7X_SUBLANES) == 0
    assert dec_len <= CHUNK and dec_len % BLOCK == 0
    assert state_p == STATE_P and b_re.shape[-1] == SSM_GROUP and w_s.shape[1] == N_HEADS

    rows3 = lambda a: a.reshape(a.shape[0], 1, -1)
    npre, npost = rows3(norm_pre), rows3(norm_post)
    lng, lnb = rows3(ln_v_g), rows3(ln_v_b)
    wa16 = ((w_in_a, 0), (w_out_a, 0))
    bst = jnp.swapaxes(b_s, 1, 2)
    ws_dec = w_s[:, :, :dec_len, :dec_len].reshape(-1)
    bs_dec = b_s[:, :, :dec_len].reshape(-1)

    b1, b2 = rows3(b_glu1), rows3(b_glu2)
    lanes_gp = lambda a, perm: jnp.transpose(a, perm).reshape(n_ssm, SSM_GROUP, n_state)
    l4r, l4i, wur, wui, vcw, kloc = _s5_prep(
        rows3(a_re), rows3(a_im), rows3(jnp.repeat(log_dt, state_p, axis=1)),
        lanes_gp(b_re, (0, 3, 1, 2)), lanes_gp(b_im, (0, 3, 1, 2)),
        lanes_gp(c_re, (0, 2, 1, 3)), lanes_gp(c_im, (0, 2, 1, 3)))
    core_consts = (l4r, l4i, wur, wui, vcw, kloc, rows3(d_skip))
    h0 = (state_ssm_re.reshape(n_ssm, n_dec, n_state), state_ssm_im.reshape(n_ssm, n_dec, n_state))

    xp = x_prompt.reshape(n_batch * seq_len, d)
    xs = jnp.transpose(x_sample, (1, 0, 2)).reshape(dec_len * n_dec, d)

    v_rows, st_p_re, st_p_im, st_s_re, st_s_im = [], [], [], [], []
    for i in range(depth):
        j = i // 2
        if i % 2 == 0:
            jobs = [(w, j) for w in (w_in_b, w_glu1, w_glu2, w_out_b)] if i + 1 < depth else []
            xp, wb16, own16 = _gmlp_prompt(xp, j, i, npre, npost, lng, lnb, *wa16, w_s, bst, jobs)
            if own16:
                wa16 = tuple((w, 0) for w in own16)
            xs, v = _gmlp_sample(xs, n_dec, dec_len, j, i, npre, npost, lng, lnb, *wa16,
                                 ws_dec, bs_dec)
            v_rows.append(v)
        else:
            win = (wb16[0], 0)
            out_consts = ((wb16[1], 0), (b1, j), (wb16[2], 0), (b2, j), (wb16[3], 0))
            jobs = [(w_in_a, j + 1), (w_out_a, j + 1)] if i + 1 < depth else []
            xp3, sr, si, wa16 = _s5_layer(xp.reshape(n_batch, seq_len, d), None, j, i, npre, win,
                                          core_consts, out_consts, npost, n_batch, SCAN_STEPS, jobs)
            wa16 = tuple((w, 0) for w in wa16)
            xp = xp3.reshape(n_batch * seq_len, d)
            st_p_re.append(sr)
            st_p_im.append(si)
            xs, sr, si, _ = _s5_layer(xs, h0, j, i, npre, win, core_consts, out_consts, npost,
                                      n_dec, dec_len)
            st_s_re.append(sr)
            st_s_im.append(si)

    y_prompt = xp.reshape(n_batch, seq_len, d)
    y_sample = jnp.transpose(xs.reshape(dec_len, n_dec, d), (1, 0, 2))
    chunk_v = jnp.stack(v_rows)
    states = lambda parts, n: jnp.stack([p.reshape(n, n_groups, state_p) for p in parts])
    return (y_prompt, y_sample, chunk_v, states(st_p_re, n_batch), states(st_p_im, n_batch),
            states(st_s_re, n_dec), states(st_s_im, n_dec))
```

```python
import functools

import jax
import jax.numpy as jnp
from jax import lax
from jax.experimental import pallas as pl
from jax.experimental.pallas import tpu as pltpu

EPS = 1e-6
CHUNK = 128
N_HEADS = 8
SSM_GROUP = 16
STATE_P = 64

V7X_LANES = 128
V7X_SUBLANES = 8
V7X_MXU_DIM = 256
V7X_VMEM_BYTES = 64 * 1024 * 1024

ROW_TILE = 1024
SCAN_STEPS = 64
BLOCK = 4
PIECE_ROWS = 256

BF16 = jnp.bfloat16
F32 = jnp.float32


def _dot(a, b):
    return jnp.dot(a, b, preferred_element_type=F32)


def _rmsnorm(x, g):
    ms = jnp.mean(x * x, axis=-1, keepdims=True)
    return x * lax.rsqrt(ms + EPS) * g


def _layernorm(x, g, b):
    mu = jnp.mean(x, axis=-1, keepdims=True)
    xc = x - mu
    var = jnp.mean(xc * xc, axis=-1, keepdims=True)
    return xc * lax.rsqrt(var + EPS) * g + b


def _silu(z):
    return z * jax.nn.sigmoid(z)


def _div_pow2(x, n):
    assert n & (n - 1) == 0
    return lax.shift_right_logical(x, n.bit_length() - 1)


def _vmem_limit(nbytes):
    return int(min(V7X_VMEM_BYTES - (4 << 20), nbytes + (12 << 20)))


def _layer_spec(stacked, layer):
    tail = stacked.shape[1:]
    return pl.BlockSpec((None,) + tail, lambda *_: (layer,) + (0,) * len(tail),
                        pipeline_mode=pl.Buffered(1))


def _layer_bytes(*stacked):
    return sum(a[0].size * a.dtype.itemsize for a in stacked)


def _cast_specs(jobs, n_steps):
    in_specs, out_specs, out_shapes = [], [], []
    for w, layer in jobs:
        _, rows, cols = w.shape
        slab = rows // n_steps
        assert rows % n_steps == 0 and slab % (2 * V7X_SUBLANES) == 0
        in_specs.append(pl.BlockSpec((None, slab, cols), lambda i, layer=layer: (layer, i, 0)))
        out_specs.append(pl.BlockSpec((None, slab, cols), lambda i: (0, i, 0)))
        out_shapes.append(jax.ShapeDtypeStruct((1, rows, cols), BF16))
    return in_specs, out_specs, out_shapes


def _cast_slabs(in_refs, out_refs):
    for src, dst in zip(in_refs, out_refs):
        dst[...] = src[...].astype(BF16)


def _row_spec(rows, width):
    return pl.BlockSpec((rows, width), lambda i: (i, 0))


_SMEM = pl.BlockSpec(memory_space=pltpu.SMEM)


def _gmlp_front(x, npre_ref, lng_ref, lnb_ref, win_ref, exp_a):
    hn, v = [], []
    for r0 in range(0, x.shape[0], PIECE_ROWS):
        hn.append(_rmsnorm(x[r0:r0 + PIECE_ROWS], npre_ref[...]).astype(BF16))
        v.append(_layernorm(_dot(hn[-1], win_ref[:, exp_a:2 * exp_a]), lng_ref[...], lnb_ref[...]))
    return jnp.concatenate(hn, axis=0), jnp.concatenate(v, axis=0)


def _gmlp_back(x, gated_ref, wout_ref, npost_ref, o_ref):
    for r0 in range(0, x.shape[0], PIECE_ROWS):
        r = slice(r0, r0 + PIECE_ROWS)
        out = _dot(gated_ref[r, :], wout_ref[...])
        o_ref[r, :] = x[r] + _rmsnorm(out, npost_ref[...])


def _gmlp_prompt_kernel(x_ref, npre_ref, npost_ref, lng_ref, lnb_ref, win_ref, wout_ref,
                        ws_ref, bst_ref, *refs, n_cast, own_layer):
    o_ref = refs[n_cast]
    if own_layer is None:
        gated_ref = refs[-1]
        _cast_slabs(refs[:n_cast], refs[n_cast + 1:-1])
    else:
        gated_ref, win_v, wout_v, stage_w, stage_o, sem = refs[-6:]
        win16_hbm, wout16_hbm = refs[-8:-6]
        _cast_slabs(refs[:n_cast], refs[n_cast + 1:-8])
        step = pl.program_id(0)
        out_copies = [pltpu.make_async_copy(win_v, win16_hbm.at[0], sem.at[4]),
                      pltpu.make_async_copy(wout_v, wout16_hbm.at[0], sem.at[5])]

        def stream(src_of, dst_of, stage, sem0, n_chunks):
            cps = [pltpu.make_async_copy(src_of(k), stage.at[k % 2], sem.at[sem0 + k % 2])
                   for k in range(n_chunks)]
            cps[0].start()
            for k in range(n_chunks):
                if k + 1 < n_chunks:
                    cps[k + 1].start()
                cps[k].wait()
                dst_of(k)[...] = stage[k % 2].astype(BF16)

        @pl.when(step == 0)
        def _():
            cw, ro = stage_w.shape[2], stage_o.shape[1]
            stream(lambda k: win_ref.at[own_layer, :, pl.ds(k * cw, cw)],
                   lambda k: win_v.at[:, pl.ds(k * cw, cw)], stage_w, 0, win_v.shape[1] // cw)
            stream(lambda k: wout_ref.at[own_layer, pl.ds(k * ro, ro), :],
                   lambda k: wout_v.at[pl.ds(k * ro, ro), :], stage_o, 2, wout_v.shape[0] // ro)
            for cp in out_copies:
                cp.start()

        @pl.when(step == pl.num_programs(0) - 1)
        def _():
            for cp in out_copies:
                cp.wait()

        win_ref, wout_ref = win_v, wout_v
    rows = x_ref.shape[0]
    exp_a = wout_ref.shape[0]
    hd = exp_a // N_HEADS
    x = x_ref[...]
    hn, v = _gmlp_front(x, npre_ref, lng_ref, lnb_ref, win_ref, exp_a)
    vb = v.astype(BF16)
    causal = (lax.broadcasted_iota(jnp.int32, (CHUNK, CHUNK), 1)
              <= lax.broadcasted_iota(jnp.int32, (CHUNK, CHUNK), 0))
    for h in range(N_HEADS):
        lo = h * hd
        wsh = jnp.where(causal, ws_ref[h], 0.0).astype(BF16)
        bias = bst_ref[:, h:h + 1]
        u = _dot(hn, win_ref[:, lo:lo + hd])
        z = _dot(hn, win_ref[:, 2 * exp_a + lo:2 * exp_a + lo + hd])
        s = jnp.concatenate(
            [_dot(wsh, vb[c * CHUNK:(c + 1) * CHUNK, lo:lo + hd]) + bias
             for c in range(rows // CHUNK)], axis=0)
        gated_ref[:, lo:lo + hd] = (u * s * _silu(z)).astype(BF16)
    _gmlp_back(x, gated_ref, wout_ref, npost_ref, o_ref)


def _gmlp_sample_kernel(ws_ref, bs_ref, x_ref, npre_ref, npost_ref, lng_ref, lnb_ref, win_hbm,
                        wout_hbm, *refs, n_seq, seq_len, layer, w_layers):
    o_ref, v_hbm, gated_ref, win_ref, wout_ref, w_sem, v_ref, v_sem = refs[-8:]
    slabs = range(v_hbm.shape[0]) if len(refs) == 8 else [layer]
    exp_a = wout_ref.shape[0]
    hd = exp_a // N_HEADS
    col_parts = [slice(exp_a, 2 * exp_a), slice(0, exp_a), slice(2 * exp_a, 3 * exp_a)]
    copies = [pltpu.make_async_copy(win_hbm.at[w_layers[0], :, p], win_ref.at[:, p], w_sem.at[k])
              for k, p in enumerate(col_parts)]
    copies.append(pltpu.make_async_copy(wout_hbm.at[w_layers[1]], wout_ref, w_sem.at[len(col_parts)]))
    for cp in copies:
        cp.start()
    x = x_ref[...]
    copies[0].wait()
    hn, v = _gmlp_front(x, npre_ref, lng_ref, lnb_ref, win_ref, exp_a)
    v_ref[...] = v
    v_copies = [pltpu.make_async_copy(v_ref.at[pl.ds(t * n_seq, n_seq), :], v_hbm.at[l, :, t, :],
                                      v_sem.at[l, t]) for l in slabs for t in range(seq_len)]
    for cp in v_copies:
        cp.start()
    copies[1].wait()
    copies[2].wait()
    for h in range(N_HEADS):
        lo = h * hd
        u = _dot(hn, win_ref[:, lo:lo + hd])
        z = _dot(hn, win_ref[:, 2 * exp_a + lo:2 * exp_a + lo + hd])
        vt = [v[t * n_seq:(t + 1) * n_seq, lo:lo + hd] for t in range(seq_len)]
        parts = []
        for t in range(seq_len):
            b_idx = (layer * N_HEADS + h) * seq_len + t
            s = ws_ref[b_idx * seq_len] * vt[0]
            for t2 in range(1, t + 1):
                s = s + ws_ref[b_idx * seq_len + t2] * vt[t2]
            parts.append(s + bs_ref[b_idx])
        s = jnp.concatenate(parts, axis=0)
        gated_ref[:, lo:lo + hd] = (u * s * _silu(z)).astype(BF16)
    copies[3].wait()
    _gmlp_back(x, gated_ref, wout_ref, npost_ref, o_ref)
    for cp in v_copies:
        cp.wait()


def _gmlp_prompt(x2d, layer_a, layer, npre, npost, lng, lnb, win, wout, ws, bst, cast_jobs):
    rows, d = x2d.shape
    (win, win_l), (wout, wout_l) = win, wout
    own = win.dtype != BF16
    exp_a = wout.shape[1]
    n_steps = rows // ROW_TILE
    c_in, c_out, c_shapes = _cast_specs(cast_jobs, n_steps)
    hbm = pl.BlockSpec(memory_space=pl.ANY)
    stage_cols, stage_rows = 4 * V7X_LANES, 4 * V7X_LANES
    own_scratch = [pltpu.VMEM(win.shape[1:], BF16), pltpu.VMEM(wout.shape[1:], BF16),
                   pltpu.VMEM((2, win.shape[1], stage_cols), F32),
                   pltpu.VMEM((2, stage_rows, wout.shape[2]), F32),
                   pltpu.SemaphoreType.DMA((6,))] if own else []
    own_shapes = [jax.ShapeDtypeStruct((1,) + w.shape[1:], BF16) for w in (win, wout)] if own else []
    footprint = ((win[0].size + wout[0].size) * 2 + _layer_bytes(ws) + 4 * ROW_TILE * d * 4
                 + ROW_TILE * exp_a * (2 + 4 + 2 + 8)
                 + 6 * sum(w[0].size for w, _ in cast_jobs) // n_steps * 2
                 + (2 * (win.shape[1] * stage_cols + stage_rows * wout.shape[2]) * 4 if own else 0))
    res = pl.pallas_call(
        functools.partial(_gmlp_prompt_kernel, n_cast=len(cast_jobs),
                          own_layer=win_l if own else None),
        grid=(n_steps,),
        in_specs=[
            _row_spec(ROW_TILE, d),
            _layer_spec(npre, layer), _layer_spec(npost, layer),
            _layer_spec(lng, layer_a), _layer_spec(lnb, layer_a),
            hbm if own else _layer_spec(win, win_l), hbm if own else _layer_spec(wout, wout_l),
            _layer_spec(ws, layer_a), _layer_spec(bst, layer_a),
        ] + c_in,
        out_specs=[_row_spec(ROW_TILE, d)] + c_out + [hbm] * len(own_shapes),
        out_shape=[jax.ShapeDtypeStruct((rows, d), F32)] + c_shapes + own_shapes,
        scratch_shapes=[pltpu.VMEM((ROW_TILE, exp_a), BF16)] + own_scratch,
        compiler_params=pltpu.CompilerParams(
            dimension_semantics=("arbitrary",) if own else ("parallel",),
            vmem_limit_bytes=_vmem_limit(footprint)),
        name="gmlp_prompt",
    )(x2d, npre, npost, lng, lnb, win, wout, ws, bst, *[w for w, _ in cast_jobs])
    n_jobs = len(cast_jobs)
    return res[0], res[1:1 + n_jobs], res[1 + n_jobs:]


def _gmlp_sample(xs, n_seq, seq_len, layer_a, layer, npre, npost, lng, lnb, win, wout,
                 ws_flat, bs_flat, v_prev):
    rows, d = xs.shape
    (win, win_l), (wout, wout_l) = win, wout
    exp_a = wout.shape[1]
    footprint = _layer_bytes(win, wout) + 4 * rows * d * 4 + rows * exp_a * (2 + 8 + 4 + 8)
    hbm = pl.BlockSpec(memory_space=pl.ANY)
    prev = [] if v_prev is None else [v_prev]
    return pl.pallas_call(
        functools.partial(_gmlp_sample_kernel, n_seq=n_seq, seq_len=seq_len, layer=layer_a,
                          w_layers=(win_l, wout_l)),
        grid=(1,),
        in_specs=[_SMEM, _SMEM, _row_spec(rows, d),
                  _layer_spec(npre, layer), _layer_spec(npost, layer),
                  _layer_spec(lng, layer_a), _layer_spec(lnb, layer_a), hbm, hbm] + [hbm] * len(prev),
        out_specs=[_row_spec(rows, d), hbm],
        out_shape=[jax.ShapeDtypeStruct((rows, d), F32),
                   jax.ShapeDtypeStruct((lng.shape[0], n_seq, seq_len, exp_a), F32)],
        input_output_aliases={9: 1} if prev else {},
        scratch_shapes=[pltpu.VMEM((rows, exp_a), BF16), pltpu.VMEM(win.shape[1:], BF16),
                        pltpu.VMEM(wout.shape[1:], BF16), pltpu.SemaphoreType.DMA((4,)),
                        pltpu.VMEM((rows, exp_a), F32),
                        pltpu.SemaphoreType.DMA((lng.shape[0], seq_len))],
        compiler_params=pltpu.CompilerParams(
            dimension_semantics=("arbitrary",),
            vmem_limit_bytes=_vmem_limit(footprint)),
        name="gmlp_sample",
    )(ws_flat, bs_flat, xs, npre, npost, lng, lnb, win, wout, *prev)


def _s5_prep_kernel(are_ref, aim_ref, ldt_ref, btr_ref, bti_ref, ctr_ref, cti_ref,
                    l4r_ref, l4i_ref, wur_ref, wui_ref, vc_ref, kloc_ref):
    dt = jnp.exp(ldt_ref[...])
    ar = are_ref[...]
    ai = aim_ref[...]
    mag = jnp.exp(dt * ar)
    ang = dt * ai
    abr = mag * jnp.cos(ang)
    abi = mag * jnp.sin(ang)
    nr = abr - 1.0
    ni = abi
    den = ar * ar + ai * ai
    cre = (nr * ar + ni * ai) / den
    cim = (ni * ar - nr * ai) / den
    btr = btr_ref[...]
    bti = bti_ref[...]
    bbr = cre * btr - cim * bti
    bbi = cre * bti + cim * btr

    def cmul(xr, xi, yr, yi):
        return xr * yr - xi * yi, xr * yi + xi * yr

    lam = [(jnp.ones_like(abr), jnp.zeros_like(abi)), (abr, abi)]
    for _ in range(BLOCK - 1):
        lam.append(cmul(*lam[-1], abr, abi))
    l4r_ref[...] = lam[BLOCK][0]
    l4i_ref[...] = lam[BLOCK][1]

    n_q, k_rows, n_cols = wur_ref.shape
    q_groups = n_cols // STATE_P
    q_rows = q_groups * SSM_GROUP
    same_q = (_div_pow2(lax.broadcasted_iota(jnp.int32, (q_rows, n_cols), 0), SSM_GROUP)
              == _div_pow2(lax.broadcasted_iota(jnp.int32, (q_rows, n_cols), 1), STATE_P))
    for ip in range(BLOCK):
        ur, ui = cmul(bbr, bbi, *lam[BLOCK - 1 - ip])
        for q in range(n_q):
            for src, dst in ((ur, wur_ref), (ui, wui_ref)):
                blk = jnp.concatenate([src[:, q * n_cols:(q + 1) * n_cols]] * q_groups, axis=0)
                dst[q, ip * q_rows:(ip + 1) * q_rows, :] = jnp.where(same_q, blk, 0.0).astype(BF16)

    ctr = ctr_ref[...]
    cti = cti_ref[...]
    cl = [cmul(ctr, cti, *lam[t]) for t in range(BLOCK + 1)]

    def tile_rows(a, q):
        blk = jnp.concatenate([a[:, q * n_cols:(q + 1) * n_cols]] * q_groups, axis=0)
        return jnp.where(same_q, blk, 0.0)

    half = V7X_LANES // 2
    low = lax.broadcasted_iota(jnp.int32, (q_rows, V7X_LANES), 1) < half
    for q in range(n_q):
        clm = [(tile_rows(c[0], q), tile_rows(c[1], q)) for c in cl]
        rows = [jnp.concatenate([clm[i + 1][0], -clm[i + 1][1]], axis=1) for i in range(BLOCK)]
        vc_ref[q] = jnp.concatenate(rows, axis=0).T.astype(BF16)
        bcat = jnp.concatenate([tile_rows(bbr, q), -tile_rows(bbi, q)], axis=1).astype(BF16)
        ccat = jnp.concatenate([jnp.concatenate(clm[t], axis=1) for t in range(BLOCK)], axis=0)
        kall = lax.dot_general(bcat, ccat.astype(BF16), (((1,), (1,)), ((), ())),
                               preferred_element_type=F32)
        k0, k1 = kall[:, :V7X_LANES], kall[:, V7X_LANES:]
        r0, r1 = pltpu.roll(k0, half, 1), pltpu.roll(k1, half, 1)
        zero = jnp.zeros_like(k0)
        shifted = [(k0, k1),
                   (jnp.where(low, zero, r0), jnp.where(low, r0, r1)),
                   (zero, k0),
                   (zero, jnp.where(low, zero, r0))]
        kloc_ref[q] = jnp.concatenate([jnp.concatenate(sh, axis=1) for sh in shifted],
                                      axis=0).astype(BF16)


def _s5_prep(are, aim, ldt, btr, bti, ctr, cti):
    n_layers, _, n_state = are.shape
    n_q = n_state // V7X_MXU_DIM

    def per_layer(shape):
        return pl.BlockSpec((None,) + shape, lambda l: (l,) + (0,) * len(shape))

    out_tails = [(1, n_state), (1, n_state),
                 (n_q, V7X_MXU_DIM, V7X_MXU_DIM), (n_q, V7X_MXU_DIM, V7X_MXU_DIM),
                 (n_q, 2 * V7X_MXU_DIM, V7X_MXU_DIM), (n_q, V7X_MXU_DIM, V7X_MXU_DIM)]
    out_dtypes = [F32, F32, BF16, BF16, BF16, BF16]
    args = (are, aim, ldt, btr, bti, ctr, cti)
    return pl.pallas_call(
        _s5_prep_kernel,
        grid=(n_layers,),
        in_specs=[per_layer(a.shape[1:]) for a in args],
        out_specs=[per_layer(t) for t in out_tails],
        out_shape=[jax.ShapeDtypeStruct((n_layers,) + t, dt) for t, dt in zip(out_tails, out_dtypes)],
        compiler_params=pltpu.CompilerParams(
            dimension_semantics=("parallel",),
            vmem_limit_bytes=_vmem_limit(2 * sum(
                int(jnp.dtype(dt).itemsize) * functools.reduce(lambda a, b: a * b, t)
                for t, dt in zip(out_tails, out_dtypes)))),
        name="s5_prep",
    )(*args)


N_S5_PARAMS = 15
S5_WEIGHTS = (1, 4, 5, 6, 7, 9, 11, 13)


def _s5_layer_kernel(*refs, n_seq, n_steps, has_h0, swap_bt, n_cast, n_prev, lazy_layers):
    if has_h0:
        x_ref, h0r_ref, h0i_ref = refs[:3]
        refs = refs[3:]
    else:
        x_ref = refs[0]
        refs = refs[1:]
    params = list(refs[:N_S5_PARAMS])
    refs = refs[N_S5_PARAMS:]
    n_in = n_cast + n_prev
    o_ref, str_ref, sti_ref = refs[n_in:n_in + 3]
    n_lazy = len(S5_WEIGHTS) + 1 if lazy_layers else 0
    n_scratch = 5 + (4 if swap_bt else 0) + n_lazy
    scratch = refs[-n_scratch:]
    hpr_ref, hpi_ref, hr_ref, hi_ref, y_ref = scratch[:5]
    _cast_slabs(refs[:n_cast], refs[n_in + 3:-n_scratch])
    step = pl.program_id(0)
    n_tiles = pl.num_programs(0)

    w_copies = []
    if lazy_layers:
        w_sem = scratch[-1]
        for k, (p, layer) in enumerate(zip(S5_WEIGHTS, lazy_layers)):
            w_copies.append(pltpu.make_async_copy(params[p].at[layer], scratch[-n_lazy + k],
                                                  w_sem.at[k]))
            params[p] = scratch[-n_lazy + k]
        for cp in w_copies:
            cp.start()

    def need(*which):
        for k in which:
            if w_copies:
                w_copies[k].wait()

    (npre_ref, win_ref, l4r_ref, l4i_ref, wur_ref, wui_ref, vc_ref, kloc_ref, dsk_ref,
     w1_ref, b1_ref, w2_ref, b2_ref, wout_ref, npost_ref) = params

    if swap_bt:
        xbuf_ref, obuf_ref, in_sem, out_sem = scratch[5:9]
        slot = step % 2

        def tile_copies(tile, sl, fetch):
            seq_rows = pl.ds(tile * n_steps, n_steps)
            if fetch:
                return [pltpu.make_async_copy(x_ref.at[b, seq_rows, :], xbuf_ref.at[sl, :, b, :],
                                              in_sem.at[sl, b]) for b in range(n_seq)]
            return [pltpu.make_async_copy(obuf_ref.at[sl, :, b, :], o_ref.at[b, seq_rows, :],
                                          out_sem.at[sl, b]) for b in range(n_seq)]

        @pl.when(step == 0)
        def _():
            for cp in tile_copies(0, 0, True):
                cp.start()

        @pl.when(step + 1 < n_tiles)
        def _():
            for cp in tile_copies(step + 1, 1 - slot, True):
                cp.start()

        for cp in tile_copies(step, slot, True):
            cp.wait()

        @pl.when(step >= 2)
        def _():
            for cp in tile_copies(step - 2, slot, False):
                cp.wait()

    @pl.when(step == 0)
    def _():
        if has_h0:
            hr_ref[...] = h0r_ref[...]
            hi_ref[...] = h0i_ref[...]
        else:
            hr_ref[...] = jnp.zeros_like(hr_ref)
            hi_ref[...] = jnp.zeros_like(hi_ref)

    d = npre_ref.shape[-1]
    width = win_ref.shape[1] // 2
    c_rows = PIECE_ROWS
    c_steps = c_rows // n_seq
    n_pieces = n_steps // c_steps
    x_c, xb_c, z_c = [], [], []
    need(0)
    for c in range(n_pieces):
        if swap_bt:
            xc = xbuf_ref.at[slot][c * c_steps:(c + 1) * c_steps].reshape(c_rows, d)
        else:
            xc = x_ref[c * c_rows:(c + 1) * c_rows, :]
        hn = _rmsnorm(xc, npre_ref[...]).astype(BF16)
        x_c.append(xc)
        xb_c.append(_dot(hn, win_ref[:, :width]))
        z_c.append(_dot(hn, win_ref[:, width:]))
    xb = jnp.concatenate(xb_c, axis=0)

    n_blocks = n_steps // BLOCK
    rows = n_blocks * n_seq
    x4 = xb.reshape(n_blocks, BLOCK, n_seq, width)
    xi = [x4[:, i].reshape(rows, width) for i in range(BLOCK)]

    sub = V7X_SUBLANES
    pair = 2 * sub
    n_q, _, n_cols = wur_ref.shape
    n_v = width // V7X_LANES
    half = V7X_LANES // 2
    low = lax.broadcasted_iota(jnp.int32, (rows, V7X_LANES), 1) < half

    lhs_q = []
    need(1, 2)
    for v in range(n_v):
        cols = slice(v * V7X_LANES, (v + 1) * V7X_LANES)
        p = [a[:, cols] for a in xi]
        pr = [pltpu.roll(a, half, 1) for a in p]
        lhs_lo = jnp.concatenate([jnp.where(low, p[i], pr[i + 1]) for i in range(0, BLOCK, 2)], axis=1)
        lhs_hi = jnp.concatenate([jnp.where(low, pr[i], p[i + 1]) for i in range(0, BLOCK, 2)], axis=1)
        for q, lhs in ((2 * v, lhs_lo.astype(BF16)), (2 * v + 1, lhs_hi.astype(BF16))):
            lanes = pl.ds(q * n_cols, n_cols)
            wr = _dot(lhs, wur_ref[q])
            wi = _dot(lhs, wui_ref[q])
            if n_seq == sub:
                ar = jnp.broadcast_to(l4r_ref[:, lanes], (sub, n_cols))
                ai = jnp.broadcast_to(l4i_ref[:, lanes], (sub, n_cols))
                hr, hi = hr_ref[:, lanes], hi_ref[:, lanes]
                for k in range(n_blocks // 2):
                    prv_r, prv_i = [], []
                    for r0 in (2 * k * sub, (2 * k + 1) * sub):
                        prv_r.append(hr)
                        prv_i.append(hi)
                        hr, hi = (ar * hr - ai * hi + wr[r0:r0 + sub],
                                  ar * hi + ai * hr + wi[r0:r0 + sub])
                    hpr_ref[k * pair:(k + 1) * pair, lanes] = jnp.concatenate(prv_r, axis=0).astype(BF16)
                    hpi_ref[k * pair:(k + 1) * pair, lanes] = jnp.concatenate(prv_i, axis=0).astype(BF16)
                hr_ref[:, lanes] = hr
                hi_ref[:, lanes] = hi
            else:
                ar = jnp.broadcast_to(l4r_ref[:, lanes], (pair, n_cols))
                ai = jnp.broadcast_to(l4i_ref[:, lanes], (pair, n_cols))
                for m in range(n_seq // pair):
                    srows = pl.ds(m * pair, pair)
                    hr, hi = hr_ref[srows, lanes], hi_ref[srows, lanes]
                    for blk in range(n_blocks):
                        r0 = blk * n_seq + m * pair
                        hpr_ref[r0:r0 + pair, lanes] = hr.astype(BF16)
                        hpi_ref[r0:r0 + pair, lanes] = hi.astype(BF16)
                        hr, hi = (ar * hr - ai * hi + wr[r0:r0 + pair],
                                  ar * hi + ai * hr + wi[r0:r0 + pair])
                    hr_ref[srows, lanes] = hr
                    hi_ref[srows, lanes] = hi
            lhs_q.append(lhs)

    need(3, 4)
    for v in range(n_v):
        cols = slice(v * V7X_LANES, (v + 1) * V7X_LANES)
        yq = []
        for q in (2 * v, 2 * v + 1):
            lanes = pl.ds(q * n_cols, n_cols)
            st = jnp.concatenate([hpr_ref[:, lanes], hpi_ref[:, lanes]], axis=1)
            yq.append(_dot(st, vc_ref[q]) + _dot(lhs_q[q], kloc_ref[q]))
        ys = []
        for i in range(BLOCK):
            lo_q, hi_q = (a[:, (i // 2) * V7X_LANES:(i // 2 + 1) * V7X_LANES] for a in yq)
            if i % 2 == 0:
                yv = jnp.where(low, lo_q, pltpu.roll(hi_q, half, 1))
            else:
                yv = jnp.where(low, pltpu.roll(lo_q, half, 1), hi_q)
            ys.append(jax.nn.gelu(yv + dsk_ref[:, cols] * xi[i][:, cols]))
        y = jnp.stack([a.reshape(n_blocks, n_seq, V7X_LANES) for a in ys], axis=1)
        y_ref[:, cols] = y.reshape(n_steps * n_seq, V7X_LANES).astype(BF16)

    need(5, 6, 7)
    for c in range(n_pieces):
        y = y_ref[c * c_rows:(c + 1) * c_rows, :]
        g = (_dot(y, w1_ref[...]) + b1_ref[...]) * jax.nn.sigmoid(_dot(y, w2_ref[...]) + b2_ref[...])
        out = _dot((g * _silu(z_c[c])).astype(BF16), wout_ref[...])
        xn = x_c[c] + _rmsnorm(out, npost_ref[...])
        if swap_bt:
            obuf_ref.at[slot][c * c_steps:(c + 1) * c_steps] = xn.reshape(c_steps, n_seq, d)
        else:
            o_ref[c * c_rows:(c + 1) * c_rows, :] = xn

    if swap_bt:
        for cp in tile_copies(step, slot, False):
            cp.start()

    @pl.when(step == n_tiles - 1)
    def _():
        if n_prev:
            str_ref[...] = hr_ref[...]
            sti_ref[...] = hi_ref[...]
        else:
            for slab in range(str_ref.shape[0]):
                str_ref[slab] = hr_ref[...]
                sti_ref[slab] = hi_ref[...]
        if swap_bt:
            for cp in tile_copies(step, slot, False):
                cp.wait()

    if swap_bt:
        @pl.when(jnp.logical_and(step == n_tiles - 1, n_tiles >= 2))
        def _():
            for cp in tile_copies(step - 1, 1 - slot, False):
                cp.wait()


def _s5_layer(x, h0, layer_b, layer, npre, win, core_consts, out_consts, npost, n_seq, n_steps,
              st_prev, cast_jobs=()):
    swap_bt = x.ndim == 3
    d = x.shape[-1]
    exp_b = win[0].shape[2] // 2
    n_state = core_consts[0].shape[-1]
    tile = n_seq * n_steps
    n_tiles = x.size // (tile * d)
    rows = tile // BLOCK
    has_h0 = h0 is not None
    assert n_steps % BLOCK == 0 and (n_seq > V7X_SUBLANES or (n_steps // BLOCK) % 2 == 0)
    assert tile % PIECE_ROWS == 0 and PIECE_ROWS % n_seq == 0
    params_b = [win] + [(c, layer_b) for c in core_consts] + list(out_consts)
    assert len(params_b) + 2 == N_S5_PARAMS
    c_in, c_out, c_shapes = _cast_specs(cast_jobs, n_tiles)
    args = ([x] + (list(h0) if has_h0 else []) + [npre] + [a for a, _ in params_b] + [npost]
            + [w for w, _ in cast_jobs] + list(st_prev))
    if swap_bt:
        io_spec = pl.BlockSpec(memory_space=pl.ANY)
        io_scratch = [pltpu.VMEM((2, n_steps, n_seq, d), F32), pltpu.VMEM((2, n_steps, n_seq, d), F32),
                      pltpu.SemaphoreType.DMA((2, n_seq)), pltpu.SemaphoreType.DMA((2, n_seq))]
    else:
        io_spec = pl.BlockSpec((tile, d), lambda i: (i, 0))
        io_scratch = []
    lazy = n_tiles == 1
    lazy_idx = [p - 1 for p in S5_WEIGHTS]
    hbm = pl.BlockSpec(memory_space=pl.ANY)
    in_specs = ([io_spec] + [_layer_spec(a, layer_b) for a in (h0 if has_h0 else ())]
                + [_layer_spec(npre, layer)]
                + [hbm if lazy and k in lazy_idx else _layer_spec(a, l)
                   for k, (a, l) in enumerate(params_b)]
                + [_layer_spec(npost, layer)] + c_in + [hbm] * len(st_prev))
    lazy_scratch = ([pltpu.VMEM(params_b[k][0].shape[1:], params_b[k][0].dtype) for k in lazy_idx]
                    + [pltpu.SemaphoreType.DMA((len(lazy_idx),))]) if lazy else []
    n_stack = core_consts[0].shape[0]
    st_spec = (pl.BlockSpec((None, n_seq, n_state), lambda i: (layer_b, 0, 0)) if st_prev else
               pl.BlockSpec((n_stack, n_seq, n_state), lambda i: (0, 0, 0)))
    st_shape = jax.ShapeDtypeStruct((n_stack, n_seq, n_state), F32)
    n_args = len(args)
    footprint = (_layer_bytes(*[a for a, _ in params_b]) + (_layer_bytes(*h0) if has_h0 else 0)
                 + 4 * tile * d * 4 + 2 * rows * n_state * 2 + 8 * n_seq * n_state * 4
                 + (0 if st_prev else 4 * (n_stack - 1) * n_seq * n_state * 4)
                 + tile * exp_b * (2 + 6 * 4)
                 + 6 * sum(w[0].size for w, _ in cast_jobs) // n_tiles * 2)
    res = pl.pallas_call(
        functools.partial(_s5_layer_kernel, n_seq=n_seq, n_steps=n_steps, has_h0=has_h0,
                          swap_bt=swap_bt, n_cast=len(cast_jobs), n_prev=len(st_prev),
                          lazy_layers=tuple(params_b[k][1] for k in lazy_idx) if lazy else ()),
        grid=(n_tiles,),
        in_specs=in_specs,
        out_specs=[io_spec, st_spec, st_spec] + c_out,
        out_shape=[jax.ShapeDtypeStruct(x.shape, F32), st_shape, st_shape] + c_shapes,
        input_output_aliases={n_args - len(st_prev) + k: 1 + k for k in range(len(st_prev))},
        scratch_shapes=[pltpu.VMEM((rows, n_state), BF16), pltpu.VMEM((rows, n_state), BF16),
                        pltpu.VMEM((n_seq, n_state), F32), pltpu.VMEM((n_seq, n_state), F32),
                        pltpu.VMEM((tile, exp_b), BF16)] + io_scratch + lazy_scratch,
        compiler_params=pltpu.CompilerParams(
            dimension_semantics=("arbitrary",),
            vmem_limit_bytes=_vmem_limit(footprint)),
        name="s5_layer_sample" if has_h0 else "s5_layer_prompt",
    )(*args)
    return res[0], res[1], res[2], res[3:]


def kernel(x_prompt, x_sample, state_ssm_re, state_ssm_im, norm_pre, norm_post,
           w_in_a, ln_v_g, ln_v_b, w_s, b_s, w_out_a,
           w_in_b, a_re, a_im, log_dt, b_re, b_im, c_re, c_im, d_skip,
           w_glu1, b_glu1, w_glu2, b_glu2, w_out_b):
    n_batch, seq_len, d = x_prompt.shape
    n_dec, dec_len, _ = x_sample.shape
    depth = norm_pre.shape[0]
    n_ssm, n_groups, state_p = a_re.shape
    n_state = n_groups * state_p
    assert seq_len % ROW_TILE == 0 and ROW_TILE % CHUNK == 0 and seq_len % SCAN_STEPS == 0
    assert n_batch == V7X_SUBLANES and n_dec % (2 * V7X_SUBLANES) == 0
    assert dec_len <= CHUNK and dec_len % BLOCK == 0
    assert state_p == STATE_P and b_re.shape[-1] == SSM_GROUP and w_s.shape[1] == N_HEADS

    rows3 = lambda a: a.reshape(a.shape[0], 1, -1)
    npre, npost = rows3(norm_pre), rows3(norm_post)
    lng, lnb = rows3(ln_v_g), rows3(ln_v_b)
    wa16 = ((w_in_a, 0), (w_out_a, 0))
    bst = jnp.swapaxes(b_s, 1, 2)
    ws_dec = w_s[:, :, :dec_len, :dec_len].reshape(-1)
    bs_dec = b_s[:, :, :dec_len].reshape(-1)

    b1, b2 = rows3(b_glu1), rows3(b_glu2)
    lanes_gp = lambda a, perm: jnp.transpose(a, perm).reshape(n_ssm, SSM_GROUP, n_state)
    l4r, l4i, wur, wui, vcw, kloc = _s5_prep(
        rows3(a_re), rows3(a_im), rows3(jnp.repeat(log_dt, state_p, axis=1)),
        lanes_gp(b_re, (0, 3, 1, 2)), lanes_gp(b_im, (0, 3, 1, 2)),
        lanes_gp(c_re, (0, 2, 1, 3)), lanes_gp(c_im, (0, 2, 1, 3)))
    core_consts = (l4r, l4i, wur, wui, vcw, kloc, rows3(d_skip))
    h0 = (state_ssm_re.reshape(n_ssm, n_dec, n_state), state_ssm_im.reshape(n_ssm, n_dec, n_state))

    xp = x_prompt.reshape(n_batch * seq_len, d)
    xs = jnp.transpose(x_sample, (1, 0, 2)).reshape(dec_len * n_dec, d)

    chunk_v, st_p, st_s = None, (), ()
    for i in range(depth):
        j = i // 2
        if i % 2 == 0:
            jobs = [(w, j) for w in (w_in_b, w_glu1, w_glu2, w_out_b)] if i + 1 < depth else []
            xp, wb16, own16 = _gmlp_prompt(xp, j, i, npre, npost, lng, lnb, *wa16, w_s, bst, jobs)
            if own16:
                wa16 = tuple((w, 0) for w in own16)
            xs, chunk_v = _gmlp_sample(xs, n_dec, dec_len, j, i, npre, npost, lng, lnb, *wa16,
                                       ws_dec, bs_dec, chunk_v)
        else:
            win = (wb16[0], 0)
            out_consts = ((wb16[1], 0), (b1, j), (wb16[2], 0), (b2, j), (wb16[3], 0))
            jobs = [(w_in_a, j + 1), (w_out_a, j + 1)] if i + 1 < depth else []
            xp3, sr, si, wa16 = _s5_layer(xp.reshape(n_batch, seq_len, d), None, j, i, npre, win,
                                          core_consts, out_consts, npost, n_batch, SCAN_STEPS,
                                          st_p, jobs)
            wa16 = tuple((w, 0) for w in wa16)
            xp = xp3.reshape(n_batch * seq_len, d)
            st_p = (sr, si)
            xs, sr, si, _ = _s5_layer(xs, h0, j, i, npre, win, core_consts, out_consts, npost,
                                      n_dec, dec_len, st_s)
            st_s = (sr, si)

    y_prompt = xp.reshape(n_batch, seq_len, d)
    y_sample = jnp.transpose(xs.reshape(dec_len, n_dec, d), (1, 0, 2))
    states = lambda st, n: st.reshape(n_ssm, n, n_groups, state_p)
    return (y_prompt, y_sample, chunk_v, states(st_p[0], n_batch), states(st_p[1], n_batch),
            states(st_s[0], n_dec), states(st_s[1], n_dec))
```

```python
import functools

import jax
import jax.numpy as jnp
from jax import lax
from jax.experimental import pallas as pl
from jax.experimental.pallas import tpu as pltpu

EPS = 1e-6
CHUNK = 128
N_HEADS = 8
SSM_GROUP = 16
STATE_P = 64

V7X_LANES = 128
V7X_SUBLANES = 8
V7X_MXU_DIM = 256
V7X_VMEM_BYTES = 64 * 1024 * 1024

ROW_TILE = 1024
SCAN_STEPS = 64
BLOCK = 4
STATE_T = 512
PIECE_ROWS = 256

BF16 = jnp.bfloat16
F32 = jnp.float32


def _dot(a, b):
    return jnp.dot(a, b, preferred_element_type=F32)


def _rmsnorm(x, g):
    ms = jnp.mean(x * x, axis=-1, keepdims=True)
    return x * lax.rsqrt(ms + EPS) * g


def _layernorm(x, g, b):
    mu = jnp.mean(x, axis=-1, keepdims=True)
    xc = x - mu
    var = jnp.mean(xc * xc, axis=-1, keepdims=True)
    return xc * lax.rsqrt(var + EPS) * g + b


def _silu(z):
    return z * jax.nn.sigmoid(z)


def _div_pow2(x, n):
    assert n & (n - 1) == 0
    return lax.shift_right_logical(x, n.bit_length() - 1)


def _vmem_limit(nbytes):
    return int(min(V7X_VMEM_BYTES - (4 << 20), nbytes + (12 << 20)))


def _layer_spec(stacked, layer):
    tail = stacked.shape[1:]
    return pl.BlockSpec((None,) + tail, lambda *_: (layer,) + (0,) * len(tail),
                        pipeline_mode=pl.Buffered(1))


def _layer_bytes(*stacked):
    return sum(a[0].size * a.dtype.itemsize for a in stacked)


def _cast_specs(jobs, n_steps):
    in_specs, out_specs, out_shapes = [], [], []
    for w, layer in jobs:
        _, rows, cols = w.shape
        slab = rows // n_steps
        assert rows % n_steps == 0 and slab % (2 * V7X_SUBLANES) == 0
        in_specs.append(pl.BlockSpec((None, slab, cols), lambda i, layer=layer: (layer, i, 0)))
        out_specs.append(pl.BlockSpec((None, slab, cols), lambda i: (0, i, 0)))
        out_shapes.append(jax.ShapeDtypeStruct((1, rows, cols), BF16))
    return in_specs, out_specs, out_shapes


def _cast_slabs(in_refs, out_refs):
    for src, dst in zip(in_refs, out_refs):
        dst[...] = src[...].astype(BF16)


def _row_spec(rows, width):
    return pl.BlockSpec((rows, width), lambda i: (i, 0))


_SMEM = pl.BlockSpec(memory_space=pltpu.SMEM)


def _gmlp_front(x, npre_ref, lng_ref, lnb_ref, win_ref, exp_a):
    hn, v = [], []
    for r0 in range(0, x.shape[0], PIECE_ROWS):
        hn.append(_rmsnorm(x[r0:r0 + PIECE_ROWS], npre_ref[...]).astype(BF16))
        v.append(_layernorm(_dot(hn[-1], win_ref[:, exp_a:2 * exp_a]), lng_ref[...], lnb_ref[...]))
    return jnp.concatenate(hn, axis=0), jnp.concatenate(v, axis=0)


def _gmlp_back(x, gated_ref, wout_ref, npost_ref, o_ref):
    for r0 in range(0, x.shape[0], PIECE_ROWS):
        r = slice(r0, r0 + PIECE_ROWS)
        out = _dot(gated_ref[r, :], wout_ref[...])
        o_ref[r, :] = x[r] + _rmsnorm(out, npost_ref[...])


def _gmlp_prompt_kernel(x_ref, npre_ref, npost_ref, lng_ref, lnb_ref, win_ref, wout_ref,
                        ws_ref, bst_ref, *refs, n_cast, own_layer):
    o_ref = refs[n_cast]
    if own_layer is None:
        gated_ref = refs[-1]
        _cast_slabs(refs[:n_cast], refs[n_cast + 1:-1])
    else:
        gated_ref, win_v, wout_v, stage_w, stage_o, sem = refs[-6:]
        win16_hbm, wout16_hbm = refs[-8:-6]
        _cast_slabs(refs[:n_cast], refs[n_cast + 1:-8])
        step = pl.program_id(0)
        out_copies = [pltpu.make_async_copy(win_v, win16_hbm.at[0], sem.at[4]),
                      pltpu.make_async_copy(wout_v, wout16_hbm.at[0], sem.at[5])]

        def stream(src_of, dst_of, stage, sem0, n_chunks):
            cps = [pltpu.make_async_copy(src_of(k), stage.at[k % 2], sem.at[sem0 + k % 2])
                   for k in range(n_chunks)]
            cps[0].start()
            for k in range(n_chunks):
                if k + 1 < n_chunks:
                    cps[k + 1].start()
                cps[k].wait()
                dst_of(k)[...] = stage[k % 2].astype(BF16)

        @pl.when(step == 0)
        def _():
            cw, ro = stage_w.shape[2], stage_o.shape[1]
            stream(lambda k: win_ref.at[own_layer, :, pl.ds(k * cw, cw)],
                   lambda k: win_v.at[:, pl.ds(k * cw, cw)], stage_w, 0, win_v.shape[1] // cw)
            stream(lambda k: wout_ref.at[own_layer, pl.ds(k * ro, ro), :],
                   lambda k: wout_v.at[pl.ds(k * ro, ro), :], stage_o, 2, wout_v.shape[0] // ro)
            for cp in out_copies:
                cp.start()

        @pl.when(step == pl.num_programs(0) - 1)
        def _():
            for cp in out_copies:
                cp.wait()

        win_ref, wout_ref = win_v, wout_v
    rows = x_ref.shape[0]
    exp_a = wout_ref.shape[0]
    hd = exp_a // N_HEADS
    x = x_ref[...]
    hn, v = _gmlp_front(x, npre_ref, lng_ref, lnb_ref, win_ref, exp_a)
    vb = v.astype(BF16)
    causal = (lax.broadcasted_iota(jnp.int32, (CHUNK, CHUNK), 1)
              <= lax.broadcasted_iota(jnp.int32, (CHUNK, CHUNK), 0))
    for h in range(N_HEADS):
        lo = h * hd
        wsh = jnp.where(causal, ws_ref[h], 0.0).astype(BF16)
        bias = bst_ref[:, h:h + 1]
        u = _dot(hn, win_ref[:, lo:lo + hd])
        z = _dot(hn, win_ref[:, 2 * exp_a + lo:2 * exp_a + lo + hd])
        s = jnp.concatenate(
            [_dot(wsh, vb[c * CHUNK:(c + 1) * CHUNK, lo:lo + hd]) + bias
             for c in range(rows // CHUNK)], axis=0)
        gated_ref[:, lo:lo + hd] = (u * s * _silu(z)).astype(BF16)
    _gmlp_back(x, gated_ref, wout_ref, npost_ref, o_ref)


def _gmlp_sample_kernel(ws_ref, bs_ref, x_ref, npre_ref, npost_ref, lng_ref, lnb_ref, win_hbm,
                        wout_hbm, *refs, n_seq, seq_len, layer, w_layers):
    o_ref, v_hbm, gated_ref, win_ref, wout_ref, w_sem, v_ref, v_sem = refs[-8:]
    slabs = range(v_hbm.shape[0]) if len(refs) == 8 else [layer]
    exp_a = wout_ref.shape[0]
    hd = exp_a // N_HEADS
    col_parts = [slice(exp_a, 2 * exp_a), slice(0, exp_a), slice(2 * exp_a, 3 * exp_a)]
    copies = [pltpu.make_async_copy(win_hbm.at[w_layers[0], :, p], win_ref.at[:, p], w_sem.at[k])
              for k, p in enumerate(col_parts)]
    copies.append(pltpu.make_async_copy(wout_hbm.at[w_layers[1]], wout_ref, w_sem.at[len(col_parts)]))
    for cp in copies:
        cp.start()
    x = x_ref[...]
    copies[0].wait()
    hn, v = _gmlp_front(x, npre_ref, lng_ref, lnb_ref, win_ref, exp_a)
    v_ref[...] = v
    v_copies = [pltpu.make_async_copy(v_ref.at[pl.ds(t * n_seq, n_seq), :], v_hbm.at[l, :, t, :],
                                      v_sem.at[l, t]) for l in slabs for t in range(seq_len)]
    for cp in v_copies:
        cp.start()
    copies[1].wait()
    copies[2].wait()
    for h in range(N_HEADS):
        lo = h * hd
        u = _dot(hn, win_ref[:, lo:lo + hd])
        z = _dot(hn, win_ref[:, 2 * exp_a + lo:2 * exp_a + lo + hd])
        vt = [v[t * n_seq:(t + 1) * n_seq, lo:lo + hd] for t in range(seq_len)]
        parts = []
        for t in range(seq_len):
            b_idx = (layer * N_HEADS + h) * seq_len + t
            s = ws_ref[b_idx * seq_len] * vt[0]
            for t2 in range(1, t + 1):
                s = s + ws_ref[b_idx * seq_len + t2] * vt[t2]
            parts.append(s + bs_ref[b_idx])
        s = jnp.concatenate(parts, axis=0)
        gated_ref[:, lo:lo + hd] = (u * s * _silu(z)).astype(BF16)
    copies[3].wait()
    _gmlp_back(x, gated_ref, wout_ref, npost_ref, o_ref)
    for cp in v_copies:
        cp.wait()


def _gmlp_prompt(x2d, layer_a, layer, npre, npost, lng, lnb, win, wout, ws, bst, cast_jobs):
    rows, d = x2d.shape
    (win, win_l), (wout, wout_l) = win, wout
    own = win.dtype != BF16
    exp_a = wout.shape[1]
    n_steps = rows // ROW_TILE
    c_in, c_out, c_shapes = _cast_specs(cast_jobs, n_steps)
    hbm = pl.BlockSpec(memory_space=pl.ANY)
    stage_cols, stage_rows = 4 * V7X_LANES, 4 * V7X_LANES
    own_scratch = [pltpu.VMEM(win.shape[1:], BF16), pltpu.VMEM(wout.shape[1:], BF16),
                   pltpu.VMEM((2, win.shape[1], stage_cols), F32),
                   pltpu.VMEM((2, stage_rows, wout.shape[2]), F32),
                   pltpu.SemaphoreType.DMA((6,))] if own else []
    own_shapes = [jax.ShapeDtypeStruct((1,) + w.shape[1:], BF16) for w in (win, wout)] if own else []
    footprint = ((win[0].size + wout[0].size) * 2 + _layer_bytes(ws) + 4 * ROW_TILE * d * 4
                 + ROW_TILE * exp_a * (2 + 4 + 2 + 8)
                 + 6 * sum(w[0].size for w, _ in cast_jobs) // n_steps * 2
                 + (2 * (win.shape[1] * stage_cols + stage_rows * wout.shape[2]) * 4 if own else 0))
    res = pl.pallas_call(
        functools.partial(_gmlp_prompt_kernel, n_cast=len(cast_jobs),
                          own_layer=win_l if own else None),
        grid=(n_steps,),
        in_specs=[
            _row_spec(ROW_TILE, d),
            _layer_spec(npre, layer), _layer_spec(npost, layer),
            _layer_spec(lng, layer_a), _layer_spec(lnb, layer_a),
            hbm if own else _layer_spec(win, win_l), hbm if own else _layer_spec(wout, wout_l),
            _layer_spec(ws, layer_a), _layer_spec(bst, layer_a),
        ] + c_in,
        out_specs=[_row_spec(ROW_TILE, d)] + c_out + [hbm] * len(own_shapes),
        out_shape=[jax.ShapeDtypeStruct((rows, d), F32)] + c_shapes + own_shapes,
        scratch_shapes=[pltpu.VMEM((ROW_TILE, exp_a), BF16)] + own_scratch,
        compiler_params=pltpu.CompilerParams(
            dimension_semantics=("arbitrary",) if own else ("parallel",),
            vmem_limit_bytes=_vmem_limit(footprint)),
        name="gmlp_prompt",
    )(x2d, npre, npost, lng, lnb, win, wout, ws, bst, *[w for w, _ in cast_jobs])
    n_jobs = len(cast_jobs)
    return res[0], res[1:1 + n_jobs], res[1 + n_jobs:]


def _gmlp_sample(xs, n_seq, seq_len, layer_a, layer, npre, npost, lng, lnb, win, wout,
                 ws_flat, bs_flat, v_prev):
    rows, d = xs.shape
    (win, win_l), (wout, wout_l) = win, wout
    exp_a = wout.shape[1]
    footprint = _layer_bytes(win, wout) + 4 * rows * d * 4 + rows * exp_a * (2 + 8 + 4 + 8)
    hbm = pl.BlockSpec(memory_space=pl.ANY)
    prev = [] if v_prev is None else [v_prev]
    return pl.pallas_call(
        functools.partial(_gmlp_sample_kernel, n_seq=n_seq, seq_len=seq_len, layer=layer_a,
                          w_layers=(win_l, wout_l)),
        grid=(1,),
        in_specs=[_SMEM, _SMEM, _row_spec(rows, d),
                  _layer_spec(npre, layer), _layer_spec(npost, layer),
                  _layer_spec(lng, layer_a), _layer_spec(lnb, layer_a), hbm, hbm] + [hbm] * len(prev),
        out_specs=[_row_spec(rows, d), hbm],
        out_shape=[jax.ShapeDtypeStruct((rows, d), F32),
                   jax.ShapeDtypeStruct((lng.shape[0], n_seq, seq_len, exp_a), F32)],
        input_output_aliases={9: 1} if prev else {},
        scratch_shapes=[pltpu.VMEM((rows, exp_a), BF16), pltpu.VMEM(win.shape[1:], BF16),
                        pltpu.VMEM(wout.shape[1:], BF16), pltpu.SemaphoreType.DMA((4,)),
                        pltpu.VMEM((rows, exp_a), F32),
                        pltpu.SemaphoreType.DMA((lng.shape[0], seq_len))],
        compiler_params=pltpu.CompilerParams(
            dimension_semantics=("arbitrary",),
            vmem_limit_bytes=_vmem_limit(footprint)),
        name="gmlp_sample",
    )(ws_flat, bs_flat, xs, npre, npost, lng, lnb, win, wout, *prev)


def _s5_prep_kernel(are_ref, aim_ref, ldt_ref, btr_ref, bti_ref, ctr_ref, cti_ref,
                    l4r_ref, l4i_ref, wur_ref, wui_ref, vc_ref, kloc_ref):
    dt = jnp.exp(ldt_ref[...])
    ar = are_ref[...]
    ai = aim_ref[...]
    mag = jnp.exp(dt * ar)
    ang = dt * ai
    abr = mag * jnp.cos(ang)
    abi = mag * jnp.sin(ang)
    nr = abr - 1.0
    ni = abi
    den = ar * ar + ai * ai
    cre = (nr * ar + ni * ai) / den
    cim = (ni * ar - nr * ai) / den
    btr = btr_ref[...]
    bti = bti_ref[...]
    bbr = cre * btr - cim * bti
    bbi = cre * bti + cim * btr

    def cmul(xr, xi, yr, yi):
        return xr * yr - xi * yi, xr * yi + xi * yr

    lam = [(jnp.ones_like(abr), jnp.zeros_like(abi)), (abr, abi)]
    for _ in range(BLOCK - 1):
        lam.append(cmul(*lam[-1], abr, abi))
    l4r_ref[...] = lam[BLOCK][0]
    l4i_ref[...] = lam[BLOCK][1]

    n_q, k_rows, n_cols = wur_ref.shape
    q_groups = n_cols // STATE_P
    q_rows = q_groups * SSM_GROUP
    same_q = (_div_pow2(lax.broadcasted_iota(jnp.int32, (q_rows, n_cols), 0), SSM_GROUP)
              == _div_pow2(lax.broadcasted_iota(jnp.int32, (q_rows, n_cols), 1), STATE_P))
    for ip in range(BLOCK):
        ur, ui = cmul(bbr, bbi, *lam[BLOCK - 1 - ip])
        for q in range(n_q):
            for src, dst in ((ur, wur_ref), (ui, wui_ref)):
                blk = jnp.concatenate([src[:, q * n_cols:(q + 1) * n_cols]] * q_groups, axis=0)
                dst[q, ip * q_rows:(ip + 1) * q_rows, :] = jnp.where(same_q, blk, 0.0).astype(BF16)

    ctr = ctr_ref[...]
    cti = cti_ref[...]
    cl = [cmul(ctr, cti, *lam[t]) for t in range(BLOCK + 1)]

    def tile_rows(a, q):
        blk = jnp.concatenate([a[:, q * n_cols:(q + 1) * n_cols]] * q_groups, axis=0)
        return jnp.where(same_q, blk, 0.0)

    half = V7X_LANES // 2
    low = lax.broadcasted_iota(jnp.int32, (q_rows, V7X_LANES), 1) < half
    for q in range(n_q):
        clm = [(tile_rows(c[0], q), tile_rows(c[1], q)) for c in cl]
        rows = [jnp.concatenate([clm[i + 1][0], -clm[i + 1][1]], axis=1) for i in range(BLOCK)]
        vc_ref[q] = jnp.concatenate(rows, axis=0).T.astype(BF16)
        bcat = jnp.concatenate([tile_rows(bbr, q), -tile_rows(bbi, q)], axis=1).astype(BF16)
        ccat = jnp.concatenate([jnp.concatenate(clm[t], axis=1) for t in range(BLOCK)], axis=0)
        kall = lax.dot_general(bcat, ccat.astype(BF16), (((1,), (1,)), ((), ())),
                               preferred_element_type=F32)
        k0, k1 = kall[:, :V7X_LANES], kall[:, V7X_LANES:]
        r0, r1 = pltpu.roll(k0, half, 1), pltpu.roll(k1, half, 1)
        zero = jnp.zeros_like(k0)
        shifted = [(k0, k1),
                   (jnp.where(low, zero, r0), jnp.where(low, r0, r1)),
                   (zero, k0),
                   (zero, jnp.where(low, zero, r0))]
        kloc_ref[q] = jnp.concatenate([jnp.concatenate(sh, axis=1) for sh in shifted],
                                      axis=0).astype(BF16)


def _s5_prep(are, aim, ldt, btr, bti, ctr, cti):
    n_layers, _, n_state = are.shape
    n_q = n_state // V7X_MXU_DIM

    def per_layer(shape):
        return pl.BlockSpec((None,) + shape, lambda l: (l,) + (0,) * len(shape))

    out_tails = [(1, n_state), (1, n_state),
                 (n_q, V7X_MXU_DIM, V7X_MXU_DIM), (n_q, V7X_MXU_DIM, V7X_MXU_DIM),
                 (n_q, 2 * V7X_MXU_DIM, V7X_MXU_DIM), (n_q, V7X_MXU_DIM, V7X_MXU_DIM)]
    out_dtypes = [F32, F32, BF16, BF16, BF16, BF16]
    args = (are, aim, ldt, btr, bti, ctr, cti)
    return pl.pallas_call(
        _s5_prep_kernel,
        grid=(n_layers,),
        in_specs=[per_layer(a.shape[1:]) for a in args],
        out_specs=[per_layer(t) for t in out_tails],
        out_shape=[jax.ShapeDtypeStruct((n_layers,) + t, dt) for t, dt in zip(out_tails, out_dtypes)],
        compiler_params=pltpu.CompilerParams(
            dimension_semantics=("parallel",),
            vmem_limit_bytes=_vmem_limit(2 * sum(
                int(jnp.dtype(dt).itemsize) * functools.reduce(lambda a, b: a * b, t)
                for t, dt in zip(out_tails, out_dtypes)))),
        name="s5_prep",
    )(*args)


N_S5_PARAMS = 15
S5_WEIGHTS = (1, 4, 5, 6, 7, 9, 11, 13)


def _s5_layer_kernel(*refs, n_seq, n_steps, has_h0, swap_bt, n_cast, n_prev, lazy_layers):
    if has_h0:
        x_ref, h0r_ref, h0i_ref = refs[:3]
        refs = refs[3:]
    else:
        x_ref = refs[0]
        refs = refs[1:]
    params = list(refs[:N_S5_PARAMS])
    refs = refs[N_S5_PARAMS:]
    n_in = n_cast + n_prev
    o_ref, str_ref, sti_ref = refs[n_in:n_in + 3]
    n_lazy = len(S5_WEIGHTS) + 1 if lazy_layers else 0
    n_scratch = 5 + (4 if swap_bt else 0) + n_lazy
    scratch = refs[-n_scratch:]
    hpr_ref, hpi_ref, hr_ref, hi_ref, y_ref = scratch[:5]
    _cast_slabs(refs[:n_cast], refs[n_in + 3:-n_scratch])
    step = pl.program_id(0)
    n_tiles = pl.num_programs(0)

    w_copies = []
    if lazy_layers:
        w_sem = scratch[-1]
        for k, (p, layer) in enumerate(zip(S5_WEIGHTS, lazy_layers)):
            w_copies.append(pltpu.make_async_copy(params[p].at[layer], scratch[-n_lazy + k],
                                                  w_sem.at[k]))
            params[p] = scratch[-n_lazy + k]
        for cp in w_copies:
            cp.start()

    def need(*which):
        for k in which:
            if w_copies:
                w_copies[k].wait()

    (npre_ref, win_ref, l4r_ref, l4i_ref, wur_ref, wui_ref, vc_ref, kloc_ref, dsk_ref,
     w1_ref, b1_ref, w2_ref, b2_ref, wout_ref, npost_ref) = params

    if swap_bt:
        xbuf_ref, obuf_ref, in_sem, out_sem = scratch[5:9]
        slot = step % 2

        def tile_copies(tile, sl, fetch):
            seq_rows = pl.ds(tile * n_steps, n_steps)
            if fetch:
                return [pltpu.make_async_copy(x_ref.at[b, seq_rows, :], xbuf_ref.at[sl, :, b, :],
                                              in_sem.at[sl, b]) for b in range(n_seq)]
            return [pltpu.make_async_copy(obuf_ref.at[sl, :, b, :], o_ref.at[b, seq_rows, :],
                                          out_sem.at[sl, b]) for b in range(n_seq)]

        @pl.when(step == 0)
        def _():
            for cp in tile_copies(0, 0, True):
                cp.start()

        @pl.when(step + 1 < n_tiles)
        def _():
            for cp in tile_copies(step + 1, 1 - slot, True):
                cp.start()

        for cp in tile_copies(step, slot, True):
            cp.wait()

        @pl.when(step >= 2)
        def _():
            for cp in tile_copies(step - 2, slot, False):
                cp.wait()

    @pl.when(step == 0)
    def _():
        if has_h0:
            for c in range(0, hr_ref.shape[1], STATE_T):
                hr_ref[:, c:c + STATE_T] = h0r_ref[c:c + STATE_T, :].T
                hi_ref[:, c:c + STATE_T] = h0i_ref[c:c + STATE_T, :].T
        else:
            hr_ref[...] = jnp.zeros_like(hr_ref)
            hi_ref[...] = jnp.zeros_like(hi_ref)

    d = npre_ref.shape[-1]
    width = win_ref.shape[1] // 2
    c_rows = PIECE_ROWS
    c_steps = c_rows // n_seq
    n_pieces = n_steps // c_steps
    x_c, xb_c, z_c = [], [], []
    need(0)
    for c in range(n_pieces):
        if swap_bt:
            xc = xbuf_ref.at[slot][c * c_steps:(c + 1) * c_steps].reshape(c_rows, d)
        else:
            xc = x_ref[c * c_rows:(c + 1) * c_rows, :]
        hn = _rmsnorm(xc, npre_ref[...]).astype(BF16)
        x_c.append(xc)
        xb_c.append(_dot(hn, win_ref[:, :width]))
        z_c.append(_dot(hn, win_ref[:, width:]))
    xb = jnp.concatenate(xb_c, axis=0)

    n_blocks = n_steps // BLOCK
    rows = n_blocks * n_seq
    x4 = xb.reshape(n_blocks, BLOCK, n_seq, width)
    xi = [x4[:, i].reshape(rows, width) for i in range(BLOCK)]

    sub = V7X_SUBLANES
    pair = 2 * sub
    n_q, _, n_cols = wur_ref.shape
    n_v = width // V7X_LANES
    half = V7X_LANES // 2
    low = lax.broadcasted_iota(jnp.int32, (rows, V7X_LANES), 1) < half

    lhs_q = []
    need(1, 2)
    for v in range(n_v):
        cols = slice(v * V7X_LANES, (v + 1) * V7X_LANES)
        p = [a[:, cols] for a in xi]
        pr = [pltpu.roll(a, half, 1) for a in p]
        lhs_lo = jnp.concatenate([jnp.where(low, p[i], pr[i + 1]) for i in range(0, BLOCK, 2)], axis=1)
        lhs_hi = jnp.concatenate([jnp.where(low, pr[i], p[i + 1]) for i in range(0, BLOCK, 2)], axis=1)
        for q, lhs in ((2 * v, lhs_lo.astype(BF16)), (2 * v + 1, lhs_hi.astype(BF16))):
            lanes = pl.ds(q * n_cols, n_cols)
            wr = _dot(lhs, wur_ref[q])
            wi = _dot(lhs, wui_ref[q])
            if n_seq == sub:
                ar = jnp.broadcast_to(l4r_ref[:, lanes], (sub, n_cols))
                ai = jnp.broadcast_to(l4i_ref[:, lanes], (sub, n_cols))
                hr, hi = hr_ref[:, lanes], hi_ref[:, lanes]
                for k in range(n_blocks // 2):
                    prv_r, prv_i = [], []
                    for r0 in (2 * k * sub, (2 * k + 1) * sub):
                        prv_r.append(hr)
                        prv_i.append(hi)
                        hr, hi = (ar * hr - ai * hi + wr[r0:r0 + sub],
                                  ar * hi + ai * hr + wi[r0:r0 + sub])
                    hpr_ref[k * pair:(k + 1) * pair, lanes] = jnp.concatenate(prv_r, axis=0).astype(BF16)
                    hpi_ref[k * pair:(k + 1) * pair, lanes] = jnp.concatenate(prv_i, axis=0).astype(BF16)
                hr_ref[:, lanes] = hr
                hi_ref[:, lanes] = hi
            else:
                ar = jnp.broadcast_to(l4r_ref[:, lanes], (pair, n_cols))
                ai = jnp.broadcast_to(l4i_ref[:, lanes], (pair, n_cols))
                for m in range(n_seq // pair):
                    srows = pl.ds(m * pair, pair)
                    hr, hi = hr_ref[srows, lanes], hi_ref[srows, lanes]
                    for blk in range(n_blocks):
                        r0 = blk * n_seq + m * pair
                        hpr_ref[r0:r0 + pair, lanes] = hr.astype(BF16)
                        hpi_ref[r0:r0 + pair, lanes] = hi.astype(BF16)
                        hr, hi = (ar * hr - ai * hi + wr[r0:r0 + pair],
                                  ar * hi + ai * hr + wi[r0:r0 + pair])
                    hr_ref[srows, lanes] = hr
                    hi_ref[srows, lanes] = hi
            lhs_q.append(lhs)

    need(3, 4)
    for v in range(n_v):
        cols = slice(v * V7X_LANES, (v + 1) * V7X_LANES)
        yq = []
        for q in (2 * v, 2 * v + 1):
            lanes = pl.ds(q * n_cols, n_cols)
            st = jnp.concatenate([hpr_ref[:, lanes], hpi_ref[:, lanes]], axis=1)
            yq.append(_dot(st, vc_ref[q]) + _dot(lhs_q[q], kloc_ref[q]))
        ys = []
        for i in range(BLOCK):
            lo_q, hi_q = (a[:, (i // 2) * V7X_LANES:(i // 2 + 1) * V7X_LANES] for a in yq)
            if i % 2 == 0:
                yv = jnp.where(low, lo_q, pltpu.roll(hi_q, half, 1))
            else:
                yv = jnp.where(low, pltpu.roll(lo_q, half, 1), hi_q)
            ys.append(jax.nn.gelu(yv + dsk_ref[:, cols] * xi[i][:, cols]))
        y = jnp.stack([a.reshape(n_blocks, n_seq, V7X_LANES) for a in ys], axis=1)
        y_ref[:, cols] = y.reshape(n_steps * n_seq, V7X_LANES).astype(BF16)

    need(5, 6, 7)
    for c in range(n_pieces):
        y = y_ref[c * c_rows:(c + 1) * c_rows, :]
        g = (_dot(y, w1_ref[...]) + b1_ref[...]) * jax.nn.sigmoid(_dot(y, w2_ref[...]) + b2_ref[...])
        out = _dot((g * _silu(z_c[c])).astype(BF16), wout_ref[...])
        xn = x_c[c] + _rmsnorm(out, npost_ref[...])
        if swap_bt:
            obuf_ref.at[slot][c * c_steps:(c + 1) * c_steps] = xn.reshape(c_steps, n_seq, d)
        else:
            o_ref[c * c_rows:(c + 1) * c_rows, :] = xn

    if swap_bt:
        for cp in tile_copies(step, slot, False):
            cp.start()

    @pl.when(step == n_tiles - 1)
    def _():
        for dst, src in ((str_ref, hr_ref), (sti_ref, hi_ref)):
            for view in ([dst] if n_prev else [dst.at[slab] for slab in range(dst.shape[0])]):
                if has_h0:
                    for c in range(0, src.shape[1], STATE_T):
                        view[c:c + STATE_T, :] = src[:, c:c + STATE_T].T
                else:
                    view[...] = src[...]
        if swap_bt:
            for cp in tile_copies(step, slot, False):
                cp.wait()

    if swap_bt:
        @pl.when(jnp.logical_and(step == n_tiles - 1, n_tiles >= 2))
        def _():
            for cp in tile_copies(step - 1, 1 - slot, False):
                cp.wait()


def _s5_layer(x, h0, layer_b, layer, npre, win, core_consts, out_consts, npost, n_seq, n_steps,
              st_prev, cast_jobs=()):
    swap_bt = x.ndim == 3
    d = x.shape[-1]
    exp_b = win[0].shape[2] // 2
    n_state = core_consts[0].shape[-1]
    tile = n_seq * n_steps
    n_tiles = x.size // (tile * d)
    rows = tile // BLOCK
    has_h0 = h0 is not None
    assert n_steps % BLOCK == 0 and (n_seq > V7X_SUBLANES or (n_steps // BLOCK) % 2 == 0)
    assert tile % PIECE_ROWS == 0 and PIECE_ROWS % n_seq == 0
    params_b = [win] + [(c, layer_b) for c in core_consts] + list(out_consts)
    assert len(params_b) + 2 == N_S5_PARAMS
    c_in, c_out, c_shapes = _cast_specs(cast_jobs, n_tiles)
    args = ([x] + (list(h0) if has_h0 else []) + [npre] + [a for a, _ in params_b] + [npost]
            + [w for w, _ in cast_jobs] + list(st_prev))
    if swap_bt:
        io_spec = pl.BlockSpec(memory_space=pl.ANY)
        io_scratch = [pltpu.VMEM((2, n_steps, n_seq, d), F32), pltpu.VMEM((2, n_steps, n_seq, d), F32),
                      pltpu.SemaphoreType.DMA((2, n_seq)), pltpu.SemaphoreType.DMA((2, n_seq))]
    else:
        io_spec = pl.BlockSpec((tile, d), lambda i: (i, 0))
        io_scratch = []
    lazy = n_tiles == 1
    lazy_idx = [p - 1 for p in S5_WEIGHTS]
    hbm = pl.BlockSpec(memory_space=pl.ANY)
    in_specs = ([io_spec] + [_layer_spec(a, layer_b) for a in (h0 if has_h0 else ())]
                + [_layer_spec(npre, layer)]
                + [hbm if lazy and k in lazy_idx else _layer_spec(a, l)
                   for k, (a, l) in enumerate(params_b)]
                + [_layer_spec(npost, layer)] + c_in + [hbm] * len(st_prev))
    lazy_scratch = ([pltpu.VMEM(params_b[k][0].shape[1:], params_b[k][0].dtype) for k in lazy_idx]
                    + [pltpu.SemaphoreType.DMA((len(lazy_idx),))]) if lazy else []
    n_stack = core_consts[0].shape[0]
    st_block = (n_state, n_seq) if has_h0 else (n_seq, n_state)
    st_spec = (pl.BlockSpec((None,) + st_block, lambda i: (layer_b, 0, 0)) if st_prev else
               pl.BlockSpec((n_stack,) + st_block, lambda i: (0, 0, 0)))
    st_shape = jax.ShapeDtypeStruct((n_stack,) + st_block, F32)
    n_args = len(args)
    footprint = (_layer_bytes(*[a for a, _ in params_b]) + (_layer_bytes(*h0) if has_h0 else 0)
                 + 4 * tile * d * 4 + 2 * rows * n_state * 2 + 8 * n_seq * n_state * 4
                 + (0 if st_prev else 4 * (n_stack - 1) * n_seq * n_state * 4)
                 + tile * exp_b * (2 + 6 * 4)
                 + 6 * sum(w[0].size for w, _ in cast_jobs) // n_tiles * 2)
    res = pl.pallas_call(
        functools.partial(_s5_layer_kernel, n_seq=n_seq, n_steps=n_steps, has_h0=has_h0,
                          swap_bt=swap_bt, n_cast=len(cast_jobs), n_prev=len(st_prev),
                          lazy_layers=tuple(params_b[k][1] for k in lazy_idx) if lazy else ()),
        grid=(n_tiles,),
        in_specs=in_specs,
        out_specs=[io_spec, st_spec, st_spec] + c_out,
        out_shape=[jax.ShapeDtypeStruct(x.shape, F32), st_shape, st_shape] + c_shapes,
        input_output_aliases={n_args - len(st_prev) + k: 1 + k for k in range(len(st_prev))},
        scratch_shapes=[pltpu.VMEM((rows, n_state), BF16), pltpu.VMEM((rows, n_state), BF16),
                        pltpu.VMEM((n_seq, n_state), F32), pltpu.VMEM((n_seq, n_state), F32),
                        pltpu.VMEM((tile, exp_b), BF16)] + io_scratch + lazy_scratch,
        compiler_params=pltpu.CompilerParams(
            dimension_semantics=("arbitrary",),
            vmem_limit_bytes=_vmem_limit(footprint)),
        name="s5_layer_sample" if has_h0 else "s5_layer_prompt",
    )(*args)
    return res[0], res[1], res[2], res[3:]


def kernel(x_prompt, x_sample, state_ssm_re, state_ssm_im, norm_pre, norm_post,
           w_in_a, ln_v_g, ln_v_b, w_s, b_s, w_out_a,
           w_in_b, a_re, a_im, log_dt, b_re, b_im, c_re, c_im, d_skip,
           w_glu1, b_glu1, w_glu2, b_glu2, w_out_b):
    n_batch, seq_len, d = x_prompt.shape
    n_dec, dec_len, _ = x_sample.shape
    depth = norm_pre.shape[0]
    n_ssm, n_groups, state_p = a_re.shape
    n_state = n_groups * state_p
    assert seq_len % ROW_TILE == 0 and ROW_TILE % CHUNK == 0 and seq_len % SCAN_STEPS == 0
    assert n_batch == V7X_SUBLANES and n_dec % (2 * V7X_SUBLANES) == 0
    assert dec_len <= CHUNK and dec_len % BLOCK == 0
    assert state_p == STATE_P and b_re.shape[-1] == SSM_GROUP and w_s.shape[1] == N_HEADS

    rows3 = lambda a: a.reshape(a.shape[0], 1, -1)
    npre, npost = rows3(norm_pre), rows3(norm_post)
    lng, lnb = rows3(ln_v_g), rows3(ln_v_b)
    wa16 = ((w_in_a, 0), (w_out_a, 0))
    bst = jnp.swapaxes(b_s, 1, 2)
    ws_dec = w_s[:, :, :dec_len, :dec_len].reshape(-1)
    bs_dec = b_s[:, :, :dec_len].reshape(-1)

    b1, b2 = rows3(b_glu1), rows3(b_glu2)
    lanes_gp = lambda a, perm: jnp.transpose(a, perm).reshape(n_ssm, SSM_GROUP, n_state)
    l4r, l4i, wur, wui, vcw, kloc = _s5_prep(
        rows3(a_re), rows3(a_im), rows3(jnp.repeat(log_dt, state_p, axis=1)),
        lanes_gp(b_re, (0, 3, 1, 2)), lanes_gp(b_im, (0, 3, 1, 2)),
        lanes_gp(c_re, (0, 2, 1, 3)), lanes_gp(c_im, (0, 2, 1, 3)))
    core_consts = (l4r, l4i, wur, wui, vcw, kloc, rows3(d_skip))
    h0 = tuple(jnp.swapaxes(st.reshape(n_ssm, n_dec, n_state), 1, 2)
               for st in (state_ssm_re, state_ssm_im))

    xp = x_prompt.reshape(n_batch * seq_len, d)
    xs = jnp.transpose(x_sample, (1, 0, 2)).reshape(dec_len * n_dec, d)

    chunk_v, st_p, st_s = None, (), ()
    for i in range(depth):
        j = i // 2
        if i % 2 == 0:
            jobs = [(w, j) for w in (w_in_b, w_glu1, w_glu2, w_out_b)] if i + 1 < depth else []
            xp, wb16, own16 = _gmlp_prompt(xp, j, i, npre, npost, lng, lnb, *wa16, w_s, bst, jobs)
            if own16:
                wa16 = tuple((w, 0) for w in own16)
            xs, chunk_v = _gmlp_sample(xs, n_dec, dec_len, j, i, npre, npost, lng, lnb, *wa16,
                                       ws_dec, bs_dec, chunk_v)
        else:
            win = (wb16[0], 0)
            out_consts = ((wb16[1], 0), (b1, j), (wb16[2], 0), (b2, j), (wb16[3], 0))
            jobs = [(w_in_a, j + 1), (w_out_a, j + 1)] if i + 1 < depth else []
            xp3, sr, si, wa16 = _s5_layer(xp.reshape(n_batch, seq_len, d), None, j, i, npre, win,
                                          core_consts, out_consts, npost, n_batch, SCAN_STEPS,
                                          st_p, jobs)
            wa16 = tuple((w, 0) for w in wa16)
            xp = xp3.reshape(n_batch * seq_len, d)
            st_p = (sr, si)
            xs, sr, si, _ = _s5_layer(xs, h0, j, i, npre, win, core_consts, out_consts, npost,
                                      n_dec, dec_len, st_s)
            st_s = (sr, si)

    y_prompt = xp.reshape(n_batch, seq_len, d)
    y_sample = jnp.transpose(xs.reshape(dec_len, n_dec, d), (1, 0, 2))
    states = lambda st, n: st.reshape(n_ssm, n, n_groups, state_p)
    st_s = tuple(jnp.swapaxes(st, 1, 2) for st in st_s)
    return (y_prompt, y_sample, chunk_v, states(st_p[0], n_batch), states(st_p[1], n_batch),
            states(st_s[0], n_dec), states(st_s[1], n_dec))
```

```python
import functools

import jax
import jax.numpy as jnp
from jax import lax
from jax.experimental import pallas as pl
from jax.experimental.pallas import tpu as pltpu

EPS = 1e-6
CHUNK = 128
N_HEADS = 8
SSM_GROUP = 16
STATE_P = 64

V7X_LANES = 128
V7X_SUBLANES = 8
V7X_MXU_DIM = 256
V7X_VMEM_BYTES = 64 * 1024 * 1024

ROW_TILE = 1024
SCAN_STEPS = 64
BLOCK = 4
STATE_T = 512
PIECE_ROWS = 256

BF16 = jnp.bfloat16
F32 = jnp.float32


def _dot(a, b):
    return jnp.dot(a, b, preferred_element_type=F32)


def _rmsnorm(x, g):
    ms = jnp.mean(x * x, axis=-1, keepdims=True)
    return x * lax.rsqrt(ms + EPS) * g


def _layernorm(x, g, b):
    mu = jnp.mean(x, axis=-1, keepdims=True)
    xc = x - mu
    var = jnp.mean(xc * xc, axis=-1, keepdims=True)
    return xc * lax.rsqrt(var + EPS) * g + b


def _silu(z):
    return z * jax.nn.sigmoid(z)


def _div_pow2(x, n):
    assert n & (n - 1) == 0
    return lax.shift_right_logical(x, n.bit_length() - 1)


def _vmem_limit(nbytes):
    return int(min(V7X_VMEM_BYTES - (4 << 20), nbytes + (12 << 20)))


def _layer_spec(stacked, layer):
    if stacked.ndim == 2:
        return pl.BlockSpec(stacked.shape, lambda *_: (0, 0), pipeline_mode=pl.Buffered(1))
    tail = stacked.shape[1:]
    return pl.BlockSpec((None,) + tail, lambda *_: (layer,) + (0,) * len(tail),
                        pipeline_mode=pl.Buffered(1))


def _row_views(body, layers):
    def narrowed(*refs):
        refs = list(refs)
        for pos, layer in layers.items():
            refs[pos] = refs[pos].at[pl.ds(layer, 1), :]
        return body(*refs)
    return narrowed


def _layer_bytes(*stacked):
    return sum(a[0].size * a.dtype.itemsize for a in stacked)


def _cast_specs(jobs, n_steps):
    in_specs, out_specs, out_shapes = [], [], []
    for w, layer in jobs:
        _, rows, cols = w.shape
        slab = rows // n_steps
        assert rows % n_steps == 0 and slab % (2 * V7X_SUBLANES) == 0
        in_specs.append(pl.BlockSpec((None, slab, cols), lambda i, layer=layer: (layer, i, 0)))
        out_specs.append(pl.BlockSpec((None, slab, cols), lambda i: (0, i, 0)))
        out_shapes.append(jax.ShapeDtypeStruct((1, rows, cols), BF16))
    return in_specs, out_specs, out_shapes


def _cast_slabs(in_refs, out_refs):
    for src, dst in zip(in_refs, out_refs):
        dst[...] = src[...].astype(BF16)


def _row_spec(rows, width):
    return pl.BlockSpec((rows, width), lambda i: (i, 0))


_SMEM = pl.BlockSpec(memory_space=pltpu.SMEM)


def _gmlp_front(x, npre_ref, lng_ref, lnb_ref, win_ref, exp_a):
    hn, v = [], []
    for r0 in range(0, x.shape[0], PIECE_ROWS):
        hn.append(_rmsnorm(x[r0:r0 + PIECE_ROWS], npre_ref[...]).astype(BF16))
        v.append(_layernorm(_dot(hn[-1], win_ref[:, exp_a:2 * exp_a]), lng_ref[...], lnb_ref[...]))
    return jnp.concatenate(hn, axis=0), jnp.concatenate(v, axis=0)


def _gmlp_back(x, gated_ref, wout_ref, npost_ref, o_ref):
    for r0 in range(0, x.shape[0], PIECE_ROWS):
        r = slice(r0, r0 + PIECE_ROWS)
        out = _dot(gated_ref[r, :], wout_ref[...])
        o_ref[r, :] = x[r] + _rmsnorm(out, npost_ref[...])


def _gmlp_prompt_kernel(x_ref, npre_ref, npost_ref, lng_ref, lnb_ref, win_ref, wout_ref,
                        ws_ref, bst_ref, *refs, n_cast, own_layer):
    o_ref = refs[n_cast]
    if own_layer is None:
        gated_ref = refs[-1]
        _cast_slabs(refs[:n_cast], refs[n_cast + 1:-1])
    else:
        gated_ref, win_v, wout_v, stage_w, stage_o, sem = refs[-6:]
        win16_hbm, wout16_hbm = refs[-8:-6]
        _cast_slabs(refs[:n_cast], refs[n_cast + 1:-8])
        step = pl.program_id(0)
        out_copies = [pltpu.make_async_copy(win_v, win16_hbm.at[0], sem.at[4]),
                      pltpu.make_async_copy(wout_v, wout16_hbm.at[0], sem.at[5])]

        def stream(src_of, dst_of, stage, sem0, n_chunks):
            cps = [pltpu.make_async_copy(src_of(k), stage.at[k % 2], sem.at[sem0 + k % 2])
                   for k in range(n_chunks)]
            cps[0].start()
            for k in range(n_chunks):
                if k + 1 < n_chunks:
                    cps[k + 1].start()
                cps[k].wait()
                dst_of(k)[...] = stage[k % 2].astype(BF16)

        @pl.when(step == 0)
        def _():
            cw, ro = stage_w.shape[2], stage_o.shape[1]
            stream(lambda k: win_ref.at[own_layer, :, pl.ds(k * cw, cw)],
                   lambda k: win_v.at[:, pl.ds(k * cw, cw)], stage_w, 0, win_v.shape[1] // cw)
            stream(lambda k: wout_ref.at[own_layer, pl.ds(k * ro, ro), :],
                   lambda k: wout_v.at[pl.ds(k * ro, ro), :], stage_o, 2, wout_v.shape[0] // ro)
            for cp in out_copies:
                cp.start()

        @pl.when(step == pl.num_programs(0) - 1)
        def _():
            for cp in out_copies:
                cp.wait()

        win_ref, wout_ref = win_v, wout_v
    rows = x_ref.shape[0]
    exp_a = wout_ref.shape[0]
    hd = exp_a // N_HEADS
    x = x_ref[...]
    hn, v = _gmlp_front(x, npre_ref, lng_ref, lnb_ref, win_ref, exp_a)
    vb = v.astype(BF16)
    causal = (lax.broadcasted_iota(jnp.int32, (CHUNK, CHUNK), 1)
              <= lax.broadcasted_iota(jnp.int32, (CHUNK, CHUNK), 0))
    for h in range(N_HEADS):
        lo = h * hd
        wsh = jnp.where(causal, ws_ref[h], 0.0).astype(BF16)
        bias = bst_ref[:, h:h + 1]
        u = _dot(hn, win_ref[:, lo:lo + hd])
        z = _dot(hn, win_ref[:, 2 * exp_a + lo:2 * exp_a + lo + hd])
        s = jnp.concatenate(
            [_dot(wsh, vb[c * CHUNK:(c + 1) * CHUNK, lo:lo + hd]) + bias
             for c in range(rows // CHUNK)], axis=0)
        gated_ref[:, lo:lo + hd] = (u * s * _silu(z)).astype(BF16)
    _gmlp_back(x, gated_ref, wout_ref, npost_ref, o_ref)


def _gmlp_sample_kernel(ws_ref, bs_ref, x_ref, npre_ref, npost_ref, lng_ref, lnb_ref, win_hbm,
                        wout_hbm, *refs, n_seq, seq_len, layer, w_layers):
    o_ref, v_hbm, gated_ref, win_ref, wout_ref, w_sem, v_ref, v_sem = refs[-8:]
    slabs = range(v_hbm.shape[0]) if len(refs) == 8 else [layer]
    exp_a = wout_ref.shape[0]
    hd = exp_a // N_HEADS
    col_parts = [slice(exp_a, 2 * exp_a), slice(0, exp_a), slice(2 * exp_a, 3 * exp_a)]
    copies = [pltpu.make_async_copy(win_hbm.at[w_layers[0], :, p], win_ref.at[:, p], w_sem.at[k])
              for k, p in enumerate(col_parts)]
    copies.append(pltpu.make_async_copy(wout_hbm.at[w_layers[1]], wout_ref, w_sem.at[len(col_parts)]))
    for cp in copies:
        cp.start()
    x = x_ref[...]
    copies[0].wait()
    hn, v = _gmlp_front(x, npre_ref, lng_ref, lnb_ref, win_ref, exp_a)
    v_ref[...] = v
    v_copies = [pltpu.make_async_copy(v_ref.at[pl.ds(t * n_seq, n_seq), :], v_hbm.at[l, :, t, :],
                                      v_sem.at[l, t]) for l in slabs for t in range(seq_len)]
    for cp in v_copies:
        cp.start()
    copies[1].wait()
    copies[2].wait()
    for h in range(N_HEADS):
        lo = h * hd
        u = _dot(hn, win_ref[:, lo:lo + hd])
        z = _dot(hn, win_ref[:, 2 * exp_a + lo:2 * exp_a + lo + hd])
        vt = [v[t * n_seq:(t + 1) * n_seq, lo:lo + hd] for t in range(seq_len)]
        parts = []
        for t in range(seq_len):
            b_idx = (layer * N_HEADS + h) * seq_len + t
            s = ws_ref[b_idx * seq_len] * vt[0]
            for t2 in range(1, t + 1):
                s = s + ws_ref[b_idx * seq_len + t2] * vt[t2]
            parts.append(s + bs_ref[b_idx])
        s = jnp.concatenate(parts, axis=0)
        gated_ref[:, lo:lo + hd] = (u * s * _silu(z)).astype(BF16)
    copies[3].wait()
    _gmlp_back(x, gated_ref, wout_ref, npost_ref, o_ref)
    for cp in v_copies:
        cp.wait()


def _gmlp_prompt(x2d, layer_a, layer, npre, npost, lng, lnb, win, wout, ws, bst, cast_jobs):
    rows, d = x2d.shape
    (win, win_l), (wout, wout_l) = win, wout
    own = win.dtype != BF16
    exp_a = wout.shape[1]
    n_steps = rows // ROW_TILE
    c_in, c_out, c_shapes = _cast_specs(cast_jobs, n_steps)
    hbm = pl.BlockSpec(memory_space=pl.ANY)
    stage_cols, stage_rows = 4 * V7X_LANES, 4 * V7X_LANES
    own_scratch = [pltpu.VMEM(win.shape[1:], BF16), pltpu.VMEM(wout.shape[1:], BF16),
                   pltpu.VMEM((2, win.shape[1], stage_cols), F32),
                   pltpu.VMEM((2, stage_rows, wout.shape[2]), F32),
                   pltpu.SemaphoreType.DMA((6,))] if own else []
    own_shapes = [jax.ShapeDtypeStruct((1,) + w.shape[1:], BF16) for w in (win, wout)] if own else []
    footprint = ((win[0].size + wout[0].size) * 2 + _layer_bytes(ws) + 4 * ROW_TILE * d * 4
                 + ROW_TILE * exp_a * (2 + 4 + 2 + 8)
                 + 6 * sum(w[0].size for w, _ in cast_jobs) // n_steps * 2
                 + (2 * (win.shape[1] * stage_cols + stage_rows * wout.shape[2]) * 4 if own else 0))
    res = pl.pallas_call(
        _row_views(functools.partial(_gmlp_prompt_kernel, n_cast=len(cast_jobs),
                                     own_layer=win_l if own else None),
                   {1: layer, 2: layer, 3: layer_a, 4: layer_a}),
        grid=(n_steps,),
        in_specs=[
            _row_spec(ROW_TILE, d),
            _layer_spec(npre, layer), _layer_spec(npost, layer),
            _layer_spec(lng, layer_a), _layer_spec(lnb, layer_a),
            hbm if own else _layer_spec(win, win_l), hbm if own else _layer_spec(wout, wout_l),
            _layer_spec(ws, layer_a), _layer_spec(bst, layer_a),
        ] + c_in,
        out_specs=[_row_spec(ROW_TILE, d)] + c_out + [hbm] * len(own_shapes),
        out_shape=[jax.ShapeDtypeStruct((rows, d), F32)] + c_shapes + own_shapes,
        scratch_shapes=[pltpu.VMEM((ROW_TILE, exp_a), BF16)] + own_scratch,
        compiler_params=pltpu.CompilerParams(
            dimension_semantics=("arbitrary",) if own else ("parallel",),
            vmem_limit_bytes=_vmem_limit(footprint)),
        name="gmlp_prompt",
    )(x2d, npre, npost, lng, lnb, win, wout, ws, bst, *[w for w, _ in cast_jobs])
    n_jobs = len(cast_jobs)
    return res[0], res[1:1 + n_jobs], res[1 + n_jobs:]


def _gmlp_sample(xs, n_seq, seq_len, layer_a, layer, npre, npost, lng, lnb, win, wout,
                 ws_flat, bs_flat, v_prev):
    rows, d = xs.shape
    (win, win_l), (wout, wout_l) = win, wout
    exp_a = wout.shape[1]
    footprint = _layer_bytes(win, wout) + 4 * rows * d * 4 + rows * exp_a * (2 + 8 + 4 + 8)
    hbm = pl.BlockSpec(memory_space=pl.ANY)
    prev = [] if v_prev is None else [v_prev]
    return pl.pallas_call(
        _row_views(functools.partial(_gmlp_sample_kernel, n_seq=n_seq, seq_len=seq_len,
                                     layer=layer_a, w_layers=(win_l, wout_l)),
                   {3: layer, 4: layer, 5: layer_a, 6: layer_a}),
        grid=(1,),
        in_specs=[_SMEM, _SMEM, _row_spec(rows, d),
                  _layer_spec(npre, layer), _layer_spec(npost, layer),
                  _layer_spec(lng, layer_a), _layer_spec(lnb, layer_a), hbm, hbm] + [hbm] * len(prev),
        out_specs=[_row_spec(rows, d), hbm],
        out_shape=[jax.ShapeDtypeStruct((rows, d), F32),
                   jax.ShapeDtypeStruct((lng.shape[0], n_seq, seq_len, exp_a), F32)],
        input_output_aliases={9: 1} if prev else {},
        scratch_shapes=[pltpu.VMEM((rows, exp_a), BF16), pltpu.VMEM(win.shape[1:], BF16),
                        pltpu.VMEM(wout.shape[1:], BF16), pltpu.SemaphoreType.DMA((4,)),
                        pltpu.VMEM((rows, exp_a), F32),
                        pltpu.SemaphoreType.DMA((lng.shape[0], seq_len))],
        compiler_params=pltpu.CompilerParams(
            dimension_semantics=("arbitrary",),
            vmem_limit_bytes=_vmem_limit(footprint)),
        name="gmlp_sample",
    )(ws_flat, bs_flat, xs, npre, npost, lng, lnb, win, wout, *prev)


def _s5_prep_kernel(are_ref, aim_ref, ldt_ref, btr_ref, bti_ref, ctr_ref, cti_ref,
                    l4r_ref, l4i_ref, wur_ref, wui_ref, vc_ref, kloc_ref):
    dt = jnp.exp(ldt_ref[...])
    ar = are_ref[...]
    ai = aim_ref[...]
    mag = jnp.exp(dt * ar)
    ang = dt * ai
    abr = mag * jnp.cos(ang)
    abi = mag * jnp.sin(ang)
    nr = abr - 1.0
    ni = abi
    den = ar * ar + ai * ai
    cre = (nr * ar + ni * ai) / den
    cim = (ni * ar - nr * ai) / den
    btr = btr_ref[...]
    bti = bti_ref[...]
    bbr = cre * btr - cim * bti
    bbi = cre * bti + cim * btr

    def cmul(xr, xi, yr, yi):
        return xr * yr - xi * yi, xr * yi + xi * yr

    lam = [(jnp.ones_like(abr), jnp.zeros_like(abi)), (abr, abi)]
    for _ in range(BLOCK - 1):
        lam.append(cmul(*lam[-1], abr, abi))
    l4r_ref[...] = lam[BLOCK][0]
    l4i_ref[...] = lam[BLOCK][1]

    n_q, k_rows, n_cols = wur_ref.shape
    q_groups = n_cols // STATE_P
    q_rows = q_groups * SSM_GROUP
    same_q = (_div_pow2(lax.broadcasted_iota(jnp.int32, (q_rows, n_cols), 0), SSM_GROUP)
              == _div_pow2(lax.broadcasted_iota(jnp.int32, (q_rows, n_cols), 1), STATE_P))
    for ip in range(BLOCK):
        ur, ui = cmul(bbr, bbi, *lam[BLOCK - 1 - ip])
        for q in range(n_q):
            for src, dst in ((ur, wur_ref), (ui, wui_ref)):
                blk = jnp.concatenate([src[:, q * n_cols:(q + 1) * n_cols]] * q_groups, axis=0)
                dst[q, ip * q_rows:(ip + 1) * q_rows, :] = jnp.where(same_q, blk, 0.0).astype(BF16)

    ctr = ctr_ref[...]
    cti = cti_ref[...]
    cl = [cmul(ctr, cti, *lam[t]) for t in range(BLOCK + 1)]

    def tile_rows(a, q):
        blk = jnp.concatenate([a[:, q * n_cols:(q + 1) * n_cols]] * q_groups, axis=0)
        return jnp.where(same_q, blk, 0.0)

    half = V7X_LANES // 2
    low = lax.broadcasted_iota(jnp.int32, (q_rows, V7X_LANES), 1) < half
    for q in range(n_q):
        clm = [(tile_rows(c[0], q), tile_rows(c[1], q)) for c in cl]
        rows = [jnp.concatenate([clm[i + 1][0], -clm[i + 1][1]], axis=1) for i in range(BLOCK)]
        vc_ref[q] = jnp.concatenate(rows, axis=0).T.astype(BF16)
        bcat = jnp.concatenate([tile_rows(bbr, q), -tile_rows(bbi, q)], axis=1).astype(BF16)
        ccat = jnp.concatenate([jnp.concatenate(clm[t], axis=1) for t in range(BLOCK)], axis=0)
        kall = lax.dot_general(bcat, ccat.astype(BF16), (((1,), (1,)), ((), ())),
                               preferred_element_type=F32)
        k0, k1 = kall[:, :V7X_LANES], kall[:, V7X_LANES:]
        r0, r1 = pltpu.roll(k0, half, 1), pltpu.roll(k1, half, 1)
        zero = jnp.zeros_like(k0)
        shifted = [(k0, k1),
                   (jnp.where(low, zero, r0), jnp.where(low, r0, r1)),
                   (zero, k0),
                   (zero, jnp.where(low, zero, r0))]
        kloc_ref[q] = jnp.concatenate([jnp.concatenate(sh, axis=1) for sh in shifted],
                                      axis=0).astype(BF16)


def _s5_prep(are, aim, ldt, btr, bti, ctr, cti):
    n_layers, _, n_state = are.shape
    n_q = n_state // V7X_MXU_DIM

    def per_layer(shape):
        return pl.BlockSpec((None,) + shape, lambda l: (l,) + (0,) * len(shape))

    out_tails = [(1, n_state), (1, n_state),
                 (n_q, V7X_MXU_DIM, V7X_MXU_DIM), (n_q, V7X_MXU_DIM, V7X_MXU_DIM),
                 (n_q, 2 * V7X_MXU_DIM, V7X_MXU_DIM), (n_q, V7X_MXU_DIM, V7X_MXU_DIM)]
    out_dtypes = [F32, F32, BF16, BF16, BF16, BF16]
    args = (are, aim, ldt, btr, bti, ctr, cti)
    return pl.pallas_call(
        _s5_prep_kernel,
        grid=(n_layers,),
        in_specs=[per_layer(a.shape[1:]) for a in args],
        out_specs=[per_layer(t) for t in out_tails],
        out_shape=[jax.ShapeDtypeStruct((n_layers,) + t, dt) for t, dt in zip(out_tails, out_dtypes)],
        compiler_params=pltpu.CompilerParams(
            dimension_semantics=("parallel",),
            vmem_limit_bytes=_vmem_limit(2 * sum(
                int(jnp.dtype(dt).itemsize) * functools.reduce(lambda a, b: a * b, t)
                for t, dt in zip(out_tails, out_dtypes)))),
        name="s5_prep",
    )(*args)


N_S5_PARAMS = 15
S5_WEIGHTS = (1, 4, 5, 6, 7, 9, 11, 13)


def _s5_layer_kernel(*refs, n_seq, n_steps, has_h0, swap_bt, n_cast, n_prev, lazy_layers):
    if has_h0:
        x_ref, h0r_ref, h0i_ref = refs[:3]
        refs = refs[3:]
    else:
        x_ref = refs[0]
        refs = refs[1:]
    params = list(refs[:N_S5_PARAMS])
    refs = refs[N_S5_PARAMS:]
    n_in = n_cast + n_prev
    o_ref, str_ref, sti_ref = refs[n_in:n_in + 3]
    n_lazy = len(S5_WEIGHTS) + 1 if lazy_layers else 0
    n_scratch = 5 + (4 if swap_bt else 0) + n_lazy
    scratch = refs[-n_scratch:]
    hpr_ref, hpi_ref, hr_ref, hi_ref, y_ref = scratch[:5]
    _cast_slabs(refs[:n_cast], refs[n_in + 3:-n_scratch])
    step = pl.program_id(0)
    n_tiles = pl.num_programs(0)

    w_copies = []
    if lazy_layers:
        w_sem = scratch[-1]
        for k, (p, layer) in enumerate(zip(S5_WEIGHTS, lazy_layers)):
            w_copies.append(pltpu.make_async_copy(params[p].at[layer], scratch[-n_lazy + k],
                                                  w_sem.at[k]))
            params[p] = scratch[-n_lazy + k]
        for cp in w_copies:
            cp.start()

    def need(*which):
        for k in which:
            if w_copies:
                w_copies[k].wait()

    (npre_ref, win_ref, l4r_ref, l4i_ref, wur_ref, wui_ref, vc_ref, kloc_ref, dsk_ref,
     w1_ref, b1_ref, w2_ref, b2_ref, wout_ref, npost_ref) = params

    if swap_bt:
        xbuf_ref, obuf_ref, in_sem, out_sem = scratch[5:9]
        slot = step % 2

        def tile_copies(tile, sl, fetch):
            seq_rows = pl.ds(tile * n_steps, n_steps)
            if fetch:
                return [pltpu.make_async_copy(x_ref.at[b, seq_rows, :], xbuf_ref.at[sl, :, b, :],
                                              in_sem.at[sl, b]) for b in range(n_seq)]
            return [pltpu.make_async_copy(obuf_ref.at[sl, :, b, :], o_ref.at[b, seq_rows, :],
                                          out_sem.at[sl, b]) for b in range(n_seq)]

        @pl.when(step == 0)
        def _():
            for cp in tile_copies(0, 0, True):
                cp.start()

        @pl.when(step + 1 < n_tiles)
        def _():
            for cp in tile_copies(step + 1, 1 - slot, True):
                cp.start()

        for cp in tile_copies(step, slot, True):
            cp.wait()

        @pl.when(step >= 2)
        def _():
            for cp in tile_copies(step - 2, slot, False):
                cp.wait()

    @pl.when(step == 0)
    def _():
        if has_h0:
            for c in range(0, hr_ref.shape[1], STATE_T):
                hr_ref[:, c:c + STATE_T] = h0r_ref[c:c + STATE_T, :].T
                hi_ref[:, c:c + STATE_T] = h0i_ref[c:c + STATE_T, :].T
        else:
            hr_ref[...] = jnp.zeros_like(hr_ref)
            hi_ref[...] = jnp.zeros_like(hi_ref)

    d = npre_ref.shape[-1]
    width = win_ref.shape[1] // 2
    c_rows = PIECE_ROWS
    c_steps = c_rows // n_seq
    n_pieces = n_steps // c_steps
    x_c, xb_c, z_c = [], [], []
    need(0)
    for c in range(n_pieces):
        if swap_bt:
            xc = xbuf_ref.at[slot][c * c_steps:(c + 1) * c_steps].reshape(c_rows, d)
        else:
            xc = x_ref[c * c_rows:(c + 1) * c_rows, :]
        hn = _rmsnorm(xc, npre_ref[...]).astype(BF16)
        x_c.append(xc)
        xb_c.append(_dot(hn, win_ref[:, :width]))
        z_c.append(_dot(hn, win_ref[:, width:]))
    xb = jnp.concatenate(xb_c, axis=0)

    n_blocks = n_steps // BLOCK
    rows = n_blocks * n_seq
    x4 = xb.reshape(n_blocks, BLOCK, n_seq, width)
    xi = [x4[:, i].reshape(rows, width) for i in range(BLOCK)]

    sub = V7X_SUBLANES
    pair = 2 * sub
    n_q, _, n_cols = wur_ref.shape
    n_v = width // V7X_LANES
    half = V7X_LANES // 2
    low = lax.broadcasted_iota(jnp.int32, (rows, V7X_LANES), 1) < half

    lhs_q = []
    need(1, 2)
    for v in range(n_v):
        cols = slice(v * V7X_LANES, (v + 1) * V7X_LANES)
        p = [a[:, cols] for a in xi]
        pr = [pltpu.roll(a, half, 1) for a in p]
        lhs_lo = jnp.concatenate([jnp.where(low, p[i], pr[i + 1]) for i in range(0, BLOCK, 2)], axis=1)
        lhs_hi = jnp.concatenate([jnp.where(low, pr[i], p[i + 1]) for i in range(0, BLOCK, 2)], axis=1)
        for q, lhs in ((2 * v, lhs_lo.astype(BF16)), (2 * v + 1, lhs_hi.astype(BF16))):
            lanes = pl.ds(q * n_cols, n_cols)
            wr = _dot(lhs, wur_ref[q])
            wi = _dot(lhs, wui_ref[q])
            if n_seq == sub:
                ar = jnp.broadcast_to(l4r_ref[:, lanes], (sub, n_cols))
                ai = jnp.broadcast_to(l4i_ref[:, lanes], (sub, n_cols))
                hr, hi = hr_ref[:, lanes], hi_ref[:, lanes]
                for k in range(n_blocks // 2):
                    prv_r, prv_i = [], []
                    for r0 in (2 * k * sub, (2 * k + 1) * sub):
                        prv_r.append(hr)
                        prv_i.append(hi)
                        hr, hi = (ar * hr - ai * hi + wr[r0:r0 + sub],
                                  ar * hi + ai * hr + wi[r0:r0 + sub])
                    hpr_ref[k * pair:(k + 1) * pair, lanes] = jnp.concatenate(prv_r, axis=0).astype(BF16)
                    hpi_ref[k * pair:(k + 1) * pair, lanes] = jnp.concatenate(prv_i, axis=0).astype(BF16)
                hr_ref[:, lanes] = hr
                hi_ref[:, lanes] = hi
            else:
                ar = jnp.broadcast_to(l4r_ref[:, lanes], (pair, n_cols))
                ai = jnp.broadcast_to(l4i_ref[:, lanes], (pair, n_cols))
                for m in range(n_seq // pair):
                    srows = pl.ds(m * pair, pair)
                    hr, hi = hr_ref[srows, lanes], hi_ref[srows, lanes]
                    for blk in range(n_blocks):
                        r0 = blk * n_seq + m * pair
                        hpr_ref[r0:r0 + pair, lanes] = hr.astype(BF16)
                        hpi_ref[r0:r0 + pair, lanes] = hi.astype(BF16)
                        hr, hi = (ar * hr - ai * hi + wr[r0:r0 + pair],
                                  ar * hi + ai * hr + wi[r0:r0 + pair])
                    hr_ref[srows, lanes] = hr
                    hi_ref[srows, lanes] = hi
            lhs_q.append(lhs)

    need(3, 4)
    for v in range(n_v):
        cols = slice(v * V7X_LANES, (v + 1) * V7X_LANES)
        yq = []
        for q in (2 * v, 2 * v + 1):
            lanes = pl.ds(q * n_cols, n_cols)
            st = jnp.concatenate([hpr_ref[:, lanes], hpi_ref[:, lanes]], axis=1)
            yq.append(_dot(st, vc_ref[q]) + _dot(lhs_q[q], kloc_ref[q]))
        ys = []
        for i in range(BLOCK):
            lo_q, hi_q = (a[:, (i // 2) * V7X_LANES:(i // 2 + 1) * V7X_LANES] for a in yq)
            if i % 2 == 0:
                yv = jnp.where(low, lo_q, pltpu.roll(hi_q, half, 1))
            else:
                yv = jnp.where(low, pltpu.roll(lo_q, half, 1), hi_q)
            ys.append(jax.nn.gelu(yv + dsk_ref[:, cols] * xi[i][:, cols]))
        y = jnp.stack([a.reshape(n_blocks, n_seq, V7X_LANES) for a in ys], axis=1)
        y_ref[:, cols] = y.reshape(n_steps * n_seq, V7X_LANES).astype(BF16)

    need(5, 6, 7)
    for c in range(n_pieces):
        y = y_ref[c * c_rows:(c + 1) * c_rows, :]
        g = (_dot(y, w1_ref[...]) + b1_ref[...]) * jax.nn.sigmoid(_dot(y, w2_ref[...]) + b2_ref[...])
        out = _dot((g * _silu(z_c[c])).astype(BF16), wout_ref[...])
        xn = x_c[c] + _rmsnorm(out, npost_ref[...])
        if swap_bt:
            obuf_ref.at[slot][c * c_steps:(c + 1) * c_steps] = xn.reshape(c_steps, n_seq, d)
        else:
            o_ref[c * c_rows:(c + 1) * c_rows, :] = xn

    if swap_bt:
        for cp in tile_copies(step, slot, False):
            cp.start()

    @pl.when(step == n_tiles - 1)
    def _():
        for dst, src in ((str_ref, hr_ref), (sti_ref, hi_ref)):
            for view in ([dst] if n_prev else [dst.at[slab] for slab in range(dst.shape[0])]):
                if has_h0:
                    for c in range(0, src.shape[1], STATE_T):
                        view[c:c + STATE_T, :] = src[:, c:c + STATE_T].T
                else:
                    view[...] = src[...]
        if swap_bt:
            for cp in tile_copies(step, slot, False):
                cp.wait()

    if swap_bt:
        @pl.when(jnp.logical_and(step == n_tiles - 1, n_tiles >= 2))
        def _():
            for cp in tile_copies(step - 1, 1 - slot, False):
                cp.wait()


def _s5_layer(x, h0, layer_b, layer, npre, win, core_consts, out_consts, npost, n_seq, n_steps,
              st_prev, cast_jobs=()):
    swap_bt = x.ndim == 3
    d = x.shape[-1]
    exp_b = win[0].shape[2] // 2
    n_state = core_consts[0].shape[-1]
    tile = n_seq * n_steps
    n_tiles = x.size // (tile * d)
    rows = tile // BLOCK
    has_h0 = h0 is not None
    assert n_steps % BLOCK == 0 and (n_seq > V7X_SUBLANES or (n_steps // BLOCK) % 2 == 0)
    assert tile % PIECE_ROWS == 0 and PIECE_ROWS % n_seq == 0
    params_b = [win] + [(c, layer_b) for c in core_consts] + list(out_consts)
    assert len(params_b) + 2 == N_S5_PARAMS
    c_in, c_out, c_shapes = _cast_specs(cast_jobs, n_tiles)
    args = ([x] + (list(h0) if has_h0 else []) + [npre] + [a for a, _ in params_b] + [npost]
            + [w for w, _ in cast_jobs] + list(st_prev))
    if swap_bt:
        io_spec = pl.BlockSpec(memory_space=pl.ANY)
        io_scratch = [pltpu.VMEM((2, n_steps, n_seq, d), F32), pltpu.VMEM((2, n_steps, n_seq, d), F32),
                      pltpu.SemaphoreType.DMA((2, n_seq)), pltpu.SemaphoreType.DMA((2, n_seq))]
    else:
        io_spec = pl.BlockSpec((tile, d), lambda i: (i, 0))
        io_scratch = []
    first = 1 + 2 * has_h0
    row_layers = {first: layer, first + 1 + len(params_b): layer}
    row_layers.update({first + 1 + k: l for k, (a, l) in enumerate(params_b) if a.ndim == 2})
    lazy = n_tiles == 1
    lazy_idx = [p - 1 for p in S5_WEIGHTS]
    hbm = pl.BlockSpec(memory_space=pl.ANY)
    in_specs = ([io_spec] + [_layer_spec(a, layer_b) for a in (h0 if has_h0 else ())]
                + [_layer_spec(npre, layer)]
                + [hbm if lazy and k in lazy_idx else _layer_spec(a, l)
                   for k, (a, l) in enumerate(params_b)]
                + [_layer_spec(npost, layer)] + c_in + [hbm] * len(st_prev))
    lazy_scratch = ([pltpu.VMEM(params_b[k][0].shape[1:], params_b[k][0].dtype) for k in lazy_idx]
                    + [pltpu.SemaphoreType.DMA((len(lazy_idx),))]) if lazy else []
    n_stack = core_consts[0].shape[0]
    st_block = (n_state, n_seq) if has_h0 else (n_seq, n_state)
    st_spec = (pl.BlockSpec((None,) + st_block, lambda i: (layer_b, 0, 0)) if st_prev else
               pl.BlockSpec((n_stack,) + st_block, lambda i: (0, 0, 0)))
    st_shape = jax.ShapeDtypeStruct((n_stack,) + st_block, F32)
    n_args = len(args)
    footprint = (_layer_bytes(*[a for a, _ in params_b]) + (_layer_bytes(*h0) if has_h0 else 0)
                 + 4 * tile * d * 4 + 2 * rows * n_state * 2 + 8 * n_seq * n_state * 4
                 + (0 if st_prev else 4 * (n_stack - 1) * n_seq * n_state * 4)
                 + tile * exp_b * (2 + 6 * 4)
                 + 6 * sum(w[0].size for w, _ in cast_jobs) // n_tiles * 2)
    res = pl.pallas_call(
        _row_views(functools.partial(
            _s5_layer_kernel, n_seq=n_seq, n_steps=n_steps, has_h0=has_h0, swap_bt=swap_bt,
            n_cast=len(cast_jobs), n_prev=len(st_prev),
            lazy_layers=tuple(params_b[k][1] for k in lazy_idx) if lazy else ()), row_layers),
        grid=(n_tiles,),
        in_specs=in_specs,
        out_specs=[io_spec, st_spec, st_spec] + c_out,
        out_shape=[jax.ShapeDtypeStruct(x.shape, F32), st_shape, st_shape] + c_shapes,
        input_output_aliases={n_args - len(st_prev) + k: 1 + k for k in range(len(st_prev))},
        scratch_shapes=[pltpu.VMEM((rows, n_state), BF16), pltpu.VMEM((rows, n_state), BF16),
                        pltpu.VMEM((n_seq, n_state), F32), pltpu.VMEM((n_seq, n_state), F32),
                        pltpu.VMEM((tile, exp_b), BF16)] + io_scratch + lazy_scratch,
        compiler_params=pltpu.CompilerParams(
            dimension_semantics=("arbitrary",),
            vmem_limit_bytes=_vmem_limit(footprint)),
        name="s5_layer_sample" if has_h0 else "s5_layer_prompt",
    )(*args)
    return res[0], res[1], res[2], res[3:]


def kernel(x_prompt, x_sample, state_ssm_re, state_ssm_im, norm_pre, norm_post,
           w_in_a, ln_v_g, ln_v_b, w_s, b_s, w_out_a,
           w_in_b, a_re, a_im, log_dt, b_re, b_im, c_re, c_im, d_skip,
           w_glu1, b_glu1, w_glu2, b_glu2, w_out_b):
    n_batch, seq_len, d = x_prompt.shape
    n_dec, dec_len, _ = x_sample.shape
    depth = norm_pre.shape[0]
    n_ssm, n_groups, state_p = a_re.shape
    n_state = n_groups * state_p
    assert seq_len % ROW_TILE == 0 and ROW_TILE % CHUNK == 0 and seq_len % SCAN_STEPS == 0
    assert n_batch == V7X_SUBLANES and n_dec % (2 * V7X_SUBLANES) == 0
    assert dec_len <= CHUNK and dec_len % BLOCK == 0
    assert state_p == STATE_P and b_re.shape[-1] == SSM_GROUP and w_s.shape[1] == N_HEADS

    rows3 = lambda a: a.reshape(a.shape[0], 1, -1)
    npre, npost, lng, lnb = norm_pre, norm_post, ln_v_g, ln_v_b
    wa16 = ((w_in_a, 0), (w_out_a, 0))
    bst = jnp.swapaxes(b_s, 1, 2)
    ws_dec = w_s[:, :, :dec_len, :dec_len].reshape(-1)
    bs_dec = b_s[:, :, :dec_len].reshape(-1)

    b1, b2 = b_glu1, b_glu2
    lanes_gp = lambda a, perm: jnp.transpose(a, perm).reshape(n_ssm, SSM_GROUP, n_state)
    l4r, l4i, wur, wui, vcw, kloc = _s5_prep(
        rows3(a_re), rows3(a_im), rows3(jnp.repeat(log_dt, state_p, axis=1)),
        lanes_gp(b_re, (0, 3, 1, 2)), lanes_gp(b_im, (0, 3, 1, 2)),
        lanes_gp(c_re, (0, 2, 1, 3)), lanes_gp(c_im, (0, 2, 1, 3)))
    core_consts = (l4r, l4i, wur, wui, vcw, kloc, d_skip)
    h0 = tuple(jnp.swapaxes(st.reshape(n_ssm, n_dec, n_state), 1, 2)
               for st in (state_ssm_re, state_ssm_im))

    xp = x_prompt.reshape(n_batch * seq_len, d)
    xs = jnp.transpose(x_sample, (1, 0, 2)).reshape(dec_len * n_dec, d)

    chunk_v, st_p, st_s = None, (), ()
    for i in range(depth):
        j = i // 2
        if i % 2 == 0:
            jobs = [(w, j) for w in (w_in_b, w_glu1, w_glu2, w_out_b)] if i + 1 < depth else []
            xp, wb16, own16 = _gmlp_prompt(xp, j, i, npre, npost, lng, lnb, *wa16, w_s, bst, jobs)
            if own16:
                wa16 = tuple((w, 0) for w in own16)
            xs, chunk_v = _gmlp_sample(xs, n_dec, dec_len, j, i, npre, npost, lng, lnb, *wa16,
                                       ws_dec, bs_dec, chunk_v)
        else:
            win = (wb16[0], 0)
            out_consts = ((wb16[1], 0), (b1, j), (wb16[2], 0), (b2, j), (wb16[3], 0))
            jobs = [(w_in_a, j + 1), (w_out_a, j + 1)] if i + 1 < depth else []
            xp3, sr, si, wa16 = _s5_layer(xp.reshape(n_batch, seq_len, d), None, j, i, npre, win,
                                          core_consts, out_consts, npost, n_batch, SCAN_STEPS,
                                          st_p, jobs)
            wa16 = tuple((w, 0) for w in wa16)
            xp = xp3.reshape(n_batch * seq_len, d)
            st_p = (sr, si)
            xs, sr, si, _ = _s5_layer(xs, h0, j, i, npre, win, core_consts, out_consts, npost,
                                      n_dec, dec_len, st_s)
            st_s = (sr, si)

    y_prompt = xp.reshape(n_batch, seq_len, d)
    y_sample = jnp.transpose(xs.reshape(dec_len, n_dec, d), (1, 0, 2))
    states = lambda st, n: st.reshape(n_ssm, n, n_groups, state_p)
    st_s = tuple(jnp.swapaxes(st, 1, 2) for st in st_s)
    return (y_prompt, y_sample, chunk_v, states(st_p[0], n_batch), states(st_p[1], n_batch),
            states(st_s[0], n_dec), states(st_s[1], n_dec))
```
